```python
import math
import jax, jax.numpy as jnp
from jax import lax
import numpy as np

D_MODEL = 1024
BATCH = 16
SEQ = 2048
DEPTH = 4

QBLOCK = 128
NEG_INF = -1e30
FORCE_SCORE = 1e9
RMS_EPS = 1e-6

A_HEADS = 4
A_HEAD_DIM = 64
A_WIDTH = A_HEADS * A_HEAD_DIM
A_KV_LATENT = 128
IDX_HEADS = 8
IDX_DIM = 64
TOPK_MAX = 256

B_HEADS = 4
B_HEAD_DIM = 128
B_WIDTH = B_HEADS * B_HEAD_DIM
CONV_WIDTH = 4
GDN_CHUNK = 64

C_HEADS = 4
C_HEAD_DIM = 64
C_WIDTH = C_HEADS * C_HEAD_DIM
CMP_BLOCK = 32
CMP_STRIDE = 16
SEL_BLOCK = 64
N_SEL = 16
WINDOW = 512
PHI_HIDDEN = 256

N_BUCKETS = 32
BUCKET_MAX_EXACT = 16
BUCKET_MAX_DIST = 128

N_BRANCH = 3
W_MIX = A_WIDTH + B_WIDTH + C_WIDTH

A_COLS = (A_WIDTH, A_KV_LATENT, IDX_HEADS * IDX_DIM, IDX_DIM, IDX_HEADS, A_WIDTH)
B_COLS = (3 * B_WIDTH, B_HEADS, B_HEADS, B_WIDTH)
C_COLS = (C_WIDTH, 6 * C_HEAD_DIM, 3 * C_HEADS, C_WIDTH)
G_COLS = (N_BRANCH * D_MODEL,)
N_IN = sum(A_COLS) + sum(B_COLS) + sum(C_COLS) + sum(G_COLS)

kernel_name = 'hybrid_dsa_gdn_nsa_trunk'


def split_cols(a, sizes):
    idx = np.cumsum(sizes)[:-1].tolist()
    return jnp.split(a, idx, axis=-1)


def split_heads(t, n, d):
    return t.reshape(*t.shape[:-1], n, d)


def rms_norm(x, g):
    x32 = x.astype(jnp.float32)
    y = x32 * lax.rsqrt(jnp.mean(x32 * x32, axis=-1, keepdims=True) + RMS_EPS)
    return (y * g.astype(jnp.float32)).astype(x.dtype)


def l2norm(x):
    x32 = x.astype(jnp.float32)
    return x32 * lax.rsqrt(jnp.sum(x32 * x32, axis=-1, keepdims=True) + RMS_EPS)


def masked_softmax(logits, mask):
    logits = jnp.where(mask, logits.astype(jnp.float32), NEG_INF)
    return jnp.where(mask, jax.nn.softmax(logits, axis=-1), 0.0)


def t5_bucket(dist):
    dist = jnp.maximum(dist, 0)
    log_ratio = jnp.log(jnp.maximum(dist, 1).astype(jnp.float32) / BUCKET_MAX_EXACT) / math.log(BUCKET_MAX_DIST / BUCKET_MAX_EXACT)
    large = BUCKET_MAX_EXACT + (log_ratio * (N_BUCKETS - BUCKET_MAX_EXACT)).astype(jnp.int32)
    large = jnp.minimum(large, N_BUCKETS - 1)
    return jnp.where(dist < BUCKET_MAX_EXACT, dist, large)


def to_qblocks(a):
    b, s = a.shape[:2]
    return jnp.moveaxis(a.reshape(b, s // QBLOCK, QBLOCK, *a.shape[2:]), 1, 0)


def from_qblocks(a):
    nb, b = a.shape[:2]
    return jnp.moveaxis(a, 0, 1).reshape(b, nb * QBLOCK, *a.shape[3:])


def dsa_mixer(q, k, v, q_idx, k_idx, w_idx, bias_tab):
    b, s = q.shape[:2]
    topk = min(TOPK_MAX, s // 4)
    bidx = jnp.arange(b)[:, None, None]
    key_pos = jnp.arange(s)
    scale = A_HEAD_DIM ** -0.5

    def block(args):
        qb, qib, wb, start = args
        t = start + jnp.arange(QBLOCK)
        idx_logits = jnp.einsum('bqhd,bsd->bqhs', qib, k_idx) * (IDX_DIM ** -0.5)
        score = jnp.einsum('bqh,bqhs->bqs', wb * (IDX_HEADS ** -0.5), jax.nn.relu(idx_logits))
        score = jnp.where(key_pos[None, None, :] <= t[None, :, None], score.astype(jnp.float32), NEG_INF)
        _, sel = lax.top_k(score, topk)
        kg = k[bidx, sel]
        vg = v[bidx, sel]
        dist = t[None, :, None] - sel
        bias = jnp.transpose(bias_tab[t5_bucket(dist)], (0, 3, 1, 2))
        logits = jnp.einsum('bqhd,bqkhd->bhqk', qb, kg).astype(jnp.float32) * scale + bias.astype(jnp.float32)
        p = masked_softmax(logits, (dist >= 0)[:, None])
        return jnp.einsum('bhqk,bqkhd->bqhd', p.astype(vg.dtype), vg)

    starts = jnp.arange(s // QBLOCK, dtype=jnp.int32) * QBLOCK
    out = lax.map(block, (to_qblocks(q), to_qblocks(q_idx), to_qblocks(w_idx), starts))
    return from_qblocks(out)


def causal_depthwise_conv(x, w):
    c = x.shape[-1]
    return lax.conv_general_dilated(x, w.astype(x.dtype)[:, None, :], window_strides=(1,), padding=[(CONV_WIDTH - 1, 0)], dimension_numbers=('NWC', 'WIO', 'NWC'), feature_group_count=c)


def gated_deltanet_mixer(qkv, a_in, b_in, conv_w, a_log, dt_bias):
    b, s = qkv.shape[:2]
    n = s // GDN_CHUNK
    qkv = jax.nn.silu(causal_depthwise_conv(qkv, conv_w))
    q, k, v = jnp.split(qkv, 3, axis=-1)
    q = l2norm(split_heads(q, B_HEADS, B_HEAD_DIM)) * (B_HEAD_DIM ** -0.5)
    k = l2norm(split_heads(k, B_HEADS, B_HEAD_DIM))
    v = split_heads(v, B_HEADS, B_HEAD_DIM).astype(jnp.float32)
    beta = jax.nn.sigmoid(b_in.astype(jnp.float32))
    g = -jnp.exp(a_log.astype(jnp.float32)) * jax.nn.softplus(a_in.astype(jnp.float32) + dt_bias.astype(jnp.float32))

    def chunk(t):
        return jnp.moveaxis(t.reshape(b, n, GDN_CHUNK, *t.shape[2:]), 3, 2)

    q, k, v, beta, g = (chunk(t) for t in (q, k, v, beta, g))
    g_cum = jnp.cumsum(g, axis=-1)
    g_last = g_cum[..., -1]
    lower = jnp.tril(jnp.ones((GDN_CHUNK, GDN_CHUNK), dtype=bool))
    strict = jnp.tril(jnp.ones((GDN_CHUNK, GDN_CHUNK), dtype=bool), -1)
    diff = g_cum[..., :, None] - g_cum[..., None, :]
    decay = jnp.where(lower, jnp.exp(jnp.where(lower, diff, 0.0)), 0.0)
    k_beta = k * beta[..., None]
    a_mat = jnp.where(strict, jnp.einsum('bnhid,bnhjd->bnhij', k_beta, k) * decay, 0.0)
    eye = jnp.eye(GDN_CHUNK, dtype=jnp.float32)
    t_inv = lax.linalg.triangular_solve(eye + a_mat, jnp.broadcast_to(eye, a_mat.shape), left_side=True, lower=True)
    u = jnp.einsum('bnhij,bnhjd->bnhid', t_inv, v * beta[..., None])
    w = jnp.einsum('bnhij,bnhjd->bnhid', t_inv, k_beta * jnp.exp(g_cum)[..., None])
    attn = jnp.einsum('bnhid,bnhjd->bnhij', q, k) * decay
    q_dec = q * jnp.exp(g_cum)[..., None]
    k_dec = k * jnp.exp(g_last[..., None] - g_cum)[..., None]

    def step(state, xs):
        u_c, w_c, q_c, k_c, attn_c, gl_c = xs
        v_new = u_c - jnp.einsum('bhik,bhkv->bhiv', w_c, state)
        o = jnp.einsum('bhik,bhkv->bhiv', q_c, state) + jnp.einsum('bhij,bhjv->bhiv', attn_c, v_new)
        state = state * jnp.exp(gl_c)[..., None, None] + jnp.einsum('bhik,bhiv->bhkv', k_c, v_new)
        return state, o

    state0 = jnp.zeros((b, B_HEADS, B_HEAD_DIM, B_HEAD_DIM), jnp.float32)
    xs = tuple(jnp.moveaxis(t, 1, 0) for t in (u, w, q_dec, k_dec, attn, g_last))
    _, o = lax.scan(step, state0, xs)
    o = jnp.moveaxis(jnp.moveaxis(o, 0, 1), 2, 3)
    return o.reshape(b, s, B_HEADS, B_HEAD_DIM)


def compress_blocks(tok, pos, w1, w2):
    b, s, d = tok.shape
    n_cmp = (s - CMP_BLOCK) // CMP_STRIDE + 1
    idx = jnp.arange(n_cmp)[:, None] * CMP_STRIDE + jnp.arange(CMP_BLOCK)[None, :]
    blocks = tok[:, idx] + pos
    return jax.nn.silu(blocks.reshape(b, n_cmp, CMP_BLOCK * d) @ w1) @ w2


def nsa_mixer(q, kv, gates, k_norm, cmp_pos, phi_w1, phi_w2, bias_tab):
    b, s = q.shape[:2]
    scale = C_HEAD_DIM ** -0.5
    t_all = jnp.arange(s)
    k_cmp, v_cmp, k_sel, v_sel, k_win, v_win = (kv[:, :, i] for i in range(6))
    k_sel = rms_norm(k_sel, k_norm[1])
    k_win = rms_norm(k_win, k_norm[2])

    kc = rms_norm(compress_blocks(k_cmp, cmp_pos[0], phi_w1[0], phi_w2[0]), k_norm[0])
    vc = compress_blocks(v_cmp, cmp_pos[1], phi_w1[1], phi_w2[1])
    n_cmp = kc.shape[1]
    cmp_start = jnp.arange(n_cmp) * CMP_STRIDE
    cmp_end = cmp_start + CMP_BLOCK - 1
    cmp_valid = cmp_end[None, :] <= t_all[:, None]
    cmp_bias = jnp.transpose(bias_tab[t5_bucket(t_all[:, None] - cmp_end[None, :])], (2, 0, 1))
    logits = jnp.einsum('bqhd,bnd->bhqn', q, kc).astype(jnp.float32) * scale + cmp_bias.astype(jnp.float32)
    p_cmp = masked_softmax(logits, cmp_valid[None, None])
    o_cmp = jnp.einsum('bhqn,bnd->bqhd', p_cmp.astype(vc.dtype), vc)

    n_sb = s // SEL_BLOCK
    n_pick = min(N_SEL, n_sb)
    sb_start = jnp.arange(n_sb) * SEL_BLOCK
    overlap = ((cmp_start[:, None] < sb_start[None, :] + SEL_BLOCK) & (cmp_start[:, None] + CMP_BLOCK > sb_start[None, :])).astype(jnp.float32)
    importance = jnp.einsum('bhqn,nj->bqj', p_cmp, overlap)
    cur = t_all // SEL_BLOCK
    blk = jnp.arange(n_sb)
    forced = (blk[None, :] == 0) | (blk[None, :] == cur[:, None]) | (blk[None, :] == jnp.maximum(cur[:, None] - 1, 0))
    admissible = sb_start[None, :] <= t_all[:, None]
    importance = jnp.where(admissible[None], jnp.where(forced[None], FORCE_SCORE, importance), NEG_INF)
    _, sel = lax.top_k(importance, n_pick)

    k_blocks = k_sel.reshape(b, n_sb, SEL_BLOCK, C_HEAD_DIM)
    v_blocks = v_sel.reshape(b, n_sb, SEL_BLOCK, C_HEAD_DIM)
    k_win_pad = jnp.pad(k_win, ((0, 0), (WINDOW, 0), (0, 0)))
    v_win_pad = jnp.pad(v_win, ((0, 0), (WINDOW, 0), (0, 0)))
    bidx = jnp.arange(b)[:, None, None]
    n_keys = n_pick * SEL_BLOCK

    def block(args):
        qb, selb, start = args
        t = start + jnp.arange(QBLOCK)
        kg = k_blocks[bidx, selb].reshape(b, QBLOCK, n_keys, C_HEAD_DIM)
        vg = v_blocks[bidx, selb].reshape(b, QBLOCK, n_keys, C_HEAD_DIM)
        s_pos = (selb[..., None] * SEL_BLOCK + jnp.arange(SEL_BLOCK)).reshape(b, QBLOCK, n_keys)
        dist = t[None, :, None] - s_pos
        bias = jnp.transpose(bias_tab[t5_bucket(dist)], (0, 3, 1, 2))
        lg = jnp.einsum('bqhd,bqkd->bhqk', qb, kg).astype(jnp.float32) * scale + bias.astype(jnp.float32)
        p = masked_softmax(lg, (dist >= 0)[:, None])
        o_sel = jnp.einsum('bhqk,bqkd->bqhd', p.astype(vg.dtype), vg)
        kw = lax.dynamic_slice_in_dim(k_win_pad, start, QBLOCK + WINDOW, axis=1)
        vw = lax.dynamic_slice_in_dim(v_win_pad, start, QBLOCK + WINDOW, axis=1)
        s_w = start - WINDOW + jnp.arange(QBLOCK + WINDOW)
        dist_w = t[:, None] - s_w[None, :]
        valid_w = (s_w[None, :] >= 0) & (dist_w >= 0) & (dist_w < WINDOW)
        bias_w = jnp.transpose(bias_tab[t5_bucket(dist_w)], (2, 0, 1))
        lw = jnp.einsum('bqhd,bsd->bhqs', qb, kw).astype(jnp.float32) * scale + bias_w.astype(jnp.float32)
        pw = masked_softmax(lw, valid_w[None, None])
        o_win = jnp.einsum('bhqs,bsd->bqhd', pw.astype(vw.dtype), vw)
        return o_sel, o_win

    starts = jnp.arange(s // QBLOCK, dtype=jnp.int32) * QBLOCK
    o_sel, o_win = lax.map(block, (to_qblocks(q), to_qblocks(sel), starts))
    o_sel = from_qblocks(o_sel)
    o_win = from_qblocks(o_win)
    g = jax.nn.sigmoid(gates.astype(jnp.float32)).reshape(b, s, C_HEADS, 3)
    out = g[..., 0:1] * o_cmp + g[..., 1:2] * o_sel + g[..., 2:3] * o_win
    return out.astype(q.dtype)


def setup_inputs(seed: int = 0) -> dict:
    key = jax.random.key(seed)
    ks = jax.random.split(key, 22)
    f32 = jnp.float32

    def nrm(k, shape, scale):
        return jax.random.normal(k, shape, f32) * scale

    def gain(k, shape):
        return 1.0 + 0.02 * jax.random.normal(k, shape, f32)

    x = nrm(ks[0], (BATCH, SEQ, D_MODEL), 1.0)
    norm_g = gain(ks[1], (DEPTH, D_MODEL))
    w_in = nrm(ks[2], (DEPTH, D_MODEL, N_IN), D_MODEL ** -0.5)
    a_kv_norm = gain(ks[3], (DEPTH, A_KV_LATENT))
    a_w_ukv = nrm(ks[4], (DEPTH, A_KV_LATENT, 2 * A_WIDTH), A_KV_LATENT ** -0.5)
    a_q_norm = gain(ks[5], (DEPTH, A_HEAD_DIM))
    a_k_norm = gain(ks[6], (DEPTH, A_HEAD_DIM))
    b_conv = nrm(ks[7], (DEPTH, CONV_WIDTH, 3 * B_WIDTH), CONV_WIDTH ** -0.5)
    b_a_log = jnp.log(jax.random.uniform(ks[8], (DEPTH, B_HEADS), f32, 1.0, 16.0))
    dt = jnp.exp(jax.random.uniform(ks[9], (DEPTH, B_HEADS), f32, math.log(1e-3), math.log(1e-1)))
    b_dt_bias = dt + jnp.log(-jnp.expm1(-dt))
    b_out_norm = gain(ks[10], (DEPTH, B_HEAD_DIM))
    c_q_norm = gain(ks[11], (DEPTH, C_HEAD_DIM))
    c_k_norm = gain(ks[12], (DEPTH, 3, C_HEAD_DIM))
    c_cmp_pos = nrm(ks[13], (DEPTH, 2, CMP_BLOCK, C_HEAD_DIM), 0.02)
    c_phi_w1 = nrm(ks[14], (DEPTH, 2, CMP_BLOCK * C_HEAD_DIM, PHI_HIDDEN), (CMP_BLOCK * C_HEAD_DIM) ** -0.5)
    c_phi_w2 = nrm(ks[15], (DEPTH, 2, PHI_HIDDEN, C_HEAD_DIM), PHI_HIDDEN ** -0.5)
    w_branch = jnp.concatenate([nrm(ks[16], (DEPTH, A_WIDTH, D_MODEL), A_WIDTH ** -0.5), nrm(ks[17], (DEPTH, B_WIDTH, D_MODEL), B_WIDTH ** -0.5), nrm(ks[18], (DEPTH, C_WIDTH, D_MODEL), C_WIDTH ** -0.5)], axis=1)
    w_out = nrm(ks[19], (DEPTH, D_MODEL, D_MODEL), D_MODEL ** -0.5)
    rel_bias = nrm(ks[20], (N_BUCKETS, A_HEADS + C_HEADS), 0.2)
    return {'x': x, 'norm_g': norm_g, 'w_in': w_in, 'a_kv_norm': a_kv_norm, 'a_w_ukv': a_w_ukv, 'a_q_norm': a_q_norm, 'a_k_norm': a_k_norm, 'b_conv': b_conv, 'b_a_log': b_a_log, 'b_dt_bias': b_dt_bias, 'b_out_norm': b_out_norm, 'c_q_norm': c_q_norm, 'c_k_norm': c_k_norm, 'c_cmp_pos': c_cmp_pos, 'c_phi_w1': c_phi_w1, 'c_phi_w2': c_phi_w2, 'w_branch': w_branch, 'w_out': w_out, 'rel_bias': rel_bias}


def reference(x, norm_g, w_in, a_kv_norm, a_w_ukv, a_q_norm, a_k_norm, b_conv, b_a_log, b_dt_bias, b_out_norm, c_q_norm, c_k_norm, c_cmp_pos, c_phi_w1, c_phi_w2, w_branch, w_out, rel_bias):
    b, s, _ = x.shape
    bias_a = rel_bias[:, :A_HEADS]
    bias_c = rel_bias[:, A_HEADS:]
    for l in range(DEPTH):
        h = rms_norm(x, norm_g[l])
        proj = h @ w_in[l]
        a_part, b_part, c_part, gate_part = split_cols(proj, (sum(A_COLS), sum(B_COLS), sum(C_COLS), sum(G_COLS)))

        a_q, a_ckv, a_qi, a_ki, a_wi, a_z = split_cols(a_part, A_COLS)
        a_k, a_v = jnp.split(rms_norm(a_ckv, a_kv_norm[l]) @ a_w_ukv[l], 2, axis=-1)
        a_q = rms_norm(split_heads(a_q, A_HEADS, A_HEAD_DIM), a_q_norm[l])
        a_k = rms_norm(split_heads(a_k, A_HEADS, A_HEAD_DIM), a_k_norm[l])
        a_v = split_heads(a_v, A_HEADS, A_HEAD_DIM)
        o_a = dsa_mixer(a_q, a_k, a_v, split_heads(a_qi, IDX_HEADS, IDX_DIM), a_ki, a_wi, bias_a)
        y_a = o_a.reshape(b, s, A_WIDTH) * jax.nn.silu(a_z)

        b_qkv, b_a, b_b, b_z = split_cols(b_part, B_COLS)
        o_b = gated_deltanet_mixer(b_qkv, b_a, b_b, b_conv[l], b_a_log[l], b_dt_bias[l])
        y_b = rms_norm(o_b, b_out_norm[l]).reshape(b, s, B_WIDTH).astype(x.dtype) * jax.nn.silu(b_z)

        c_q, c_kv, c_g, c_z = split_cols(c_part, C_COLS)
        c_q = rms_norm(split_heads(c_q, C_HEADS, C_HEAD_DIM), c_q_norm[l])
        o_c = nsa_mixer(c_q, split_heads(c_kv, 6, C_HEAD_DIM), c_g, c_k_norm[l], c_cmp_pos[l], c_phi_w1[l], c_phi_w2[l], bias_c)
        y_c = o_c.reshape(b, s, C_WIDTH) * jax.nn.silu(c_z)

        wb = w_branch[l]
        g_a, g_b, g_c = jnp.split(jax.nn.sigmoid(gate_part), 3, axis=-1)
        merged = g_a * (y_a @ wb[:A_WIDTH]) + g_b * (y_b @ wb[A_WIDTH:A_WIDTH + B_WIDTH]) + g_c * (y_c @ wb[A_WIDTH + B_WIDTH:])
        x = x + (merged @ w_out[l]).astype(x.dtype)
    return x
```

```python
import functools
import math

import jax
import jax.numpy as jnp
import numpy as np
from jax import lax
from jax.experimental import pallas as pl
from jax.experimental.pallas import tpu as pltpu

D_MODEL = 1024
DEPTH = 4
QBLOCK = 128
NEG_INF = -1e30
FORCE_SCORE = 1e9
RMS_EPS = 1e-6

A_HEADS = 4
A_HEAD_DIM = 64
A_WIDTH = A_HEADS * A_HEAD_DIM
A_KV_LATENT = 128
IDX_HEADS = 8
IDX_DIM = 64
TOPK_MAX = 256

B_HEADS = 4
B_HEAD_DIM = 128
B_WIDTH = B_HEADS * B_HEAD_DIM
CONV_WIDTH = 4
GDN_CHUNK = 64

C_HEADS = 4
C_HEAD_DIM = 64
C_WIDTH = C_HEADS * C_HEAD_DIM
CMP_BLOCK = 32
CMP_STRIDE = 16
SEL_BLOCK = 64
N_SEL = 16
WINDOW = 512
PHI_HIDDEN = 256

N_BUCKETS = 32
BUCKET_MAX_EXACT = 16
BUCKET_MAX_DIST = 128

VMEM_LIMIT = 48 * 1024 * 1024

_ORIG = {}
_o = 0
for _name, _w in (("a_q", 256), ("a_ckv", 128), ("a_qi", 512), ("a_ki", 64), ("a_wi", 8), ("a_z", 256),
                  ("b_qkv", 1536), ("b_a", 4), ("b_b", 4), ("b_z", 512),
                  ("c_q", 256), ("c_kv", 384), ("c_g", 12), ("c_z", 256), ("g", 3072)):
    _ORIG[_name] = (_o, _w)
    _o += _w
N_IN = _o

_NEW_ORDER = ("g", "b_qkv", "b_z", "a_qi", "a_q", "a_z", "c_q", "c_z", "c_kv", "a_ckv",
              "a_ki", "a_wi", "b_a", "b_b", "c_g")
OFF = {}
_o = 0
for _name in _NEW_ORDER:
    OFF[_name] = _o
    _o += _ORIG[_name][1]
MISC_OFF = OFF["a_ki"]
N_PAD = 7296
assert MISC_OFF == 7168 and _o <= N_PAD


def _perm_indices():
    idx = np.zeros((N_PAD,), np.int32)
    valid = np.zeros((N_PAD,), bool)
    for name in _NEW_ORDER:
        o_old, w = _ORIG[name]
        idx[OFF[name]:OFF[name] + w] = np.arange(o_old, o_old + w)
        valid[OFF[name]:OFF[name] + w] = True
    return idx, valid


_PERM_IDX, _PERM_VALID = _perm_indices()


IN_TM = 512
IN_TN = N_PAD // 3


def _in_proj_kernel(x_ref, g_ref, w_ref, o_ref):
    x = x_ref[...]
    ms = jnp.mean(x * x, axis=-1, keepdims=True)
    h = (x * lax.rsqrt(ms + RMS_EPS)) * g_ref[...]
    o_ref[...] = jnp.dot(h.astype(jnp.bfloat16), w_ref[...], preferred_element_type=jnp.float32)


def in_proj(xf, g, w_bf16):
    n = xf.shape[0]
    return pl.pallas_call(
        _in_proj_kernel,
        grid=(N_PAD // IN_TN, n // IN_TM),
        in_specs=[
            pl.BlockSpec((IN_TM, D_MODEL), lambda j, i: (i, 0)),
            pl.BlockSpec((1, D_MODEL), lambda j, i: (0, 0)),
            pl.BlockSpec((D_MODEL, IN_TN), lambda j, i: (0, j)),
        ],
        out_specs=pl.BlockSpec((IN_TM, IN_TN), lambda j, i: (i, j)),
        out_shape=jax.ShapeDtypeStruct((n, N_PAD), jnp.float32),
        compiler_params=pltpu.CompilerParams(
            dimension_semantics=("arbitrary", "arbitrary"), vmem_limit_bytes=VMEM_LIMIT),
        name="in_proj",
    )(xf, g.reshape(1, D_MODEL), w_bf16)


MG_TM = 256


def _merge_kernel(x_ref, g_ref, ya_ref, yb_ref, yc_ref, wa_ref, wb_ref, wc_ref, wo_ref, o_ref):
    def branch(y_ref, w_ref, k):
        p = jnp.dot(y_ref[...].astype(jnp.bfloat16), w_ref[...], preferred_element_type=jnp.float32)
        return jax.nn.sigmoid(g_ref[:, k * D_MODEL:(k + 1) * D_MODEL]) * p

    merged = branch(ya_ref, wa_ref, 0) + branch(yb_ref, wb_ref, 1) + branch(yc_ref, wc_ref, 2)
    o_ref[...] = x_ref[...] + jnp.dot(merged.astype(jnp.bfloat16), wo_ref[...],
                                      preferred_element_type=jnp.float32)


def merge(xf, proj, ya, yb, yc, wa, wb, wc, wo):
    n = xf.shape[0]
    row = lambda w: pl.BlockSpec((MG_TM, w), lambda i: (i, 0))
    full = lambda a: pl.BlockSpec(a.shape, lambda i: (0, 0))
    return pl.pallas_call(
        _merge_kernel,
        grid=(n // MG_TM,),
        in_specs=[row(D_MODEL), row(3 * D_MODEL), row(A_WIDTH), row(B_WIDTH), row(C_WIDTH),
                  full(wa), full(wb), full(wc), full(wo)],
        out_specs=row(D_MODEL),
        out_shape=jax.ShapeDtypeStruct((n, D_MODEL), jnp.float32),
        compiler_params=pltpu.CompilerParams(
            dimension_semantics=("arbitrary",), vmem_limit_bytes=VMEM_LIMIT),
        name="merge",
    )(xf, proj, ya, yb, yc, wa, wb, wc, wo)


def _rms_norm(x, g):
    y = x * lax.rsqrt(jnp.mean(x * x, axis=-1, keepdims=True) + RMS_EPS)
    return y * g


def _l2norm(x):
    return x * lax.rsqrt(jnp.sum(x * x, axis=-1, keepdims=True) + RMS_EPS)


def _masked_softmax(logits, mask):
    logits = jnp.where(mask, logits, NEG_INF)
    return jnp.where(mask, jax.nn.softmax(logits, axis=-1), 0.0)


def _t5_bucket(dist):
    dist = jnp.maximum(dist, 0)
    log_ratio = jnp.log(jnp.maximum(dist, 1).astype(jnp.float32) / BUCKET_MAX_EXACT) / math.log(BUCKET_MAX_DIST / BUCKET_MAX_EXACT)
    large = BUCKET_MAX_EXACT + (log_ratio * (N_BUCKETS - BUCKET_MAX_EXACT)).astype(jnp.int32)
    large = jnp.minimum(large, N_BUCKETS - 1)
    return jnp.where(dist < BUCKET_MAX_EXACT, dist, large)


def _to_qblocks(a):
    b, s = a.shape[:2]
    return jnp.moveaxis(a.reshape(b, s // QBLOCK, QBLOCK, *a.shape[2:]), 1, 0)


def _from_qblocks(a):
    nb, b = a.shape[:2]
    return jnp.moveaxis(a, 0, 1).reshape(b, nb * QBLOCK, *a.shape[3:])


def _dsa_mixer(q, k, v, q_idx, k_idx, w_idx, bias_tab):
    b, s = q.shape[:2]
    topk = min(TOPK_MAX, s // 4)
    bidx = jnp.arange(b)[:, None, None]
    key_pos = jnp.arange(s)
    scale = A_HEAD_DIM ** -0.5

    def block(args):
        qb, qib, wb, start = args
        t = start + jnp.arange(QBLOCK)
        idx_logits = jnp.einsum('bqhd,bsd->bqhs', qib, k_idx) * (IDX_DIM ** -0.5)
        score = jnp.einsum('bqh,bqhs->bqs', wb * (IDX_HEADS ** -0.5), jax.nn.relu(idx_logits))
        score = jnp.where(key_pos[None, None, :] <= t[None, :, None], score, NEG_INF)
        _, sel = lax.top_k(score, topk)
        kg = k[bidx, sel]
        vg = v[bidx, sel]
        dist = t[None, :, None] - sel
        bias = jnp.transpose(bias_tab[_t5_bucket(dist)], (0, 3, 1, 2))
        logits = jnp.einsum('bqhd,bqkhd->bhqk', qb, kg) * scale + bias
        p = _masked_softmax(logits, (dist >= 0)[:, None])
        return jnp.einsum('bhqk,bqkhd->bqhd', p, vg)

    starts = jnp.arange(s // QBLOCK, dtype=jnp.int32) * QBLOCK
    out = lax.map(block, (_to_qblocks(q), _to_qblocks(q_idx), _to_qblocks(w_idx), starts))
    return _from_qblocks(out)


def _causal_depthwise_conv(x, w):
    c = x.shape[-1]
    return lax.conv_general_dilated(x, w[:, None, :], window_strides=(1,), padding=[(CONV_WIDTH - 1, 0)],
                                    dimension_numbers=('NWC', 'WIO', 'NWC'), feature_group_count=c)


def _gdn_mixer(qkv, a_in, b_in, conv_w, a_log, dt_bias):
    b, s = qkv.shape[:2]
    n = s // GDN_CHUNK
    qkv = jax.nn.silu(_causal_depthwise_conv(qkv, conv_w))
    q, k, v = jnp.split(qkv, 3, axis=-1)
    sh = lambda t: t.reshape(b, s, B_HEADS, B_HEAD_DIM)
    q = _l2norm(sh(q)) * (B_HEAD_DIM ** -0.5)
    k = _l2norm(sh(k))
    v = sh(v)
    beta = jax.nn.sigmoid(b_in)
    g = -jnp.exp(a_log) * jax.nn.softplus(a_in + dt_bias)

    def chunk(t):
        return jnp.moveaxis(t.reshape(b, n, GDN_CHUNK, *t.shape[2:]), 3, 2)

    q, k, v, beta, g = (chunk(t) for t in (q, k, v, beta, g))
    g_cum = jnp.cumsum(g, axis=-1)
    g_last = g_cum[..., -1]
    lower = jnp.tril(jnp.ones((GDN_CHUNK, GDN_CHUNK), dtype=bool))
    strict = jnp.tril(jnp.ones((GDN_CHUNK, GDN_CHUNK), dtype=bool), -1)
    diff = g_cum[..., :, None] - g_cum[..., None, :]
    decay = jnp.where(lower, jnp.exp(jnp.where(lower, diff, 0.0)), 0.0)
    k_beta = k * beta[..., None]
    a_mat = jnp.where(strict, jnp.einsum('bnhid,bnhjd->bnhij', k_beta, k) * decay, 0.0)
    eye = jnp.eye(GDN_CHUNK, dtype=jnp.float32)
    t_inv = lax.linalg.triangular_solve(eye + a_mat, jnp.broadcast_to(eye, a_mat.shape), left_side=True, lower=True)
    u = jnp.einsum('bnhij,bnhjd->bnhid', t_inv, v * beta[..., None])
    w = jnp.einsum('bnhij,bnhjd->bnhid', t_inv, k_beta * jnp.exp(g_cum)[..., None])
    attn = jnp.einsum('bnhid,bnhjd->bnhij', q, k) * decay
    q_dec = q * jnp.exp(g_cum)[..., None]
    k_dec = k * jnp.exp(g_last[..., None] - g_cum)[..., None]

    def step(state, xs):
        u_c, w_c, q_c, k_c, attn_c, gl_c = xs
        v_new = u_c - jnp.einsum('bhik,bhkv->bhiv', w_c, state)
        o = jnp.einsum('bhik,bhkv->bhiv', q_c, state) + jnp.einsum('bhij,bhjv->bhiv', attn_c, v_new)
        state = state * jnp.exp(gl_c)[..., None, None] + jnp.einsum('bhik,bhiv->bhkv', k_c, v_new)
        return state, o

    state0 = jnp.zeros((b, B_HEADS, B_HEAD_DIM, B_HEAD_DIM), jnp.float32)
    xs = tuple(jnp.moveaxis(t, 1, 0) for t in (u, w, q_dec, k_dec, attn, g_last))
    _, o = lax.scan(step, state0, xs)
    o = jnp.moveaxis(jnp.moveaxis(o, 0, 1), 2, 3)
    return o.reshape(b, s, B_HEADS, B_HEAD_DIM)


def _compress_blocks(tok, pos, w1, w2):
    b, s, d = tok.shape
    n_cmp = (s - CMP_BLOCK) // CMP_STRIDE + 1
    idx = jnp.arange(n_cmp)[:, None] * CMP_STRIDE + jnp.arange(CMP_BLOCK)[None, :]
    blocks = tok[:, idx] + pos
    return jax.nn.silu(blocks.reshape(b, n_cmp, CMP_BLOCK * d) @ w1) @ w2


def _nsa_mixer(q, kv, gates, k_norm, cmp_pos, phi_w1, phi_w2, bias_tab):
    b, s = q.shape[:2]
    scale = C_HEAD_DIM ** -0.5
    t_all = jnp.arange(s)
    k_cmp, v_cmp, k_sel, v_sel, k_win, v_win = (kv[:, :, i] for i in range(6))
    k_sel = _rms_norm(k_sel, k_norm[1])
    k_win = _rms_norm(k_win, k_norm[2])
    kc = _rms_norm(_compress_blocks(k_cmp, cmp_pos[0], phi_w1[0], phi_w2[0]), k_norm[0])
    vc = _compress_blocks(v_cmp, cmp_pos[1], phi_w1[1], phi_w2[1])
    n_cmp = kc.shape[1]
    cmp_start = jnp.arange(n_cmp) * CMP_STRIDE
    cmp_end = cmp_start + CMP_BLOCK - 1
    cmp_valid = cmp_end[None, :] <= t_all[:, None]
    cmp_bias = jnp.transpose(bias_tab[_t5_bucket(t_all[:, None] - cmp_end[None, :])], (2, 0, 1))
    logits = jnp.einsum('bqhd,bnd->bhqn', q, kc) * scale + cmp_bias
    p_cmp = _masked_softmax(logits, cmp_valid[None, None])
    o_cmp = jnp.einsum('bhqn,bnd->bqhd', p_cmp, vc)
    n_sb = s // SEL_BLOCK
    n_pick = min(N_SEL, n_sb)
    sb_start = jnp.arange(n_sb) * SEL_BLOCK
    overlap = ((cmp_start[:, None] < sb_start[None, :] + SEL_BLOCK) & (cmp_start[:, None] + CMP_BLOCK > sb_start[None, :])).astype(jnp.float32)
    importance = jnp.einsum('bhqn,nj->bqj', p_cmp, overlap)
    cur = t_all // SEL_BLOCK
    blk = jnp.arange(n_sb)
    forced = (blk[None, :] == 0) | (blk[None, :] == cur[:, None]) | (blk[None, :] == jnp.maximum(cur[:, None] - 1, 0))
    admissible = sb_start[None, :] <= t_all[:, None]
    importance = jnp.where(admissible[None], jnp.where(forced[None], FORCE_SCORE, importance), NEG_INF)
    _, sel = lax.top_k(importance, n_pick)
    k_blocks = k_sel.reshape(b, n_sb, SEL_BLOCK, C_HEAD_DIM)
    v_blocks = v_sel.reshape(b, n_sb, SEL_BLOCK, C_HEAD_DIM)
    k_win_pad = jnp.pad(k_win, ((0, 0), (WINDOW, 0), (0, 0)))
    v_win_pad = jnp.pad(v_win, ((0, 0), (WINDOW, 0), (0, 0)))
    bidx = jnp.arange(b)[:, None, None]
    n_keys = n_pick * SEL_BLOCK

    def block(args):
        qb, selb, start = args
        t = start + jnp.arange(QBLOCK)
        kg = k_blocks[bidx, selb].reshape(b, QBLOCK, n_keys, C_HEAD_DIM)
        vg = v_blocks[bidx, selb].reshape(b, QBLOCK, n_keys, C_HEAD_DIM)
        s_pos = (selb[..., None] * SEL_BLOCK + jnp.arange(SEL_BLOCK)).reshape(b, QBLOCK, n_keys)
        dist = t[None, :, None] - s_pos
        bias = jnp.transpose(bias_tab[_t5_bucket(dist)], (0, 3, 1, 2))
        lg = jnp.einsum('bqhd,bqkd->bhqk', qb, kg) * scale + bias
        p = _masked_softmax(lg, (dist >= 0)[:, None])
        o_sel = jnp.einsum('bhqk,bqkd->bqhd', p, vg)
        kw = lax.dynamic_slice_in_dim(k_win_pad, start, QBLOCK + WINDOW, axis=1)
        vw = lax.dynamic_slice_in_dim(v_win_pad, start, QBLOCK + WINDOW, axis=1)
        s_w = start - WINDOW + jnp.arange(QBLOCK + WINDOW)
        dist_w = t[:, None] - s_w[None, :]
        valid_w = (s_w[None, :] >= 0) & (dist_w >= 0) & (dist_w < WINDOW)
        bias_w = jnp.transpose(bias_tab[_t5_bucket(dist_w)], (2, 0, 1))
        lw = jnp.einsum('bqhd,bsd->bhqs', qb, kw) * scale + bias_w
        pw = _masked_softmax(lw, valid_w[None, None])
        o_win = jnp.einsum('bhqs,bsd->bqhd', pw, vw)
        return o_sel, o_win

    starts = jnp.arange(s // QBLOCK, dtype=jnp.int32) * QBLOCK
    o_sel, o_win = lax.map(block, (_to_qblocks(q), _to_qblocks(sel), starts))
    o_sel = _from_qblocks(o_sel)
    o_win = _from_qblocks(o_win)
    g = jax.nn.sigmoid(gates).reshape(b, s, C_HEADS, 3)
    return g[..., 0:1] * o_cmp + g[..., 1:2] * o_sel + g[..., 2:3] * o_win


def _field(proj3, name):
    return proj3[..., OFF[name]:OFF[name] + _ORIG[name][1]]


def kernel(x, norm_g, w_in, a_kv_norm, a_w_ukv, a_q_norm, a_k_norm, b_conv, b_a_log, b_dt_bias, b_out_norm, c_q_norm, c_k_norm, c_cmp_pos, c_phi_w1, c_phi_w2, w_branch, w_out, rel_bias):
    b, s, _ = x.shape
    n = b * s
    bias_a = rel_bias[:, :A_HEADS]
    bias_c = rel_bias[:, A_HEADS:]
    xf = x.reshape(n, D_MODEL)
    perm = jnp.asarray(_PERM_IDX)
    valid = jnp.asarray(_PERM_VALID)
    sh = lambda t, h, d: t.reshape(b, s, h, d)
    for l in range(DEPTH):
        w_p = jnp.where(valid[None, :], w_in[l][:, perm], 0.0).astype(jnp.bfloat16)
        proj = in_proj(xf, norm_g[l], w_p)
        p3 = proj.reshape(b, s, N_PAD)

        a_k, a_v = jnp.split(_rms_norm(_field(p3, "a_ckv"), a_kv_norm[l]) @ a_w_ukv[l], 2, axis=-1)
        a_q = _rms_norm(sh(_field(p3, "a_q"), A_HEADS, A_HEAD_DIM), a_q_norm[l])
        a_k = _rms_norm(sh(a_k, A_HEADS, A_HEAD_DIM), a_k_norm[l])
        o_a = _dsa_mixer(a_q, a_k, sh(a_v, A_HEADS, A_HEAD_DIM), sh(_field(p3, "a_qi"), IDX_HEADS, IDX_DIM),
                         _field(p3, "a_ki"), _field(p3, "a_wi"), bias_a)
        y_a = o_a.reshape(n, A_WIDTH) * jax.nn.silu(_field(p3, "a_z")).reshape(n, A_WIDTH)

        o_b = _gdn_mixer(_field(p3, "b_qkv"), _field(p3, "b_a"), _field(p3, "b_b"), b_conv[l], b_a_log[l], b_dt_bias[l])
        y_b = _rms_norm(o_b, b_out_norm[l]).reshape(n, B_WIDTH) * jax.nn.silu(_field(p3, "b_z")).reshape(n, B_WIDTH)

        c_q = _rms_norm(sh(_field(p3, "c_q"), C_HEADS, C_HEAD_DIM), c_q_norm[l])
        o_c = _nsa_mixer(c_q, sh(_field(p3, "c_kv"), 6, C_HEAD_DIM), _field(p3, "c_g"), c_k_norm[l], c_cmp_pos[l],
                         c_phi_w1[l], c_phi_w2[l], bias_c)
        y_c = o_c.reshape(n, C_WIDTH) * jax.nn.silu(_field(p3, "c_z")).reshape(n, C_WIDTH)

        wbr = w_branch[l].astype(jnp.bfloat16)
        xf = merge(xf, proj, y_a, y_b, y_c, wbr[:A_WIDTH], wbr[A_WIDTH:A_WIDTH + B_WIDTH],
                   wbr[A_WIDTH + B_WIDTH:], w_out[l].astype(jnp.bfloat16))
    return xf.reshape(b, s, D_MODEL)
```

```python
import functools
import math

import jax
import jax.numpy as jnp
import numpy as np
from jax import lax
from jax.experimental import pallas as pl
from jax.experimental.pallas import tpu as pltpu

D_MODEL = 1024
DEPTH = 4
QBLOCK = 128
NEG_INF = -1e30
FORCE_SCORE = 1e9
RMS_EPS = 1e-6

A_HEADS = 4
A_HEAD_DIM = 64
A_WIDTH = A_HEADS * A_HEAD_DIM
A_KV_LATENT = 128
IDX_HEADS = 8
IDX_DIM = 64
TOPK_MAX = 256

B_HEADS = 4
B_HEAD_DIM = 128
B_WIDTH = B_HEADS * B_HEAD_DIM
CONV_WIDTH = 4
GDN_CHUNK = 64

C_HEADS = 4
C_HEAD_DIM = 64
C_WIDTH = C_HEADS * C_HEAD_DIM
CMP_BLOCK = 32
CMP_STRIDE = 16
SEL_BLOCK = 64
N_SEL = 16
WINDOW = 512
PHI_HIDDEN = 256

N_BUCKETS = 32
BUCKET_MAX_EXACT = 16
BUCKET_MAX_DIST = 128

VMEM_LIMIT = 48 * 1024 * 1024

_ORIG = {}
_o = 0
for _name, _w in (("a_q", 256), ("a_ckv", 128), ("a_qi", 512), ("a_ki", 64), ("a_wi", 8), ("a_z", 256),
                  ("b_qkv", 1536), ("b_a", 4), ("b_b", 4), ("b_z", 512),
                  ("c_q", 256), ("c_kv", 384), ("c_g", 12), ("c_z", 256), ("g", 3072)):
    _ORIG[_name] = (_o, _w)
    _o += _w
N_IN = _o

_NEW_ORDER = ("g", "b_qkv", "b_z", "a_qi", "a_q", "a_z", "c_q", "c_z", "c_kv", "a_ckv",
              "a_ki", "a_wi", "b_a", "b_b", "c_g")
OFF = {}
_o = 0
for _name in _NEW_ORDER:
    OFF[_name] = _o
    _o += _ORIG[_name][1]
MISC_OFF = OFF["a_ki"]
N_PAD = 7296
assert MISC_OFF == 7168 and _o <= N_PAD


def _perm_indices():
    idx = np.zeros((N_PAD,), np.int32)
    valid = np.zeros((N_PAD,), bool)
    for name in _NEW_ORDER:
        o_old, w = _ORIG[name]
        idx[OFF[name]:OFF[name] + w] = np.arange(o_old, o_old + w)
        valid[OFF[name]:OFF[name] + w] = True
    return idx, valid


_PERM_IDX, _PERM_VALID = _perm_indices()


IN_TM = 512
IN_TN = N_PAD // 3


def _in_proj_kernel(x_ref, g_ref, w_ref, o_ref):
    x = x_ref[...]
    ms = jnp.mean(x * x, axis=-1, keepdims=True)
    h = (x * lax.rsqrt(ms + RMS_EPS)) * g_ref[...]
    o_ref[...] = jnp.dot(h.astype(jnp.bfloat16), w_ref[...], preferred_element_type=jnp.float32)


def in_proj(xf, g, w_bf16):
    n = xf.shape[0]
    return pl.pallas_call(
        _in_proj_kernel,
        grid=(N_PAD // IN_TN, n // IN_TM),
        in_specs=[
            pl.BlockSpec((IN_TM, D_MODEL), lambda j, i: (i, 0)),
            pl.BlockSpec((1, D_MODEL), lambda j, i: (0, 0)),
            pl.BlockSpec((D_MODEL, IN_TN), lambda j, i: (0, j)),
        ],
        out_specs=pl.BlockSpec((IN_TM, IN_TN), lambda j, i: (i, j)),
        out_shape=jax.ShapeDtypeStruct((n, N_PAD), jnp.float32),
        compiler_params=pltpu.CompilerParams(
            dimension_semantics=("arbitrary", "arbitrary"), vmem_limit_bytes=VMEM_LIMIT),
        name="in_proj",
    )(xf, g.reshape(1, D_MODEL), w_bf16)


MG_TM = 256


def _merge_kernel(x_ref, g_ref, ya_ref, yb_ref, yc_ref, wa_ref, wb_ref, wc_ref, wo_ref, o_ref):
    def branch(y_ref, w_ref, k):
        p = jnp.dot(y_ref[...].astype(jnp.bfloat16), w_ref[...], preferred_element_type=jnp.float32)
        return jax.nn.sigmoid(g_ref[:, k * D_MODEL:(k + 1) * D_MODEL]) * p

    merged = branch(ya_ref, wa_ref, 0) + branch(yb_ref, wb_ref, 1) + branch(yc_ref, wc_ref, 2)
    o_ref[...] = x_ref[...] + jnp.dot(merged.astype(jnp.bfloat16), wo_ref[...],
                                      preferred_element_type=jnp.float32)


def merge(xf, proj, ya, yb, yc, wa, wb, wc, wo):
    n = xf.shape[0]
    row = lambda w: pl.BlockSpec((MG_TM, w), lambda i: (i, 0))
    full = lambda a: pl.BlockSpec(a.shape, lambda i: (0, 0))
    return pl.pallas_call(
        _merge_kernel,
        grid=(n // MG_TM,),
        in_specs=[row(D_MODEL), row(3 * D_MODEL), row(A_WIDTH), row(B_WIDTH), row(C_WIDTH),
                  full(wa), full(wb), full(wc), full(wo)],
        out_specs=row(D_MODEL),
        out_shape=jax.ShapeDtypeStruct((n, D_MODEL), jnp.float32),
        compiler_params=pltpu.CompilerParams(
            dimension_semantics=("arbitrary",), vmem_limit_bytes=VMEM_LIMIT),
        name="merge",
    )(xf, proj, ya, yb, yc, wa, wb, wc, wo)


def _bucket_np(dist):
    d = np.maximum(np.asarray(dist, np.int64), 0)
    ratio = np.log(np.maximum(d, 1).astype(np.float64) / BUCKET_MAX_EXACT) / math.log(BUCKET_MAX_DIST / BUCKET_MAX_EXACT)
    scaled = ratio * (N_BUCKETS - BUCKET_MAX_EXACT)
    frac = scaled - np.floor(scaled)
    edge = (d > BUCKET_MAX_EXACT) & (d < BUCKET_MAX_DIST) & ((frac < 1e-4) | (frac > 1 - 1e-4))
    assert not edge.any(), "bucket boundary too close to an integer distance"
    large = np.minimum(BUCKET_MAX_EXACT + np.floor(scaled + 1e-9).astype(np.int64), N_BUCKETS - 1)
    return np.where(d < BUCKET_MAX_EXACT, d, large).astype(np.int32)


_NEAR_BUCKET_T = _bucket_np(np.arange(QBLOCK)[None, :] + QBLOCK - np.arange(2 * QBLOCK)[:, None])
_FAR_BUCKET = int(_bucket_np(np.array([BUCKET_MAX_DIST]))[0])
assert (_bucket_np(np.arange(BUCKET_MAX_DIST, 4096)) == _FAR_BUCKET).all()

INT_MIN = -2 ** 31


A_PRE_TM = 512


def _a_pre_kernel(ckv_ref, q_ref, gkv_ref, wukv_ref, gq_ref, gk_ref, hm_ref, kn_ref, vt_ref, qn_ref):
    c = ckv_ref[...]
    c = c * lax.rsqrt(jnp.mean(c * c, axis=-1, keepdims=True) + RMS_EPS) * gkv_ref[...]
    kv = jnp.dot(c.astype(jnp.bfloat16), wukv_ref[...], preferred_element_type=jnp.float32)
    hm = hm_ref[...]

    def head_rms(x, g):
        ms = jnp.dot(x * x, hm, precision=lax.Precision.HIGHEST, preferred_element_type=jnp.float32)
        return x * lax.rsqrt(ms + RMS_EPS) * g

    kn_ref[...] = head_rms(kv[:, :A_WIDTH], gk_ref[...])
    qn_ref[...] = head_rms(q_ref[...], gq_ref[...]) * (A_HEAD_DIM ** -0.5)
    vt_ref[0] = kv[:, A_WIDTH:].T


def a_pre(proj, b, s, gkv, wukv_bf16, gq, gk):
    n = b * s
    nt = s // A_PRE_TM
    hm = jnp.asarray(np.kron(np.eye(A_HEADS), np.ones((A_HEAD_DIM, A_HEAD_DIM))) / A_HEAD_DIM, jnp.float32)
    row = lambda bb, j: (bb * nt + j, 0)
    full = lambda a: pl.BlockSpec(a.shape, lambda bb, j: (0,) * a.ndim)
    gq = jnp.tile(gq, A_HEADS).reshape(1, A_WIDTH)
    gk = jnp.tile(gk, A_HEADS).reshape(1, A_WIDTH)
    gkv = gkv.reshape(1, A_KV_LATENT)
    return pl.pallas_call(
        _a_pre_kernel,
        grid=(b, nt),
        in_specs=[
            pl.BlockSpec((A_PRE_TM, A_KV_LATENT), lambda bb, j: (bb * nt + j, OFF["a_ckv"] // A_KV_LATENT)),
            pl.BlockSpec((A_PRE_TM, A_WIDTH), lambda bb, j: (bb * nt + j, OFF["a_q"] // A_WIDTH)),
            full(gkv), full(wukv_bf16), full(gq), full(gk), full(hm),
        ],
        out_specs=[
            pl.BlockSpec((A_PRE_TM, A_WIDTH), row),
            pl.BlockSpec((1, A_WIDTH, A_PRE_TM), lambda bb, j: (bb, 0, j)),
            pl.BlockSpec((A_PRE_TM, A_WIDTH), row),
        ],
        out_shape=[
            jax.ShapeDtypeStruct((n, A_WIDTH), jnp.float32),
            jax.ShapeDtypeStruct((b, A_WIDTH, s), jnp.float32),
            jax.ShapeDtypeStruct((n, A_WIDTH), jnp.float32),
        ],
        compiler_params=pltpu.CompilerParams(
            dimension_semantics=("arbitrary", "arbitrary"), vmem_limit_bytes=VMEM_LIMIT),
        name="a_pre",
    )(proj, proj, gkv, wukv_bf16, gq, gk, hm)


def _a_main_kernel(far_ref, qn_ref, qi_ref, z_ref, misct_ref, kn_ref, vt_ref, ki_ref, near_ref, o_ref,
                   key_scr, lg_scr, j_scr):
    s = kn_ref.shape[0]
    i = pl.program_id(1)
    t0 = pl.multiple_of(i * QBLOCK, QBLOCK)
    f32, bf16 = jnp.float32, jnp.bfloat16
    nt = (((1,), (1,)), ((), ()))

    qi = qi_ref[...]
    qstack = jnp.concatenate([qi[:, h * IDX_DIM:(h + 1) * IDX_DIM] for h in range(IDX_HEADS)], axis=0)
    ki = ki_ref[:, 0:IDX_DIM]
    sc = lax.dot_general(ki.astype(bf16), qstack.astype(bf16), nt, preferred_element_type=f32)
    wt = misct_ref[0, IDX_DIM:IDX_DIM + IDX_HEADS, :] * (IDX_HEADS ** -0.5)
    score = jnp.zeros((s, QBLOCK), f32)
    for h in range(IDX_HEADS):
        score = score + wt[h:h + 1, :] * jnp.maximum(sc[:, h * QBLOCK:(h + 1) * QBLOCK] * (IDX_DIM ** -0.5), 0.0)
    score = score + 0.0
    kpos = lax.broadcasted_iota(jnp.int32, (s, QBLOCK), 0)
    tpos = t0 + lax.broadcasted_iota(jnp.int32, (s, QBLOCK), 1)
    causal = kpos <= tpos
    bits = pltpu.bitcast(score, jnp.int32)
    key = jnp.where(bits < 0, bits ^ jnp.int32(0x7FFFFFFF), bits)
    key_scr[...] = jnp.where(causal, key, jnp.int32(INT_MIN))

    kf = float(TOPK_MAX)

    def count_ge(cand):
        return jnp.sum(jnp.where(key_scr[...] >= cand, 1.0, 0.0), axis=0, keepdims=True)

    def bisect(it, thr):
        cand = thr + lax.shift_left(jnp.int32(1), 31 - it)
        return jnp.where(count_ge(cand) >= kf, cand, thr)

    thr = lax.fori_loop(0, 32, bisect, jnp.full((1, QBLOCK), INT_MIN, jnp.int32))
    n_ge = count_ge(thr)
    need = kf - count_ge(thr + 1)

    j_scr[...] = jnp.full((1, QBLOCK), s - 1, jnp.int32)
    surplus = jnp.where((n_ge > kf) & (thr > INT_MIN), 1.0, 0.0)

    @pl.when(jnp.max(surplus) > 0.0)
    def _():
        def bisect_idx(it, lohi):
            lo, hi = lohi
            mid = lax.shift_right_arithmetic(lo + hi, 1)
            k = key_scr[...]
            kp = lax.broadcasted_iota(jnp.int32, (s, QBLOCK), 0)
            c = jnp.sum(jnp.where((k == thr) & (kp <= mid), 1.0, 0.0), axis=0, keepdims=True)
            ok = c >= need
            return jnp.where(ok, lo, mid), jnp.where(ok, mid, hi)

        lo0 = jnp.full((1, QBLOCK), -1, jnp.int32)
        hi0 = jnp.full((1, QBLOCK), s - 1, jnp.int32)
        _, hi = lax.fori_loop(0, 11, bisect_idx, (lo0, hi0))
        j_scr[...] = hi

    key = key_scr[...]
    sel = ((key > thr) | ((key == thr) & (kpos <= j_scr[...]))) & causal

    qn = qn_ref[...]
    lane = lax.broadcasted_iota(jnp.int32, (QBLOCK, A_WIDTH), 1)
    qblk = jnp.concatenate(
        [jnp.where((lane >= h * A_HEAD_DIM) & (lane < (h + 1) * A_HEAD_DIM), qn, 0.0) for h in range(A_HEADS)],
        axis=0)
    lg_scr[0:QBLOCK, :] = jnp.zeros((QBLOCK, A_HEADS * QBLOCK), f32)
    lg_scr[QBLOCK:, :] = lax.dot_general(kn_ref[...].astype(bf16), qblk.astype(bf16), nt,
                                         preferred_element_type=f32)
    outs = []
    for h in range(A_HEADS):
        cols = slice(h * QBLOCK, (h + 1) * QBLOCK)
        far = far_ref[h]
        lg_scr[pl.ds(t0, 2 * QBLOCK), cols] += near_ref[h] - far
        l = jnp.where(sel, lg_scr[QBLOCK:, cols] + far, NEG_INF)
        m = jnp.max(l, axis=0, keepdims=True)
        p = jnp.where(sel, jnp.exp(l - m), 0.0)
        den = jnp.sum(p, axis=0, keepdims=True)
        vt = vt_ref[0, h * A_HEAD_DIM:(h + 1) * A_HEAD_DIM, :]
        o_t = jnp.dot(vt.astype(bf16), p.astype(bf16), preferred_element_type=f32)
        outs.append(o_t / den)
    o = jnp.concatenate(outs, axis=0).T
    o_ref[...] = o * jax.nn.silu(z_ref[...])


def a_main(proj, misct, qn, kn, vt, near_t, far, b, s):
    n = b * s
    nq = s // QBLOCK
    row = lambda bb, i: bb * nq + i
    return pl.pallas_call(
        _a_main_kernel,
        grid=(b, nq),
        in_specs=[
            pl.BlockSpec(memory_space=pltpu.SMEM),
            pl.BlockSpec((QBLOCK, A_WIDTH), lambda bb, i: (row(bb, i), 0)),
            pl.BlockSpec((QBLOCK, IDX_HEADS * IDX_DIM), lambda bb, i: (row(bb, i), OFF["a_qi"] // (IDX_HEADS * IDX_DIM))),
            pl.BlockSpec((QBLOCK, A_WIDTH), lambda bb, i: (row(bb, i), OFF["a_z"] // A_WIDTH)),
            pl.BlockSpec((1, 128, QBLOCK), lambda bb, i: (bb, 0, i)),
            pl.BlockSpec((s, A_WIDTH), lambda bb, i: (bb, 0)),
            pl.BlockSpec((1, A_WIDTH, s), lambda bb, i: (bb, 0, 0)),
            pl.BlockSpec((s, 128), lambda bb, i: (bb, MISC_OFF // 128)),
            pl.BlockSpec((A_HEADS, 2 * QBLOCK, QBLOCK), lambda bb, i: (0, 0, 0)),
        ],
        out_specs=pl.BlockSpec((QBLOCK, A_WIDTH), lambda bb, i: (row(bb, i), 0)),
        out_shape=jax.ShapeDtypeStruct((n, A_WIDTH), jnp.float32),
        scratch_shapes=[
            pltpu.VMEM((s, QBLOCK), jnp.int32),
            pltpu.VMEM((s + QBLOCK, A_HEADS * QBLOCK), jnp.float32),
            pltpu.VMEM((1, QBLOCK), jnp.int32),
        ],
        compiler_params=pltpu.CompilerParams(
            dimension_semantics=("arbitrary", "arbitrary"), vmem_limit_bytes=VMEM_LIMIT),
        name="a_main",
    )(far, qn, proj, proj, misct, kn, vt, proj, near_t)


def mixer_a(proj, misct, b, s, gkv, wukv, gq, gk, bias_a):
    kn, vt, qn = a_pre(proj, b, s, gkv, wukv.astype(jnp.bfloat16), gq, gk)
    near_t = jnp.transpose(bias_a[jnp.asarray(_NEAR_BUCKET_T)], (2, 0, 1))
    far = bias_a[_FAR_BUCKET]
    return a_main(proj, misct, qn, kn, vt, near_t, far, b, s)


def _rms_norm(x, g):
    y = x * lax.rsqrt(jnp.mean(x * x, axis=-1, keepdims=True) + RMS_EPS)
    return y * g


def _l2norm(x):
    return x * lax.rsqrt(jnp.sum(x * x, axis=-1, keepdims=True) + RMS_EPS)


def _masked_softmax(logits, mask):
    logits = jnp.where(mask, logits, NEG_INF)
    return jnp.where(mask, jax.nn.softmax(logits, axis=-1), 0.0)


def _t5_bucket(dist):
    dist = jnp.maximum(dist, 0)
    log_ratio = jnp.log(jnp.maximum(dist, 1).astype(jnp.float32) / BUCKET_MAX_EXACT) / math.log(BUCKET_MAX_DIST / BUCKET_MAX_EXACT)
    large = BUCKET_MAX_EXACT + (log_ratio * (N_BUCKETS - BUCKET_MAX_EXACT)).astype(jnp.int32)
    large = jnp.minimum(large, N_BUCKETS - 1)
    return jnp.where(dist < BUCKET_MAX_EXACT, dist, large)


def _to_qblocks(a):
    b, s = a.shape[:2]
    return jnp.moveaxis(a.reshape(b, s // QBLOCK, QBLOCK, *a.shape[2:]), 1, 0)


def _from_qblocks(a):
    nb, b = a.shape[:2]
    return jnp.moveaxis(a, 0, 1).reshape(b, nb * QBLOCK, *a.shape[3:])


def _dsa_mixer(q, k, v, q_idx, k_idx, w_idx, bias_tab):
    b, s = q.shape[:2]
    topk = min(TOPK_MAX, s // 4)
    bidx = jnp.arange(b)[:, None, None]
    key_pos = jnp.arange(s)
    scale = A_HEAD_DIM ** -0.5

    def block(args):
        qb, qib, wb, start = args
        t = start + jnp.arange(QBLOCK)
        idx_logits = jnp.einsum('bqhd,bsd->bqhs', qib, k_idx) * (IDX_DIM ** -0.5)
        score = jnp.einsum('bqh,bqhs->bqs', wb * (IDX_HEADS ** -0.5), jax.nn.relu(idx_logits))
        score = jnp.where(key_pos[None, None, :] <= t[None, :, None], score, NEG_INF)
        _, sel = lax.top_k(score, topk)
        kg = k[bidx, sel]
        vg = v[bidx, sel]
        dist = t[None, :, None] - sel
        bias = jnp.transpose(bias_tab[_t5_bucket(dist)], (0, 3, 1, 2))
        logits = jnp.einsum('bqhd,bqkhd->bhqk', qb, kg) * scale + bias
        p = _masked_softmax(logits, (dist >= 0)[:, None])
        return jnp.einsum('bhqk,bqkhd->bqhd', p, vg)

    starts = jnp.arange(s // QBLOCK, dtype=jnp.int32) * QBLOCK
    out = lax.map(block, (_to_qblocks(q), _to_qblocks(q_idx), _to_qblocks(w_idx), starts))
    return _from_qblocks(out)


def _causal_depthwise_conv(x, w):
    c = x.shape[-1]
    return lax.conv_general_dilated(x, w[:, None, :], window_strides=(1,), padding=[(CONV_WIDTH - 1, 0)],
                                    dimension_numbers=('NWC', 'WIO', 'NWC'), feature_group_count=c)


def _gdn_mixer(qkv, a_in, b_in, conv_w, a_log, dt_bias):
    b, s = qkv.shape[:2]
    n = s // GDN_CHUNK
    qkv = jax.nn.silu(_causal_depthwise_conv(qkv, conv_w))
    q, k, v = jnp.split(qkv, 3, axis=-1)
    sh = lambda t: t.reshape(b, s, B_HEADS, B_HEAD_DIM)
    q = _l2norm(sh(q)) * (B_HEAD_DIM ** -0.5)
    k = _l2norm(sh(k))
    v = sh(v)
    beta = jax.nn.sigmoid(b_in)
    g = -jnp.exp(a_log) * jax.nn.softplus(a_in + dt_bias)

    def chunk(t):
        return jnp.moveaxis(t.reshape(b, n, GDN_CHUNK, *t.shape[2:]), 3, 2)

    q, k, v, beta, g = (chunk(t) for t in (q, k, v, beta, g))
    g_cum = jnp.cumsum(g, axis=-1)
    g_last = g_cum[..., -1]
    lower = jnp.tril(jnp.ones((GDN_CHUNK, GDN_CHUNK), dtype=bool))
    strict = jnp.tril(jnp.ones((GDN_CHUNK, GDN_CHUNK), dtype=bool), -1)
    diff = g_cum[..., :, None] - g_cum[..., None, :]
    decay = jnp.where(lower, jnp.exp(jnp.where(lower, diff, 0.0)), 0.0)
    k_beta = k * beta[..., None]
    a_mat = jnp.where(strict, jnp.einsum('bnhid,bnhjd->bnhij', k_beta, k) * decay, 0.0)
    eye = jnp.eye(GDN_CHUNK, dtype=jnp.float32)
    t_inv = lax.linalg.triangular_solve(eye + a_mat, jnp.broadcast_to(eye, a_mat.shape), left_side=True, lower=True)
    u = jnp.einsum('bnhij,bnhjd->bnhid', t_inv, v * beta[..., None])
    w = jnp.einsum('bnhij,bnhjd->bnhid', t_inv, k_beta * jnp.exp(g_cum)[..., None])
    attn = jnp.einsum('bnhid,bnhjd->bnhij', q, k) * decay
    q_dec = q * jnp.exp(g_cum)[..., None]
    k_dec = k * jnp.exp(g_last[..., None] - g_cum)[..., None]

    def step(state, xs):
        u_c, w_c, q_c, k_c, attn_c, gl_c = xs
        v_new = u_c - jnp.einsum('bhik,bhkv->bhiv', w_c, state)
        o = jnp.einsum('bhik,bhkv->bhiv', q_c, state) + jnp.einsum('bhij,bhjv->bhiv', attn_c, v_new)
        state = state * jnp.exp(gl_c)[..., None, None] + jnp.einsum('bhik,bhiv->bhkv', k_c, v_new)
        return state, o

    state0 = jnp.zeros((b, B_HEADS, B_HEAD_DIM, B_HEAD_DIM), jnp.float32)
    xs = tuple(jnp.moveaxis(t, 1, 0) for t in (u, w, q_dec, k_dec, attn, g_last))
    _, o = lax.scan(step, state0, xs)
    o = jnp.moveaxis(jnp.moveaxis(o, 0, 1), 2, 3)
    return o.reshape(b, s, B_HEADS, B_HEAD_DIM)


def _compress_blocks(tok, pos, w1, w2):
    b, s, d = tok.shape
    n_cmp = (s - CMP_BLOCK) // CMP_STRIDE + 1
    idx = jnp.arange(n_cmp)[:, None] * CMP_STRIDE + jnp.arange(CMP_BLOCK)[None, :]
    blocks = tok[:, idx] + pos
    return jax.nn.silu(blocks.reshape(b, n_cmp, CMP_BLOCK * d) @ w1) @ w2


def _nsa_mixer(q, kv, gates, k_norm, cmp_pos, phi_w1, phi_w2, bias_tab):
    b, s = q.shape[:2]
    scale = C_HEAD_DIM ** -0.5
    t_all = jnp.arange(s)
    k_cmp, v_cmp, k_sel, v_sel, k_win, v_win = (kv[:, :, i] for i in range(6))
    k_sel = _rms_norm(k_sel, k_norm[1])
    k_win = _rms_norm(k_win, k_norm[2])
    kc = _rms_norm(_compress_blocks(k_cmp, cmp_pos[0], phi_w1[0], phi_w2[0]), k_norm[0])
    vc = _compress_blocks(v_cmp, cmp_pos[1], phi_w1[1], phi_w2[1])
    n_cmp = kc.shape[1]
    cmp_start = jnp.arange(n_cmp) * CMP_STRIDE
    cmp_end = cmp_start + CMP_BLOCK - 1
    cmp_valid = cmp_end[None, :] <= t_all[:, None]
    cmp_bias = jnp.transpose(bias_tab[_t5_bucket(t_all[:, None] - cmp_end[None, :])], (2, 0, 1))
    logits = jnp.einsum('bqhd,bnd->bhqn', q, kc) * scale + cmp_bias
    p_cmp = _masked_softmax(logits, cmp_valid[None, None])
    o_cmp = jnp.einsum('bhqn,bnd->bqhd', p_cmp, vc)
    n_sb = s // SEL_BLOCK
    n_pick = min(N_SEL, n_sb)
    sb_start = jnp.arange(n_sb) * SEL_BLOCK
    overlap = ((cmp_start[:, None] < sb_start[None, :] + SEL_BLOCK) & (cmp_start[:, None] + CMP_BLOCK > sb_start[None, :])).astype(jnp.float32)
    importance = jnp.einsum('bhqn,nj->bqj', p_cmp, overlap)
    cur = t_all // SEL_BLOCK
    blk = jnp.arange(n_sb)
    forced = (blk[None, :] == 0) | (blk[None, :] == cur[:, None]) | (blk[None, :] == jnp.maximum(cur[:, None] - 1, 0))
    admissible = sb_start[None, :] <= t_all[:, None]
    importance = jnp.where(admissible[None], jnp.where(forced[None], FORCE_SCORE, importance), NEG_INF)
    _, sel = lax.top_k(importance, n_pick)
    k_blocks = k_sel.reshape(b, n_sb, SEL_BLOCK, C_HEAD_DIM)
    v_blocks = v_sel.reshape(b, n_sb, SEL_BLOCK, C_HEAD_DIM)
    k_win_pad = jnp.pad(k_win, ((0, 0), (WINDOW, 0), (0, 0)))
    v_win_pad = jnp.pad(v_win, ((0, 0), (WINDOW, 0), (0, 0)))
    bidx = jnp.arange(b)[:, None, None]
    n_keys = n_pick * SEL_BLOCK

    def block(args):
        qb, selb, start = args
        t = start + jnp.arange(QBLOCK)
        kg = k_blocks[bidx, selb].reshape(b, QBLOCK, n_keys, C_HEAD_DIM)
        vg = v_blocks[bidx, selb].reshape(b, QBLOCK, n_keys, C_HEAD_DIM)
        s_pos = (selb[..., None] * SEL_BLOCK + jnp.arange(SEL_BLOCK)).reshape(b, QBLOCK, n_keys)
        dist = t[None, :, None] - s_pos
        bias = jnp.transpose(bias_tab[_t5_bucket(dist)], (0, 3, 1, 2))
        lg = jnp.einsum('bqhd,bqkd->bhqk', qb, kg) * scale + bias
        p = _masked_softmax(lg, (dist >= 0)[:, None])
        o_sel = jnp.einsum('bhqk,bqkd->bqhd', p, vg)
        kw = lax.dynamic_slice_in_dim(k_win_pad, start, QBLOCK + WINDOW, axis=1)
        vw = lax.dynamic_slice_in_dim(v_win_pad, start, QBLOCK + WINDOW, axis=1)
        s_w = start - WINDOW + jnp.arange(QBLOCK + WINDOW)
        dist_w = t[:, None] - s_w[None, :]
        valid_w = (s_w[None, :] >= 0) & (dist_w >= 0) & (dist_w < WINDOW)
        bias_w = jnp.transpose(bias_tab[_t5_bucket(dist_w)], (2, 0, 1))
        lw = jnp.einsum('bqhd,bsd->bhqs', qb, kw) * scale + bias_w
        pw = _masked_softmax(lw, valid_w[None, None])
        o_win = jnp.einsum('bhqs,bsd->bqhd', pw, vw)
        return o_sel, o_win

    starts = jnp.arange(s // QBLOCK, dtype=jnp.int32) * QBLOCK
    o_sel, o_win = lax.map(block, (_to_qblocks(q), _to_qblocks(sel), starts))
    o_sel = _from_qblocks(o_sel)
    o_win = _from_qblocks(o_win)
    g = jax.nn.sigmoid(gates).reshape(b, s, C_HEADS, 3)
    return g[..., 0:1] * o_cmp + g[..., 1:2] * o_sel + g[..., 2:3] * o_win


def _field(proj3, name):
    return proj3[..., OFF[name]:OFF[name] + _ORIG[name][1]]


def kernel(x, norm_g, w_in, a_kv_norm, a_w_ukv, a_q_norm, a_k_norm, b_conv, b_a_log, b_dt_bias, b_out_norm, c_q_norm, c_k_norm, c_cmp_pos, c_phi_w1, c_phi_w2, w_branch, w_out, rel_bias):
    b, s, _ = x.shape
    n = b * s
    bias_a = rel_bias[:, :A_HEADS]
    bias_c = rel_bias[:, A_HEADS:]
    xf = x.reshape(n, D_MODEL)
    perm = jnp.asarray(_PERM_IDX)
    valid = jnp.asarray(_PERM_VALID)
    sh = lambda t, h, d: t.reshape(b, s, h, d)
    for l in range(DEPTH):
        w_p = jnp.where(valid[None, :], w_in[l][:, perm], 0.0).astype(jnp.bfloat16)
        proj = in_proj(xf, norm_g[l], w_p)
        p3 = proj.reshape(b, s, N_PAD)

        misct = jnp.transpose(p3[..., MISC_OFF:], (0, 2, 1))
        y_a = mixer_a(proj, misct, b, s, a_kv_norm[l], a_w_ukv[l], a_q_norm[l], a_k_norm[l], bias_a)

        o_b = _gdn_mixer(_field(p3, "b_qkv"), _field(p3, "b_a"), _field(p3, "b_b"), b_conv[l], b_a_log[l], b_dt_bias[l])
        y_b = _rms_norm(o_b, b_out_norm[l]).reshape(n, B_WIDTH) * jax.nn.silu(_field(p3, "b_z")).reshape(n, B_WIDTH)

        c_q = _rms_norm(sh(_field(p3, "c_q"), C_HEADS, C_HEAD_DIM), c_q_norm[l])
        o_c = _nsa_mixer(c_q, sh(_field(p3, "c_kv"), 6, C_HEAD_DIM), _field(p3, "c_g"), c_k_norm[l], c_cmp_pos[l],
                         c_phi_w1[l], c_phi_w2[l], bias_c)
        y_c = o_c.reshape(n, C_WIDTH) * jax.nn.silu(_field(p3, "c_z")).reshape(n, C_WIDTH)

        wbr = w_branch[l].astype(jnp.bfloat16)
        xf = merge(xf, proj, y_a, y_b, y_c, wbr[:A_WIDTH], wbr[A_WIDTH:A_WIDTH + B_WIDTH],
                   wbr[A_WIDTH + B_WIDTH:], w_out[l].astype(jnp.bfloat16))
    return xf.reshape(b, s, D_MODEL)
```

```python
import functools
import math

import jax
import jax.numpy as jnp
import numpy as np
from jax import lax
from jax.experimental import pallas as pl
from jax.experimental.pallas import tpu as pltpu

D_MODEL = 1024
DEPTH = 4
QBLOCK = 128
NEG_INF = -1e30
FORCE_SCORE = 1e9
RMS_EPS = 1e-6

A_HEADS = 4
A_HEAD_DIM = 64
A_WIDTH = A_HEADS * A_HEAD_DIM
A_KV_LATENT = 128
IDX_HEADS = 8
IDX_DIM = 64
TOPK_MAX = 256

B_HEADS = 4
B_HEAD_DIM = 128
B_WIDTH = B_HEADS * B_HEAD_DIM
CONV_WIDTH = 4
GDN_CHUNK = 64

C_HEADS = 4
C_HEAD_DIM = 64
C_WIDTH = C_HEADS * C_HEAD_DIM
CMP_BLOCK = 32
CMP_STRIDE = 16
SEL_BLOCK = 64
N_SEL = 16
WINDOW = 512
PHI_HIDDEN = 256

N_BUCKETS = 32
BUCKET_MAX_EXACT = 16
BUCKET_MAX_DIST = 128

VMEM_LIMIT = 48 * 1024 * 1024

_ORIG = {}
_o = 0
for _name, _w in (("a_q", 256), ("a_ckv", 128), ("a_qi", 512), ("a_ki", 64), ("a_wi", 8), ("a_z", 256),
                  ("b_qkv", 1536), ("b_a", 4), ("b_b", 4), ("b_z", 512),
                  ("c_q", 256), ("c_kv", 384), ("c_g", 12), ("c_z", 256), ("g", 3072)):
    _ORIG[_name] = (_o, _w)
    _o += _w
N_IN = _o

_NEW_ORDER = ("g", "b_qkv", "b_z", "a_qi", "a_q", "a_z", "c_q", "c_z", "c_kv", "a_ckv",
              "a_ki", "a_wi", "b_a", "b_b", "c_g")
OFF = {}
_o = 0
for _name in _NEW_ORDER:
    OFF[_name] = _o
    _o += _ORIG[_name][1]
MISC_OFF = OFF["a_ki"]
N_PAD = 7296
assert MISC_OFF == 7168 and _o <= N_PAD


def _perm_indices():
    idx = np.zeros((N_PAD,), np.int32)
    valid = np.zeros((N_PAD,), bool)
    for name in _NEW_ORDER:
        o_old, w = _ORIG[name]
        idx[OFF[name]:OFF[name] + w] = np.arange(o_old, o_old + w)
        valid[OFF[name]:OFF[name] + w] = True
    return idx, valid


_PERM_IDX, _PERM_VALID = _perm_indices()


IN_TM = 512
IN_TN = N_PAD // 3


def _in_proj_kernel(x_ref, g_ref, w_ref, o_ref):
    x = x_ref[...]
    ms = jnp.mean(x * x, axis=-1, keepdims=True)
    h = (x * lax.rsqrt(ms + RMS_EPS)) * g_ref[...]
    o_ref[...] = jnp.dot(h.astype(jnp.bfloat16), w_ref[...], preferred_element_type=jnp.float32)


def in_proj(xf, g, w_bf16):
    n = xf.shape[0]
    return pl.pallas_call(
        _in_proj_kernel,
        grid=(N_PAD // IN_TN, n // IN_TM),
        in_specs=[
            pl.BlockSpec((IN_TM, D_MODEL), lambda j, i: (i, 0)),
            pl.BlockSpec((1, D_MODEL), lambda j, i: (0, 0)),
            pl.BlockSpec((D_MODEL, IN_TN), lambda j, i: (0, j)),
        ],
        out_specs=pl.BlockSpec((IN_TM, IN_TN), lambda j, i: (i, j)),
        out_shape=jax.ShapeDtypeStruct((n, N_PAD), jnp.float32),
        compiler_params=pltpu.CompilerParams(
            dimension_semantics=("arbitrary", "arbitrary"), vmem_limit_bytes=VMEM_LIMIT),
        name="in_proj",
    )(xf, g.reshape(1, D_MODEL), w_bf16)


MG_TM = 256


def _merge_kernel(x_ref, g_ref, ya_ref, yb_ref, yc_ref, wa_ref, wb_ref, wc_ref, wo_ref, o_ref):
    def branch(y_ref, w_ref, k):
        p = jnp.dot(y_ref[...].astype(jnp.bfloat16), w_ref[...], preferred_element_type=jnp.float32)
        return jax.nn.sigmoid(g_ref[:, k * D_MODEL:(k + 1) * D_MODEL]) * p

    merged = branch(ya_ref, wa_ref, 0) + branch(yb_ref, wb_ref, 1) + branch(yc_ref, wc_ref, 2)
    o_ref[...] = x_ref[...] + jnp.dot(merged.astype(jnp.bfloat16), wo_ref[...],
                                      preferred_element_type=jnp.float32)


def merge(xf, proj, ya, yb, yc, wa, wb, wc, wo):
    n = xf.shape[0]
    row = lambda w: pl.BlockSpec((MG_TM, w), lambda i: (i, 0))
    full = lambda a: pl.BlockSpec(a.shape, lambda i: (0, 0))
    return pl.pallas_call(
        _merge_kernel,
        grid=(n // MG_TM,),
        in_specs=[row(D_MODEL), row(3 * D_MODEL), row(A_WIDTH), row(B_WIDTH), row(C_WIDTH),
                  full(wa), full(wb), full(wc), full(wo)],
        out_specs=row(D_MODEL),
        out_shape=jax.ShapeDtypeStruct((n, D_MODEL), jnp.float32),
        compiler_params=pltpu.CompilerParams(
            dimension_semantics=("arbitrary",), vmem_limit_bytes=VMEM_LIMIT),
        name="merge",
    )(xf, proj, ya, yb, yc, wa, wb, wc, wo)


def _bucket_np(dist):
    d = np.maximum(np.asarray(dist, np.int64), 0)
    ratio = np.log(np.maximum(d, 1).astype(np.float64) / BUCKET_MAX_EXACT) / math.log(BUCKET_MAX_DIST / BUCKET_MAX_EXACT)
    scaled = ratio * (N_BUCKETS - BUCKET_MAX_EXACT)
    frac = scaled - np.floor(scaled)
    edge = (d > BUCKET_MAX_EXACT) & (d < BUCKET_MAX_DIST) & ((frac < 1e-4) | (frac > 1 - 1e-4))
    assert not edge.any(), "bucket boundary too close to an integer distance"
    large = np.minimum(BUCKET_MAX_EXACT + np.floor(scaled + 1e-9).astype(np.int64), N_BUCKETS - 1)
    return np.where(d < BUCKET_MAX_EXACT, d, large).astype(np.int32)


_NEAR_BUCKET_T = _bucket_np(np.arange(QBLOCK)[None, :] + QBLOCK - np.arange(2 * QBLOCK)[:, None])
_FAR_BUCKET = int(_bucket_np(np.array([BUCKET_MAX_DIST]))[0])
assert (_bucket_np(np.arange(BUCKET_MAX_DIST, 4096)) == _FAR_BUCKET).all()

INT_MIN = -2 ** 31


A_PRE_TM = 512


def _a_pre_kernel(ckv_ref, q_ref, gkv_ref, wukv_ref, gq_ref, gk_ref, hm_ref, kn_ref, vt_ref, qn_ref):
    c = ckv_ref[...]
    c = c * lax.rsqrt(jnp.mean(c * c, axis=-1, keepdims=True) + RMS_EPS) * gkv_ref[...]
    kv = jnp.dot(c.astype(jnp.bfloat16), wukv_ref[...], preferred_element_type=jnp.float32)
    hm = hm_ref[...]

    def head_rms(x, g):
        ms = jnp.dot(x * x, hm, precision=lax.Precision.HIGHEST, preferred_element_type=jnp.float32)
        return x * lax.rsqrt(ms + RMS_EPS) * g

    kn_ref[...] = head_rms(kv[:, :A_WIDTH], gk_ref[...])
    qn_ref[...] = head_rms(q_ref[...], gq_ref[...]) * (A_HEAD_DIM ** -0.5)
    vt_ref[0] = kv[:, A_WIDTH:].T


def a_pre(proj, b, s, gkv, wukv_bf16, gq, gk):
    n = b * s
    nt = s // A_PRE_TM
    hm = jnp.asarray(np.kron(np.eye(A_HEADS), np.ones((A_HEAD_DIM, A_HEAD_DIM))) / A_HEAD_DIM, jnp.float32)
    row = lambda bb, j: (bb * nt + j, 0)
    full = lambda a: pl.BlockSpec(a.shape, lambda bb, j: (0,) * a.ndim)
    gq = jnp.tile(gq, A_HEADS).reshape(1, A_WIDTH)
    gk = jnp.tile(gk, A_HEADS).reshape(1, A_WIDTH)
    gkv = gkv.reshape(1, A_KV_LATENT)
    return pl.pallas_call(
        _a_pre_kernel,
        grid=(b, nt),
        in_specs=[
            pl.BlockSpec((A_PRE_TM, A_KV_LATENT), lambda bb, j: (bb * nt + j, OFF["a_ckv"] // A_KV_LATENT)),
            pl.BlockSpec((A_PRE_TM, A_WIDTH), lambda bb, j: (bb * nt + j, OFF["a_q"] // A_WIDTH)),
            full(gkv), full(wukv_bf16), full(gq), full(gk), full(hm),
        ],
        out_specs=[
            pl.BlockSpec((A_PRE_TM, A_WIDTH), row),
            pl.BlockSpec((1, A_WIDTH, A_PRE_TM), lambda bb, j: (bb, 0, j)),
            pl.BlockSpec((A_PRE_TM, A_WIDTH), row),
        ],
        out_shape=[
            jax.ShapeDtypeStruct((n, A_WIDTH), jnp.float32),
            jax.ShapeDtypeStruct((b, A_WIDTH, s), jnp.float32),
            jax.ShapeDtypeStruct((n, A_WIDTH), jnp.float32),
        ],
        compiler_params=pltpu.CompilerParams(
            dimension_semantics=("arbitrary", "arbitrary"), vmem_limit_bytes=VMEM_LIMIT),
        name="a_pre",
    )(proj, proj, gkv, wukv_bf16, gq, gk, hm)


def _a_main_kernel(far_ref, qn_ref, qi_ref, z_ref, misct_ref, kn_ref, vt_ref, ki_ref, near_ref, o_ref,
                   key_scr, lg_scr, j_scr):
    s = kn_ref.shape[0]
    i = pl.program_id(1)
    t0 = pl.multiple_of(i * QBLOCK, QBLOCK)
    f32, bf16 = jnp.float32, jnp.bfloat16
    nt = (((1,), (1,)), ((), ()))

    qi = qi_ref[...]
    qstack = jnp.concatenate([qi[:, h * IDX_DIM:(h + 1) * IDX_DIM] for h in range(IDX_HEADS)], axis=0)
    ki = ki_ref[:, 0:IDX_DIM]
    sc = lax.dot_general(ki.astype(bf16), qstack.astype(bf16), nt, preferred_element_type=f32)
    wt = misct_ref[0, IDX_DIM:IDX_DIM + IDX_HEADS, :] * (IDX_HEADS ** -0.5)
    score = jnp.zeros((s, QBLOCK), f32)
    for h in range(IDX_HEADS):
        score = score + wt[h:h + 1, :] * jnp.maximum(sc[:, h * QBLOCK:(h + 1) * QBLOCK] * (IDX_DIM ** -0.5), 0.0)
    score = score + 0.0
    kpos = lax.broadcasted_iota(jnp.int32, (s, QBLOCK), 0)
    tpos = t0 + lax.broadcasted_iota(jnp.int32, (s, QBLOCK), 1)
    causal = kpos <= tpos
    bits = pltpu.bitcast(score, jnp.int32)
    key = jnp.where(bits < 0, bits ^ jnp.int32(0x7FFFFFFF), bits)
    key_scr[...] = jnp.where(causal, key, jnp.int32(INT_MIN))

    kf = float(TOPK_MAX)

    def count_ge(cand):
        return jnp.sum(jnp.where(key_scr[...] >= cand, 1.0, 0.0), axis=0, keepdims=True)

    def bisect(it, thr):
        cand = thr + lax.shift_left(jnp.int32(1), 31 - it)
        return jnp.where(count_ge(cand) >= kf, cand, thr)

    thr = lax.fori_loop(0, 32, bisect, jnp.full((1, QBLOCK), INT_MIN, jnp.int32))
    n_ge = count_ge(thr)
    need = kf - count_ge(thr + 1)

    j_scr[...] = jnp.full((1, QBLOCK), s - 1, jnp.int32)
    surplus = jnp.where((n_ge > kf) & (thr > INT_MIN), 1.0, 0.0)

    @pl.when(jnp.max(surplus) > 0.0)
    def _():
        def bisect_idx(it, lohi):
            lo, hi = lohi
            mid = lax.shift_right_arithmetic(lo + hi, 1)
            k = key_scr[...]
            kp = lax.broadcasted_iota(jnp.int32, (s, QBLOCK), 0)
            c = jnp.sum(jnp.where((k == thr) & (kp <= mid), 1.0, 0.0), axis=0, keepdims=True)
            ok = c >= need
            return jnp.where(ok, lo, mid), jnp.where(ok, mid, hi)

        lo0 = jnp.full((1, QBLOCK), -1, jnp.int32)
        hi0 = jnp.full((1, QBLOCK), s - 1, jnp.int32)
        _, hi = lax.fori_loop(0, 11, bisect_idx, (lo0, hi0))
        j_scr[...] = hi

    key = key_scr[...]
    sel = ((key > thr) | ((key == thr) & (kpos <= j_scr[...]))) & causal

    qn = qn_ref[...]
    lane = lax.broadcasted_iota(jnp.int32, (QBLOCK, A_WIDTH), 1)
    qblk = jnp.concatenate(
        [jnp.where((lane >= h * A_HEAD_DIM) & (lane < (h + 1) * A_HEAD_DIM), qn, 0.0) for h in range(A_HEADS)],
        axis=0)
    lg_scr[0:QBLOCK, :] = jnp.zeros((QBLOCK, A_HEADS * QBLOCK), f32)
    lg_scr[QBLOCK:, :] = lax.dot_general(kn_ref[...].astype(bf16), qblk.astype(bf16), nt,
                                         preferred_element_type=f32)
    outs = []
    for h in range(A_HEADS):
        cols = slice(h * QBLOCK, (h + 1) * QBLOCK)
        far = far_ref[h]
        lg_scr[pl.ds(t0, 2 * QBLOCK), cols] += near_ref[h] - far
        l = jnp.where(sel, lg_scr[QBLOCK:, cols] + far, NEG_INF)
        m = jnp.max(l, axis=0, keepdims=True)
        p = jnp.where(sel, jnp.exp(l - m), 0.0)
        den = jnp.sum(p, axis=0, keepdims=True)
        vt = vt_ref[0, h * A_HEAD_DIM:(h + 1) * A_HEAD_DIM, :]
        o_t = jnp.dot(vt.astype(bf16), p.astype(bf16), preferred_element_type=f32)
        outs.append(o_t / den)
    o = jnp.concatenate(outs, axis=0).T
    o_ref[...] = o * jax.nn.silu(z_ref[...])


def a_main(proj, misct, qn, kn, vt, near_t, far, b, s):
    n = b * s
    nq = s // QBLOCK
    row = lambda bb, i: bb * nq + i
    return pl.pallas_call(
        _a_main_kernel,
        grid=(b, nq),
        in_specs=[
            pl.BlockSpec(memory_space=pltpu.SMEM),
            pl.BlockSpec((QBLOCK, A_WIDTH), lambda bb, i: (row(bb, i), 0)),
            pl.BlockSpec((QBLOCK, IDX_HEADS * IDX_DIM), lambda bb, i: (row(bb, i), OFF["a_qi"] // (IDX_HEADS * IDX_DIM))),
            pl.BlockSpec((QBLOCK, A_WIDTH), lambda bb, i: (row(bb, i), OFF["a_z"] // A_WIDTH)),
            pl.BlockSpec((1, 128, QBLOCK), lambda bb, i: (bb, 0, i)),
            pl.BlockSpec((s, A_WIDTH), lambda bb, i: (bb, 0)),
            pl.BlockSpec((1, A_WIDTH, s), lambda bb, i: (bb, 0, 0)),
            pl.BlockSpec((s, 128), lambda bb, i: (bb, MISC_OFF // 128)),
            pl.BlockSpec((A_HEADS, 2 * QBLOCK, QBLOCK), lambda bb, i: (0, 0, 0)),
        ],
        out_specs=pl.BlockSpec((QBLOCK, A_WIDTH), lambda bb, i: (row(bb, i), 0)),
        out_shape=jax.ShapeDtypeStruct((n, A_WIDTH), jnp.float32),
        scratch_shapes=[
            pltpu.VMEM((s, QBLOCK), jnp.int32),
            pltpu.VMEM((s + QBLOCK, A_HEADS * QBLOCK), jnp.float32),
            pltpu.VMEM((1, QBLOCK), jnp.int32),
        ],
        compiler_params=pltpu.CompilerParams(
            dimension_semantics=("arbitrary", "arbitrary"), vmem_limit_bytes=VMEM_LIMIT),
        name="a_main",
    )(far, qn, proj, proj, misct, kn, vt, proj, near_t)


def mixer_a(proj, misct, b, s, gkv, wukv, gq, gk, bias_a):
    kn, vt, qn = a_pre(proj, b, s, gkv, wukv.astype(jnp.bfloat16), gq, gk)
    near_t = jnp.transpose(bias_a[jnp.asarray(_NEAR_BUCKET_T)], (2, 0, 1))
    far = bias_a[_FAR_BUCKET]
    return a_main(proj, misct, qn, kn, vt, near_t, far, b, s)


N_CMP_PAD = 128
N_SB = 32
WIN_KEYS = WINDOW + QBLOCK
CMP_GROUPS = CMP_BLOCK // CMP_STRIDE

_CMP_BUCKET_T = _bucket_np((np.arange(16)[:, None, None] * QBLOCK + np.arange(QBLOCK)[None, None, :])
                           - (np.arange(N_CMP_PAD)[None, :, None] * CMP_STRIDE + CMP_BLOCK - 1))
_WIN_BUCKET_T = _bucket_np(np.arange(QBLOCK)[None, :] + WINDOW - np.arange(WIN_KEYS)[:, None])
_OVERLAP_T = np.array([[1.0 if (n * CMP_STRIDE < j * SEL_BLOCK + SEL_BLOCK and n * CMP_STRIDE + CMP_BLOCK > j * SEL_BLOCK
                              and n < N_CMP_PAD - 1) else 0.0 for n in range(N_CMP_PAD)] for j in range(N_SB)], np.float32)
_SB_EXPAND = (np.arange(2048)[:, None] // SEL_BLOCK == np.arange(N_SB)[None, :]).astype(np.float32)


def _c_pre_kernel(cmp_ref, sel_ref, win_ref, kn_ref, pos_ref, w1_ref, w2_ref,
                  kvc_ref, ksel_ref, vselt_ref, kwin_ref, vwin_ref):
    f32, bf16 = jnp.float32, jnp.bfloat16
    d = C_HEAD_DIM

    def rms(x, g):
        return x * lax.rsqrt(jnp.mean(x * x, axis=-1, keepdims=True) + RMS_EPS) * g

    acc = [jnp.zeros((N_CMP_PAD, 2 * PHI_HIDDEN), f32) for _ in range(CMP_GROUPS)]
    for j in range(CMP_STRIDE):
        xs = cmp_ref[pl.ds(j, N_CMP_PAD, stride=CMP_STRIDE), :]
        for half in range(CMP_GROUPS):
            jj = half * CMP_STRIDE + j
            acc[half] = acc[half] + jnp.dot((xs + pos_ref[jj:jj + 1, :]).astype(bf16), w1_ref[jj],
                                            preferred_element_type=f32)
    hid = acc[0] + pltpu.roll(acc[1], N_CMP_PAD - 1, axis=0)
    kv = jnp.dot(jax.nn.silu(hid).astype(bf16), w2_ref[...], preferred_element_type=f32)
    kvc_ref[0, :, 0:d] = rms(kv[:, 0:d], kn_ref[0:1, :])
    kvc_ref[0, :, d:2 * d] = kv[:, d:2 * d]

    sel = sel_ref[...]
    ksel_ref[...] = rms(sel[:, 0:d], kn_ref[1:2, :])
    vselt_ref[0] = sel.T[d:2 * d, :]
    win = win_ref[...]
    kwin_ref[0, 0:WINDOW, :] = jnp.zeros((WINDOW, d), f32)
    vwin_ref[0, 0:WINDOW, :] = jnp.zeros((WINDOW, d), f32)
    kwin_ref[0, WINDOW:, :] = rms(win[:, 0:d], kn_ref[2:3, :])
    vwin_ref[0, WINDOW:, :] = win[:, d:2 * d]


def c_pre(proj, b, s, k_norm, cmp_pos, phi_w1, phi_w2):
    d = C_HEAD_DIM
    kv0 = OFF["c_kv"] // 128
    pos = jnp.concatenate([cmp_pos[0], cmp_pos[1]], axis=-1)
    w1 = phi_w1.reshape(2, CMP_BLOCK, d, PHI_HIDDEN)
    zero = jnp.zeros((CMP_BLOCK, d, PHI_HIDDEN), jnp.float32)
    w1c = jnp.concatenate([jnp.concatenate([w1[0], zero], axis=-1),
                           jnp.concatenate([zero, w1[1]], axis=-1)], axis=1).astype(jnp.bfloat16)
    z2 = jnp.zeros((PHI_HIDDEN, d), jnp.float32)
    w2c = jnp.concatenate([jnp.concatenate([phi_w2[0], z2], axis=-1),
                           jnp.concatenate([z2, phi_w2[1]], axis=-1)], axis=0).astype(jnp.bfloat16)
    full = lambda a: pl.BlockSpec(a.shape, lambda bb: (0,) * a.ndim)
    return pl.pallas_call(
        _c_pre_kernel,
        grid=(b,),
        in_specs=[
            pl.BlockSpec((s, 128), lambda bb: (bb, kv0)),
            pl.BlockSpec((s, 128), lambda bb: (bb, kv0 + 1)),
            pl.BlockSpec((s, 128), lambda bb: (bb, kv0 + 2)),
            full(k_norm), full(pos), full(w1c), full(w2c),
        ],
        out_specs=[
            pl.BlockSpec((1, N_CMP_PAD, 128), lambda bb: (bb, 0, 0)),
            pl.BlockSpec((s, d), lambda bb: (bb, 0)),
            pl.BlockSpec((1, d, s), lambda bb: (bb, 0, 0)),
            pl.BlockSpec((1, s + WINDOW, d), lambda bb: (bb, 0, 0)),
            pl.BlockSpec((1, s + WINDOW, d), lambda bb: (bb, 0, 0)),
        ],
        out_shape=[
            jax.ShapeDtypeStruct((b, N_CMP_PAD, 128), jnp.float32),
            jax.ShapeDtypeStruct((b * s, d), jnp.float32),
            jax.ShapeDtypeStruct((b, d, s), jnp.float32),
            jax.ShapeDtypeStruct((b, s + WINDOW, d), jnp.float32),
            jax.ShapeDtypeStruct((b, s + WINDOW, d), jnp.float32),
        ],
        compiler_params=pltpu.CompilerParams(dimension_semantics=("arbitrary",), vmem_limit_bytes=VMEM_LIMIT),
        name="c_pre",
    )(proj, proj, proj, k_norm, pos, w1c, w2c)


def _c_main_kernel(far_ref, cq_ref, z_ref, misct_ref, gq_ref, kvc_ref, ksel_ref, vselt_ref, kwin_ref, vwin_ref,
                   cmptab_ref, near_ref, farrow_ref, wintab_ref, ovt_ref, exp_ref, o_ref, ls_scr):
    s = ksel_ref.shape[0]
    d = C_HEAD_DIM
    hq = C_HEADS * QBLOCK
    i = pl.program_id(1)
    t0 = pl.multiple_of(i * QBLOCK, QBLOCK)
    f32, bf16 = jnp.float32, jnp.bfloat16
    nt = (((1,), (1,)), ((), ()))
    tn = (((0,), (0,)), ((), ()))

    cq = cq_ref[...]
    qs = jnp.concatenate([cq[:, h * d:(h + 1) * d] for h in range(C_HEADS)], axis=0)
    qs = qs * lax.rsqrt(jnp.mean(qs * qs, axis=-1, keepdims=True) + RMS_EPS) * gq_ref[...] * (d ** -0.5)
    qs = qs.astype(bf16)

    def softmax_rows(l, valid):
        l = jnp.where(valid, l, NEG_INF)
        m = jnp.max(l, axis=0, keepdims=True)
        p = jnp.where(valid, jnp.exp(l - m), 0.0)
        den = jnp.sum(p, axis=0, keepdims=True)
        return p, den

    kvc = kvc_ref[0]
    lc = lax.dot_general(kvc[:, 0:d].astype(bf16), qs, nt, preferred_element_type=f32) + cmptab_ref[0]
    n_idx = lax.broadcasted_iota(jnp.int32, (N_CMP_PAD, hq), 0)
    t_c = t0 + (lax.broadcasted_iota(jnp.int32, (N_CMP_PAD, hq), 1) & (QBLOCK - 1))
    cmp_valid = n_idx * CMP_STRIDE + (CMP_BLOCK - 1) <= t_c
    pc, den_c = softmax_rows(lc, cmp_valid)
    pc = pc * jnp.where(den_c > 0.0, 1.0 / den_c, 0.0)
    o_cmp = lax.dot_general(kvc[:, d:2 * d].astype(bf16), pc.astype(bf16), tn, preferred_element_type=f32)

    psum = pc[:, 0:QBLOCK]
    for h in range(1, C_HEADS):
        psum = psum + pc[:, h * QBLOCK:(h + 1) * QBLOCK]
    p_hi = psum.astype(bf16)
    p_lo = (psum - p_hi.astype(f32)).astype(bf16)
    ovt = ovt_ref[...]
    imp = jnp.dot(ovt, p_hi, preferred_element_type=f32) + jnp.dot(ovt, p_lo, preferred_element_type=f32)
    j_idx = lax.broadcasted_iota(jnp.int32, (N_SB, QBLOCK), 0)
    t_b = t0 + lax.broadcasted_iota(jnp.int32, (N_SB, QBLOCK), 1)
    cur = lax.shift_right_arithmetic(t_b, 6)
    forced = (j_idx == 0) | (j_idx == cur) | (j_idx == jnp.maximum(cur - 1, 0))
    imp = jnp.where(j_idx * SEL_BLOCK <= t_b, jnp.where(forced, FORCE_SCORE, imp), NEG_INF)
    rank = jnp.zeros((N_SB, QBLOCK), f32)
    for r in range(N_SB):
        row = imp[r:r + 1, :]
        rank = rank + jnp.where((row > imp) | ((row == imp) & (j_idx > r)), 1.0, 0.0)
    picked = jnp.where(rank < float(N_SEL), 1.0, 0.0).astype(bf16)
    kpos = lax.broadcasted_iota(jnp.int32, (s, QBLOCK), 0)
    tpos = t0 + lax.broadcasted_iota(jnp.int32, (s, QBLOCK), 1)
    sel_valid = (jnp.dot(exp_ref[...], picked, preferred_element_type=f32) > 0.5) & (kpos <= tpos)

    ls_scr[0:QBLOCK, :] = jnp.zeros((QBLOCK, hq), f32)
    ls_scr[QBLOCK:, :] = lax.dot_general(ksel_ref[...].astype(bf16), qs, nt, preferred_element_type=f32)
    ls_scr[pl.ds(t0, 2 * QBLOCK), :] += near_ref[...] - farrow_ref[...]
    o_sel = []
    for h in range(C_HEADS):
        p, den = softmax_rows(ls_scr[QBLOCK:, h * QBLOCK:(h + 1) * QBLOCK] + far_ref[h], sel_valid)
        o_sel.append(jnp.dot(vselt_ref[0].astype(bf16), p.astype(bf16), preferred_element_type=f32) / den)

    kw = kwin_ref[0, pl.ds(t0, WIN_KEYS), :]
    vw = vwin_ref[0, pl.ds(t0, WIN_KEYS), :]
    lw = lax.dot_general(kw.astype(bf16), qs, nt, preferred_element_type=f32) + wintab_ref[...]
    r_idx = lax.broadcasted_iota(jnp.int32, (WIN_KEYS, hq), 0)
    dist = (lax.broadcasted_iota(jnp.int32, (WIN_KEYS, hq), 1) & (QBLOCK - 1)) + WINDOW - r_idx
    win_valid = (dist >= 0) & (dist < WINDOW) & (r_idx + t0 >= WINDOW)
    pw, den_w = softmax_rows(lw, win_valid)
    o_win = lax.dot_general(vw.astype(bf16), pw.astype(bf16), tn, preferred_element_type=f32) / den_w

    g = jax.nn.sigmoid(misct_ref[0, 80:80 + 3 * C_HEADS, :])
    outs = []
    for h in range(C_HEADS):
        cols = slice(h * QBLOCK, (h + 1) * QBLOCK)
        outs.append(g[3 * h:3 * h + 1, :] * o_cmp[:, cols] + g[3 * h + 1:3 * h + 2, :] * o_sel[h]
                    + g[3 * h + 2:3 * h + 3, :] * o_win[:, cols])
    o_ref[...] = jnp.concatenate(outs, axis=0).T * jax.nn.silu(z_ref[...])


def c_main(proj, misct, gq, kvc, ksel, vselt, kwin, vwin, cmptab, near, farrow, wintab, far, b, s):
    n = b * s
    nq = s // QBLOCK
    d = C_HEAD_DIM
    hq = C_HEADS * QBLOCK
    row = lambda bb, i: bb * nq + i
    ovt = jnp.asarray(_OVERLAP_T, jnp.bfloat16)
    expand = jnp.asarray(_SB_EXPAND, jnp.bfloat16)
    const = lambda a: pl.BlockSpec(a.shape, lambda bb, i: (0,) * a.ndim)
    return pl.pallas_call(
        _c_main_kernel,
        grid=(b, nq),
        in_specs=[
            pl.BlockSpec(memory_space=pltpu.SMEM),
            pl.BlockSpec((QBLOCK, C_WIDTH), lambda bb, i: (row(bb, i), OFF["c_q"] // C_WIDTH)),
            pl.BlockSpec((QBLOCK, C_WIDTH), lambda bb, i: (row(bb, i), OFF["c_z"] // C_WIDTH)),
            pl.BlockSpec((1, 128, QBLOCK), lambda bb, i: (bb, 0, i)),
            const(gq),
            pl.BlockSpec((1, N_CMP_PAD, 128), lambda bb, i: (bb, 0, 0)),
            pl.BlockSpec((s, d), lambda bb, i: (bb, 0)),
            pl.BlockSpec((1, d, s), lambda bb, i: (bb, 0, 0)),
            pl.BlockSpec((1, s + WINDOW, d), lambda bb, i: (bb, 0, 0)),
            pl.BlockSpec((1, s + WINDOW, d), lambda bb, i: (bb, 0, 0)),
            pl.BlockSpec((1, N_CMP_PAD, hq), lambda bb, i: (i, 0, 0)),
            const(near), const(farrow), const(wintab), const(ovt), const(expand),
        ],
        out_specs=pl.BlockSpec((QBLOCK, C_WIDTH), lambda bb, i: (row(bb, i), 0)),
        out_shape=jax.ShapeDtypeStruct((n, C_WIDTH), jnp.float32),
        scratch_shapes=[pltpu.VMEM((s + QBLOCK, hq), jnp.float32)],
        compiler_params=pltpu.CompilerParams(
            dimension_semantics=("arbitrary", "arbitrary"), vmem_limit_bytes=VMEM_LIMIT),
        name="c_main",
    )(far, proj, proj, misct, gq, kvc, ksel, vselt, kwin, vwin, cmptab, near, farrow, wintab, ovt, expand)


def _head_cols(tab):
    return jnp.moveaxis(tab, -1, -2).reshape(*tab.shape[:-2], tab.shape[-1] * tab.shape[-2])


def mixer_c(proj, misct, b, s, gq, k_norm, cmp_pos, phi_w1, phi_w2, bias_c):
    kvc, ksel, vselt, kwin, vwin = c_pre(proj, b, s, k_norm, cmp_pos, phi_w1, phi_w2)
    cmptab = _head_cols(bias_c[jnp.asarray(_CMP_BUCKET_T)])
    near = _head_cols(bias_c[jnp.asarray(_NEAR_BUCKET_T)])
    wintab = _head_cols(bias_c[jnp.asarray(_WIN_BUCKET_T)])
    far = bias_c[_FAR_BUCKET]
    farrow = jnp.repeat(far, QBLOCK).reshape(1, C_HEADS * QBLOCK)
    return c_main(proj, misct, gq.reshape(1, C_HEAD_DIM), kvc, ksel, vselt, kwin, vwin, cmptab, near, farrow,
                  wintab, far, b, s)


def _rms_norm(x, g):
    y = x * lax.rsqrt(jnp.mean(x * x, axis=-1, keepdims=True) + RMS_EPS)
    return y * g


def _l2norm(x):
    return x * lax.rsqrt(jnp.sum(x * x, axis=-1, keepdims=True) + RMS_EPS)


def _masked_softmax(logits, mask):
    logits = jnp.where(mask, logits, NEG_INF)
    return jnp.where(mask, jax.nn.softmax(logits, axis=-1), 0.0)


def _t5_bucket(dist):
    dist = jnp.maximum(dist, 0)
    log_ratio = jnp.log(jnp.maximum(dist, 1).astype(jnp.float32) / BUCKET_MAX_EXACT) / math.log(BUCKET_MAX_DIST / BUCKET_MAX_EXACT)
    large = BUCKET_MAX_EXACT + (log_ratio * (N_BUCKETS - BUCKET_MAX_EXACT)).astype(jnp.int32)
    large = jnp.minimum(large, N_BUCKETS - 1)
    return jnp.where(dist < BUCKET_MAX_EXACT, dist, large)


def _to_qblocks(a):
    b, s = a.shape[:2]
    return jnp.moveaxis(a.reshape(b, s // QBLOCK, QBLOCK, *a.shape[2:]), 1, 0)


def _from_qblocks(a):
    nb, b = a.shape[:2]
    return jnp.moveaxis(a, 0, 1).reshape(b, nb * QBLOCK, *a.shape[3:])


def _dsa_mixer(q, k, v, q_idx, k_idx, w_idx, bias_tab):
    b, s = q.shape[:2]
    topk = min(TOPK_MAX, s // 4)
    bidx = jnp.arange(b)[:, None, None]
    key_pos = jnp.arange(s)
    scale = A_HEAD_DIM ** -0.5

    def block(args):
        qb, qib, wb, start = args
        t = start + jnp.arange(QBLOCK)
        idx_logits = jnp.einsum('bqhd,bsd->bqhs', qib, k_idx) * (IDX_DIM ** -0.5)
        score = jnp.einsum('bqh,bqhs->bqs', wb * (IDX_HEADS ** -0.5), jax.nn.relu(idx_logits))
        score = jnp.where(key_pos[None, None, :] <= t[None, :, None], score, NEG_INF)
        _, sel = lax.top_k(score, topk)
        kg = k[bidx, sel]
        vg = v[bidx, sel]
        dist = t[None, :, None] - sel
        bias = jnp.transpose(bias_tab[_t5_bucket(dist)], (0, 3, 1, 2))
        logits = jnp.einsum('bqhd,bqkhd->bhqk', qb, kg) * scale + bias
        p = _masked_softmax(logits, (dist >= 0)[:, None])
        return jnp.einsum('bhqk,bqkhd->bqhd', p, vg)

    starts = jnp.arange(s // QBLOCK, dtype=jnp.int32) * QBLOCK
    out = lax.map(block, (_to_qblocks(q), _to_qblocks(q_idx), _to_qblocks(w_idx), starts))
    return _from_qblocks(out)


def _causal_depthwise_conv(x, w):
    c = x.shape[-1]
    return lax.conv_general_dilated(x, w[:, None, :], window_strides=(1,), padding=[(CONV_WIDTH - 1, 0)],
                                    dimension_numbers=('NWC', 'WIO', 'NWC'), feature_group_count=c)


def _gdn_mixer(qkv, a_in, b_in, conv_w, a_log, dt_bias):
    b, s = qkv.shape[:2]
    n = s // GDN_CHUNK
    qkv = jax.nn.silu(_causal_depthwise_conv(qkv, conv_w))
    q, k, v = jnp.split(qkv, 3, axis=-1)
    sh = lambda t: t.reshape(b, s, B_HEADS, B_HEAD_DIM)
    q = _l2norm(sh(q)) * (B_HEAD_DIM ** -0.5)
    k = _l2norm(sh(k))
    v = sh(v)
    beta = jax.nn.sigmoid(b_in)
    g = -jnp.exp(a_log) * jax.nn.softplus(a_in + dt_bias)

    def chunk(t):
        return jnp.moveaxis(t.reshape(b, n, GDN_CHUNK, *t.shape[2:]), 3, 2)

    q, k, v, beta, g = (chunk(t) for t in (q, k, v, beta, g))
    g_cum = jnp.cumsum(g, axis=-1)
    g_last = g_cum[..., -1]
    lower = jnp.tril(jnp.ones((GDN_CHUNK, GDN_CHUNK), dtype=bool))
    strict = jnp.tril(jnp.ones((GDN_CHUNK, GDN_CHUNK), dtype=bool), -1)
    diff = g_cum[..., :, None] - g_cum[..., None, :]
    decay = jnp.where(lower, jnp.exp(jnp.where(lower, diff, 0.0)), 0.0)
    k_beta = k * beta[..., None]
    a_mat = jnp.where(strict, jnp.einsum('bnhid,bnhjd->bnhij', k_beta, k) * decay, 0.0)
    eye = jnp.eye(GDN_CHUNK, dtype=jnp.float32)
    t_inv = lax.linalg.triangular_solve(eye + a_mat, jnp.broadcast_to(eye, a_mat.shape), left_side=True, lower=True)
    u = jnp.einsum('bnhij,bnhjd->bnhid', t_inv, v * beta[..., None])
    w = jnp.einsum('bnhij,bnhjd->bnhid', t_inv, k_beta * jnp.exp(g_cum)[..., None])
    attn = jnp.einsum('bnhid,bnhjd->bnhij', q, k) * decay
    q_dec = q * jnp.exp(g_cum)[..., None]
    k_dec = k * jnp.exp(g_last[..., None] - g_cum)[..., None]

    def step(state, xs):
        u_c, w_c, q_c, k_c, attn_c, gl_c = xs
        v_new = u_c - jnp.einsum('bhik,bhkv->bhiv', w_c, state)
        o = jnp.einsum('bhik,bhkv->bhiv', q_c, state) + jnp.einsum('bhij,bhjv->bhiv', attn_c, v_new)
        state = state * jnp.exp(gl_c)[..., None, None] + jnp.einsum('bhik,bhiv->bhkv', k_c, v_new)
        return state, o

    state0 = jnp.zeros((b, B_HEADS, B_HEAD_DIM, B_HEAD_DIM), jnp.float32)
    xs = tuple(jnp.moveaxis(t, 1, 0) for t in (u, w, q_dec, k_dec, attn, g_last))
    _, o = lax.scan(step, state0, xs)
    o = jnp.moveaxis(jnp.moveaxis(o, 0, 1), 2, 3)
    return o.reshape(b, s, B_HEADS, B_HEAD_DIM)


def _compress_blocks(tok, pos, w1, w2):
    b, s, d = tok.shape
    n_cmp = (s - CMP_BLOCK) // CMP_STRIDE + 1
    idx = jnp.arange(n_cmp)[:, None] * CMP_STRIDE + jnp.arange(CMP_BLOCK)[None, :]
    blocks = tok[:, idx] + pos
    return jax.nn.silu(blocks.reshape(b, n_cmp, CMP_BLOCK * d) @ w1) @ w2


def _nsa_mixer(q, kv, gates, k_norm, cmp_pos, phi_w1, phi_w2, bias_tab):
    b, s = q.shape[:2]
    scale = C_HEAD_DIM ** -0.5
    t_all = jnp.arange(s)
    k_cmp, v_cmp, k_sel, v_sel, k_win, v_win = (kv[:, :, i] for i in range(6))
    k_sel = _rms_norm(k_sel, k_norm[1])
    k_win = _rms_norm(k_win, k_norm[2])
    kc = _rms_norm(_compress_blocks(k_cmp, cmp_pos[0], phi_w1[0], phi_w2[0]), k_norm[0])
    vc = _compress_blocks(v_cmp, cmp_pos[1], phi_w1[1], phi_w2[1])
    n_cmp = kc.shape[1]
    cmp_start = jnp.arange(n_cmp) * CMP_STRIDE
    cmp_end = cmp_start + CMP_BLOCK - 1
    cmp_valid = cmp_end[None, :] <= t_all[:, None]
    cmp_bias = jnp.transpose(bias_tab[_t5_bucket(t_all[:, None] - cmp_end[None, :])], (2, 0, 1))
    logits = jnp.einsum('bqhd,bnd->bhqn', q, kc) * scale + cmp_bias
    p_cmp = _masked_softmax(logits, cmp_valid[None, None])
    o_cmp = jnp.einsum('bhqn,bnd->bqhd', p_cmp, vc)
    n_sb = s // SEL_BLOCK
    n_pick = min(N_SEL, n_sb)
    sb_start = jnp.arange(n_sb) * SEL_BLOCK
    overlap = ((cmp_start[:, None] < sb_start[None, :] + SEL_BLOCK) & (cmp_start[:, None] + CMP_BLOCK > sb_start[None, :])).astype(jnp.float32)
    importance = jnp.einsum('bhqn,nj->bqj', p_cmp, overlap)
    cur = t_all // SEL_BLOCK
    blk = jnp.arange(n_sb)
    forced = (blk[None, :] == 0) | (blk[None, :] == cur[:, None]) | (blk[None, :] == jnp.maximum(cur[:, None] - 1, 0))
    admissible = sb_start[None, :] <= t_all[:, None]
    importance = jnp.where(admissible[None], jnp.where(forced[None], FORCE_SCORE, importance), NEG_INF)
    _, sel = lax.top_k(importance, n_pick)
    k_blocks = k_sel.reshape(b, n_sb, SEL_BLOCK, C_HEAD_DIM)
    v_blocks = v_sel.reshape(b, n_sb, SEL_BLOCK, C_HEAD_DIM)
    k_win_pad = jnp.pad(k_win, ((0, 0), (WINDOW, 0), (0, 0)))
    v_win_pad = jnp.pad(v_win, ((0, 0), (WINDOW, 0), (0, 0)))
    bidx = jnp.arange(b)[:, None, None]
    n_keys = n_pick * SEL_BLOCK

    def block(args):
        qb, selb, start = args
        t = start + jnp.arange(QBLOCK)
        kg = k_blocks[bidx, selb].reshape(b, QBLOCK, n_keys, C_HEAD_DIM)
        vg = v_blocks[bidx, selb].reshape(b, QBLOCK, n_keys, C_HEAD_DIM)
        s_pos = (selb[..., None] * SEL_BLOCK + jnp.arange(SEL_BLOCK)).reshape(b, QBLOCK, n_keys)
        dist = t[None, :, None] - s_pos
        bias = jnp.transpose(bias_tab[_t5_bucket(dist)], (0, 3, 1, 2))
        lg = jnp.einsum('bqhd,bqkd->bhqk', qb, kg) * scale + bias
        p = _masked_softmax(lg, (dist >= 0)[:, None])
        o_sel = jnp.einsum('bhqk,bqkd->bqhd', p, vg)
        kw = lax.dynamic_slice_in_dim(k_win_pad, start, QBLOCK + WINDOW, axis=1)
        vw = lax.dynamic_slice_in_dim(v_win_pad, start, QBLOCK + WINDOW, axis=1)
        s_w = start - WINDOW + jnp.arange(QBLOCK + WINDOW)
        dist_w = t[:, None] - s_w[None, :]
        valid_w = (s_w[None, :] >= 0) & (dist_w >= 0) & (dist_w < WINDOW)
        bias_w = jnp.transpose(bias_tab[_t5_bucket(dist_w)], (2, 0, 1))
        lw = jnp.einsum('bqhd,bsd->bhqs', qb, kw) * scale + bias_w
        pw = _masked_softmax(lw, valid_w[None, None])
        o_win = jnp.einsum('bhqs,bsd->bqhd', pw, vw)
        return o_sel, o_win

    starts = jnp.arange(s // QBLOCK, dtype=jnp.int32) * QBLOCK
    o_sel, o_win = lax.map(block, (_to_qblocks(q), _to_qblocks(sel), starts))
    o_sel = _from_qblocks(o_sel)
    o_win = _from_qblocks(o_win)
    g = jax.nn.sigmoid(gates).reshape(b, s, C_HEADS, 3)
    return g[..., 0:1] * o_cmp + g[..., 1:2] * o_sel + g[..., 2:3] * o_win


def _field(proj3, name):
    return proj3[..., OFF[name]:OFF[name] + _ORIG[name][1]]


def kernel(x, norm_g, w_in, a_kv_norm, a_w_ukv, a_q_norm, a_k_norm, b_conv, b_a_log, b_dt_bias, b_out_norm, c_q_norm, c_k_norm, c_cmp_pos, c_phi_w1, c_phi_w2, w_branch, w_out, rel_bias):
    b, s, _ = x.shape
    n = b * s
    bias_a = rel_bias[:, :A_HEADS]
    bias_c = rel_bias[:, A_HEADS:]
    xf = x.reshape(n, D_MODEL)
    perm = jnp.asarray(_PERM_IDX)
    valid = jnp.asarray(_PERM_VALID)
    sh = lambda t, h, d: t.reshape(b, s, h, d)
    for l in range(DEPTH):
        w_p = jnp.where(valid[None, :], w_in[l][:, perm], 0.0).astype(jnp.bfloat16)
        proj = in_proj(xf, norm_g[l], w_p)
        p3 = proj.reshape(b, s, N_PAD)

        misct = jnp.transpose(p3[..., MISC_OFF:], (0, 2, 1))
        y_a = mixer_a(proj, misct, b, s, a_kv_norm[l], a_w_ukv[l], a_q_norm[l], a_k_norm[l], bias_a)

        o_b = _gdn_mixer(_field(p3, "b_qkv"), _field(p3, "b_a"), _field(p3, "b_b"), b_conv[l], b_a_log[l], b_dt_bias[l])
        y_b = _rms_norm(o_b, b_out_norm[l]).reshape(n, B_WIDTH) * jax.nn.silu(_field(p3, "b_z")).reshape(n, B_WIDTH)

        y_c = mixer_c(proj, misct, b, s, c_q_norm[l], c_k_norm[l], c_cmp_pos[l], c_phi_w1[l], c_phi_w2[l], bias_c)

        wbr = w_branch[l].astype(jnp.bfloat16)
        xf = merge(xf, proj, y_a, y_b, y_c, wbr[:A_WIDTH], wbr[A_WIDTH:A_WIDTH + B_WIDTH],
                   wbr[A_WIDTH + B_WIDTH:], w_out[l].astype(jnp.bfloat16))
    return xf.reshape(b, s, D_MODEL)
```

```python
import functools
import math

import jax
import jax.numpy as jnp
import numpy as np
from jax import lax
from jax.experimental import pallas as pl
from jax.experimental.pallas import tpu as pltpu

D_MODEL = 1024
DEPTH = 4
QBLOCK = 128
NEG_INF = -1e30
FORCE_SCORE = 1e9
RMS_EPS = 1e-6

A_HEADS = 4
A_HEAD_DIM = 64
A_WIDTH = A_HEADS * A_HEAD_DIM
A_KV_LATENT = 128
IDX_HEADS = 8
IDX_DIM = 64
TOPK_MAX = 256

B_HEADS = 4
B_HEAD_DIM = 128
B_WIDTH = B_HEADS * B_HEAD_DIM
CONV_WIDTH = 4
GDN_CHUNK = 64

C_HEADS = 4
C_HEAD_DIM = 64
C_WIDTH = C_HEADS * C_HEAD_DIM
CMP_BLOCK = 32
CMP_STRIDE = 16
SEL_BLOCK = 64
N_SEL = 16
WINDOW = 512
PHI_HIDDEN = 256

N_BUCKETS = 32
BUCKET_MAX_EXACT = 16
BUCKET_MAX_DIST = 128

VMEM_LIMIT = 48 * 1024 * 1024

_ORIG = {}
_o = 0
for _name, _w in (("a_q", 256), ("a_ckv", 128), ("a_qi", 512), ("a_ki", 64), ("a_wi", 8), ("a_z", 256),
                  ("b_qkv", 1536), ("b_a", 4), ("b_b", 4), ("b_z", 512),
                  ("c_q", 256), ("c_kv", 384), ("c_g", 12), ("c_z", 256), ("g", 3072)):
    _ORIG[_name] = (_o, _w)
    _o += _w
N_IN = _o

_NEW_ORDER = ("g", "b_qkv", "b_z", "a_qi", "a_q", "a_z", "c_q", "c_z", "c_kv", "a_ckv",
              "a_ki", "a_wi", "b_a", "b_b", "c_g")
OFF = {}
_o = 0
for _name in _NEW_ORDER:
    OFF[_name] = _o
    _o += _ORIG[_name][1]
MISC_OFF = OFF["a_ki"]
N_PAD = 7296
assert MISC_OFF == 7168 and _o <= N_PAD


def _perm_indices():
    idx = np.zeros((N_PAD,), np.int32)
    valid = np.zeros((N_PAD,), bool)
    for name in _NEW_ORDER:
        o_old, w = _ORIG[name]
        idx[OFF[name]:OFF[name] + w] = np.arange(o_old, o_old + w)
        valid[OFF[name]:OFF[name] + w] = True
    return idx, valid


_PERM_IDX, _PERM_VALID = _perm_indices()


IN_TM = 512
IN_TN = N_PAD // 3


def _in_proj_kernel(x_ref, g_ref, w_ref, o_ref):
    x = x_ref[...]
    ms = jnp.mean(x * x, axis=-1, keepdims=True)
    h = (x * lax.rsqrt(ms + RMS_EPS)) * g_ref[...]
    o_ref[...] = jnp.dot(h.astype(jnp.bfloat16), w_ref[...], preferred_element_type=jnp.float32)


def in_proj(xf, g, w_bf16):
    n = xf.shape[0]
    return pl.pallas_call(
        _in_proj_kernel,
        grid=(N_PAD // IN_TN, n // IN_TM),
        in_specs=[
            pl.BlockSpec((IN_TM, D_MODEL), lambda j, i: (i, 0)),
            pl.BlockSpec((1, D_MODEL), lambda j, i: (0, 0)),
            pl.BlockSpec((D_MODEL, IN_TN), lambda j, i: (0, j)),
        ],
        out_specs=pl.BlockSpec((IN_TM, IN_TN), lambda j, i: (i, j)),
        out_shape=jax.ShapeDtypeStruct((n, N_PAD), jnp.float32),
        compiler_params=pltpu.CompilerParams(
            dimension_semantics=("arbitrary", "arbitrary"), vmem_limit_bytes=VMEM_LIMIT),
        name="in_proj",
    )(xf, g.reshape(1, D_MODEL), w_bf16)


MG_TM = 256


def _merge_kernel(x_ref, g_ref, ya_ref, yb_ref, yc_ref, wa_ref, wb_ref, wc_ref, wo_ref, o_ref):
    def branch(y_ref, w_ref, k):
        p = jnp.dot(y_ref[...].astype(jnp.bfloat16), w_ref[...], preferred_element_type=jnp.float32)
        return jax.nn.sigmoid(g_ref[:, k * D_MODEL:(k + 1) * D_MODEL]) * p

    merged = branch(ya_ref, wa_ref, 0) + branch(yb_ref, wb_ref, 1) + branch(yc_ref, wc_ref, 2)
    o_ref[...] = x_ref[...] + jnp.dot(merged.astype(jnp.bfloat16), wo_ref[...],
                                      preferred_element_type=jnp.float32)


def merge(xf, proj, ya, yb, yc, wa, wb, wc, wo):
    n = xf.shape[0]
    row = lambda w: pl.BlockSpec((MG_TM, w), lambda i: (i, 0))
    full = lambda a: pl.BlockSpec(a.shape, lambda i: (0, 0))
    return pl.pallas_call(
        _merge_kernel,
        grid=(n // MG_TM,),
        in_specs=[row(D_MODEL), row(3 * D_MODEL), row(A_WIDTH), row(B_WIDTH), row(C_WIDTH),
                  full(wa), full(wb), full(wc), full(wo)],
        out_specs=row(D_MODEL),
        out_shape=jax.ShapeDtypeStruct((n, D_MODEL), jnp.float32),
        compiler_params=pltpu.CompilerParams(
            dimension_semantics=("arbitrary",), vmem_limit_bytes=VMEM_LIMIT),
        name="merge",
    )(xf, proj, ya, yb, yc, wa, wb, wc, wo)


def _bucket_np(dist):
    d = np.maximum(np.asarray(dist, np.int64), 0)
    ratio = np.log(np.maximum(d, 1).astype(np.float64) / BUCKET_MAX_EXACT) / math.log(BUCKET_MAX_DIST / BUCKET_MAX_EXACT)
    scaled = ratio * (N_BUCKETS - BUCKET_MAX_EXACT)
    frac = scaled - np.floor(scaled)
    edge = (d > BUCKET_MAX_EXACT) & (d < BUCKET_MAX_DIST) & ((frac < 1e-4) | (frac > 1 - 1e-4))
    assert not edge.any(), "bucket boundary too close to an integer distance"
    large = np.minimum(BUCKET_MAX_EXACT + np.floor(scaled + 1e-9).astype(np.int64), N_BUCKETS - 1)
    return np.where(d < BUCKET_MAX_EXACT, d, large).astype(np.int32)


_NEAR_BUCKET_T = _bucket_np(np.arange(QBLOCK)[None, :] + QBLOCK - np.arange(2 * QBLOCK)[:, None])
_FAR_BUCKET = int(_bucket_np(np.array([BUCKET_MAX_DIST]))[0])
assert (_bucket_np(np.arange(BUCKET_MAX_DIST, 4096)) == _FAR_BUCKET).all()

INT_MIN = -2 ** 31


A_PRE_TM = 512


def _a_pre_kernel(ckv_ref, q_ref, gkv_ref, wukv_ref, gq_ref, gk_ref, hm_ref, kn_ref, vt_ref, qn_ref):
    c = ckv_ref[...]
    c = c * lax.rsqrt(jnp.mean(c * c, axis=-1, keepdims=True) + RMS_EPS) * gkv_ref[...]
    kv = jnp.dot(c.astype(jnp.bfloat16), wukv_ref[...], preferred_element_type=jnp.float32)
    hm = hm_ref[...]

    def head_rms(x, g):
        ms = jnp.dot(x * x, hm, precision=lax.Precision.HIGHEST, preferred_element_type=jnp.float32)
        return x * lax.rsqrt(ms + RMS_EPS) * g

    kn_ref[...] = head_rms(kv[:, :A_WIDTH], gk_ref[...])
    qn_ref[...] = head_rms(q_ref[...], gq_ref[...]) * (A_HEAD_DIM ** -0.5)
    vt_ref[0] = kv[:, A_WIDTH:].T


def a_pre(proj, b, s, gkv, wukv_bf16, gq, gk):
    n = b * s
    nt = s // A_PRE_TM
    hm = jnp.asarray(np.kron(np.eye(A_HEADS), np.ones((A_HEAD_DIM, A_HEAD_DIM))) / A_HEAD_DIM, jnp.float32)
    row = lambda bb, j: (bb * nt + j, 0)
    full = lambda a: pl.BlockSpec(a.shape, lambda bb, j: (0,) * a.ndim)
    gq = jnp.tile(gq, A_HEADS).reshape(1, A_WIDTH)
    gk = jnp.tile(gk, A_HEADS).reshape(1, A_WIDTH)
    gkv = gkv.reshape(1, A_KV_LATENT)
    return pl.pallas_call(
        _a_pre_kernel,
        grid=(b, nt),
        in_specs=[
            pl.BlockSpec((A_PRE_TM, A_KV_LATENT), lambda bb, j: (bb * nt + j, OFF["a_ckv"] // A_KV_LATENT)),
            pl.BlockSpec((A_PRE_TM, A_WIDTH), lambda bb, j: (bb * nt + j, OFF["a_q"] // A_WIDTH)),
            full(gkv), full(wukv_bf16), full(gq), full(gk), full(hm),
        ],
        out_specs=[
            pl.BlockSpec((A_PRE_TM, A_WIDTH), row),
            pl.BlockSpec((1, A_WIDTH, A_PRE_TM), lambda bb, j: (bb, 0, j)),
            pl.BlockSpec((A_PRE_TM, A_WIDTH), row),
        ],
        out_shape=[
            jax.ShapeDtypeStruct((n, A_WIDTH), jnp.float32),
            jax.ShapeDtypeStruct((b, A_WIDTH, s), jnp.float32),
            jax.ShapeDtypeStruct((n, A_WIDTH), jnp.float32),
        ],
        compiler_params=pltpu.CompilerParams(
            dimension_semantics=("arbitrary", "arbitrary"), vmem_limit_bytes=VMEM_LIMIT),
        name="a_pre",
    )(proj, proj, gkv, wukv_bf16, gq, gk, hm)


def _a_main_kernel(far_ref, qn_ref, qi_ref, z_ref, misct_ref, kn_ref, vt_ref, ki_ref, near_ref, o_ref,
                   key_scr, lg_scr, j_scr):
    s = kn_ref.shape[0]
    i = pl.program_id(1)
    t0 = pl.multiple_of(i * QBLOCK, QBLOCK)
    f32, bf16 = jnp.float32, jnp.bfloat16
    nt = (((1,), (1,)), ((), ()))

    qi = qi_ref[...]
    qstack = jnp.concatenate([qi[:, h * IDX_DIM:(h + 1) * IDX_DIM] for h in range(IDX_HEADS)], axis=0)
    ki = ki_ref[:, 0:IDX_DIM]
    sc = lax.dot_general(ki.astype(bf16), qstack.astype(bf16), nt, preferred_element_type=f32)
    wt = misct_ref[0, IDX_DIM:IDX_DIM + IDX_HEADS, :] * (IDX_HEADS ** -0.5)
    score = jnp.zeros((s, QBLOCK), f32)
    for h in range(IDX_HEADS):
        score = score + wt[h:h + 1, :] * jnp.maximum(sc[:, h * QBLOCK:(h + 1) * QBLOCK] * (IDX_DIM ** -0.5), 0.0)
    score = score + 0.0
    kpos = lax.broadcasted_iota(jnp.int32, (s, QBLOCK), 0)
    tpos = t0 + lax.broadcasted_iota(jnp.int32, (s, QBLOCK), 1)
    causal = kpos <= tpos
    bits = pltpu.bitcast(score, jnp.int32)
    key = jnp.where(bits < 0, bits ^ jnp.int32(0x7FFFFFFF), bits)
    key_scr[...] = jnp.where(causal, key, jnp.int32(INT_MIN))

    kf = float(TOPK_MAX)

    def count_ge(cand):
        return jnp.sum(jnp.where(key_scr[...] >= cand, 1.0, 0.0), axis=0, keepdims=True)

    def bisect(it, thr):
        cand = thr + lax.shift_left(jnp.int32(1), 31 - it)
        return jnp.where(count_ge(cand) >= kf, cand, thr)

    thr = lax.fori_loop(0, 32, bisect, jnp.full((1, QBLOCK), INT_MIN, jnp.int32))
    n_ge = count_ge(thr)
    need = kf - count_ge(thr + 1)

    j_scr[...] = jnp.full((1, QBLOCK), s - 1, jnp.int32)
    surplus = jnp.where((n_ge > kf) & (thr > INT_MIN), 1.0, 0.0)

    @pl.when(jnp.max(surplus) > 0.0)
    def _():
        def bisect_idx(it, lohi):
            lo, hi = lohi
            mid = lax.shift_right_arithmetic(lo + hi, 1)
            k = key_scr[...]
            kp = lax.broadcasted_iota(jnp.int32, (s, QBLOCK), 0)
            c = jnp.sum(jnp.where((k == thr) & (kp <= mid), 1.0, 0.0), axis=0, keepdims=True)
            ok = c >= need
            return jnp.where(ok, lo, mid), jnp.where(ok, mid, hi)

        lo0 = jnp.full((1, QBLOCK), -1, jnp.int32)
        hi0 = jnp.full((1, QBLOCK), s - 1, jnp.int32)
        _, hi = lax.fori_loop(0, 11, bisect_idx, (lo0, hi0))
        j_scr[...] = hi

    key = key_scr[...]
    sel = ((key > thr) | ((key == thr) & (kpos <= j_scr[...]))) & causal

    qn = qn_ref[...]
    lane = lax.broadcasted_iota(jnp.int32, (QBLOCK, A_WIDTH), 1)
    qblk = jnp.concatenate(
        [jnp.where((lane >= h * A_HEAD_DIM) & (lane < (h + 1) * A_HEAD_DIM), qn, 0.0) for h in range(A_HEADS)],
        axis=0)
    lg_scr[0:QBLOCK, :] = jnp.zeros((QBLOCK, A_HEADS * QBLOCK), f32)
    lg_scr[QBLOCK:, :] = lax.dot_general(kn_ref[...].astype(bf16), qblk.astype(bf16), nt,
                                         preferred_element_type=f32)
    outs = []
    for h in range(A_HEADS):
        cols = slice(h * QBLOCK, (h + 1) * QBLOCK)
        far = far_ref[h]
        lg_scr[pl.ds(t0, 2 * QBLOCK), cols] += near_ref[h] - far
        l = jnp.where(sel, lg_scr[QBLOCK:, cols] + far, NEG_INF)
        m = jnp.max(l, axis=0, keepdims=True)
        p = jnp.where(sel, jnp.exp(l - m), 0.0)
        den = jnp.sum(p, axis=0, keepdims=True)
        vt = vt_ref[0, h * A_HEAD_DIM:(h + 1) * A_HEAD_DIM, :]
        o_t = jnp.dot(vt.astype(bf16), p.astype(bf16), preferred_element_type=f32)
        outs.append(o_t / den)
    o = jnp.concatenate(outs, axis=0).T
    o_ref[...] = o * jax.nn.silu(z_ref[...])


def a_main(proj, misct, qn, kn, vt, near_t, far, b, s):
    n = b * s
    nq = s // QBLOCK
    row = lambda bb, i: bb * nq + i
    return pl.pallas_call(
        _a_main_kernel,
        grid=(b, nq),
        in_specs=[
            pl.BlockSpec(memory_space=pltpu.SMEM),
            pl.BlockSpec((QBLOCK, A_WIDTH), lambda bb, i: (row(bb, i), 0)),
            pl.BlockSpec((QBLOCK, IDX_HEADS * IDX_DIM), lambda bb, i: (row(bb, i), OFF["a_qi"] // (IDX_HEADS * IDX_DIM))),
            pl.BlockSpec((QBLOCK, A_WIDTH), lambda bb, i: (row(bb, i), OFF["a_z"] // A_WIDTH)),
            pl.BlockSpec((1, 128, QBLOCK), lambda bb, i: (bb, 0, i)),
            pl.BlockSpec((s, A_WIDTH), lambda bb, i: (bb, 0)),
            pl.BlockSpec((1, A_WIDTH, s), lambda bb, i: (bb, 0, 0)),
            pl.BlockSpec((s, 128), lambda bb, i: (bb, MISC_OFF // 128)),
            pl.BlockSpec((A_HEADS, 2 * QBLOCK, QBLOCK), lambda bb, i: (0, 0, 0)),
        ],
        out_specs=pl.BlockSpec((QBLOCK, A_WIDTH), lambda bb, i: (row(bb, i), 0)),
        out_shape=jax.ShapeDtypeStruct((n, A_WIDTH), jnp.float32),
        scratch_shapes=[
            pltpu.VMEM((s, QBLOCK), jnp.int32),
            pltpu.VMEM((s + QBLOCK, A_HEADS * QBLOCK), jnp.float32),
            pltpu.VMEM((1, QBLOCK), jnp.int32),
        ],
        compiler_params=pltpu.CompilerParams(
            dimension_semantics=("arbitrary", "arbitrary"), vmem_limit_bytes=VMEM_LIMIT),
        name="a_main",
    )(far, qn, proj, proj, misct, kn, vt, proj, near_t)


def mixer_a(proj, misct, b, s, gkv, wukv, gq, gk, bias_a):
    kn, vt, qn = a_pre(proj, b, s, gkv, wukv.astype(jnp.bfloat16), gq, gk)
    near_t = jnp.transpose(bias_a[jnp.asarray(_NEAR_BUCKET_T)], (2, 0, 1))
    far = bias_a[_FAR_BUCKET]
    return a_main(proj, misct, qn, kn, vt, near_t, far, b, s)


N_CMP_PAD = 128
N_SB = 32
WIN_KEYS = WINDOW + QBLOCK
CMP_GROUPS = CMP_BLOCK // CMP_STRIDE

_CMP_BUCKET_T = _bucket_np((np.arange(16)[:, None, None] * QBLOCK + np.arange(QBLOCK)[None, None, :])
                           - (np.arange(N_CMP_PAD)[None, :, None] * CMP_STRIDE + CMP_BLOCK - 1))
_WIN_BUCKET_T = _bucket_np(np.arange(QBLOCK)[None, :] + WINDOW - np.arange(WIN_KEYS)[:, None])
_OVERLAP_T = np.array([[1.0 if (n * CMP_STRIDE < j * SEL_BLOCK + SEL_BLOCK and n * CMP_STRIDE + CMP_BLOCK > j * SEL_BLOCK
                              and n < N_CMP_PAD - 1) else 0.0 for n in range(N_CMP_PAD)] for j in range(N_SB)], np.float32)
_SB_EXPAND = (np.arange(2048)[:, None] // SEL_BLOCK == np.arange(N_SB)[None, :]).astype(np.float32)


def _c_pre_kernel(cmp_ref, sel_ref, win_ref, kn_ref, pos_ref, w1_ref, w2_ref,
                  kvc_ref, ksel_ref, vselt_ref, kwin_ref, vwin_ref):
    f32, bf16 = jnp.float32, jnp.bfloat16
    d = C_HEAD_DIM

    def rms(x, g):
        return x * lax.rsqrt(jnp.mean(x * x, axis=-1, keepdims=True) + RMS_EPS) * g

    acc = [jnp.zeros((N_CMP_PAD, 2 * PHI_HIDDEN), f32) for _ in range(CMP_GROUPS)]
    for j in range(CMP_STRIDE):
        xs = cmp_ref[pl.ds(j, N_CMP_PAD, stride=CMP_STRIDE), :]
        for half in range(CMP_GROUPS):
            jj = half * CMP_STRIDE + j
            acc[half] = acc[half] + jnp.dot((xs + pos_ref[jj:jj + 1, :]).astype(bf16), w1_ref[jj],
                                            preferred_element_type=f32)
    hid = acc[0] + pltpu.roll(acc[1], N_CMP_PAD - 1, axis=0)
    kv = jnp.dot(jax.nn.silu(hid).astype(bf16), w2_ref[...], preferred_element_type=f32)
    kvc_ref[0, :, 0:d] = rms(kv[:, 0:d], kn_ref[0:1, :])
    kvc_ref[0, :, d:2 * d] = kv[:, d:2 * d]

    sel = sel_ref[...]
    ksel_ref[...] = rms(sel[:, 0:d], kn_ref[1:2, :])
    vselt_ref[0] = sel.T[d:2 * d, :]
    win = win_ref[...]
    kwin_ref[0, 0:WINDOW, :] = jnp.zeros((WINDOW, d), f32)
    vwin_ref[0, 0:WINDOW, :] = jnp.zeros((WINDOW, d), f32)
    kwin_ref[0, WINDOW:, :] = rms(win[:, 0:d], kn_ref[2:3, :])
    vwin_ref[0, WINDOW:, :] = win[:, d:2 * d]


def c_pre(proj, b, s, k_norm, cmp_pos, phi_w1, phi_w2):
    d = C_HEAD_DIM
    kv0 = OFF["c_kv"] // 128
    pos = jnp.concatenate([cmp_pos[0], cmp_pos[1]], axis=-1)
    w1 = phi_w1.reshape(2, CMP_BLOCK, d, PHI_HIDDEN)
    zero = jnp.zeros((CMP_BLOCK, d, PHI_HIDDEN), jnp.float32)
    w1c = jnp.concatenate([jnp.concatenate([w1[0], zero], axis=-1),
                           jnp.concatenate([zero, w1[1]], axis=-1)], axis=1).astype(jnp.bfloat16)
    z2 = jnp.zeros((PHI_HIDDEN, d), jnp.float32)
    w2c = jnp.concatenate([jnp.concatenate([phi_w2[0], z2], axis=-1),
                           jnp.concatenate([z2, phi_w2[1]], axis=-1)], axis=0).astype(jnp.bfloat16)
    full = lambda a: pl.BlockSpec(a.shape, lambda bb: (0,) * a.ndim)
    return pl.pallas_call(
        _c_pre_kernel,
        grid=(b,),
        in_specs=[
            pl.BlockSpec((s, 128), lambda bb: (bb, kv0)),
            pl.BlockSpec((s, 128), lambda bb: (bb, kv0 + 1)),
            pl.BlockSpec((s, 128), lambda bb: (bb, kv0 + 2)),
            full(k_norm), full(pos), full(w1c), full(w2c),
        ],
        out_specs=[
            pl.BlockSpec((1, N_CMP_PAD, 128), lambda bb: (bb, 0, 0)),
            pl.BlockSpec((s, d), lambda bb: (bb, 0)),
            pl.BlockSpec((1, d, s), lambda bb: (bb, 0, 0)),
            pl.BlockSpec((1, s + WINDOW, d), lambda bb: (bb, 0, 0)),
            pl.BlockSpec((1, s + WINDOW, d), lambda bb: (bb, 0, 0)),
        ],
        out_shape=[
            jax.ShapeDtypeStruct((b, N_CMP_PAD, 128), jnp.float32),
            jax.ShapeDtypeStruct((b * s, d), jnp.float32),
            jax.ShapeDtypeStruct((b, d, s), jnp.float32),
            jax.ShapeDtypeStruct((b, s + WINDOW, d), jnp.float32),
            jax.ShapeDtypeStruct((b, s + WINDOW, d), jnp.float32),
        ],
        compiler_params=pltpu.CompilerParams(dimension_semantics=("arbitrary",), vmem_limit_bytes=VMEM_LIMIT),
        name="c_pre",
    )(proj, proj, proj, k_norm, pos, w1c, w2c)


def _c_main_kernel(far_ref, cq_ref, z_ref, misct_ref, gq_ref, kvc_ref, ksel_ref, vselt_ref, kwin_ref, vwin_ref,
                   cmptab_ref, near_ref, farrow_ref, wintab_ref, ovt_ref, exp_ref, o_ref, ls_scr):
    s = ksel_ref.shape[0]
    d = C_HEAD_DIM
    hq = C_HEADS * QBLOCK
    i = pl.program_id(1)
    t0 = pl.multiple_of(i * QBLOCK, QBLOCK)
    f32, bf16 = jnp.float32, jnp.bfloat16
    nt = (((1,), (1,)), ((), ()))
    tn = (((0,), (0,)), ((), ()))

    cq = cq_ref[...]
    qs = jnp.concatenate([cq[:, h * d:(h + 1) * d] for h in range(C_HEADS)], axis=0)
    qs = qs * lax.rsqrt(jnp.mean(qs * qs, axis=-1, keepdims=True) + RMS_EPS) * gq_ref[...] * (d ** -0.5)
    qs = qs.astype(bf16)

    def softmax_rows(l, valid):
        l = jnp.where(valid, l, NEG_INF)
        m = jnp.max(l, axis=0, keepdims=True)
        p = jnp.where(valid, jnp.exp(l - m), 0.0)
        den = jnp.sum(p, axis=0, keepdims=True)
        return p, den

    kvc = kvc_ref[0]
    lc = lax.dot_general(kvc[:, 0:d].astype(bf16), qs, nt, preferred_element_type=f32) + cmptab_ref[0]
    n_idx = lax.broadcasted_iota(jnp.int32, (N_CMP_PAD, hq), 0)
    t_c = t0 + (lax.broadcasted_iota(jnp.int32, (N_CMP_PAD, hq), 1) & (QBLOCK - 1))
    cmp_valid = n_idx * CMP_STRIDE + (CMP_BLOCK - 1) <= t_c
    pc, den_c = softmax_rows(lc, cmp_valid)
    pc = pc * jnp.where(den_c > 0.0, 1.0 / den_c, 0.0)
    o_cmp = lax.dot_general(kvc[:, d:2 * d].astype(bf16), pc.astype(bf16), tn, preferred_element_type=f32)

    psum = pc[:, 0:QBLOCK]
    for h in range(1, C_HEADS):
        psum = psum + pc[:, h * QBLOCK:(h + 1) * QBLOCK]
    p_hi = psum.astype(bf16)
    p_lo = (psum - p_hi.astype(f32)).astype(bf16)
    ovt = ovt_ref[...]
    imp = jnp.dot(ovt, p_hi, preferred_element_type=f32) + jnp.dot(ovt, p_lo, preferred_element_type=f32)
    j_idx = lax.broadcasted_iota(jnp.int32, (N_SB, QBLOCK), 0)
    t_b = t0 + lax.broadcasted_iota(jnp.int32, (N_SB, QBLOCK), 1)
    cur = lax.shift_right_arithmetic(t_b, 6)
    forced = (j_idx == 0) | (j_idx == cur) | (j_idx == jnp.maximum(cur - 1, 0))
    imp = jnp.where(j_idx * SEL_BLOCK <= t_b, jnp.where(forced, FORCE_SCORE, imp), NEG_INF)
    rank = jnp.zeros((N_SB, QBLOCK), f32)
    for r in range(N_SB):
        row = imp[r:r + 1, :]
        rank = rank + jnp.where((row > imp) | ((row == imp) & (j_idx > r)), 1.0, 0.0)
    picked = jnp.where(rank < float(N_SEL), 1.0, 0.0).astype(bf16)
    kpos = lax.broadcasted_iota(jnp.int32, (s, QBLOCK), 0)
    tpos = t0 + lax.broadcasted_iota(jnp.int32, (s, QBLOCK), 1)
    sel_valid = (jnp.dot(exp_ref[...], picked, preferred_element_type=f32) > 0.5) & (kpos <= tpos)

    ls_scr[0:QBLOCK, :] = jnp.zeros((QBLOCK, hq), f32)
    ls_scr[QBLOCK:, :] = lax.dot_general(ksel_ref[...].astype(bf16), qs, nt, preferred_element_type=f32)
    ls_scr[pl.ds(t0, 2 * QBLOCK), :] += near_ref[...] - farrow_ref[...]
    o_sel = []
    for h in range(C_HEADS):
        p, den = softmax_rows(ls_scr[QBLOCK:, h * QBLOCK:(h + 1) * QBLOCK] + far_ref[h], sel_valid)
        o_sel.append(jnp.dot(vselt_ref[0].astype(bf16), p.astype(bf16), preferred_element_type=f32) / den)

    kw = kwin_ref[0, pl.ds(t0, WIN_KEYS), :]
    vw = vwin_ref[0, pl.ds(t0, WIN_KEYS), :]
    lw = lax.dot_general(kw.astype(bf16), qs, nt, preferred_element_type=f32) + wintab_ref[...]
    r_idx = lax.broadcasted_iota(jnp.int32, (WIN_KEYS, hq), 0)
    dist = (lax.broadcasted_iota(jnp.int32, (WIN_KEYS, hq), 1) & (QBLOCK - 1)) + WINDOW - r_idx
    win_valid = (dist >= 0) & (dist < WINDOW) & (r_idx + t0 >= WINDOW)
    pw, den_w = softmax_rows(lw, win_valid)
    o_win = lax.dot_general(vw.astype(bf16), pw.astype(bf16), tn, preferred_element_type=f32) / den_w

    g = jax.nn.sigmoid(misct_ref[0, 80:80 + 3 * C_HEADS, :])
    outs = []
    for h in range(C_HEADS):
        cols = slice(h * QBLOCK, (h + 1) * QBLOCK)
        outs.append(g[3 * h:3 * h + 1, :] * o_cmp[:, cols] + g[3 * h + 1:3 * h + 2, :] * o_sel[h]
                    + g[3 * h + 2:3 * h + 3, :] * o_win[:, cols])
    o_ref[...] = jnp.concatenate(outs, axis=0).T * jax.nn.silu(z_ref[...])


def c_main(proj, misct, gq, kvc, ksel, vselt, kwin, vwin, cmptab, near, farrow, wintab, far, b, s):
    n = b * s
    nq = s // QBLOCK
    d = C_HEAD_DIM
    hq = C_HEADS * QBLOCK
    row = lambda bb, i: bb * nq + i
    ovt = jnp.asarray(_OVERLAP_T, jnp.bfloat16)
    expand = jnp.asarray(_SB_EXPAND, jnp.bfloat16)
    const = lambda a: pl.BlockSpec(a.shape, lambda bb, i: (0,) * a.ndim)
    return pl.pallas_call(
        _c_main_kernel,
        grid=(b, nq),
        in_specs=[
            pl.BlockSpec(memory_space=pltpu.SMEM),
            pl.BlockSpec((QBLOCK, C_WIDTH), lambda bb, i: (row(bb, i), OFF["c_q"] // C_WIDTH)),
            pl.BlockSpec((QBLOCK, C_WIDTH), lambda bb, i: (row(bb, i), OFF["c_z"] // C_WIDTH)),
            pl.BlockSpec((1, 128, QBLOCK), lambda bb, i: (bb, 0, i)),
            const(gq),
            pl.BlockSpec((1, N_CMP_PAD, 128), lambda bb, i: (bb, 0, 0)),
            pl.BlockSpec((s, d), lambda bb, i: (bb, 0)),
            pl.BlockSpec((1, d, s), lambda bb, i: (bb, 0, 0)),
            pl.BlockSpec((1, s + WINDOW, d), lambda bb, i: (bb, 0, 0)),
            pl.BlockSpec((1, s + WINDOW, d), lambda bb, i: (bb, 0, 0)),
            pl.BlockSpec((1, N_CMP_PAD, hq), lambda bb, i: (i, 0, 0)),
            const(near), const(farrow), const(wintab), const(ovt), const(expand),
        ],
        out_specs=pl.BlockSpec((QBLOCK, C_WIDTH), lambda bb, i: (row(bb, i), 0)),
        out_shape=jax.ShapeDtypeStruct((n, C_WIDTH), jnp.float32),
        scratch_shapes=[pltpu.VMEM((s + QBLOCK, hq), jnp.float32)],
        compiler_params=pltpu.CompilerParams(
            dimension_semantics=("arbitrary", "arbitrary"), vmem_limit_bytes=VMEM_LIMIT),
        name="c_main",
    )(far, proj, proj, misct, gq, kvc, ksel, vselt, kwin, vwin, cmptab, near, farrow, wintab, ovt, expand)


def _head_cols(tab):
    return jnp.moveaxis(tab, -1, -2).reshape(*tab.shape[:-2], tab.shape[-1] * tab.shape[-2])


def mixer_c(proj, misct, b, s, gq, k_norm, cmp_pos, phi_w1, phi_w2, bias_c):
    kvc, ksel, vselt, kwin, vwin = c_pre(proj, b, s, k_norm, cmp_pos, phi_w1, phi_w2)
    cmptab = _head_cols(bias_c[jnp.asarray(_CMP_BUCKET_T)])
    near = _head_cols(bias_c[jnp.asarray(_NEAR_BUCKET_T)])
    wintab = _head_cols(bias_c[jnp.asarray(_WIN_BUCKET_T)])
    far = bias_c[_FAR_BUCKET]
    farrow = jnp.repeat(far, QBLOCK).reshape(1, C_HEADS * QBLOCK)
    return c_main(proj, misct, gq.reshape(1, C_HEAD_DIM), kvc, ksel, vselt, kwin, vwin, cmptab, near, farrow,
                  wintab, far, b, s)


B_HPS = 2
B_A_LANE = 72
B_B_LANE = 76


def _b_kernel(alog_ref, dtb_ref, q_ref, k_ref, v_ref, z_ref, misc_ref, arow_ref, cwq_ref, cwk_ref, cwv_ref,
              gon_ref, o_ref, qs, ks, vs, gtok, btok, gcrow):
    s = q_ref.shape[0]
    dh = B_HEAD_DIM
    c_len = GDN_CHUNK
    j = pl.program_id(1)
    f32, bf16 = jnp.float32, jnp.bfloat16
    hi = lax.Precision.HIGHEST
    nt = (((1,), (1,)), ((), ()))
    tn = (((0,), (0,)), ((), ()))

    rows = lax.broadcasted_iota(jnp.int32, (s, B_HPS * dh), 0)

    def conv_silu(x_ref, w_ref):
        x = x_ref[...]
        acc = x * w_ref[CONV_WIDTH - 1:CONV_WIDTH, :]
        for k in range(1, CONV_WIDTH):
            shifted = jnp.where(rows >= k, pltpu.roll(x, k, axis=0), 0.0)
            acc = acc + shifted * w_ref[CONV_WIDTH - 1 - k:CONV_WIDTH - k, :]
        return jax.nn.silu(acc)

    def l2n(x):
        return x * lax.rsqrt(jnp.sum(x * x, axis=-1, keepdims=True) + RMS_EPS)

    qc = conv_silu(q_ref, cwq_ref)
    kc = conv_silu(k_ref, cwk_ref)
    vs[...] = conv_silu(v_ref, cwv_ref)
    misc = misc_ref[...]
    lane = lax.broadcasted_iota(jnp.int32, misc.shape, 1)
    ri = lax.broadcasted_iota(jnp.int32, (c_len, c_len), 0)
    ci = lax.broadcasted_iota(jnp.int32, (c_len, c_len), 1)
    lower = ci <= ri
    strict = ci < ri
    tri_l = jnp.where(lower, 1.0, 0.0)
    tri_u = jnp.where(ri <= ci, 1.0, 0.0)
    for hh in range(B_HPS):
        sl = slice(hh * dh, (hh + 1) * dh)
        qs[:, sl] = l2n(qc[:, sl]) * (dh ** -0.5)
        ks[:, sl] = l2n(kc[:, sl])
        h = B_HPS * j + hh
        neg_a = -jnp.exp(alog_ref[h])
        dtb = dtb_ref[h]
        a_tok = jnp.sum(jnp.where(lane == B_A_LANE + h, misc, 0.0), axis=-1, keepdims=True)
        b_tok = jnp.sum(jnp.where(lane == B_B_LANE + h, misc, 0.0), axis=-1, keepdims=True)
        gtok[hh] = jnp.broadcast_to(neg_a * jax.nn.softplus(a_tok + dtb), (s, dh))
        btok[hh] = jnp.broadcast_to(jax.nn.sigmoid(b_tok), (s, dh))
        g_row = neg_a * jax.nn.softplus(arow_ref[0, hh] + dtb)
        gcrow[hh] = jnp.dot(g_row, tri_u, precision=hi, preferred_element_type=f32)

    def chunk(c, states):
        r0 = pl.multiple_of(c * c_len, c_len)
        rs = pl.ds(r0, c_len)
        new_states = []
        for hh in range(B_HPS):
            sl = slice(hh * dh, (hh + 1) * dh)
            q_c, k_c, v_c = qs[rs, sl], ks[rs, sl], vs[rs, sl]
            gb = jnp.dot(tri_l, gtok[hh, rs, :], precision=hi, preferred_element_type=f32)
            beta = btok[hh, rs, :]
            g_row = gcrow[hh, pl.ds(c, 1), :]
            decay = jnp.where(lower, jnp.exp(jnp.where(lower, gb[:, 0:c_len] - g_row, 0.0)), 0.0)
            kb = k_c * beta
            kk = lax.dot_general(kb.astype(bf16), k_c.astype(bf16), nt, preferred_element_type=f32)
            p = -jnp.where(strict, kk * decay, 0.0)
            eg = jnp.exp(gb)
            x = jnp.concatenate([v_c * beta, kb * eg], axis=1)
            x = x + jnp.dot(p, x, precision=hi, preferred_element_type=f32)
            for _ in range(5):
                p = jnp.dot(p, p, precision=hi, preferred_element_type=f32)
                x = x + jnp.dot(p, x, precision=hi, preferred_element_type=f32)
            u, w = x[:, 0:dh], x[:, dh:2 * dh]
            attn = jnp.where(lower, lax.dot_general(q_c.astype(bf16), k_c.astype(bf16), nt,
                                                    preferred_element_type=f32) * decay, 0.0)
            g_last = g_row[:, c_len - 1:c_len]
            q_dec = q_c * eg
            k_dec = k_c * jnp.exp(g_last - gb)
            st = states[hh]
            st_b = st.astype(bf16)
            v_new = u - jnp.dot(w.astype(bf16), st_b, preferred_element_type=f32)
            o = (jnp.dot(q_dec.astype(bf16), st_b, preferred_element_type=f32)
                 + jnp.dot(attn.astype(bf16), v_new.astype(bf16), preferred_element_type=f32))
            new_states.append(st * jnp.exp(g_last)
                              + lax.dot_general(k_dec.astype(bf16), v_new.astype(bf16), tn,
                                                preferred_element_type=f32))
            o = o * lax.rsqrt(jnp.mean(o * o, axis=-1, keepdims=True) + RMS_EPS) * gon_ref[...]
            o_ref[rs, sl] = o * jax.nn.silu(z_ref[rs, sl])
        return tuple(new_states)

    lax.fori_loop(0, s // c_len, chunk, tuple(jnp.zeros((dh, dh), f32) for _ in range(B_HPS)))


def mixer_b(proj, misct, b, s, conv_w, a_log, dt_bias, out_norm):
    n = b * s
    w2 = B_HPS * B_HEAD_DIM
    nc = s // GDN_CHUNK
    arow = misct[:, B_A_LANE:B_A_LANE + B_HEADS, :].reshape(b, B_HEADS, nc, GDN_CHUNK)
    qkv0 = OFF["b_qkv"] // w2
    kstep = B_WIDTH // w2
    smem = pl.BlockSpec(memory_space=pltpu.SMEM)
    return pl.pallas_call(
        _b_kernel,
        grid=(b, B_HEADS // B_HPS),
        in_specs=[
            smem, smem,
            pl.BlockSpec((s, w2), lambda bb, j: (bb, qkv0 + j)),
            pl.BlockSpec((s, w2), lambda bb, j: (bb, qkv0 + kstep + j)),
            pl.BlockSpec((s, w2), lambda bb, j: (bb, qkv0 + 2 * kstep + j)),
            pl.BlockSpec((s, w2), lambda bb, j: (bb, OFF["b_z"] // w2 + j)),
            pl.BlockSpec((s, 128), lambda bb, j: (bb, MISC_OFF // 128)),
            pl.BlockSpec((1, B_HPS, nc, GDN_CHUNK), lambda bb, j: (bb, j, 0, 0)),
            pl.BlockSpec((CONV_WIDTH, w2), lambda bb, j: (0, j)),
            pl.BlockSpec((CONV_WIDTH, w2), lambda bb, j: (0, kstep + j)),
            pl.BlockSpec((CONV_WIDTH, w2), lambda bb, j: (0, 2 * kstep + j)),
            pl.BlockSpec((1, B_HEAD_DIM), lambda bb, j: (0, 0)),
        ],
        out_specs=pl.BlockSpec((s, w2), lambda bb, j: (bb, j)),
        out_shape=jax.ShapeDtypeStruct((n, B_WIDTH), jnp.float32),
        scratch_shapes=[
            pltpu.VMEM((s, w2), jnp.float32), pltpu.VMEM((s, w2), jnp.float32), pltpu.VMEM((s, w2), jnp.float32),
            pltpu.VMEM((B_HPS, s, B_HEAD_DIM), jnp.float32), pltpu.VMEM((B_HPS, s, B_HEAD_DIM), jnp.float32),
            pltpu.VMEM((B_HPS, nc, GDN_CHUNK), jnp.float32),
        ],
        compiler_params=pltpu.CompilerParams(
            dimension_semantics=("arbitrary", "arbitrary"), vmem_limit_bytes=VMEM_LIMIT),
        name="gdn",
    )(a_log, dt_bias, proj, proj, proj, proj, proj, arow, conv_w, conv_w, conv_w, out_norm.reshape(1, B_HEAD_DIM))


def _rms_norm(x, g):
    y = x * lax.rsqrt(jnp.mean(x * x, axis=-1, keepdims=True) + RMS_EPS)
    return y * g


def _l2norm(x):
    return x * lax.rsqrt(jnp.sum(x * x, axis=-1, keepdims=True) + RMS_EPS)


def _masked_softmax(logits, mask):
    logits = jnp.where(mask, logits, NEG_INF)
    return jnp.where(mask, jax.nn.softmax(logits, axis=-1), 0.0)


def _t5_bucket(dist):
    dist = jnp.maximum(dist, 0)
    log_ratio = jnp.log(jnp.maximum(dist, 1).astype(jnp.float32) / BUCKET_MAX_EXACT) / math.log(BUCKET_MAX_DIST / BUCKET_MAX_EXACT)
    large = BUCKET_MAX_EXACT + (log_ratio * (N_BUCKETS - BUCKET_MAX_EXACT)).astype(jnp.int32)
    large = jnp.minimum(large, N_BUCKETS - 1)
    return jnp.where(dist < BUCKET_MAX_EXACT, dist, large)


def _to_qblocks(a):
    b, s = a.shape[:2]
    return jnp.moveaxis(a.reshape(b, s // QBLOCK, QBLOCK, *a.shape[2:]), 1, 0)


def _from_qblocks(a):
    nb, b = a.shape[:2]
    return jnp.moveaxis(a, 0, 1).reshape(b, nb * QBLOCK, *a.shape[3:])


def _dsa_mixer(q, k, v, q_idx, k_idx, w_idx, bias_tab):
    b, s = q.shape[:2]
    topk = min(TOPK_MAX, s // 4)
    bidx = jnp.arange(b)[:, None, None]
    key_pos = jnp.arange(s)
    scale = A_HEAD_DIM ** -0.5

    def block(args):
        qb, qib, wb, start = args
        t = start + jnp.arange(QBLOCK)
        idx_logits = jnp.einsum('bqhd,bsd->bqhs', qib, k_idx) * (IDX_DIM ** -0.5)
        score = jnp.einsum('bqh,bqhs->bqs', wb * (IDX_HEADS ** -0.5), jax.nn.relu(idx_logits))
        score = jnp.where(key_pos[None, None, :] <= t[None, :, None], score, NEG_INF)
        _, sel = lax.top_k(score, topk)
        kg = k[bidx, sel]
        vg = v[bidx, sel]
        dist = t[None, :, None] - sel
        bias = jnp.transpose(bias_tab[_t5_bucket(dist)], (0, 3, 1, 2))
        logits = jnp.einsum('bqhd,bqkhd->bhqk', qb, kg) * scale + bias
        p = _masked_softmax(logits, (dist >= 0)[:, None])
        return jnp.einsum('bhqk,bqkhd->bqhd', p, vg)

    starts = jnp.arange(s // QBLOCK, dtype=jnp.int32) * QBLOCK
    out = lax.map(block, (_to_qblocks(q), _to_qblocks(q_idx), _to_qblocks(w_idx), starts))
    return _from_qblocks(out)


def _causal_depthwise_conv(x, w):
    c = x.shape[-1]
    return lax.conv_general_dilated(x, w[:, None, :], window_strides=(1,), padding=[(CONV_WIDTH - 1, 0)],
                                    dimension_numbers=('NWC', 'WIO', 'NWC'), feature_group_count=c)


def _gdn_mixer(qkv, a_in, b_in, conv_w, a_log, dt_bias):
    b, s = qkv.shape[:2]
    n = s // GDN_CHUNK
    qkv = jax.nn.silu(_causal_depthwise_conv(qkv, conv_w))
    q, k, v = jnp.split(qkv, 3, axis=-1)
    sh = lambda t: t.reshape(b, s, B_HEADS, B_HEAD_DIM)
    q = _l2norm(sh(q)) * (B_HEAD_DIM ** -0.5)
    k = _l2norm(sh(k))
    v = sh(v)
    beta = jax.nn.sigmoid(b_in)
    g = -jnp.exp(a_log) * jax.nn.softplus(a_in + dt_bias)

    def chunk(t):
        return jnp.moveaxis(t.reshape(b, n, GDN_CHUNK, *t.shape[2:]), 3, 2)

    q, k, v, beta, g = (chunk(t) for t in (q, k, v, beta, g))
    g_cum = jnp.cumsum(g, axis=-1)
    g_last = g_cum[..., -1]
    lower = jnp.tril(jnp.ones((GDN_CHUNK, GDN_CHUNK), dtype=bool))
    strict = jnp.tril(jnp.ones((GDN_CHUNK, GDN_CHUNK), dtype=bool), -1)
    diff = g_cum[..., :, None] - g_cum[..., None, :]
    decay = jnp.where(lower, jnp.exp(jnp.where(lower, diff, 0.0)), 0.0)
    k_beta = k * beta[..., None]
    a_mat = jnp.where(strict, jnp.einsum('bnhid,bnhjd->bnhij', k_beta, k) * decay, 0.0)
    eye = jnp.eye(GDN_CHUNK, dtype=jnp.float32)
    t_inv = lax.linalg.triangular_solve(eye + a_mat, jnp.broadcast_to(eye, a_mat.shape), left_side=True, lower=True)
    u = jnp.einsum('bnhij,bnhjd->bnhid', t_inv, v * beta[..., None])
    w = jnp.einsum('bnhij,bnhjd->bnhid', t_inv, k_beta * jnp.exp(g_cum)[..., None])
    attn = jnp.einsum('bnhid,bnhjd->bnhij', q, k) * decay
    q_dec = q * jnp.exp(g_cum)[..., None]
    k_dec = k * jnp.exp(g_last[..., None] - g_cum)[..., None]

    def step(state, xs):
        u_c, w_c, q_c, k_c, attn_c, gl_c = xs
        v_new = u_c - jnp.einsum('bhik,bhkv->bhiv', w_c, state)
        o = jnp.einsum('bhik,bhkv->bhiv', q_c, state) + jnp.einsum('bhij,bhjv->bhiv', attn_c, v_new)
        state = state * jnp.exp(gl_c)[..., None, None] + jnp.einsum('bhik,bhiv->bhkv', k_c, v_new)
        return state, o

    state0 = jnp.zeros((b, B_HEADS, B_HEAD_DIM, B_HEAD_DIM), jnp.float32)
    xs = tuple(jnp.moveaxis(t, 1, 0) for t in (u, w, q_dec, k_dec, attn, g_last))
    _, o = lax.scan(step, state0, xs)
    o = jnp.moveaxis(jnp.moveaxis(o, 0, 1), 2, 3)
    return o.reshape(b, s, B_HEADS, B_HEAD_DIM)


def _compress_blocks(tok, pos, w1, w2):
    b, s, d = tok.shape
    n_cmp = (s - CMP_BLOCK) // CMP_STRIDE + 1
    idx = jnp.arange(n_cmp)[:, None] * CMP_STRIDE + jnp.arange(CMP_BLOCK)[None, :]
    blocks = tok[:, idx] + pos
    return jax.nn.silu(blocks.reshape(b, n_cmp, CMP_BLOCK * d) @ w1) @ w2


def _nsa_mixer(q, kv, gates, k_norm, cmp_pos, phi_w1, phi_w2, bias_tab):
    b, s = q.shape[:2]
    scale = C_HEAD_DIM ** -0.5
    t_all = jnp.arange(s)
    k_cmp, v_cmp, k_sel, v_sel, k_win, v_win = (kv[:, :, i] for i in range(6))
    k_sel = _rms_norm(k_sel, k_norm[1])
    k_win = _rms_norm(k_win, k_norm[2])
    kc = _rms_norm(_compress_blocks(k_cmp, cmp_pos[0], phi_w1[0], phi_w2[0]), k_norm[0])
    vc = _compress_blocks(v_cmp, cmp_pos[1], phi_w1[1], phi_w2[1])
    n_cmp = kc.shape[1]
    cmp_start = jnp.arange(n_cmp) * CMP_STRIDE
    cmp_end = cmp_start + CMP_BLOCK - 1
    cmp_valid = cmp_end[None, :] <= t_all[:, None]
    cmp_bias = jnp.transpose(bias_tab[_t5_bucket(t_all[:, None] - cmp_end[None, :])], (2, 0, 1))
    logits = jnp.einsum('bqhd,bnd->bhqn', q, kc) * scale + cmp_bias
    p_cmp = _masked_softmax(logits, cmp_valid[None, None])
    o_cmp = jnp.einsum('bhqn,bnd->bqhd', p_cmp, vc)
    n_sb = s // SEL_BLOCK
    n_pick = min(N_SEL, n_sb)
    sb_start = jnp.arange(n_sb) * SEL_BLOCK
    overlap = ((cmp_start[:, None] < sb_start[None, :] + SEL_BLOCK) & (cmp_start[:, None] + CMP_BLOCK > sb_start[None, :])).astype(jnp.float32)
    importance = jnp.einsum('bhqn,nj->bqj', p_cmp, overlap)
    cur = t_all // SEL_BLOCK
    blk = jnp.arange(n_sb)
    forced = (blk[None, :] == 0) | (blk[None, :] == cur[:, None]) | (blk[None, :] == jnp.maximum(cur[:, None] - 1, 0))
    admissible = sb_start[None, :] <= t_all[:, None]
    importance = jnp.where(admissible[None], jnp.where(forced[None], FORCE_SCORE, importance), NEG_INF)
    _, sel = lax.top_k(importance, n_pick)
    k_blocks = k_sel.reshape(b, n_sb, SEL_BLOCK, C_HEAD_DIM)
    v_blocks = v_sel.reshape(b, n_sb, SEL_BLOCK, C_HEAD_DIM)
    k_win_pad = jnp.pad(k_win, ((0, 0), (WINDOW, 0), (0, 0)))
    v_win_pad = jnp.pad(v_win, ((0, 0), (WINDOW, 0), (0, 0)))
    bidx = jnp.arange(b)[:, None, None]
    n_keys = n_pick * SEL_BLOCK

    def block(args):
        qb, selb, start = args
        t = start + jnp.arange(QBLOCK)
        kg = k_blocks[bidx, selb].reshape(b, QBLOCK, n_keys, C_HEAD_DIM)
        vg = v_blocks[bidx, selb].reshape(b, QBLOCK, n_keys, C_HEAD_DIM)
        s_pos = (selb[..., None] * SEL_BLOCK + jnp.arange(SEL_BLOCK)).reshape(b, QBLOCK, n_keys)
        dist = t[None, :, None] - s_pos
        bias = jnp.transpose(bias_tab[_t5_bucket(dist)], (0, 3, 1, 2))
        lg = jnp.einsum('bqhd,bqkd->bhqk', qb, kg) * scale + bias
        p = _masked_softmax(lg, (dist >= 0)[:, None])
        o_sel = jnp.einsum('bhqk,bqkd->bqhd', p, vg)
        kw = lax.dynamic_slice_in_dim(k_win_pad, start, QBLOCK + WINDOW, axis=1)
        vw = lax.dynamic_slice_in_dim(v_win_pad, start, QBLOCK + WINDOW, axis=1)
        s_w = start - WINDOW + jnp.arange(QBLOCK + WINDOW)
        dist_w = t[:, None] - s_w[None, :]
        valid_w = (s_w[None, :] >= 0) & (dist_w >= 0) & (dist_w < WINDOW)
        bias_w = jnp.transpose(bias_tab[_t5_bucket(dist_w)], (2, 0, 1))
        lw = jnp.einsum('bqhd,bsd->bhqs', qb, kw) * scale + bias_w
        pw = _masked_softmax(lw, valid_w[None, None])
        o_win = jnp.einsum('bhqs,bsd->bqhd', pw, vw)
        return o_sel, o_win

    starts = jnp.arange(s // QBLOCK, dtype=jnp.int32) * QBLOCK
    o_sel, o_win = lax.map(block, (_to_qblocks(q), _to_qblocks(sel), starts))
    o_sel = _from_qblocks(o_sel)
    o_win = _from_qblocks(o_win)
    g = jax.nn.sigmoid(gates).reshape(b, s, C_HEADS, 3)
    return g[..., 0:1] * o_cmp + g[..., 1:2] * o_sel + g[..., 2:3] * o_win


def _field(proj3, name):
    return proj3[..., OFF[name]:OFF[name] + _ORIG[name][1]]


def kernel(x, norm_g, w_in, a_kv_norm, a_w_ukv, a_q_norm, a_k_norm, b_conv, b_a_log, b_dt_bias, b_out_norm, c_q_norm, c_k_norm, c_cmp_pos, c_phi_w1, c_phi_w2, w_branch, w_out, rel_bias):
    b, s, _ = x.shape
    n = b * s
    bias_a = rel_bias[:, :A_HEADS]
    bias_c = rel_bias[:, A_HEADS:]
    xf = x.reshape(n, D_MODEL)
    perm = jnp.asarray(_PERM_IDX)
    valid = jnp.asarray(_PERM_VALID)
    sh = lambda t, h, d: t.reshape(b, s, h, d)
    for l in range(DEPTH):
        w_p = jnp.where(valid[None, :], w_in[l][:, perm], 0.0).astype(jnp.bfloat16)
        proj = in_proj(xf, norm_g[l], w_p)
        p3 = proj.reshape(b, s, N_PAD)

        misct = jnp.transpose(p3[..., MISC_OFF:], (0, 2, 1))
        y_a = mixer_a(proj, misct, b, s, a_kv_norm[l], a_w_ukv[l], a_q_norm[l], a_k_norm[l], bias_a)

        y_b = mixer_b(proj, misct, b, s, b_conv[l], b_a_log[l], b_dt_bias[l], b_out_norm[l])
        y_c = mixer_c(proj, misct, b, s, c_q_norm[l], c_k_norm[l], c_cmp_pos[l], c_phi_w1[l], c_phi_w2[l], bias_c)

        wbr = w_branch[l].astype(jnp.bfloat16)
        xf = merge(xf, proj, y_a, y_b, y_c, wbr[:A_WIDTH], wbr[A_WIDTH:A_WIDTH + B_WIDTH],
                   wbr[A_WIDTH + B_WIDTH:], w_out[l].astype(jnp.bfloat16))
    return xf.reshape(b, s, D_MODEL)
```

```python
import functools
import math

import jax
import jax.numpy as jnp
import numpy as np
from jax import lax
from jax.experimental import pallas as pl
from jax.experimental.pallas import tpu as pltpu

D_MODEL = 1024
DEPTH = 4
QBLOCK = 128
NEG_INF = -1e30
FORCE_SCORE = 1e9
RMS_EPS = 1e-6

A_HEADS = 4
A_HEAD_DIM = 64
A_WIDTH = A_HEADS * A_HEAD_DIM
A_KV_LATENT = 128
IDX_HEADS = 8
IDX_DIM = 64
TOPK_MAX = 256

B_HEADS = 4
B_HEAD_DIM = 128
B_WIDTH = B_HEADS * B_HEAD_DIM
CONV_WIDTH = 4
GDN_CHUNK = 64

C_HEADS = 4
C_HEAD_DIM = 64
C_WIDTH = C_HEADS * C_HEAD_DIM
CMP_BLOCK = 32
CMP_STRIDE = 16
SEL_BLOCK = 64
N_SEL = 16
WINDOW = 512
PHI_HIDDEN = 256

N_BUCKETS = 32
BUCKET_MAX_EXACT = 16
BUCKET_MAX_DIST = 128

VMEM_LIMIT = 48 * 1024 * 1024

_ORIG = {}
_o = 0
for _name, _w in (("a_q", 256), ("a_ckv", 128), ("a_qi", 512), ("a_ki", 64), ("a_wi", 8), ("a_z", 256),
                  ("b_qkv", 1536), ("b_a", 4), ("b_b", 4), ("b_z", 512),
                  ("c_q", 256), ("c_kv", 384), ("c_g", 12), ("c_z", 256), ("g", 3072)):
    _ORIG[_name] = (_o, _w)
    _o += _w
N_IN = _o

_NEW_ORDER = ("g", "b_qkv", "b_z", "a_qi", "a_q", "a_z", "c_q", "c_z", "c_kv", "a_ckv",
              "a_ki", "a_wi", "b_a", "b_b", "c_g")
OFF = {}
_o = 0
for _name in _NEW_ORDER:
    OFF[_name] = _o
    _o += _ORIG[_name][1]
MISC_OFF = OFF["a_ki"]
N_PAD = 7296
assert MISC_OFF == 7168 and _o <= N_PAD


def _perm_indices():
    idx = np.zeros((N_PAD,), np.int32)
    valid = np.zeros((N_PAD,), bool)
    for name in _NEW_ORDER:
        o_old, w = _ORIG[name]
        idx[OFF[name]:OFF[name] + w] = np.arange(o_old, o_old + w)
        valid[OFF[name]:OFF[name] + w] = True
    return idx, valid


_PERM_IDX, _PERM_VALID = _perm_indices()


IN_TM = 512
IN_TN = N_PAD // 3


def _in_proj_kernel(x_ref, g_ref, w_ref, o_ref):
    x = x_ref[...]
    ms = jnp.mean(x * x, axis=-1, keepdims=True)
    h = (x * lax.rsqrt(ms + RMS_EPS)) * g_ref[...]
    o_ref[...] = jnp.dot(h.astype(jnp.bfloat16), w_ref[...], preferred_element_type=jnp.float32)


def in_proj(xf, g, w_bf16):
    n = xf.shape[0]
    return pl.pallas_call(
        _in_proj_kernel,
        grid=(N_PAD // IN_TN, n // IN_TM),
        in_specs=[
            pl.BlockSpec((IN_TM, D_MODEL), lambda j, i: (i, 0)),
            pl.BlockSpec((1, D_MODEL), lambda j, i: (0, 0)),
            pl.BlockSpec((D_MODEL, IN_TN), lambda j, i: (0, j)),
        ],
        out_specs=pl.BlockSpec((IN_TM, IN_TN), lambda j, i: (i, j)),
        out_shape=jax.ShapeDtypeStruct((n, N_PAD), jnp.float32),
        compiler_params=pltpu.CompilerParams(
            dimension_semantics=("arbitrary", "arbitrary"), vmem_limit_bytes=VMEM_LIMIT),
        name="in_proj",
    )(xf, g.reshape(1, D_MODEL), w_bf16)


MG_TM = 256


def _merge_kernel(x_ref, g_ref, ya_ref, yb_ref, yc_ref, wa_ref, wb_ref, wc_ref, wo_ref, o_ref):
    def branch(y_ref, w_ref, k):
        p = jnp.dot(y_ref[...].astype(jnp.bfloat16), w_ref[...], preferred_element_type=jnp.float32)
        return jax.nn.sigmoid(g_ref[:, k * D_MODEL:(k + 1) * D_MODEL]) * p

    merged = branch(ya_ref, wa_ref, 0) + branch(yb_ref, wb_ref, 1) + branch(yc_ref, wc_ref, 2)
    o_ref[...] = x_ref[...] + jnp.dot(merged.astype(jnp.bfloat16), wo_ref[...],
                                      preferred_element_type=jnp.float32)


def merge(xf, proj, ya, yb, yc, wa, wb, wc, wo):
    n = xf.shape[0]
    row = lambda w: pl.BlockSpec((MG_TM, w), lambda i: (i, 0))
    full = lambda a: pl.BlockSpec(a.shape, lambda i: (0, 0))
    return pl.pallas_call(
        _merge_kernel,
        grid=(n // MG_TM,),
        in_specs=[row(D_MODEL), row(3 * D_MODEL), row(A_WIDTH), row(B_WIDTH), row(C_WIDTH),
                  full(wa), full(wb), full(wc), full(wo)],
        out_specs=row(D_MODEL),
        out_shape=jax.ShapeDtypeStruct((n, D_MODEL), jnp.float32),
        compiler_params=pltpu.CompilerParams(
            dimension_semantics=("arbitrary",), vmem_limit_bytes=VMEM_LIMIT),
        name="merge",
    )(xf, proj, ya, yb, yc, wa, wb, wc, wo)


def _bucket_np(dist):
    d = np.maximum(np.asarray(dist, np.int64), 0)
    ratio = np.log(np.maximum(d, 1).astype(np.float64) / BUCKET_MAX_EXACT) / math.log(BUCKET_MAX_DIST / BUCKET_MAX_EXACT)
    scaled = ratio * (N_BUCKETS - BUCKET_MAX_EXACT)
    frac = scaled - np.floor(scaled)
    edge = (d > BUCKET_MAX_EXACT) & (d < BUCKET_MAX_DIST) & ((frac < 1e-4) | (frac > 1 - 1e-4))
    assert not edge.any(), "bucket boundary too close to an integer distance"
    large = np.minimum(BUCKET_MAX_EXACT + np.floor(scaled + 1e-9).astype(np.int64), N_BUCKETS - 1)
    return np.where(d < BUCKET_MAX_EXACT, d, large).astype(np.int32)


_NEAR_BUCKET_T = _bucket_np(np.arange(QBLOCK)[None, :] + QBLOCK - np.arange(2 * QBLOCK)[:, None])
_FAR_BUCKET = int(_bucket_np(np.array([BUCKET_MAX_DIST]))[0])
assert (_bucket_np(np.arange(BUCKET_MAX_DIST, 4096)) == _FAR_BUCKET).all()

INT_MIN = -2 ** 31


def _bias_lookup(table, buckets):
    idx = jnp.asarray(buckets)[..., None]
    out = jnp.zeros(idx.shape[:-1] + (table.shape[1],), table.dtype)
    for bkt in range(N_BUCKETS):
        out = jnp.where(idx == bkt, table[bkt], out)
    return out


A_PRE_TM = 512


def _a_pre_kernel(ckv_ref, q_ref, gkv_ref, wukv_ref, gq_ref, gk_ref, hm_ref, kn_ref, vt_ref, qn_ref):
    c = ckv_ref[...]
    c = c * lax.rsqrt(jnp.mean(c * c, axis=-1, keepdims=True) + RMS_EPS) * gkv_ref[...]
    kv = jnp.dot(c.astype(jnp.bfloat16), wukv_ref[...], preferred_element_type=jnp.float32)
    hm = hm_ref[...]

    def head_rms(x, g):
        ms = jnp.dot(x * x, hm, precision=lax.Precision.HIGHEST, preferred_element_type=jnp.float32)
        return x * lax.rsqrt(ms + RMS_EPS) * g

    kn_ref[...] = head_rms(kv[:, :A_WIDTH], gk_ref[...])
    qn_ref[...] = head_rms(q_ref[...], gq_ref[...]) * (A_HEAD_DIM ** -0.5)
    vt_ref[0] = kv[:, A_WIDTH:].T


def a_pre(proj, b, s, gkv, wukv_bf16, gq, gk):
    n = b * s
    nt = s // A_PRE_TM
    hm = jnp.asarray(np.kron(np.eye(A_HEADS), np.ones((A_HEAD_DIM, A_HEAD_DIM))) / A_HEAD_DIM, jnp.float32)
    row = lambda bb, j: (bb * nt + j, 0)
    full = lambda a: pl.BlockSpec(a.shape, lambda bb, j: (0,) * a.ndim)
    gq = jnp.tile(gq, A_HEADS).reshape(1, A_WIDTH)
    gk = jnp.tile(gk, A_HEADS).reshape(1, A_WIDTH)
    gkv = gkv.reshape(1, A_KV_LATENT)
    return pl.pallas_call(
        _a_pre_kernel,
        grid=(b, nt),
        in_specs=[
            pl.BlockSpec((A_PRE_TM, A_KV_LATENT), lambda bb, j: (bb * nt + j, OFF["a_ckv"] // A_KV_LATENT)),
            pl.BlockSpec((A_PRE_TM, A_WIDTH), lambda bb, j: (bb * nt + j, OFF["a_q"] // A_WIDTH)),
            full(gkv), full(wukv_bf16), full(gq), full(gk), full(hm),
        ],
        out_specs=[
            pl.BlockSpec((A_PRE_TM, A_WIDTH), row),
            pl.BlockSpec((1, A_WIDTH, A_PRE_TM), lambda bb, j: (bb, 0, j)),
            pl.BlockSpec((A_PRE_TM, A_WIDTH), row),
        ],
        out_shape=[
            jax.ShapeDtypeStruct((n, A_WIDTH), jnp.float32),
            jax.ShapeDtypeStruct((b, A_WIDTH, s), jnp.float32),
            jax.ShapeDtypeStruct((n, A_WIDTH), jnp.float32),
        ],
        compiler_params=pltpu.CompilerParams(
            dimension_semantics=("arbitrary", "arbitrary"), vmem_limit_bytes=VMEM_LIMIT),
        name="a_pre",
    )(proj, proj, gkv, wukv_bf16, gq, gk, hm)


def _a_main_kernel(far_ref, qn_ref, qi_ref, z_ref, misct_ref, kn_ref, vt_ref, ki_ref, near_ref, o_ref,
                   key_scr, lg_scr, j_scr):
    s = kn_ref.shape[0]
    i = pl.program_id(1)
    t0 = pl.multiple_of(i * QBLOCK, QBLOCK)
    f32, bf16 = jnp.float32, jnp.bfloat16
    nt = (((1,), (1,)), ((), ()))
    kf = float(TOPK_MAX)

    def run(nrows):

        qi = qi_ref[...] * (IDX_DIM ** -0.5)
        qstack = jnp.concatenate([qi[:, h * IDX_DIM:(h + 1) * IDX_DIM] for h in range(IDX_HEADS)], axis=0)
        ki = ki_ref[0:nrows, 0:IDX_DIM]
        sc = lax.dot_general(ki.astype(bf16), qstack.astype(bf16), nt, preferred_element_type=f32)
        wt = misct_ref[0, IDX_DIM:IDX_DIM + IDX_HEADS, :] * (IDX_HEADS ** -0.5)
        score = jnp.zeros((nrows, QBLOCK), f32)
        for h in range(IDX_HEADS):
            score = score + wt[h:h + 1, :] * jnp.maximum(sc[:, h * QBLOCK:(h + 1) * QBLOCK], 0.0)
        score = score + 0.0
        kpos = lax.broadcasted_iota(jnp.int32, (nrows, QBLOCK), 0)
        tpos = t0 + lax.broadcasted_iota(jnp.int32, (nrows, QBLOCK), 1)
        causal = kpos <= tpos
        bits = pltpu.bitcast(score, jnp.int32)
        key = jnp.where(bits < 0, bits ^ jnp.int32(0x7FFFFFFF), bits)
        key_scr[0:nrows, :] = jnp.where(causal, key, jnp.int32(INT_MIN))

        def count_ge(cand):
            return jnp.sum(jnp.where(key_scr[0:nrows, :] >= cand, 1.0, 0.0), axis=0, keepdims=True)

        def bisect(it, thr):
            cand = thr + lax.shift_left(jnp.int32(1), 31 - it)
            return jnp.where(count_ge(cand) >= kf, cand, thr)

        thr = lax.fori_loop(0, 32, bisect, jnp.full((1, QBLOCK), INT_MIN, jnp.int32))
        n_ge = count_ge(thr)
        need = kf - count_ge(thr + 1)

        j_scr[...] = jnp.full((1, QBLOCK), nrows - 1, jnp.int32)
        surplus = jnp.where((n_ge > kf) & (thr > INT_MIN), 1.0, 0.0)

        @pl.when(jnp.max(surplus) > 0.0)
        def _():
            def bisect_idx(it, lohi):
                lo, hi = lohi
                mid = lax.shift_right_arithmetic(lo + hi, 1)
                k = key_scr[0:nrows, :]
                kp = lax.broadcasted_iota(jnp.int32, (nrows, QBLOCK), 0)
                c = jnp.sum(jnp.where((k == thr) & (kp <= mid), 1.0, 0.0), axis=0, keepdims=True)
                ok = c >= need
                return jnp.where(ok, lo, mid), jnp.where(ok, mid, hi)

            lo0 = jnp.full((1, QBLOCK), -1, jnp.int32)
            hi0 = jnp.full((1, QBLOCK), nrows - 1, jnp.int32)
            _, hi = lax.fori_loop(0, 11, bisect_idx, (lo0, hi0))
            j_scr[...] = hi

        key = key_scr[0:nrows, :]
        sel = ((key > thr) | ((key == thr) & (kpos <= j_scr[...]))) & causal

        qn = qn_ref[...]
        lane = lax.broadcasted_iota(jnp.int32, (QBLOCK, A_WIDTH), 1)
        qblk = jnp.concatenate(
            [jnp.where((lane >= h * A_HEAD_DIM) & (lane < (h + 1) * A_HEAD_DIM), qn, 0.0) for h in range(A_HEADS)],
            axis=0)
        lg_scr[0:QBLOCK, :] = jnp.zeros((QBLOCK, A_HEADS * QBLOCK), f32)
        lg_scr[QBLOCK:QBLOCK + nrows, :] = lax.dot_general(kn_ref[0:nrows, :].astype(bf16), qblk.astype(bf16), nt,
                                                           preferred_element_type=f32)
        outs = []
        for h in range(A_HEADS):
            cols = slice(h * QBLOCK, (h + 1) * QBLOCK)
            lg_scr[pl.ds(t0, 2 * QBLOCK), cols] += near_ref[h] - far_ref[h]
            l = jnp.where(sel, lg_scr[QBLOCK:QBLOCK + nrows, cols], NEG_INF)
            m = jnp.max(l, axis=0, keepdims=True)
            p = jnp.exp(l - m)
            den = jnp.sum(p, axis=0, keepdims=True)
            vt = vt_ref[0, h * A_HEAD_DIM:(h + 1) * A_HEAD_DIM, 0:nrows]
            o_t = jnp.dot(vt.astype(bf16), p.astype(bf16), preferred_element_type=f32)
            outs.append(o_t / den)
        o = jnp.concatenate(outs, axis=0).T
        o_ref[...] = o * jax.nn.silu(z_ref[...])

    n_var = 4
    per = (s // QBLOCK) // n_var
    for v in range(n_var):
        pl.when((i >= v * per) & (i < (v + 1) * per))(functools.partial(run, (v + 1) * per * QBLOCK))


def a_main(proj, misct, qn, kn, vt, near_t, far, b, s):
    n = b * s
    nq = s // QBLOCK
    row = lambda bb, i: bb * nq + i
    return pl.pallas_call(
        _a_main_kernel,
        grid=(b, nq),
        in_specs=[
            pl.BlockSpec(memory_space=pltpu.SMEM),
            pl.BlockSpec((QBLOCK, A_WIDTH), lambda bb, i: (row(bb, i), 0)),
            pl.BlockSpec((QBLOCK, IDX_HEADS * IDX_DIM), lambda bb, i: (row(bb, i), OFF["a_qi"] // (IDX_HEADS * IDX_DIM))),
            pl.BlockSpec((QBLOCK, A_WIDTH), lambda bb, i: (row(bb, i), OFF["a_z"] // A_WIDTH)),
            pl.BlockSpec((1, 128, QBLOCK), lambda bb, i: (bb, 0, i)),
            pl.BlockSpec((s, A_WIDTH), lambda bb, i: (bb, 0)),
            pl.BlockSpec((1, A_WIDTH, s), lambda bb, i: (bb, 0, 0)),
            pl.BlockSpec((s, 128), lambda bb, i: (bb, MISC_OFF // 128)),
            pl.BlockSpec((A_HEADS, 2 * QBLOCK, QBLOCK), lambda bb, i: (0, 0, 0)),
        ],
        out_specs=pl.BlockSpec((QBLOCK, A_WIDTH), lambda bb, i: (row(bb, i), 0)),
        out_shape=jax.ShapeDtypeStruct((n, A_WIDTH), jnp.float32),
        scratch_shapes=[
            pltpu.VMEM((s, QBLOCK), jnp.int32),
            pltpu.VMEM((s + QBLOCK, A_HEADS * QBLOCK), jnp.float32),
            pltpu.VMEM((1, QBLOCK), jnp.int32),
        ],
        compiler_params=pltpu.CompilerParams(
            dimension_semantics=("arbitrary", "arbitrary"), vmem_limit_bytes=VMEM_LIMIT),
        name="a_main",
    )(far, qn, proj, proj, misct, kn, vt, proj, near_t)


def mixer_a(proj, misct, b, s, gkv, wukv, gq, gk, bias_a):
    kn, vt, qn = a_pre(proj, b, s, gkv, wukv.astype(jnp.bfloat16), gq, gk)
    near_t = jnp.transpose(_bias_lookup(bias_a, _NEAR_BUCKET_T), (2, 0, 1))
    far = bias_a[_FAR_BUCKET]
    return a_main(proj, misct, qn, kn, vt, near_t, far, b, s)


N_CMP_PAD = 128
N_SB = 32
WIN_KEYS = WINDOW + QBLOCK
CMP_GROUPS = CMP_BLOCK // CMP_STRIDE

_CMP_BUCKET_T = _bucket_np((np.arange(16)[:, None, None] * QBLOCK + np.arange(QBLOCK)[None, None, :])
                           - (np.arange(N_CMP_PAD)[None, :, None] * CMP_STRIDE + CMP_BLOCK - 1))
_WIN_BUCKET_T = _bucket_np(np.arange(QBLOCK)[None, :] + WINDOW - np.arange(WIN_KEYS)[:, None])
_OVERLAP_T = np.array([[1.0 if (n * CMP_STRIDE < j * SEL_BLOCK + SEL_BLOCK and n * CMP_STRIDE + CMP_BLOCK > j * SEL_BLOCK
                              and n < N_CMP_PAD - 1) else 0.0 for n in range(N_CMP_PAD)] for j in range(N_SB)], np.float32)
_SB_EXPAND = (np.arange(2048)[:, None] // SEL_BLOCK == np.arange(N_SB)[None, :]).astype(np.float32)


def _c_pre_kernel(cmp_ref, sel_ref, win_ref, kn_ref, pos_ref, w1_ref, w2_ref,
                  kvc_ref, ksel_ref, vselt_ref, kwin_ref, vwin_ref):
    f32, bf16 = jnp.float32, jnp.bfloat16
    d = C_HEAD_DIM

    def rms(x, g):
        return x * lax.rsqrt(jnp.mean(x * x, axis=-1, keepdims=True) + RMS_EPS) * g

    acc = [jnp.zeros((N_CMP_PAD, 2 * PHI_HIDDEN), f32) for _ in range(CMP_GROUPS)]
    for j in range(CMP_STRIDE):
        xs = cmp_ref[pl.ds(j, N_CMP_PAD, stride=CMP_STRIDE), :]
        for half in range(CMP_GROUPS):
            jj = half * CMP_STRIDE + j
            acc[half] = acc[half] + jnp.dot((xs + pos_ref[jj:jj + 1, :]).astype(bf16), w1_ref[jj],
                                            preferred_element_type=f32)
    hid = acc[0] + pltpu.roll(acc[1], N_CMP_PAD - 1, axis=0)
    kv = jnp.dot(jax.nn.silu(hid).astype(bf16), w2_ref[...], preferred_element_type=f32)
    kvc_ref[0, :, 0:d] = rms(kv[:, 0:d], kn_ref[0:1, :])
    kvc_ref[0, :, d:2 * d] = kv[:, d:2 * d]

    sel = sel_ref[...]
    ksel_ref[...] = rms(sel[:, 0:d], kn_ref[1:2, :])
    vselt_ref[0] = sel.T[d:2 * d, :]
    win = win_ref[...]
    kwin_ref[0, 0:WINDOW, :] = jnp.zeros((WINDOW, d), f32)
    vwin_ref[0, 0:WINDOW, :] = jnp.zeros((WINDOW, d), f32)
    kwin_ref[0, WINDOW:, :] = rms(win[:, 0:d], kn_ref[2:3, :])
    vwin_ref[0, WINDOW:, :] = win[:, d:2 * d]


def c_pre(proj, b, s, k_norm, cmp_pos, phi_w1, phi_w2):
    d = C_HEAD_DIM
    kv0 = OFF["c_kv"] // 128
    pos = jnp.concatenate([cmp_pos[0], cmp_pos[1]], axis=-1)
    w1 = phi_w1.reshape(2, CMP_BLOCK, d, PHI_HIDDEN)
    zero = jnp.zeros((CMP_BLOCK, d, PHI_HIDDEN), jnp.float32)
    w1c = jnp.concatenate([jnp.concatenate([w1[0], zero], axis=-1),
                           jnp.concatenate([zero, w1[1]], axis=-1)], axis=1).astype(jnp.bfloat16)
    z2 = jnp.zeros((PHI_HIDDEN, d), jnp.float32)
    w2c = jnp.concatenate([jnp.concatenate([phi_w2[0], z2], axis=-1),
                           jnp.concatenate([z2, phi_w2[1]], axis=-1)], axis=0).astype(jnp.bfloat16)
    full = lambda a: pl.BlockSpec(a.shape, lambda bb: (0,) * a.ndim)
    return pl.pallas_call(
        _c_pre_kernel,
        grid=(b,),
        in_specs=[
            pl.BlockSpec((s, 128), lambda bb: (bb, kv0)),
            pl.BlockSpec((s, 128), lambda bb: (bb, kv0 + 1)),
            pl.BlockSpec((s, 128), lambda bb: (bb, kv0 + 2)),
            full(k_norm), full(pos), full(w1c), full(w2c),
        ],
        out_specs=[
            pl.BlockSpec((1, N_CMP_PAD, 128), lambda bb: (bb, 0, 0)),
            pl.BlockSpec((s, d), lambda bb: (bb, 0)),
            pl.BlockSpec((1, d, s), lambda bb: (bb, 0, 0)),
            pl.BlockSpec((1, s + WINDOW, d), lambda bb: (bb, 0, 0)),
            pl.BlockSpec((1, s + WINDOW, d), lambda bb: (bb, 0, 0)),
        ],
        out_shape=[
            jax.ShapeDtypeStruct((b, N_CMP_PAD, 128), jnp.float32),
            jax.ShapeDtypeStruct((b * s, d), jnp.float32),
            jax.ShapeDtypeStruct((b, d, s), jnp.float32),
            jax.ShapeDtypeStruct((b, s + WINDOW, d), jnp.float32),
            jax.ShapeDtypeStruct((b, s + WINDOW, d), jnp.float32),
        ],
        compiler_params=pltpu.CompilerParams(dimension_semantics=("arbitrary",), vmem_limit_bytes=VMEM_LIMIT),
        name="c_pre",
    )(proj, proj, proj, k_norm, pos, w1c, w2c)


def _c_main_kernel(cq_ref, z_ref, misct_ref, gq_ref, kvc_ref, ksel_ref, vselt_ref, kwin_ref, vwin_ref,
                   cmptab_ref, near_ref, farrow_ref, wintab_ref, ovt_ref, exp_ref, o_ref, ls_scr):
    s = ksel_ref.shape[0]
    d = C_HEAD_DIM
    hq = C_HEADS * QBLOCK
    i = pl.program_id(1)
    t0 = pl.multiple_of(i * QBLOCK, QBLOCK)
    f32, bf16 = jnp.float32, jnp.bfloat16
    nt = (((1,), (1,)), ((), ()))
    tn = (((0,), (0,)), ((), ()))

    cq = cq_ref[...]
    qs = jnp.concatenate([cq[:, h * d:(h + 1) * d] for h in range(C_HEADS)], axis=0)
    qs = qs * lax.rsqrt(jnp.mean(qs * qs, axis=-1, keepdims=True) + RMS_EPS) * gq_ref[...] * (d ** -0.5)
    qs = qs.astype(bf16)

    def softmax_rows(l, valid):
        l = jnp.where(valid, l, NEG_INF)
        m = jnp.max(l, axis=0, keepdims=True)
        p = jnp.where(valid, jnp.exp(l - m), 0.0)
        den = jnp.sum(p, axis=0, keepdims=True)
        return p, den

    kvc = kvc_ref[0]
    lc = lax.dot_general(kvc[:, 0:d].astype(bf16), qs, nt, preferred_element_type=f32) + cmptab_ref[0]
    n_idx = lax.broadcasted_iota(jnp.int32, (N_CMP_PAD, hq), 0)
    t_c = t0 + (lax.broadcasted_iota(jnp.int32, (N_CMP_PAD, hq), 1) & (QBLOCK - 1))
    cmp_valid = n_idx * CMP_STRIDE + (CMP_BLOCK - 1) <= t_c
    pc, den_c = softmax_rows(lc, cmp_valid)
    pc = pc * jnp.where(den_c > 0.0, 1.0 / den_c, 0.0)
    o_cmp = lax.dot_general(kvc[:, d:2 * d].astype(bf16), pc.astype(bf16), tn, preferred_element_type=f32)

    psum = pc[:, 0:QBLOCK]
    for h in range(1, C_HEADS):
        psum = psum + pc[:, h * QBLOCK:(h + 1) * QBLOCK]
    p_hi = psum.astype(bf16)
    p_lo = (psum - p_hi.astype(f32)).astype(bf16)
    ovt = ovt_ref[...]
    imp = jnp.dot(ovt, p_hi, preferred_element_type=f32) + jnp.dot(ovt, p_lo, preferred_element_type=f32)
    j_idx = lax.broadcasted_iota(jnp.int32, (N_SB, QBLOCK), 0)
    t_b = t0 + lax.broadcasted_iota(jnp.int32, (N_SB, QBLOCK), 1)
    cur = lax.shift_right_arithmetic(t_b, 6)
    forced = (j_idx == 0) | (j_idx == cur) | (j_idx == jnp.maximum(cur - 1, 0))
    imp = jnp.where(j_idx * SEL_BLOCK <= t_b, jnp.where(forced, FORCE_SCORE, imp), NEG_INF)
    rank = jnp.zeros((N_SB, QBLOCK), f32)
    for r in range(N_SB):
        row = imp[r:r + 1, :]
        rank = rank + jnp.where((row > imp) | ((row == imp) & (j_idx > r)), 1.0, 0.0)
    picked = jnp.where(rank < float(N_SEL), 1.0, 0.0).astype(bf16)

    kw = kwin_ref[0, pl.ds(t0, WIN_KEYS), :]
    vw = vwin_ref[0, pl.ds(t0, WIN_KEYS), :]
    lw = lax.dot_general(kw.astype(bf16), qs, nt, preferred_element_type=f32) + wintab_ref[...]
    r_idx = lax.broadcasted_iota(jnp.int32, (WIN_KEYS, hq), 0)
    dist = (lax.broadcasted_iota(jnp.int32, (WIN_KEYS, hq), 1) & (QBLOCK - 1)) + WINDOW - r_idx
    win_valid = (dist >= 0) & (dist < WINDOW) & (r_idx + t0 >= WINDOW)
    pw, den_w = softmax_rows(lw, win_valid)
    o_win = lax.dot_general(vw.astype(bf16), pw.astype(bf16), tn, preferred_element_type=f32) / den_w

    g = jax.nn.sigmoid(misct_ref[0, 80:80 + 3 * C_HEADS, :])

    def run(nrows):
        kpos = lax.broadcasted_iota(jnp.int32, (nrows, QBLOCK), 0)
        tpos = t0 + lax.broadcasted_iota(jnp.int32, (nrows, QBLOCK), 1)
        sel_valid = (jnp.dot(exp_ref[0:nrows, :], picked, preferred_element_type=f32) > 0.5) & (kpos <= tpos)
        ls_scr[0:QBLOCK, :] = jnp.zeros((QBLOCK, hq), f32)
        ls_scr[QBLOCK:QBLOCK + nrows, :] = lax.dot_general(ksel_ref[0:nrows, :].astype(bf16), qs, nt,
                                                           preferred_element_type=f32)
        ls_scr[pl.ds(t0, 2 * QBLOCK), :] += near_ref[...] - farrow_ref[...]
        outs = []
        for h in range(C_HEADS):
            cols = slice(h * QBLOCK, (h + 1) * QBLOCK)
            l = jnp.where(sel_valid, ls_scr[QBLOCK:QBLOCK + nrows, cols], NEG_INF)
            m = jnp.max(l, axis=0, keepdims=True)
            p = jnp.exp(l - m)
            den = jnp.sum(p, axis=0, keepdims=True)
            o_sel = jnp.dot(vselt_ref[0, :, 0:nrows].astype(bf16), p.astype(bf16), preferred_element_type=f32) / den
            outs.append(g[3 * h:3 * h + 1, :] * o_cmp[:, cols] + g[3 * h + 1:3 * h + 2, :] * o_sel
                        + g[3 * h + 2:3 * h + 3, :] * o_win[:, cols])
        o_ref[...] = jnp.concatenate(outs, axis=0).T * jax.nn.silu(z_ref[...])

    n_var = 4
    per = (s // QBLOCK) // n_var
    for v in range(n_var):
        pl.when((i >= v * per) & (i < (v + 1) * per))(functools.partial(run, (v + 1) * per * QBLOCK))


def c_main(proj, misct, gq, kvc, ksel, vselt, kwin, vwin, cmptab, near, farrow, wintab, b, s):
    n = b * s
    nq = s // QBLOCK
    d = C_HEAD_DIM
    hq = C_HEADS * QBLOCK
    row = lambda bb, i: bb * nq + i
    ovt = jnp.asarray(_OVERLAP_T, jnp.bfloat16)
    expand = jnp.asarray(_SB_EXPAND, jnp.bfloat16)
    const = lambda a: pl.BlockSpec(a.shape, lambda bb, i: (0,) * a.ndim)
    return pl.pallas_call(
        _c_main_kernel,
        grid=(b, nq),
        in_specs=[
            pl.BlockSpec((QBLOCK, C_WIDTH), lambda bb, i: (row(bb, i), OFF["c_q"] // C_WIDTH)),
            pl.BlockSpec((QBLOCK, C_WIDTH), lambda bb, i: (row(bb, i), OFF["c_z"] // C_WIDTH)),
            pl.BlockSpec((1, 128, QBLOCK), lambda bb, i: (bb, 0, i)),
            const(gq),
            pl.BlockSpec((1, N_CMP_PAD, 128), lambda bb, i: (bb, 0, 0)),
            pl.BlockSpec((s, d), lambda bb, i: (bb, 0)),
            pl.BlockSpec((1, d, s), lambda bb, i: (bb, 0, 0)),
            pl.BlockSpec((1, s + WINDOW, d), lambda bb, i: (bb, 0, 0)),
            pl.BlockSpec((1, s + WINDOW, d), lambda bb, i: (bb, 0, 0)),
            pl.BlockSpec((1, N_CMP_PAD, hq), lambda bb, i: (i, 0, 0)),
            const(near), const(farrow), const(wintab), const(ovt), const(expand),
        ],
        out_specs=pl.BlockSpec((QBLOCK, C_WIDTH), lambda bb, i: (row(bb, i), 0)),
        out_shape=jax.ShapeDtypeStruct((n, C_WIDTH), jnp.float32),
        scratch_shapes=[pltpu.VMEM((s + QBLOCK, hq), jnp.float32)],
        compiler_params=pltpu.CompilerParams(
            dimension_semantics=("arbitrary", "arbitrary"), vmem_limit_bytes=VMEM_LIMIT),
        name="c_main",
    )(proj, proj, misct, gq, kvc, ksel, vselt, kwin, vwin, cmptab, near, farrow, wintab, ovt, expand)


def _head_cols(tab):
    return jnp.moveaxis(tab, -1, -2).reshape(*tab.shape[:-2], tab.shape[-1] * tab.shape[-2])


def mixer_c(proj, misct, b, s, gq, k_norm, cmp_pos, phi_w1, phi_w2, bias_c):
    kvc, ksel, vselt, kwin, vwin = c_pre(proj, b, s, k_norm, cmp_pos, phi_w1, phi_w2)
    cmptab = _head_cols(_bias_lookup(bias_c, _CMP_BUCKET_T))
    near = _head_cols(_bias_lookup(bias_c, _NEAR_BUCKET_T))
    wintab = _head_cols(_bias_lookup(bias_c, _WIN_BUCKET_T))
    farrow = jnp.repeat(bias_c[_FAR_BUCKET], QBLOCK).reshape(1, C_HEADS * QBLOCK)
    return c_main(proj, misct, gq.reshape(1, C_HEAD_DIM), kvc, ksel, vselt, kwin, vwin, cmptab, near, farrow,
                  wintab, b, s)


B_HPS = 2
B_GROUP = 4
B_A_LANE = 72
B_B_LANE = 76


def _b_kernel(alog_ref, dtb_ref, q_ref, k_ref, v_ref, z_ref, misc_ref, arow_ref, cwq_ref, cwk_ref, cwv_ref,
              gon_ref, o_ref, qs, ks, vs, ws, at, gtok, btok, gcrow):
    s = q_ref.shape[0]
    dh = B_HEAD_DIM
    c_len = GDN_CHUNK
    j = pl.program_id(1)
    f32, bf16 = jnp.float32, jnp.bfloat16
    hi = lax.Precision.HIGHEST
    nt = (((1,), (1,)), ((), ()))
    tn = (((0,), (0,)), ((), ()))

    rows = lax.broadcasted_iota(jnp.int32, (s, B_HPS * dh), 0)

    def conv_silu(x_ref, w_ref):
        x = x_ref[...]
        acc = x * w_ref[CONV_WIDTH - 1:CONV_WIDTH, :]
        for k in range(1, CONV_WIDTH):
            shifted = jnp.where(rows >= k, pltpu.roll(x, k, axis=0), 0.0)
            acc = acc + shifted * w_ref[CONV_WIDTH - 1 - k:CONV_WIDTH - k, :]
        return jax.nn.silu(acc)

    def l2n(x):
        return x * lax.rsqrt(jnp.sum(x * x, axis=-1, keepdims=True) + RMS_EPS)

    qc = conv_silu(q_ref, cwq_ref)
    kc = conv_silu(k_ref, cwk_ref)
    vs[...] = conv_silu(v_ref, cwv_ref)
    misc = misc_ref[...]
    lane = lax.broadcasted_iota(jnp.int32, misc.shape, 1)
    ri = lax.broadcasted_iota(jnp.int32, (c_len, c_len), 0)
    ci = lax.broadcasted_iota(jnp.int32, (c_len, c_len), 1)
    lower = ci <= ri
    strict = ci < ri
    tri_u = jnp.where(ri <= ci, 1.0, 0.0)
    row_in_chunk = lax.broadcasted_iota(jnp.int32, (s, dh), 0) & (c_len - 1)
    for hh in range(B_HPS):
        sl = slice(hh * dh, (hh + 1) * dh)
        qs[:, sl] = l2n(qc[:, sl]) * (dh ** -0.5)
        ks[:, sl] = l2n(kc[:, sl])
        h = B_HPS * j + hh
        neg_a = -jnp.exp(alog_ref[h])
        dtb = dtb_ref[h]
        a_tok = jnp.sum(jnp.where(lane == B_A_LANE + h, misc, 0.0), axis=-1, keepdims=True)
        b_tok = jnp.sum(jnp.where(lane == B_B_LANE + h, misc, 0.0), axis=-1, keepdims=True)
        g = jnp.broadcast_to(neg_a * jax.nn.softplus(a_tok + dtb), (s, dh))
        for sh in (1, 2, 4, 8, 16, 32):
            g = g + jnp.where(row_in_chunk >= sh, pltpu.roll(g, sh, axis=0), 0.0)
        gtok[hh] = g
        btok[hh] = jnp.broadcast_to(jax.nn.sigmoid(b_tok), (s, dh))
        g_row = neg_a * jax.nn.softplus(arow_ref[0, hh] + dtb)
        gcrow[hh] = jnp.dot(g_row, tri_u, precision=hi, preferred_element_type=f32)

    lo_lanes = (lax.broadcasted_iota(jnp.int32, (c_len, 4 * c_len), 1) & c_len) != 0

    def split_lhs(p):
        p4 = jnp.concatenate([p, p, p, p], axis=1)
        hi4 = p4.astype(bf16).astype(f32)
        return jnp.where(lo_lanes, p4 - hi4, hi4).astype(bf16)

    def split_rhs(x):
        xh = x.astype(bf16)
        xl = (x - xh.astype(f32)).astype(bf16)
        return jnp.concatenate([xh, xh, xl, xl], axis=0)

    def mm(p_split, x):
        return jnp.dot(p_split, split_rhs(x), preferred_element_type=f32)

    def intra_pair(i, carry):
        where, q_c, k_c, v_c, gb, beta, g_row = [], [], [], [], [], [], []
        for c in [B_GROUP * i + cc for cc in range(B_GROUP)]:
            rs = pl.ds(pl.multiple_of(c * c_len, c_len), c_len)
            for hh in range(B_HPS):
                sl = slice(hh * dh, (hh + 1) * dh)
                where.append((rs, hh, sl))
                q_c.append(qs[rs, sl])
                k_c.append(ks[rs, sl])
                v_c.append(vs[rs, sl])
                gb.append(gtok[hh, rs, :])
                beta.append(btok[hh, rs, :])
                g_row.append(gcrow[hh, pl.ds(c, 1), :])
        nch = range(len(where))
        decay = [jnp.where(lower, jnp.exp(jnp.where(lower, gb[n][:, 0:c_len] - g_row[n], 0.0)), 0.0) for n in nch]
        kb = [k_c[n] * beta[n] for n in nch]
        kk = [lax.dot_general(kb[n].astype(bf16), k_c[n].astype(bf16), nt, preferred_element_type=f32) for n in nch]
        p = [-jnp.where(strict, kk[n] * decay[n], 0.0) for n in nch]
        eg = [jnp.exp(gb[n]) for n in nch]
        x = [jnp.concatenate([v_c[n] * beta[n], kb[n] * eg[n]], axis=1) for n in nch]
        ps = [split_lhs(p[n]) for n in nch]
        x = [x[n] + mm(ps[n], x[n]) for n in nch]
        for _ in range(5):
            p = [mm(ps[n], p[n]) for n in nch]
            ps = [split_lhs(p[n]) for n in nch]
            x = [x[n] + mm(ps[n], x[n]) for n in nch]
        attn = [jnp.where(lower, lax.dot_general(q_c[n].astype(bf16), k_c[n].astype(bf16), nt,
                                                 preferred_element_type=f32) * decay[n], 0.0) for n in nch]
        for n, (rs, hh, sl) in enumerate(where):
            g_last = g_row[n][:, c_len - 1:c_len]
            vs[rs, sl] = x[n][:, 0:dh]
            ws[rs, sl] = x[n][:, dh:2 * dh]
            qs[rs, sl] = q_c[n] * eg[n]
            ks[rs, sl] = k_c[n] * jnp.exp(g_last - gb[n])
            at[rs, hh * c_len:(hh + 1) * c_len] = attn[n]
        return carry

    lax.fori_loop(0, s // (B_GROUP * c_len), intra_pair, 0)

    def recur(c, states):
        rs = pl.ds(pl.multiple_of(c * c_len, c_len), c_len)
        hs = range(B_HPS)
        sls = [slice(hh * dh, (hh + 1) * dh) for hh in hs]
        st_b = [states[hh].astype(bf16) for hh in hs]
        v_new = [vs[rs, sls[hh]] - jnp.dot(ws[rs, sls[hh]].astype(bf16), st_b[hh], preferred_element_type=f32)
                 for hh in hs]
        v_nb = [v_new[hh].astype(bf16) for hh in hs]
        o_st = [jnp.dot(qs[rs, sls[hh]].astype(bf16), st_b[hh], preferred_element_type=f32) for hh in hs]
        new_states = [states[hh] * jnp.exp(gcrow[hh, pl.ds(c, 1), c_len - 1:c_len])
                      + lax.dot_general(ks[rs, sls[hh]].astype(bf16), v_nb[hh], tn, preferred_element_type=f32)
                      for hh in hs]
        for hh in hs:
            o_ref[rs, sls[hh]] = o_st[hh] + jnp.dot(at[rs, hh * c_len:(hh + 1) * c_len].astype(bf16), v_nb[hh],
                                                    preferred_element_type=f32)
        return tuple(new_states)

    lax.fori_loop(0, s // c_len, recur, tuple(jnp.zeros((dh, dh), f32) for _ in range(B_HPS)))

    for hh in range(B_HPS):
        sl = slice(hh * dh, (hh + 1) * dh)
        o = o_ref[:, sl]
        o = o * lax.rsqrt(jnp.mean(o * o, axis=-1, keepdims=True) + RMS_EPS) * gon_ref[...]
        o_ref[:, sl] = o * jax.nn.silu(z_ref[:, sl])


def mixer_b(proj, misct, b, s, conv_w, a_log, dt_bias, out_norm):
    n = b * s
    w2 = B_HPS * B_HEAD_DIM
    nc = s // GDN_CHUNK
    arow = misct[:, B_A_LANE:B_A_LANE + B_HEADS, :].reshape(b, B_HEADS, nc, GDN_CHUNK)
    qkv0 = OFF["b_qkv"] // w2
    kstep = B_WIDTH // w2
    smem = pl.BlockSpec(memory_space=pltpu.SMEM)
    return pl.pallas_call(
        _b_kernel,
        grid=(b, B_HEADS // B_HPS),
        in_specs=[
            smem, smem,
            pl.BlockSpec((s, w2), lambda bb, j: (bb, qkv0 + j)),
            pl.BlockSpec((s, w2), lambda bb, j: (bb, qkv0 + kstep + j)),
            pl.BlockSpec((s, w2), lambda bb, j: (bb, qkv0 + 2 * kstep + j)),
            pl.BlockSpec((s, w2), lambda bb, j: (bb, OFF["b_z"] // w2 + j)),
            pl.BlockSpec((s, 128), lambda bb, j: (bb, MISC_OFF // 128)),
            pl.BlockSpec((1, B_HPS, nc, GDN_CHUNK), lambda bb, j: (bb, j, 0, 0)),
            pl.BlockSpec((CONV_WIDTH, w2), lambda bb, j: (0, j)),
            pl.BlockSpec((CONV_WIDTH, w2), lambda bb, j: (0, kstep + j)),
            pl.BlockSpec((CONV_WIDTH, w2), lambda bb, j: (0, 2 * kstep + j)),
            pl.BlockSpec((1, B_HEAD_DIM), lambda bb, j: (0, 0)),
        ],
        out_specs=pl.BlockSpec((s, w2), lambda bb, j: (bb, j)),
        out_shape=jax.ShapeDtypeStruct((n, B_WIDTH), jnp.float32),
        scratch_shapes=[
            pltpu.VMEM((s, w2), jnp.float32), pltpu.VMEM((s, w2), jnp.float32), pltpu.VMEM((s, w2), jnp.float32),
            pltpu.VMEM((s, w2), jnp.float32), pltpu.VMEM((s, B_HPS * GDN_CHUNK), jnp.float32),
            pltpu.VMEM((B_HPS, s, B_HEAD_DIM), jnp.float32), pltpu.VMEM((B_HPS, s, B_HEAD_DIM), jnp.float32),
            pltpu.VMEM((B_HPS, nc, GDN_CHUNK), jnp.float32),
        ],
        compiler_params=pltpu.CompilerParams(
            dimension_semantics=("arbitrary", "arbitrary"), vmem_limit_bytes=VMEM_LIMIT),
        name="gdn",
    )(a_log, dt_bias, proj, proj, proj, proj, proj, arow, conv_w, conv_w, conv_w, out_norm.reshape(1, B_HEAD_DIM))


def _rms_norm(x, g):
    y = x * lax.rsqrt(jnp.mean(x * x, axis=-1, keepdims=True) + RMS_EPS)
    return y * g


def _l2norm(x):
    return x * lax.rsqrt(jnp.sum(x * x, axis=-1, keepdims=True) + RMS_EPS)


def _masked_softmax(logits, mask):
    logits = jnp.where(mask, logits, NEG_INF)
    return jnp.where(mask, jax.nn.softmax(logits, axis=-1), 0.0)


def _t5_bucket(dist):
    dist = jnp.maximum(dist, 0)
    log_ratio = jnp.log(jnp.maximum(dist, 1).astype(jnp.float32) / BUCKET_MAX_EXACT) / math.log(BUCKET_MAX_DIST / BUCKET_MAX_EXACT)
    large = BUCKET_MAX_EXACT + (log_ratio * (N_BUCKETS - BUCKET_MAX_EXACT)).astype(jnp.int32)
    large = jnp.minimum(large, N_BUCKETS - 1)
    return jnp.where(dist < BUCKET_MAX_EXACT, dist, large)


def _to_qblocks(a):
    b, s = a.shape[:2]
    return jnp.moveaxis(a.reshape(b, s // QBLOCK, QBLOCK, *a.shape[2:]), 1, 0)


def _from_qblocks(a):
    nb, b = a.shape[:2]
    return jnp.moveaxis(a, 0, 1).reshape(b, nb * QBLOCK, *a.shape[3:])


def _dsa_mixer(q, k, v, q_idx, k_idx, w_idx, bias_tab):
    b, s = q.shape[:2]
    topk = min(TOPK_MAX, s // 4)
    bidx = jnp.arange(b)[:, None, None]
    key_pos = jnp.arange(s)
    scale = A_HEAD_DIM ** -0.5

    def block(args):
        qb, qib, wb, start = args
        t = start + jnp.arange(QBLOCK)
        idx_logits = jnp.einsum('bqhd,bsd->bqhs', qib, k_idx) * (IDX_DIM ** -0.5)
        score = jnp.einsum('bqh,bqhs->bqs', wb * (IDX_HEADS ** -0.5), jax.nn.relu(idx_logits))
        score = jnp.where(key_pos[None, None, :] <= t[None, :, None], score, NEG_INF)
        _, sel = lax.top_k(score, topk)
        kg = k[bidx, sel]
        vg = v[bidx, sel]
        dist = t[None, :, None] - sel
        bias = jnp.transpose(bias_tab[_t5_bucket(dist)], (0, 3, 1, 2))
        logits = jnp.einsum('bqhd,bqkhd->bhqk', qb, kg) * scale + bias
        p = _masked_softmax(logits, (dist >= 0)[:, None])
        return jnp.einsum('bhqk,bqkhd->bqhd', p, vg)

    starts = jnp.arange(s // QBLOCK, dtype=jnp.int32) * QBLOCK
    out = lax.map(block, (_to_qblocks(q), _to_qblocks(q_idx), _to_qblocks(w_idx), starts))
    return _from_qblocks(out)


def _causal_depthwise_conv(x, w):
    c = x.shape[-1]
    return lax.conv_general_dilated(x, w[:, None, :], window_strides=(1,), padding=[(CONV_WIDTH - 1, 0)],
                                    dimension_numbers=('NWC', 'WIO', 'NWC'), feature_group_count=c)


def _gdn_mixer(qkv, a_in, b_in, conv_w, a_log, dt_bias):
    b, s = qkv.shape[:2]
    n = s // GDN_CHUNK
    qkv = jax.nn.silu(_causal_depthwise_conv(qkv, conv_w))
    q, k, v = jnp.split(qkv, 3, axis=-1)
    sh = lambda t: t.reshape(b, s, B_HEADS, B_HEAD_DIM)
    q = _l2norm(sh(q)) * (B_HEAD_DIM ** -0.5)
    k = _l2norm(sh(k))
    v = sh(v)
    beta = jax.nn.sigmoid(b_in)
    g = -jnp.exp(a_log) * jax.nn.softplus(a_in + dt_bias)

    def chunk(t):
        return jnp.moveaxis(t.reshape(b, n, GDN_CHUNK, *t.shape[2:]), 3, 2)

    q, k, v, beta, g = (chunk(t) for t in (q, k, v, beta, g))
    g_cum = jnp.cumsum(g, axis=-1)
    g_last = g_cum[..., -1]
    lower = jnp.tril(jnp.ones((GDN_CHUNK, GDN_CHUNK), dtype=bool))
    strict = jnp.tril(jnp.ones((GDN_CHUNK, GDN_CHUNK), dtype=bool), -1)
    diff = g_cum[..., :, None] - g_cum[..., None, :]
    decay = jnp.where(lower, jnp.exp(jnp.where(lower, diff, 0.0)), 0.0)
    k_beta = k * beta[..., None]
    a_mat = jnp.where(strict, jnp.einsum('bnhid,bnhjd->bnhij', k_beta, k) * decay, 0.0)
    eye = jnp.eye(GDN_CHUNK, dtype=jnp.float32)
    t_inv = lax.linalg.triangular_solve(eye + a_mat, jnp.broadcast_to(eye, a_mat.shape), left_side=True, lower=True)
    u = jnp.einsum('bnhij,bnhjd->bnhid', t_inv, v * beta[..., None])
    w = jnp.einsum('bnhij,bnhjd->bnhid', t_inv, k_beta * jnp.exp(g_cum)[..., None])
    attn = jnp.einsum('bnhid,bnhjd->bnhij', q, k) * decay
    q_dec = q * jnp.exp(g_cum)[..., None]
    k_dec = k * jnp.exp(g_last[..., None] - g_cum)[..., None]

    def step(state, xs):
        u_c, w_c, q_c, k_c, attn_c, gl_c = xs
        v_new = u_c - jnp.einsum('bhik,bhkv->bhiv', w_c, state)
        o = jnp.einsum('bhik,bhkv->bhiv', q_c, state) + jnp.einsum('bhij,bhjv->bhiv', attn_c, v_new)
        state = state * jnp.exp(gl_c)[..., None, None] + jnp.einsum('bhik,bhiv->bhkv', k_c, v_new)
        return state, o

    state0 = jnp.zeros((b, B_HEADS, B_HEAD_DIM, B_HEAD_DIM), jnp.float32)
    xs = tuple(jnp.moveaxis(t, 1, 0) for t in (u, w, q_dec, k_dec, attn, g_last))
    _, o = lax.scan(step, state0, xs)
    o = jnp.moveaxis(jnp.moveaxis(o, 0, 1), 2, 3)
    return o.reshape(b, s, B_HEADS, B_HEAD_DIM)


def _compress_blocks(tok, pos, w1, w2):
    b, s, d = tok.shape
    n_cmp = (s - CMP_BLOCK) // CMP_STRIDE + 1
    idx = jnp.arange(n_cmp)[:, None] * CMP_STRIDE + jnp.arange(CMP_BLOCK)[None, :]
    blocks = tok[:, idx] + pos
    return jax.nn.silu(blocks.reshape(b, n_cmp, CMP_BLOCK * d) @ w1) @ w2


def _nsa_mixer(q, kv, gates, k_norm, cmp_pos, phi_w1, phi_w2, bias_tab):
    b, s = q.shape[:2]
    scale = C_HEAD_DIM ** -0.5
    t_all = jnp.arange(s)
    k_cmp, v_cmp, k_sel, v_sel, k_win, v_win = (kv[:, :, i] for i in range(6))
    k_sel = _rms_norm(k_sel, k_norm[1])
    k_win = _rms_norm(k_win, k_norm[2])
    kc = _rms_norm(_compress_blocks(k_cmp, cmp_pos[0], phi_w1[0], phi_w2[0]), k_norm[0])
    vc = _compress_blocks(v_cmp, cmp_pos[1], phi_w1[1], phi_w2[1])
    n_cmp = kc.shape[1]
    cmp_start = jnp.arange(n_cmp) * CMP_STRIDE
    cmp_end = cmp_start + CMP_BLOCK - 1
    cmp_valid = cmp_end[None, :] <= t_all[:, None]
    cmp_bias = jnp.transpose(bias_tab[_t5_bucket(t_all[:, None] - cmp_end[None, :])], (2, 0, 1))
    logits = jnp.einsum('bqhd,bnd->bhqn', q, kc) * scale + cmp_bias
    p_cmp = _masked_softmax(logits, cmp_valid[None, None])
    o_cmp = jnp.einsum('bhqn,bnd->bqhd', p_cmp, vc)
    n_sb = s // SEL_BLOCK
    n_pick = min(N_SEL, n_sb)
    sb_start = jnp.arange(n_sb) * SEL_BLOCK
    overlap = ((cmp_start[:, None] < sb_start[None, :] + SEL_BLOCK) & (cmp_start[:, None] + CMP_BLOCK > sb_start[None, :])).astype(jnp.float32)
    importance = jnp.einsum('bhqn,nj->bqj', p_cmp, overlap)
    cur = t_all // SEL_BLOCK
    blk = jnp.arange(n_sb)
    forced = (blk[None, :] == 0) | (blk[None, :] == cur[:, None]) | (blk[None, :] == jnp.maximum(cur[:, None] - 1, 0))
    admissible = sb_start[None, :] <= t_all[:, None]
    importance = jnp.where(admissible[None], jnp.where(forced[None], FORCE_SCORE, importance), NEG_INF)
    _, sel = lax.top_k(importance, n_pick)
    k_blocks = k_sel.reshape(b, n_sb, SEL_BLOCK, C_HEAD_DIM)
    v_blocks = v_sel.reshape(b, n_sb, SEL_BLOCK, C_HEAD_DIM)
    k_win_pad = jnp.pad(k_win, ((0, 0), (WINDOW, 0), (0, 0)))
    v_win_pad = jnp.pad(v_win, ((0, 0), (WINDOW, 0), (0, 0)))
    bidx = jnp.arange(b)[:, None, None]
    n_keys = n_pick * SEL_BLOCK

    def block(args):
        qb, selb, start = args
        t = start + jnp.arange(QBLOCK)
        kg = k_blocks[bidx, selb].reshape(b, QBLOCK, n_keys, C_HEAD_DIM)
        vg = v_blocks[bidx, selb].reshape(b, QBLOCK, n_keys, C_HEAD_DIM)
        s_pos = (selb[..., None] * SEL_BLOCK + jnp.arange(SEL_BLOCK)).reshape(b, QBLOCK, n_keys)
        dist = t[None, :, None] - s_pos
        bias = jnp.transpose(bias_tab[_t5_bucket(dist)], (0, 3, 1, 2))
        lg = jnp.einsum('bqhd,bqkd->bhqk', qb, kg) * scale + bias
        p = _masked_softmax(lg, (dist >= 0)[:, None])
        o_sel = jnp.einsum('bhqk,bqkd->bqhd', p, vg)
        kw = lax.dynamic_slice_in_dim(k_win_pad, start, QBLOCK + WINDOW, axis=1)
        vw = lax.dynamic_slice_in_dim(v_win_pad, start, QBLOCK + WINDOW, axis=1)
        s_w = start - WINDOW + jnp.arange(QBLOCK + WINDOW)
        dist_w = t[:, None] - s_w[None, :]
        valid_w = (s_w[None, :] >= 0) & (dist_w >= 0) & (dist_w < WINDOW)
        bias_w = jnp.transpose(bias_tab[_t5_bucket(dist_w)], (2, 0, 1))
        lw = jnp.einsum('bqhd,bsd->bhqs', qb, kw) * scale + bias_w
        pw = _masked_softmax(lw, valid_w[None, None])
        o_win = jnp.einsum('bhqs,bsd->bqhd', pw, vw)
        return o_sel, o_win

    starts = jnp.arange(s // QBLOCK, dtype=jnp.int32) * QBLOCK
    o_sel, o_win = lax.map(block, (_to_qblocks(q), _to_qblocks(sel), starts))
    o_sel = _from_qblocks(o_sel)
    o_win = _from_qblocks(o_win)
    g = jax.nn.sigmoid(gates).reshape(b, s, C_HEADS, 3)
    return g[..., 0:1] * o_cmp + g[..., 1:2] * o_sel + g[..., 2:3] * o_win


def _field(proj3, name):
    return proj3[..., OFF[name]:OFF[name] + _ORIG[name][1]]


def kernel(x, norm_g, w_in, a_kv_norm, a_w_ukv, a_q_norm, a_k_norm, b_conv, b_a_log, b_dt_bias, b_out_norm, c_q_norm, c_k_norm, c_cmp_pos, c_phi_w1, c_phi_w2, w_branch, w_out, rel_bias):
    b, s, _ = x.shape
    n = b * s
    bias_a = rel_bias[:, :A_HEADS]
    bias_c = rel_bias[:, A_HEADS:]
    xf = x.reshape(n, D_MODEL)
    perm = jnp.asarray(_PERM_IDX)
    valid = jnp.asarray(_PERM_VALID)
    sh = lambda t, h, d: t.reshape(b, s, h, d)
    for l in range(DEPTH):
        w_p = jnp.where(valid[None, :], w_in[l][:, perm], 0.0).astype(jnp.bfloat16)
        proj = in_proj(xf, norm_g[l], w_p)
        p3 = proj.reshape(b, s, N_PAD)

        misct = jnp.transpose(p3[..., MISC_OFF:], (0, 2, 1))
        y_a = mixer_a(proj, misct, b, s, a_kv_norm[l], a_w_ukv[l], a_q_norm[l], a_k_norm[l], bias_a)

        y_b = mixer_b(proj, misct, b, s, b_conv[l], b_a_log[l], b_dt_bias[l], b_out_norm[l])
        y_c = mixer_c(proj, misct, b, s, c_q_norm[l], c_k_norm[l], c_cmp_pos[l], c_phi_w1[l], c_phi_w2[l], bias_c)

        wbr = w_branch[l].astype(jnp.bfloat16)
        xf = merge(xf, proj, y_a, y_b, y_c, wbr[:A_WIDTH], wbr[A_WIDTH:A_WIDTH + B_WIDTH],
                   wbr[A_WIDTH + B_WIDTH:], w_out[l].astype(jnp.bfloat16))
    return xf.reshape(b, s, D_MODEL)
```

```python
import functools
import math

import jax
import jax.numpy as jnp
import numpy as np
from jax import lax
from jax.experimental import pallas as pl
from jax.experimental.pallas import tpu as pltpu

D_MODEL = 1024
DEPTH = 4
QBLOCK = 128
NEG_INF = -1e30
FORCE_SCORE = 1e9
RMS_EPS = 1e-6

A_HEADS = 4
A_HEAD_DIM = 64
A_WIDTH = A_HEADS * A_HEAD_DIM
A_KV_LATENT = 128
IDX_HEADS = 8
IDX_DIM = 64
TOPK_MAX = 256

B_HEADS = 4
B_HEAD_DIM = 128
B_WIDTH = B_HEADS * B_HEAD_DIM
CONV_WIDTH = 4
GDN_CHUNK = 64

C_HEADS = 4
C_HEAD_DIM = 64
C_WIDTH = C_HEADS * C_HEAD_DIM
CMP_BLOCK = 32
CMP_STRIDE = 16
SEL_BLOCK = 64
N_SEL = 16
WINDOW = 512
PHI_HIDDEN = 256

N_BUCKETS = 32
BUCKET_MAX_EXACT = 16
BUCKET_MAX_DIST = 128

VMEM_LIMIT = 48 * 1024 * 1024

_ORIG = {}
_o = 0
for _name, _w in (("a_q", 256), ("a_ckv", 128), ("a_qi", 512), ("a_ki", 64), ("a_wi", 8), ("a_z", 256),
                  ("b_qkv", 1536), ("b_a", 4), ("b_b", 4), ("b_z", 512),
                  ("c_q", 256), ("c_kv", 384), ("c_g", 12), ("c_z", 256), ("g", 3072)):
    _ORIG[_name] = (_o, _w)
    _o += _w
N_IN = _o

_NEW_ORDER = ("g", "b_qkv", "b_z", "a_qi", "a_q", "a_z", "c_q", "c_z", "c_kv", "a_ckv",
              "a_ki", "a_wi", "b_a", "b_b", "c_g")
OFF = {}
_o = 0
for _name in _NEW_ORDER:
    OFF[_name] = _o
    _o += _ORIG[_name][1]
MISC_OFF = OFF["a_ki"]
N_PAD = 7296
assert MISC_OFF == 7168 and _o <= N_PAD


def _perm_indices():
    idx = np.zeros((N_PAD,), np.int32)
    valid = np.zeros((N_PAD,), bool)
    for name in _NEW_ORDER:
        o_old, w = _ORIG[name]
        idx[OFF[name]:OFF[name] + w] = np.arange(o_old, o_old + w)
        valid[OFF[name]:OFF[name] + w] = True
    return idx, valid


_PERM_IDX, _PERM_VALID = _perm_indices()


IN_TM = 512
IN_TN = N_PAD // 3


def _in_proj_kernel(x_ref, g_ref, w_ref, o_ref):
    x = x_ref[...]
    ms = jnp.mean(x * x, axis=-1, keepdims=True)
    h = (x * lax.rsqrt(ms + RMS_EPS)) * g_ref[...]
    o_ref[...] = jnp.dot(h.astype(jnp.bfloat16), w_ref[...], preferred_element_type=jnp.float32)


def in_proj(xf, g, w_bf16):
    n = xf.shape[0]
    return pl.pallas_call(
        _in_proj_kernel,
        grid=(N_PAD // IN_TN, n // IN_TM),
        in_specs=[
            pl.BlockSpec((IN_TM, D_MODEL), lambda j, i: (i, 0)),
            pl.BlockSpec((1, D_MODEL), lambda j, i: (0, 0)),
            pl.BlockSpec((D_MODEL, IN_TN), lambda j, i: (0, j)),
        ],
        out_specs=pl.BlockSpec((IN_TM, IN_TN), lambda j, i: (i, j)),
        out_shape=jax.ShapeDtypeStruct((n, N_PAD), jnp.float32),
        compiler_params=pltpu.CompilerParams(
            dimension_semantics=("arbitrary", "arbitrary"), vmem_limit_bytes=VMEM_LIMIT),
        name="in_proj",
    )(xf, g.reshape(1, D_MODEL), w_bf16)


MG_TM = 256


def _merge_kernel(x_ref, g_ref, ya_ref, yb_ref, yc_ref, wa_ref, wb_ref, wc_ref, wo_ref, o_ref):
    def branch(y_ref, w_ref, k):
        p = jnp.dot(y_ref[...].astype(jnp.bfloat16), w_ref[...], preferred_element_type=jnp.float32)
        return jax.nn.sigmoid(g_ref[:, k * D_MODEL:(k + 1) * D_MODEL]) * p

    merged = branch(ya_ref, wa_ref, 0) + branch(yb_ref, wb_ref, 1) + branch(yc_ref, wc_ref, 2)
    o_ref[...] = x_ref[...] + jnp.dot(merged.astype(jnp.bfloat16), wo_ref[...],
                                      preferred_element_type=jnp.float32)


def merge(xf, proj, ya, yb, yc, wa, wb, wc, wo):
    n = xf.shape[0]
    row = lambda w: pl.BlockSpec((MG_TM, w), lambda i: (i, 0))
    full = lambda a: pl.BlockSpec(a.shape, lambda i: (0, 0))
    return pl.pallas_call(
        _merge_kernel,
        grid=(n // MG_TM,),
        in_specs=[row(D_MODEL), row(3 * D_MODEL), row(A_WIDTH), row(B_WIDTH), row(C_WIDTH),
                  full(wa), full(wb), full(wc), full(wo)],
        out_specs=row(D_MODEL),
        out_shape=jax.ShapeDtypeStruct((n, D_MODEL), jnp.float32),
        compiler_params=pltpu.CompilerParams(
            dimension_semantics=("arbitrary",), vmem_limit_bytes=VMEM_LIMIT),
        name="merge",
    )(xf, proj, ya, yb, yc, wa, wb, wc, wo)


def _bucket_np(dist):
    d = np.maximum(np.asarray(dist, np.int64), 0)
    ratio = np.log(np.maximum(d, 1).astype(np.float64) / BUCKET_MAX_EXACT) / math.log(BUCKET_MAX_DIST / BUCKET_MAX_EXACT)
    scaled = ratio * (N_BUCKETS - BUCKET_MAX_EXACT)
    frac = scaled - np.floor(scaled)
    edge = (d > BUCKET_MAX_EXACT) & (d < BUCKET_MAX_DIST) & ((frac < 1e-4) | (frac > 1 - 1e-4))
    assert not edge.any(), "bucket boundary too close to an integer distance"
    large = np.minimum(BUCKET_MAX_EXACT + np.floor(scaled + 1e-9).astype(np.int64), N_BUCKETS - 1)
    return np.where(d < BUCKET_MAX_EXACT, d, large).astype(np.int32)


_NEAR_BUCKET_T = _bucket_np(np.arange(QBLOCK)[None, :] + QBLOCK - np.arange(2 * QBLOCK)[:, None])
_FAR_BUCKET = int(_bucket_np(np.array([BUCKET_MAX_DIST]))[0])
assert (_bucket_np(np.arange(BUCKET_MAX_DIST, 4096)) == _FAR_BUCKET).all()

INT_MIN = -2 ** 31


COL_SLAB = 64


def _col_reduce(x, op):
    rows = x.shape[0]
    if rows > COL_SLAB and rows % COL_SLAB == 0:
        x = op(x.reshape(rows // COL_SLAB, COL_SLAB, x.shape[1]), axis=0)
    return op(x, axis=0, keepdims=True)


def _bias_lookup(table, buckets):
    idx = jnp.asarray(buckets)[..., None]
    out = jnp.zeros(idx.shape[:-1] + (table.shape[1],), table.dtype)
    for bkt in range(N_BUCKETS):
        out = jnp.where(idx == bkt, table[bkt], out)
    return out


A_PRE_TM = 512


def _a_pre_kernel(ckv_ref, q_ref, gkv_ref, wukv_ref, gq_ref, gk_ref, hm_ref, kn_ref, vt_ref, qn_ref):
    c = ckv_ref[...]
    c = c * lax.rsqrt(jnp.mean(c * c, axis=-1, keepdims=True) + RMS_EPS) * gkv_ref[...]
    kv = jnp.dot(c.astype(jnp.bfloat16), wukv_ref[...], preferred_element_type=jnp.float32)
    hm = hm_ref[...]

    def head_rms(x, g):
        ms = jnp.dot(x * x, hm, precision=lax.Precision.HIGHEST, preferred_element_type=jnp.float32)
        return x * lax.rsqrt(ms + RMS_EPS) * g

    kn_ref[...] = head_rms(kv[:, :A_WIDTH], gk_ref[...])
    qn_ref[...] = head_rms(q_ref[...], gq_ref[...]) * (A_HEAD_DIM ** -0.5)
    vt_ref[0] = kv[:, A_WIDTH:].T


def a_pre(proj, b, s, gkv, wukv_bf16, gq, gk):
    n = b * s
    nt = s // A_PRE_TM
    hm = jnp.asarray(np.kron(np.eye(A_HEADS), np.ones((A_HEAD_DIM, A_HEAD_DIM))) / A_HEAD_DIM, jnp.float32)
    row = lambda bb, j: (bb * nt + j, 0)
    full = lambda a: pl.BlockSpec(a.shape, lambda bb, j: (0,) * a.ndim)
    gq = jnp.tile(gq, A_HEADS).reshape(1, A_WIDTH)
    gk = jnp.tile(gk, A_HEADS).reshape(1, A_WIDTH)
    gkv = gkv.reshape(1, A_KV_LATENT)
    return pl.pallas_call(
        _a_pre_kernel,
        grid=(b, nt),
        in_specs=[
            pl.BlockSpec((A_PRE_TM, A_KV_LATENT), lambda bb, j: (bb * nt + j, OFF["a_ckv"] // A_KV_LATENT)),
            pl.BlockSpec((A_PRE_TM, A_WIDTH), lambda bb, j: (bb * nt + j, OFF["a_q"] // A_WIDTH)),
            full(gkv), full(wukv_bf16), full(gq), full(gk), full(hm),
        ],
        out_specs=[
            pl.BlockSpec((A_PRE_TM, A_WIDTH), row),
            pl.BlockSpec((1, A_WIDTH, A_PRE_TM), lambda bb, j: (bb, 0, j)),
            pl.BlockSpec((A_PRE_TM, A_WIDTH), row),
        ],
        out_shape=[
            jax.ShapeDtypeStruct((n, A_WIDTH), jnp.float32),
            jax.ShapeDtypeStruct((b, A_WIDTH, s), jnp.float32),
            jax.ShapeDtypeStruct((n, A_WIDTH), jnp.float32),
        ],
        compiler_params=pltpu.CompilerParams(
            dimension_semantics=("arbitrary", "arbitrary"), vmem_limit_bytes=VMEM_LIMIT),
        name="a_pre",
    )(proj, proj, gkv, wukv_bf16, gq, gk, hm)


def _a_main_kernel(far_ref, qn_ref, qi_ref, z_ref, misct_ref, kn_ref, vt_ref, ki_ref, near_ref, o_ref,
                   key_scr, lg_scr, j_scr):
    s = kn_ref.shape[0]
    i = pl.program_id(1)
    t0 = pl.multiple_of(i * QBLOCK, QBLOCK)
    f32, bf16 = jnp.float32, jnp.bfloat16
    nt = (((1,), (1,)), ((), ()))
    kf = float(TOPK_MAX)

    def run(nrows):

        qi = qi_ref[...] * (IDX_DIM ** -0.5)
        qstack = jnp.concatenate([qi[:, h * IDX_DIM:(h + 1) * IDX_DIM] for h in range(IDX_HEADS)], axis=0)
        ki = ki_ref[0:nrows, 0:IDX_DIM]
        sc = lax.dot_general(ki.astype(bf16), qstack.astype(bf16), nt, preferred_element_type=f32)
        wt = misct_ref[0, IDX_DIM:IDX_DIM + IDX_HEADS, :] * (IDX_HEADS ** -0.5)
        score = jnp.zeros((nrows, QBLOCK), f32)
        for h in range(IDX_HEADS):
            score = score + wt[h:h + 1, :] * jnp.maximum(sc[:, h * QBLOCK:(h + 1) * QBLOCK], 0.0)
        score = score + 0.0
        kpos = lax.broadcasted_iota(jnp.int32, (nrows, QBLOCK), 0)
        tpos = t0 + lax.broadcasted_iota(jnp.int32, (nrows, QBLOCK), 1)
        causal = kpos <= tpos
        bits = pltpu.bitcast(score, jnp.int32)
        key = jnp.where(bits < 0, bits ^ jnp.int32(0x7FFFFFFF), bits)
        key_scr[0:nrows, :] = jnp.where(causal, key, jnp.int32(INT_MIN))

        def count_ge(cand):
            return _col_reduce(jnp.where(key_scr[0:nrows, :] >= cand, 1.0, 0.0), jnp.sum)

        def bisect(it, thr):
            cand = thr + lax.shift_left(jnp.int32(1), 31 - it)
            return jnp.where(count_ge(cand) >= kf, cand, thr)

        thr = lax.fori_loop(0, 32, bisect, jnp.full((1, QBLOCK), INT_MIN, jnp.int32))
        n_ge = count_ge(thr)
        need = kf - count_ge(thr + 1)

        j_scr[...] = jnp.full((1, QBLOCK), nrows - 1, jnp.int32)
        surplus = jnp.where((n_ge > kf) & (thr > INT_MIN), 1.0, 0.0)

        @pl.when(jnp.max(surplus) > 0.0)
        def _():
            def bisect_idx(it, lohi):
                lo, hi = lohi
                mid = lax.shift_right_arithmetic(lo + hi, 1)
                k = key_scr[0:nrows, :]
                kp = lax.broadcasted_iota(jnp.int32, (nrows, QBLOCK), 0)
                c = _col_reduce(jnp.where((k == thr) & (kp <= mid), 1.0, 0.0), jnp.sum)
                ok = c >= need
                return jnp.where(ok, lo, mid), jnp.where(ok, mid, hi)

            lo0 = jnp.full((1, QBLOCK), -1, jnp.int32)
            hi0 = jnp.full((1, QBLOCK), nrows - 1, jnp.int32)
            _, hi = lax.fori_loop(0, 11, bisect_idx, (lo0, hi0))
            j_scr[...] = hi

        key = key_scr[0:nrows, :]
        sel = ((key > thr) | ((key == thr) & (kpos <= j_scr[...]))) & causal

        qn = qn_ref[...]
        lane = lax.broadcasted_iota(jnp.int32, (QBLOCK, A_WIDTH), 1)
        qblk = jnp.concatenate(
            [jnp.where((lane >= h * A_HEAD_DIM) & (lane < (h + 1) * A_HEAD_DIM), qn, 0.0) for h in range(A_HEADS)],
            axis=0)
        lg_scr[0:QBLOCK, :] = jnp.zeros((QBLOCK, A_HEADS * QBLOCK), f32)
        lg_scr[QBLOCK:QBLOCK + nrows, :] = lax.dot_general(kn_ref[0:nrows, :].astype(bf16), qblk.astype(bf16), nt,
                                                           preferred_element_type=f32)
        outs = []
        for h in range(A_HEADS):
            cols = slice(h * QBLOCK, (h + 1) * QBLOCK)
            lg_scr[pl.ds(t0, 2 * QBLOCK), cols] += near_ref[h] - far_ref[h]
            l = jnp.where(sel, lg_scr[QBLOCK:QBLOCK + nrows, cols], NEG_INF)
            m = _col_reduce(l, jnp.max)
            p = jnp.exp(l - m)
            den = _col_reduce(p, jnp.sum)
            vt = vt_ref[0, h * A_HEAD_DIM:(h + 1) * A_HEAD_DIM, 0:nrows]
            o_t = jnp.dot(vt.astype(bf16), p.astype(bf16), preferred_element_type=f32)
            outs.append(o_t / den)
        o = jnp.concatenate(outs, axis=0).T
        o_ref[...] = o * jax.nn.silu(z_ref[...])

    n_var = 4
    per = (s // QBLOCK) // n_var
    for v in range(n_var):
        pl.when((i >= v * per) & (i < (v + 1) * per))(functools.partial(run, (v + 1) * per * QBLOCK))


def a_main(proj, misct, qn, kn, vt, near_t, far, b, s):
    n = b * s
    nq = s // QBLOCK
    row = lambda bb, i: bb * nq + i
    return pl.pallas_call(
        _a_main_kernel,
        grid=(b, nq),
        in_specs=[
            pl.BlockSpec(memory_space=pltpu.SMEM),
            pl.BlockSpec((QBLOCK, A_WIDTH), lambda bb, i: (row(bb, i), 0)),
            pl.BlockSpec((QBLOCK, IDX_HEADS * IDX_DIM), lambda bb, i: (row(bb, i), OFF["a_qi"] // (IDX_HEADS * IDX_DIM))),
            pl.BlockSpec((QBLOCK, A_WIDTH), lambda bb, i: (row(bb, i), OFF["a_z"] // A_WIDTH)),
            pl.BlockSpec((1, 128, QBLOCK), lambda bb, i: (bb, 0, i)),
            pl.BlockSpec((s, A_WIDTH), lambda bb, i: (bb, 0)),
            pl.BlockSpec((1, A_WIDTH, s), lambda bb, i: (bb, 0, 0)),
            pl.BlockSpec((s, 128), lambda bb, i: (bb, MISC_OFF // 128)),
            pl.BlockSpec((A_HEADS, 2 * QBLOCK, QBLOCK), lambda bb, i: (0, 0, 0)),
        ],
        out_specs=pl.BlockSpec((QBLOCK, A_WIDTH), lambda bb, i: (row(bb, i), 0)),
        out_shape=jax.ShapeDtypeStruct((n, A_WIDTH), jnp.float32),
        scratch_shapes=[
            pltpu.VMEM((s, QBLOCK), jnp.int32),
            pltpu.VMEM((s + QBLOCK, A_HEADS * QBLOCK), jnp.float32),
            pltpu.VMEM((1, QBLOCK), jnp.int32),
        ],
        compiler_params=pltpu.CompilerParams(
            dimension_semantics=("arbitrary", "arbitrary"), vmem_limit_bytes=VMEM_LIMIT),
        name="a_main",
    )(far, qn, proj, proj, misct, kn, vt, proj, near_t)


def mixer_a(proj, misct, b, s, gkv, wukv, gq, gk, bias_a):
    kn, vt, qn = a_pre(proj, b, s, gkv, wukv.astype(jnp.bfloat16), gq, gk)
    near_t = jnp.transpose(_bias_lookup(bias_a, _NEAR_BUCKET_T), (2, 0, 1))
    far = bias_a[_FAR_BUCKET]
    return a_main(proj, misct, qn, kn, vt, near_t, far, b, s)


N_CMP_PAD = 128
N_SB = 32
WIN_KEYS = WINDOW + QBLOCK
CMP_GROUPS = CMP_BLOCK // CMP_STRIDE

_CMP_BUCKET_T = _bucket_np((np.arange(16)[:, None, None] * QBLOCK + np.arange(QBLOCK)[None, None, :])
                           - (np.arange(N_CMP_PAD)[None, :, None] * CMP_STRIDE + CMP_BLOCK - 1))
_WIN_BUCKET_T = _bucket_np(np.arange(QBLOCK)[None, :] + WINDOW - np.arange(WIN_KEYS)[:, None])
_OVERLAP_T = np.array([[1.0 if (n * CMP_STRIDE < j * SEL_BLOCK + SEL_BLOCK and n * CMP_STRIDE + CMP_BLOCK > j * SEL_BLOCK
                              and n < N_CMP_PAD - 1) else 0.0 for n in range(N_CMP_PAD)] for j in range(N_SB)], np.float32)
_SB_EXPAND = (np.arange(2048)[:, None] // SEL_BLOCK == np.arange(N_SB)[None, :]).astype(np.float32)


def _c_pre_kernel(cmp_ref, sel_ref, win_ref, kn_ref, pos_ref, w1_ref, w2_ref,
                  kvc_ref, ksel_ref, vselt_ref, kwin_ref, vwin_ref):
    f32, bf16 = jnp.float32, jnp.bfloat16
    d = C_HEAD_DIM

    def rms(x, g):
        return x * lax.rsqrt(jnp.mean(x * x, axis=-1, keepdims=True) + RMS_EPS) * g

    acc = [jnp.zeros((N_CMP_PAD, 2 * PHI_HIDDEN), f32) for _ in range(CMP_GROUPS)]
    for j in range(CMP_STRIDE):
        xs = cmp_ref[pl.ds(j, N_CMP_PAD, stride=CMP_STRIDE), :]
        for half in range(CMP_GROUPS):
            jj = half * CMP_STRIDE + j
            acc[half] = acc[half] + jnp.dot((xs + pos_ref[jj:jj + 1, :]).astype(bf16), w1_ref[jj],
                                            preferred_element_type=f32)
    hid = acc[0] + pltpu.roll(acc[1], N_CMP_PAD - 1, axis=0)
    kv = jnp.dot(jax.nn.silu(hid).astype(bf16), w2_ref[...], preferred_element_type=f32)
    kvc_ref[0, :, 0:d] = rms(kv[:, 0:d], kn_ref[0:1, :])
    kvc_ref[0, :, d:2 * d] = kv[:, d:2 * d]

    sel = sel_ref[...]
    ksel_ref[...] = rms(sel[:, 0:d], kn_ref[1:2, :])
    vselt_ref[0] = sel.T[d:2 * d, :]
    win = win_ref[...]
    kwin_ref[0, 0:WINDOW, :] = jnp.zeros((WINDOW, d), f32)
    vwin_ref[0, 0:WINDOW, :] = jnp.zeros((WINDOW, d), f32)
    kwin_ref[0, WINDOW:, :] = rms(win[:, 0:d], kn_ref[2:3, :])
    vwin_ref[0, WINDOW:, :] = win[:, d:2 * d]


def c_pre(proj, b, s, k_norm, cmp_pos, phi_w1, phi_w2):
    d = C_HEAD_DIM
    kv0 = OFF["c_kv"] // 128
    pos = jnp.concatenate([cmp_pos[0], cmp_pos[1]], axis=-1)
    w1 = phi_w1.reshape(2, CMP_BLOCK, d, PHI_HIDDEN)
    zero = jnp.zeros((CMP_BLOCK, d, PHI_HIDDEN), jnp.float32)
    w1c = jnp.concatenate([jnp.concatenate([w1[0], zero], axis=-1),
                           jnp.concatenate([zero, w1[1]], axis=-1)], axis=1).astype(jnp.bfloat16)
    z2 = jnp.zeros((PHI_HIDDEN, d), jnp.float32)
    w2c = jnp.concatenate([jnp.concatenate([phi_w2[0], z2], axis=-1),
                           jnp.concatenate([z2, phi_w2[1]], axis=-1)], axis=0).astype(jnp.bfloat16)
    full = lambda a: pl.BlockSpec(a.shape, lambda bb: (0,) * a.ndim)
    return pl.pallas_call(
        _c_pre_kernel,
        grid=(b,),
        in_specs=[
            pl.BlockSpec((s, 128), lambda bb: (bb, kv0)),
            pl.BlockSpec((s, 128), lambda bb: (bb, kv0 + 1)),
            pl.BlockSpec((s, 128), lambda bb: (bb, kv0 + 2)),
            full(k_norm), full(pos), full(w1c), full(w2c),
        ],
        out_specs=[
            pl.BlockSpec((1, N_CMP_PAD, 128), lambda bb: (bb, 0, 0)),
            pl.BlockSpec((s, d), lambda bb: (bb, 0)),
            pl.BlockSpec((1, d, s), lambda bb: (bb, 0, 0)),
            pl.BlockSpec((1, s + WINDOW, d), lambda bb: (bb, 0, 0)),
            pl.BlockSpec((1, s + WINDOW, d), lambda bb: (bb, 0, 0)),
        ],
        out_shape=[
            jax.ShapeDtypeStruct((b, N_CMP_PAD, 128), jnp.float32),
            jax.ShapeDtypeStruct((b * s, d), jnp.float32),
            jax.ShapeDtypeStruct((b, d, s), jnp.float32),
            jax.ShapeDtypeStruct((b, s + WINDOW, d), jnp.float32),
            jax.ShapeDtypeStruct((b, s + WINDOW, d), jnp.float32),
        ],
        compiler_params=pltpu.CompilerParams(dimension_semantics=("arbitrary",), vmem_limit_bytes=VMEM_LIMIT),
        name="c_pre",
    )(proj, proj, proj, k_norm, pos, w1c, w2c)


def _c_main_kernel(cq_ref, z_ref, misct_ref, gq_ref, kvc_ref, ksel_ref, vselt_ref, kwin_ref, vwin_ref,
                   cmptab_ref, near_ref, farrow_ref, wintab_ref, ovt_ref, exp_ref, o_ref, ls_scr):
    s = ksel_ref.shape[0]
    d = C_HEAD_DIM
    hq = C_HEADS * QBLOCK
    i = pl.program_id(1)
    t0 = pl.multiple_of(i * QBLOCK, QBLOCK)
    f32, bf16 = jnp.float32, jnp.bfloat16
    nt = (((1,), (1,)), ((), ()))
    tn = (((0,), (0,)), ((), ()))

    cq = cq_ref[...]
    qs = jnp.concatenate([cq[:, h * d:(h + 1) * d] for h in range(C_HEADS)], axis=0)
    qs = qs * lax.rsqrt(jnp.mean(qs * qs, axis=-1, keepdims=True) + RMS_EPS) * gq_ref[...] * (d ** -0.5)
    qs = qs.astype(bf16)

    def softmax_rows(l, valid):
        l = jnp.where(valid, l, NEG_INF)
        m = _col_reduce(l, jnp.max)
        p = jnp.where(valid, jnp.exp(l - m), 0.0)
        den = _col_reduce(p, jnp.sum)
        return p, den

    kvc = kvc_ref[0]
    lc = lax.dot_general(kvc[:, 0:d].astype(bf16), qs, nt, preferred_element_type=f32) + cmptab_ref[0]
    n_idx = lax.broadcasted_iota(jnp.int32, (N_CMP_PAD, hq), 0)
    t_c = t0 + (lax.broadcasted_iota(jnp.int32, (N_CMP_PAD, hq), 1) & (QBLOCK - 1))
    cmp_valid = n_idx * CMP_STRIDE + (CMP_BLOCK - 1) <= t_c
    pc, den_c = softmax_rows(lc, cmp_valid)
    pc = pc * jnp.where(den_c > 0.0, 1.0 / den_c, 0.0)
    o_cmp = lax.dot_general(kvc[:, d:2 * d].astype(bf16), pc.astype(bf16), tn, preferred_element_type=f32)

    psum = pc[:, 0:QBLOCK]
    for h in range(1, C_HEADS):
        psum = psum + pc[:, h * QBLOCK:(h + 1) * QBLOCK]
    p_hi = psum.astype(bf16)
    p_lo = (psum - p_hi.astype(f32)).astype(bf16)
    ovt = ovt_ref[...]
    imp = jnp.dot(ovt, p_hi, preferred_element_type=f32) + jnp.dot(ovt, p_lo, preferred_element_type=f32)
    j_idx = lax.broadcasted_iota(jnp.int32, (N_SB, QBLOCK), 0)
    t_b = t0 + lax.broadcasted_iota(jnp.int32, (N_SB, QBLOCK), 1)
    cur = lax.shift_right_arithmetic(t_b, 6)
    forced = (j_idx == 0) | (j_idx == cur) | (j_idx == jnp.maximum(cur - 1, 0))
    imp = jnp.where(j_idx * SEL_BLOCK <= t_b, jnp.where(forced, FORCE_SCORE, imp), NEG_INF)
    rank = jnp.zeros((N_SB, QBLOCK), f32)
    for r in range(N_SB):
        row = imp[r:r + 1, :]
        rank = rank + jnp.where((row > imp) | ((row == imp) & (j_idx > r)), 1.0, 0.0)
    picked = jnp.where(rank < float(N_SEL), 1.0, 0.0).astype(bf16)

    kw = kwin_ref[0, pl.ds(t0, WIN_KEYS), :]
    vw = vwin_ref[0, pl.ds(t0, WIN_KEYS), :]
    lw = lax.dot_general(kw.astype(bf16), qs, nt, preferred_element_type=f32) + wintab_ref[...]
    r_idx = lax.broadcasted_iota(jnp.int32, (WIN_KEYS, hq), 0)
    dist = (lax.broadcasted_iota(jnp.int32, (WIN_KEYS, hq), 1) & (QBLOCK - 1)) + WINDOW - r_idx
    win_valid = (dist >= 0) & (dist < WINDOW) & (r_idx + t0 >= WINDOW)
    pw, den_w = softmax_rows(lw, win_valid)
    o_win = lax.dot_general(vw.astype(bf16), pw.astype(bf16), tn, preferred_element_type=f32) / den_w

    g = jax.nn.sigmoid(misct_ref[0, 80:80 + 3 * C_HEADS, :])

    def run(nrows):
        kpos = lax.broadcasted_iota(jnp.int32, (nrows, QBLOCK), 0)
        tpos = t0 + lax.broadcasted_iota(jnp.int32, (nrows, QBLOCK), 1)
        sel_valid = (jnp.dot(exp_ref[0:nrows, :], picked, preferred_element_type=f32) > 0.5) & (kpos <= tpos)
        ls_scr[0:QBLOCK, :] = jnp.zeros((QBLOCK, hq), f32)
        ls_scr[QBLOCK:QBLOCK + nrows, :] = lax.dot_general(ksel_ref[0:nrows, :].astype(bf16), qs, nt,
                                                           preferred_element_type=f32)
        ls_scr[pl.ds(t0, 2 * QBLOCK), :] += near_ref[...] - farrow_ref[...]
        outs = []
        for h in range(C_HEADS):
            cols = slice(h * QBLOCK, (h + 1) * QBLOCK)
            l = jnp.where(sel_valid, ls_scr[QBLOCK:QBLOCK + nrows, cols], NEG_INF)
            m = _col_reduce(l, jnp.max)
            p = jnp.exp(l - m)
            den = _col_reduce(p, jnp.sum)
            o_sel = jnp.dot(vselt_ref[0, :, 0:nrows].astype(bf16), p.astype(bf16), preferred_element_type=f32) / den
            outs.append(g[3 * h:3 * h + 1, :] * o_cmp[:, cols] + g[3 * h + 1:3 * h + 2, :] * o_sel
                        + g[3 * h + 2:3 * h + 3, :] * o_win[:, cols])
        o_ref[...] = jnp.concatenate(outs, axis=0).T * jax.nn.silu(z_ref[...])

    n_var = 4
    per = (s // QBLOCK) // n_var
    for v in range(n_var):
        pl.when((i >= v * per) & (i < (v + 1) * per))(functools.partial(run, (v + 1) * per * QBLOCK))


def c_main(proj, misct, gq, kvc, ksel, vselt, kwin, vwin, cmptab, near, farrow, wintab, b, s):
    n = b * s
    nq = s // QBLOCK
    d = C_HEAD_DIM
    hq = C_HEADS * QBLOCK
    row = lambda bb, i: bb * nq + i
    ovt = jnp.asarray(_OVERLAP_T, jnp.bfloat16)
    expand = jnp.asarray(_SB_EXPAND, jnp.bfloat16)
    const = lambda a: pl.BlockSpec(a.shape, lambda bb, i: (0,) * a.ndim)
    return pl.pallas_call(
        _c_main_kernel,
        grid=(b, nq),
        in_specs=[
            pl.BlockSpec((QBLOCK, C_WIDTH), lambda bb, i: (row(bb, i), OFF["c_q"] // C_WIDTH)),
            pl.BlockSpec((QBLOCK, C_WIDTH), lambda bb, i: (row(bb, i), OFF["c_z"] // C_WIDTH)),
            pl.BlockSpec((1, 128, QBLOCK), lambda bb, i: (bb, 0, i)),
            const(gq),
            pl.BlockSpec((1, N_CMP_PAD, 128), lambda bb, i: (bb, 0, 0)),
            pl.BlockSpec((s, d), lambda bb, i: (bb, 0)),
            pl.BlockSpec((1, d, s), lambda bb, i: (bb, 0, 0)),
            pl.BlockSpec((1, s + WINDOW, d), lambda bb, i: (bb, 0, 0)),
            pl.BlockSpec((1, s + WINDOW, d), lambda bb, i: (bb, 0, 0)),
            pl.BlockSpec((1, N_CMP_PAD, hq), lambda bb, i: (i, 0, 0)),
            const(near), const(farrow), const(wintab), const(ovt), const(expand),
        ],
        out_specs=pl.BlockSpec((QBLOCK, C_WIDTH), lambda bb, i: (row(bb, i), 0)),
        out_shape=jax.ShapeDtypeStruct((n, C_WIDTH), jnp.float32),
        scratch_shapes=[pltpu.VMEM((s + QBLOCK, hq), jnp.float32)],
        compiler_params=pltpu.CompilerParams(
            dimension_semantics=("arbitrary", "arbitrary"), vmem_limit_bytes=VMEM_LIMIT),
        name="c_main",
    )(proj, proj, misct, gq, kvc, ksel, vselt, kwin, vwin, cmptab, near, farrow, wintab, ovt, expand)


def _head_cols(tab):
    return jnp.moveaxis(tab, -1, -2).reshape(*tab.shape[:-2], tab.shape[-1] * tab.shape[-2])


def mixer_c(proj, misct, b, s, gq, k_norm, cmp_pos, phi_w1, phi_w2, bias_c):
    kvc, ksel, vselt, kwin, vwin = c_pre(proj, b, s, k_norm, cmp_pos, phi_w1, phi_w2)
    cmptab = _head_cols(_bias_lookup(bias_c, _CMP_BUCKET_T))
    near = _head_cols(_bias_lookup(bias_c, _NEAR_BUCKET_T))
    wintab = _head_cols(_bias_lookup(bias_c, _WIN_BUCKET_T))
    farrow = jnp.repeat(bias_c[_FAR_BUCKET], QBLOCK).reshape(1, C_HEADS * QBLOCK)
    return c_main(proj, misct, gq.reshape(1, C_HEAD_DIM), kvc, ksel, vselt, kwin, vwin, cmptab, near, farrow,
                  wintab, b, s)


B_HPS = 2
B_GROUP = 4
B_A_LANE = 72
B_B_LANE = 76


def _b_kernel(alog_ref, dtb_ref, q_ref, k_ref, v_ref, z_ref, misc_ref, arow_ref, cwq_ref, cwk_ref, cwv_ref,
              gon_ref, o_ref, qs, ks, vs, ws, at, gtok, btok, gcrow):
    s = q_ref.shape[0]
    dh = B_HEAD_DIM
    c_len = GDN_CHUNK
    j = pl.program_id(1)
    f32, bf16 = jnp.float32, jnp.bfloat16
    hi = lax.Precision.HIGHEST
    nt = (((1,), (1,)), ((), ()))
    tn = (((0,), (0,)), ((), ()))

    rows = lax.broadcasted_iota(jnp.int32, (s, B_HPS * dh), 0)

    def conv_silu(x_ref, w_ref):
        x = x_ref[...]
        acc = x * w_ref[CONV_WIDTH - 1:CONV_WIDTH, :]
        for k in range(1, CONV_WIDTH):
            shifted = jnp.where(rows >= k, pltpu.roll(x, k, axis=0), 0.0)
            acc = acc + shifted * w_ref[CONV_WIDTH - 1 - k:CONV_WIDTH - k, :]
        return jax.nn.silu(acc)

    def l2n(x):
        return x * lax.rsqrt(jnp.sum(x * x, axis=-1, keepdims=True) + RMS_EPS)

    qc = conv_silu(q_ref, cwq_ref)
    kc = conv_silu(k_ref, cwk_ref)
    vs[...] = conv_silu(v_ref, cwv_ref)
    misc = misc_ref[...]
    lane = lax.broadcasted_iota(jnp.int32, misc.shape, 1)
    ri = lax.broadcasted_iota(jnp.int32, (c_len, c_len), 0)
    ci = lax.broadcasted_iota(jnp.int32, (c_len, c_len), 1)
    lower = ci <= ri
    strict = ci < ri
    tri_u = jnp.where(ri <= ci, 1.0, 0.0)
    row_in_chunk = lax.broadcasted_iota(jnp.int32, (s, dh), 0) & (c_len - 1)
    for hh in range(B_HPS):
        sl = slice(hh * dh, (hh + 1) * dh)
        qs[:, sl] = l2n(qc[:, sl]) * (dh ** -0.5)
        ks[:, sl] = l2n(kc[:, sl])
        h = B_HPS * j + hh
        neg_a = -jnp.exp(alog_ref[h])
        dtb = dtb_ref[h]
        a_tok = jnp.sum(jnp.where(lane == B_A_LANE + h, misc, 0.0), axis=-1, keepdims=True)
        b_tok = jnp.sum(jnp.where(lane == B_B_LANE + h, misc, 0.0), axis=-1, keepdims=True)
        g = jnp.broadcast_to(neg_a * jax.nn.softplus(a_tok + dtb), (s, dh))
        for sh in (1, 2, 4, 8, 16, 32):
            g = g + jnp.where(row_in_chunk >= sh, pltpu.roll(g, sh, axis=0), 0.0)
        gtok[hh] = g
        btok[hh] = jnp.broadcast_to(jax.nn.sigmoid(b_tok), (s, dh))
        g_row = neg_a * jax.nn.softplus(arow_ref[0, hh] + dtb)
        gcrow[hh] = jnp.dot(g_row, tri_u, precision=hi, preferred_element_type=f32)

    lo_lanes = (lax.broadcasted_iota(jnp.int32, (c_len, 4 * c_len), 1) & c_len) != 0

    def split_lhs(p):
        p4 = jnp.concatenate([p, p, p, p], axis=1)
        hi4 = p4.astype(bf16).astype(f32)
        return jnp.where(lo_lanes, p4 - hi4, hi4).astype(bf16)

    def split_rhs(x):
        xh = x.astype(bf16)
        xl = (x - xh.astype(f32)).astype(bf16)
        return jnp.concatenate([xh, xh, xl, xl], axis=0)

    def mm(p_split, x):
        return jnp.dot(p_split, split_rhs(x), preferred_element_type=f32)

    def intra_pair(i, carry):
        where, q_c, k_c, v_c, gb, beta, g_row = [], [], [], [], [], [], []
        for c in [B_GROUP * i + cc for cc in range(B_GROUP)]:
            rs = pl.ds(pl.multiple_of(c * c_len, c_len), c_len)
            for hh in range(B_HPS):
                sl = slice(hh * dh, (hh + 1) * dh)
                where.append((rs, hh, sl))
                q_c.append(qs[rs, sl])
                k_c.append(ks[rs, sl])
                v_c.append(vs[rs, sl])
                gb.append(gtok[hh, rs, :])
                beta.append(btok[hh, rs, :])
                g_row.append(gcrow[hh, pl.ds(c, 1), :])
        nch = range(len(where))
        decay = [jnp.where(lower, jnp.exp(jnp.where(lower, gb[n][:, 0:c_len] - g_row[n], 0.0)), 0.0) for n in nch]
        kb = [k_c[n] * beta[n] for n in nch]
        kk = [lax.dot_general(kb[n].astype(bf16), k_c[n].astype(bf16), nt, preferred_element_type=f32) for n in nch]
        p = [-jnp.where(strict, kk[n] * decay[n], 0.0) for n in nch]
        eg = [jnp.exp(gb[n]) for n in nch]
        x = [jnp.concatenate([v_c[n] * beta[n], kb[n] * eg[n]], axis=1) for n in nch]
        ps = [split_lhs(p[n]) for n in nch]
        x = [x[n] + mm(ps[n], x[n]) for n in nch]
        for _ in range(5):
            p = [mm(ps[n], p[n]) for n in nch]
            ps = [split_lhs(p[n]) for n in nch]
            x = [x[n] + mm(ps[n], x[n]) for n in nch]
        attn = [jnp.where(lower, lax.dot_general(q_c[n].astype(bf16), k_c[n].astype(bf16), nt,
                                                 preferred_element_type=f32) * decay[n], 0.0) for n in nch]
        for n, (rs, hh, sl) in enumerate(where):
            g_last = g_row[n][:, c_len - 1:c_len]
            vs[rs, sl] = x[n][:, 0:dh]
            ws[rs, sl] = x[n][:, dh:2 * dh]
            qs[rs, sl] = q_c[n] * eg[n]
            ks[rs, sl] = k_c[n] * jnp.exp(g_last - gb[n])
            at[rs, hh * c_len:(hh + 1) * c_len] = attn[n]
        return carry

    lax.fori_loop(0, s // (B_GROUP * c_len), intra_pair, 0)

    def recur(c, states):
        rs = pl.ds(pl.multiple_of(c * c_len, c_len), c_len)
        hs = range(B_HPS)
        sls = [slice(hh * dh, (hh + 1) * dh) for hh in hs]
        st_b = [states[hh].astype(bf16) for hh in hs]
        v_new = [vs[rs, sls[hh]] - jnp.dot(ws[rs, sls[hh]].astype(bf16), st_b[hh], preferred_element_type=f32)
                 for hh in hs]
        v_nb = [v_new[hh].astype(bf16) for hh in hs]
        o_st = [jnp.dot(qs[rs, sls[hh]].astype(bf16), st_b[hh], preferred_element_type=f32) for hh in hs]
        new_states = [states[hh] * jnp.exp(gcrow[hh, pl.ds(c, 1), c_len - 1:c_len])
                      + lax.dot_general(ks[rs, sls[hh]].astype(bf16), v_nb[hh], tn, preferred_element_type=f32)
                      for hh in hs]
        for hh in hs:
            o_ref[rs, sls[hh]] = o_st[hh] + jnp.dot(at[rs, hh * c_len:(hh + 1) * c_len].astype(bf16), v_nb[hh],
                                                    preferred_element_type=f32)
        return tuple(new_states)

    lax.fori_loop(0, s // c_len, recur, tuple(jnp.zeros((dh, dh), f32) for _ in range(B_HPS)))

    for hh in range(B_HPS):
        sl = slice(hh * dh, (hh + 1) * dh)
        o = o_ref[:, sl]
        o = o * lax.rsqrt(jnp.mean(o * o, axis=-1, keepdims=True) + RMS_EPS) * gon_ref[...]
        o_ref[:, sl] = o * jax.nn.silu(z_ref[:, sl])


def mixer_b(proj, misct, b, s, conv_w, a_log, dt_bias, out_norm):
    n = b * s
    w2 = B_HPS * B_HEAD_DIM
    nc = s // GDN_CHUNK
    arow = misct[:, B_A_LANE:B_A_LANE + B_HEADS, :].reshape(b, B_HEADS, nc, GDN_CHUNK)
    qkv0 = OFF["b_qkv"] // w2
    kstep = B_WIDTH // w2
    smem = pl.BlockSpec(memory_space=pltpu.SMEM)
    return pl.pallas_call(
        _b_kernel,
        grid=(b, B_HEADS // B_HPS),
        in_specs=[
            smem, smem,
            pl.BlockSpec((s, w2), lambda bb, j: (bb, qkv0 + j)),
            pl.BlockSpec((s, w2), lambda bb, j: (bb, qkv0 + kstep + j)),
            pl.BlockSpec((s, w2), lambda bb, j: (bb, qkv0 + 2 * kstep + j)),
            pl.BlockSpec((s, w2), lambda bb, j: (bb, OFF["b_z"] // w2 + j)),
            pl.BlockSpec((s, 128), lambda bb, j: (bb, MISC_OFF // 128)),
            pl.BlockSpec((1, B_HPS, nc, GDN_CHUNK), lambda bb, j: (bb, j, 0, 0)),
            pl.BlockSpec((CONV_WIDTH, w2), lambda bb, j: (0, j)),
            pl.BlockSpec((CONV_WIDTH, w2), lambda bb, j: (0, kstep + j)),
            pl.BlockSpec((CONV_WIDTH, w2), lambda bb, j: (0, 2 * kstep + j)),
            pl.BlockSpec((1, B_HEAD_DIM), lambda bb, j: (0, 0)),
        ],
        out_specs=pl.BlockSpec((s, w2), lambda bb, j: (bb, j)),
        out_shape=jax.ShapeDtypeStruct((n, B_WIDTH), jnp.float32),
        scratch_shapes=[
            pltpu.VMEM((s, w2), jnp.float32), pltpu.VMEM((s, w2), jnp.float32), pltpu.VMEM((s, w2), jnp.float32),
            pltpu.VMEM((s, w2), jnp.float32), pltpu.VMEM((s, B_HPS * GDN_CHUNK), jnp.float32),
            pltpu.VMEM((B_HPS, s, B_HEAD_DIM), jnp.float32), pltpu.VMEM((B_HPS, s, B_HEAD_DIM), jnp.float32),
            pltpu.VMEM((B_HPS, nc, GDN_CHUNK), jnp.float32),
        ],
        compiler_params=pltpu.CompilerParams(
            dimension_semantics=("arbitrary", "arbitrary"), vmem_limit_bytes=VMEM_LIMIT),
        name="gdn",
    )(a_log, dt_bias, proj, proj, proj, proj, proj, arow, conv_w, conv_w, conv_w, out_norm.reshape(1, B_HEAD_DIM))


def _rms_norm(x, g):
    y = x * lax.rsqrt(jnp.mean(x * x, axis=-1, keepdims=True) + RMS_EPS)
    return y * g


def _l2norm(x):
    return x * lax.rsqrt(jnp.sum(x * x, axis=-1, keepdims=True) + RMS_EPS)


def _masked_softmax(logits, mask):
    logits = jnp.where(mask, logits, NEG_INF)
    return jnp.where(mask, jax.nn.softmax(logits, axis=-1), 0.0)


def _t5_bucket(dist):
    dist = jnp.maximum(dist, 0)
    log_ratio = jnp.log(jnp.maximum(dist, 1).astype(jnp.float32) / BUCKET_MAX_EXACT) / math.log(BUCKET_MAX_DIST / BUCKET_MAX_EXACT)
    large = BUCKET_MAX_EXACT + (log_ratio * (N_BUCKETS - BUCKET_MAX_EXACT)).astype(jnp.int32)
    large = jnp.minimum(large, N_BUCKETS - 1)
    return jnp.where(dist < BUCKET_MAX_EXACT, dist, large)


def _to_qblocks(a):
    b, s = a.shape[:2]
    return jnp.moveaxis(a.reshape(b, s // QBLOCK, QBLOCK, *a.shape[2:]), 1, 0)


def _from_qblocks(a):
    nb, b = a.shape[:2]
    return jnp.moveaxis(a, 0, 1).reshape(b, nb * QBLOCK, *a.shape[3:])


def _dsa_mixer(q, k, v, q_idx, k_idx, w_idx, bias_tab):
    b, s = q.shape[:2]
    topk = min(TOPK_MAX, s // 4)
    bidx = jnp.arange(b)[:, None, None]
    key_pos = jnp.arange(s)
    scale = A_HEAD_DIM ** -0.5

    def block(args):
        qb, qib, wb, start = args
        t = start + jnp.arange(QBLOCK)
        idx_logits = jnp.einsum('bqhd,bsd->bqhs', qib, k_idx) * (IDX_DIM ** -0.5)
        score = jnp.einsum('bqh,bqhs->bqs', wb * (IDX_HEADS ** -0.5), jax.nn.relu(idx_logits))
        score = jnp.where(key_pos[None, None, :] <= t[None, :, None], score, NEG_INF)
        _, sel = lax.top_k(score, topk)
        kg = k[bidx, sel]
        vg = v[bidx, sel]
        dist = t[None, :, None] - sel
        bias = jnp.transpose(bias_tab[_t5_bucket(dist)], (0, 3, 1, 2))
        logits = jnp.einsum('bqhd,bqkhd->bhqk', qb, kg) * scale + bias
        p = _masked_softmax(logits, (dist >= 0)[:, None])
        return jnp.einsum('bhqk,bqkhd->bqhd', p, vg)

    starts = jnp.arange(s // QBLOCK, dtype=jnp.int32) * QBLOCK
    out = lax.map(block, (_to_qblocks(q), _to_qblocks(q_idx), _to_qblocks(w_idx), starts))
    return _from_qblocks(out)


def _causal_depthwise_conv(x, w):
    c = x.shape[-1]
    return lax.conv_general_dilated(x, w[:, None, :], window_strides=(1,), padding=[(CONV_WIDTH - 1, 0)],
                                    dimension_numbers=('NWC', 'WIO', 'NWC'), feature_group_count=c)


def _gdn_mixer(qkv, a_in, b_in, conv_w, a_log, dt_bias):
    b, s = qkv.shape[:2]
    n = s // GDN_CHUNK
    qkv = jax.nn.silu(_causal_depthwise_conv(qkv, conv_w))
    q, k, v = jnp.split(qkv, 3, axis=-1)
    sh = lambda t: t.reshape(b, s, B_HEADS, B_HEAD_DIM)
    q = _l2norm(sh(q)) * (B_HEAD_DIM ** -0.5)
    k = _l2norm(sh(k))
    v = sh(v)
    beta = jax.nn.sigmoid(b_in)
    g = -jnp.exp(a_log) * jax.nn.softplus(a_in + dt_bias)

    def chunk(t):
        return jnp.moveaxis(t.reshape(b, n, GDN_CHUNK, *t.shape[2:]), 3, 2)

    q, k, v, beta, g = (chunk(t) for t in (q, k, v, beta, g))
    g_cum = jnp.cumsum(g, axis=-1)
    g_last = g_cum[..., -1]
    lower = jnp.tril(jnp.ones((GDN_CHUNK, GDN_CHUNK), dtype=bool))
    strict = jnp.tril(jnp.ones((GDN_CHUNK, GDN_CHUNK), dtype=bool), -1)
    diff = g_cum[..., :, None] - g_cum[..., None, :]
    decay = jnp.where(lower, jnp.exp(jnp.where(lower, diff, 0.0)), 0.0)
    k_beta = k * beta[..., None]
    a_mat = jnp.where(strict, jnp.einsum('bnhid,bnhjd->bnhij', k_beta, k) * decay, 0.0)
    eye = jnp.eye(GDN_CHUNK, dtype=jnp.float32)
    t_inv = lax.linalg.triangular_solve(eye + a_mat, jnp.broadcast_to(eye, a_mat.shape), left_side=True, lower=True)
    u = jnp.einsum('bnhij,bnhjd->bnhid', t_inv, v * beta[..., None])
    w = jnp.einsum('bnhij,bnhjd->bnhid', t_inv, k_beta * jnp.exp(g_cum)[..., None])
    attn = jnp.einsum('bnhid,bnhjd->bnhij', q, k) * decay
    q_dec = q * jnp.exp(g_cum)[..., None]
    k_dec = k * jnp.exp(g_last[..., None] - g_cum)[..., None]

    def step(state, xs):
        u_c, w_c, q_c, k_c, attn_c, gl_c = xs
        v_new = u_c - jnp.einsum('bhik,bhkv->bhiv', w_c, state)
        o = jnp.einsum('bhik,bhkv->bhiv', q_c, state) + jnp.einsum('bhij,bhjv->bhiv', attn_c, v_new)
        state = state * jnp.exp(gl_c)[..., None, None] + jnp.einsum('bhik,bhiv->bhkv', k_c, v_new)
        return state, o

    state0 = jnp.zeros((b, B_HEADS, B_HEAD_DIM, B_HEAD_DIM), jnp.float32)
    xs = tuple(jnp.moveaxis(t, 1, 0) for t in (u, w, q_dec, k_dec, attn, g_last))
    _, o = lax.scan(step, state0, xs)
    o = jnp.moveaxis(jnp.moveaxis(o, 0, 1), 2, 3)
    return o.reshape(b, s, B_HEADS, B_HEAD_DIM)


def _compress_blocks(tok, pos, w1, w2):
    b, s, d = tok.shape
    n_cmp = (s - CMP_BLOCK) // CMP_STRIDE + 1
    idx = jnp.arange(n_cmp)[:, None] * CMP_STRIDE + jnp.arange(CMP_BLOCK)[None, :]
    blocks = tok[:, idx] + pos
    return jax.nn.silu(blocks.reshape(b, n_cmp, CMP_BLOCK * d) @ w1) @ w2


def _nsa_mixer(q, kv, gates, k_norm, cmp_pos, phi_w1, phi_w2, bias_tab):
    b, s = q.shape[:2]
    scale = C_HEAD_DIM ** -0.5
    t_all = jnp.arange(s)
    k_cmp, v_cmp, k_sel, v_sel, k_win, v_win = (kv[:, :, i] for i in range(6))
    k_sel = _rms_norm(k_sel, k_norm[1])
    k_win = _rms_norm(k_win, k_norm[2])
    kc = _rms_norm(_compress_blocks(k_cmp, cmp_pos[0], phi_w1[0], phi_w2[0]), k_norm[0])
    vc = _compress_blocks(v_cmp, cmp_pos[1], phi_w1[1], phi_w2[1])
    n_cmp = kc.shape[1]
    cmp_start = jnp.arange(n_cmp) * CMP_STRIDE
    cmp_end = cmp_start + CMP_BLOCK - 1
    cmp_valid = cmp_end[None, :] <= t_all[:, None]
    cmp_bias = jnp.transpose(bias_tab[_t5_bucket(t_all[:, None] - cmp_end[None, :])], (2, 0, 1))
    logits = jnp.einsum('bqhd,bnd->bhqn', q, kc) * scale + cmp_bias
    p_cmp = _masked_softmax(logits, cmp_valid[None, None])
    o_cmp = jnp.einsum('bhqn,bnd->bqhd', p_cmp, vc)
    n_sb = s // SEL_BLOCK
    n_pick = min(N_SEL, n_sb)
    sb_start = jnp.arange(n_sb) * SEL_BLOCK
    overlap = ((cmp_start[:, None] < sb_start[None, :] + SEL_BLOCK) & (cmp_start[:, None] + CMP_BLOCK > sb_start[None, :])).astype(jnp.float32)
    importance = jnp.einsum('bhqn,nj->bqj', p_cmp, overlap)
    cur = t_all // SEL_BLOCK
    blk = jnp.arange(n_sb)
    forced = (blk[None, :] == 0) | (blk[None, :] == cur[:, None]) | (blk[None, :] == jnp.maximum(cur[:, None] - 1, 0))
    admissible = sb_start[None, :] <= t_all[:, None]
    importance = jnp.where(admissible[None], jnp.where(forced[None], FORCE_SCORE, importance), NEG_INF)
    _, sel = lax.top_k(importance, n_pick)
    k_blocks = k_sel.reshape(b, n_sb, SEL_BLOCK, C_HEAD_DIM)
    v_blocks = v_sel.reshape(b, n_sb, SEL_BLOCK, C_HEAD_DIM)
    k_win_pad = jnp.pad(k_win, ((0, 0), (WINDOW, 0), (0, 0)))
    v_win_pad = jnp.pad(v_win, ((0, 0), (WINDOW, 0), (0, 0)))
    bidx = jnp.arange(b)[:, None, None]
    n_keys = n_pick * SEL_BLOCK

    def block(args):
        qb, selb, start = args
        t = start + jnp.arange(QBLOCK)
        kg = k_blocks[bidx, selb].reshape(b, QBLOCK, n_keys, C_HEAD_DIM)
        vg = v_blocks[bidx, selb].reshape(b, QBLOCK, n_keys, C_HEAD_DIM)
        s_pos = (selb[..., None] * SEL_BLOCK + jnp.arange(SEL_BLOCK)).reshape(b, QBLOCK, n_keys)
        dist = t[None, :, None] - s_pos
        bias = jnp.transpose(bias_tab[_t5_bucket(dist)], (0, 3, 1, 2))
        lg = jnp.einsum('bqhd,bqkd->bhqk', qb, kg) * scale + bias
        p = _masked_softmax(lg, (dist >= 0)[:, None])
        o_sel = jnp.einsum('bhqk,bqkd->bqhd', p, vg)
        kw = lax.dynamic_slice_in_dim(k_win_pad, start, QBLOCK + WINDOW, axis=1)
        vw = lax.dynamic_slice_in_dim(v_win_pad, start, QBLOCK + WINDOW, axis=1)
        s_w = start - WINDOW + jnp.arange(QBLOCK + WINDOW)
        dist_w = t[:, None] - s_w[None, :]
        valid_w = (s_w[None, :] >= 0) & (dist_w >= 0) & (dist_w < WINDOW)
        bias_w = jnp.transpose(bias_tab[_t5_bucket(dist_w)], (2, 0, 1))
        lw = jnp.einsum('bqhd,bsd->bhqs', qb, kw) * scale + bias_w
        pw = _masked_softmax(lw, valid_w[None, None])
        o_win = jnp.einsum('bhqs,bsd->bqhd', pw, vw)
        return o_sel, o_win

    starts = jnp.arange(s // QBLOCK, dtype=jnp.int32) * QBLOCK
    o_sel, o_win = lax.map(block, (_to_qblocks(q), _to_qblocks(sel), starts))
    o_sel = _from_qblocks(o_sel)
    o_win = _from_qblocks(o_win)
    g = jax.nn.sigmoid(gates).reshape(b, s, C_HEADS, 3)
    return g[..., 0:1] * o_cmp + g[..., 1:2] * o_sel + g[..., 2:3] * o_win


def _field(proj3, name):
    return proj3[..., OFF[name]:OFF[name] + _ORIG[name][1]]


def kernel(x, norm_g, w_in, a_kv_norm, a_w_ukv, a_q_norm, a_k_norm, b_conv, b_a_log, b_dt_bias, b_out_norm, c_q_norm, c_k_norm, c_cmp_pos, c_phi_w1, c_phi_w2, w_branch, w_out, rel_bias):
    b, s, _ = x.shape
    n = b * s
    bias_a = rel_bias[:, :A_HEADS]
    bias_c = rel_bias[:, A_HEADS:]
    xf = x.reshape(n, D_MODEL)
    perm = jnp.asarray(_PERM_IDX)
    valid = jnp.asarray(_PERM_VALID)
    sh = lambda t, h, d: t.reshape(b, s, h, d)
    for l in range(DEPTH):
        w_p = jnp.where(valid[None, :], w_in[l][:, perm], 0.0).astype(jnp.bfloat16)
        proj = in_proj(xf, norm_g[l], w_p)
        p3 = proj.reshape(b, s, N_PAD)

        misct = jnp.transpose(p3[..., MISC_OFF:], (0, 2, 1))
        y_a = mixer_a(proj, misct, b, s, a_kv_norm[l], a_w_ukv[l], a_q_norm[l], a_k_norm[l], bias_a)

        y_b = mixer_b(proj, misct, b, s, b_conv[l], b_a_log[l], b_dt_bias[l], b_out_norm[l])
        y_c = mixer_c(proj, misct, b, s, c_q_norm[l], c_k_norm[l], c_cmp_pos[l], c_phi_w1[l], c_phi_w2[l], bias_c)

        wbr = w_branch[l].astype(jnp.bfloat16)
        xf = merge(xf, proj, y_a, y_b, y_c, wbr[:A_WIDTH], wbr[A_WIDTH:A_WIDTH + B_WIDTH],
                   wbr[A_WIDTH + B_WIDTH:], w_out[l].astype(jnp.bfloat16))
    return xf.reshape(b, s, D_MODEL)
```

```python
import functools
import math

import jax
import jax.numpy as jnp
import numpy as np
from jax import lax
from jax.experimental import pallas as pl
from jax.experimental.pallas import tpu as pltpu

D_MODEL = 1024
DEPTH = 4
QBLOCK = 128
NEG_INF = -1e30
FORCE_SCORE = 1e9
RMS_EPS = 1e-6

A_HEADS = 4
A_HEAD_DIM = 64
A_WIDTH = A_HEADS * A_HEAD_DIM
A_KV_LATENT = 128
IDX_HEADS = 8
IDX_DIM = 64
TOPK_MAX = 256

B_HEADS = 4
B_HEAD_DIM = 128
B_WIDTH = B_HEADS * B_HEAD_DIM
CONV_WIDTH = 4
GDN_CHUNK = 64

C_HEADS = 4
C_HEAD_DIM = 64
C_WIDTH = C_HEADS * C_HEAD_DIM
CMP_BLOCK = 32
CMP_STRIDE = 16
SEL_BLOCK = 64
N_SEL = 16
WINDOW = 512
PHI_HIDDEN = 256

N_BUCKETS = 32
BUCKET_MAX_EXACT = 16
BUCKET_MAX_DIST = 128

VMEM_LIMIT = 48 * 1024 * 1024

_ORIG = {}
_o = 0
for _name, _w in (("a_q", 256), ("a_ckv", 128), ("a_qi", 512), ("a_ki", 64), ("a_wi", 8), ("a_z", 256),
                  ("b_qkv", 1536), ("b_a", 4), ("b_b", 4), ("b_z", 512),
                  ("c_q", 256), ("c_kv", 384), ("c_g", 12), ("c_z", 256), ("g", 3072)):
    _ORIG[_name] = (_o, _w)
    _o += _w
N_IN = _o

_NEW_ORDER = ("g", "b_qkv", "b_z", "a_qi", "a_q", "a_z", "c_q", "c_z", "c_kv", "a_ckv",
              "a_ki", "a_wi", "b_a", "b_b", "c_g")
OFF = {}
_o = 0
for _name in _NEW_ORDER:
    OFF[_name] = _o
    _o += _ORIG[_name][1]
MISC_OFF = OFF["a_ki"]
N_PAD = 7296
assert MISC_OFF == 7168 and _o <= N_PAD
A_WI_LANE = OFF["a_wi"] - MISC_OFF
B_A_LANE = OFF["b_a"] - MISC_OFF
B_B_LANE = OFF["b_b"] - MISC_OFF
C_G_LANE = OFF["c_g"] - MISC_OFF


def _perm_indices():
    idx = np.zeros((N_PAD,), np.int32)
    valid = np.zeros((N_PAD,), bool)
    for name in _NEW_ORDER:
        o_old, w = _ORIG[name]
        idx[OFF[name]:OFF[name] + w] = np.arange(o_old, o_old + w)
        valid[OFF[name]:OFF[name] + w] = True
    return idx, valid


_PERM_IDX, _PERM_VALID = _perm_indices()


IN_TM = 512
IN_TN = N_PAD // 3


def _in_proj_kernel(x_ref, g_ref, w_ref, o_ref):
    x = x_ref[...]
    ms = jnp.mean(x * x, axis=-1, keepdims=True)
    h = (x * lax.rsqrt(ms + RMS_EPS)) * g_ref[...]
    o_ref[...] = jnp.dot(h.astype(jnp.bfloat16), w_ref[...], preferred_element_type=jnp.float32)


def in_proj(xf, g, w_bf16):
    n = xf.shape[0]
    return pl.pallas_call(
        _in_proj_kernel,
        grid=(N_PAD // IN_TN, n // IN_TM),
        in_specs=[
            pl.BlockSpec((IN_TM, D_MODEL), lambda j, i: (i, 0)),
            pl.BlockSpec((1, D_MODEL), lambda j, i: (0, 0)),
            pl.BlockSpec((D_MODEL, IN_TN), lambda j, i: (0, j)),
        ],
        out_specs=pl.BlockSpec((IN_TM, IN_TN), lambda j, i: (i, j)),
        out_shape=jax.ShapeDtypeStruct((n, N_PAD), jnp.float32),
        compiler_params=pltpu.CompilerParams(
            dimension_semantics=("arbitrary", "arbitrary"), vmem_limit_bytes=VMEM_LIMIT),
        name="in_proj",
    )(xf, g.reshape(1, D_MODEL), w_bf16)


MG_TM = 256


def _merge_kernel(x_ref, g_ref, ya_ref, yb_ref, yc_ref, wa_ref, wb_ref, wc_ref, wo_ref, o_ref):
    def branch(y_ref, w_ref, k):
        p = jnp.dot(y_ref[...].astype(jnp.bfloat16), w_ref[...], preferred_element_type=jnp.float32)
        return _sigmoid(g_ref[:, k * D_MODEL:(k + 1) * D_MODEL]) * p

    merged = branch(ya_ref, wa_ref, 0) + branch(yb_ref, wb_ref, 1) + branch(yc_ref, wc_ref, 2)
    o_ref[...] = x_ref[...] + jnp.dot(merged.astype(jnp.bfloat16), wo_ref[...],
                                      preferred_element_type=jnp.float32)


def merge(xf, proj, ya, yb, yc, wa, wb, wc, wo):
    n = xf.shape[0]
    row = lambda w: pl.BlockSpec((MG_TM, w), lambda i: (i, 0))
    full = lambda a: pl.BlockSpec(a.shape, lambda i: (0, 0))
    return pl.pallas_call(
        _merge_kernel,
        grid=(n // MG_TM,),
        in_specs=[row(D_MODEL), row(3 * D_MODEL), row(A_WIDTH), row(B_WIDTH), row(C_WIDTH),
                  full(wa), full(wb), full(wc), full(wo)],
        out_specs=row(D_MODEL),
        out_shape=jax.ShapeDtypeStruct((n, D_MODEL), jnp.float32),
        compiler_params=pltpu.CompilerParams(
            dimension_semantics=("arbitrary",), vmem_limit_bytes=VMEM_LIMIT),
        name="merge",
    )(xf, proj, ya, yb, yc, wa, wb, wc, wo)


def _bucket_np(dist):
    d = np.maximum(np.asarray(dist, np.int64), 0)
    ratio = np.log(np.maximum(d, 1).astype(np.float64) / BUCKET_MAX_EXACT) / math.log(BUCKET_MAX_DIST / BUCKET_MAX_EXACT)
    scaled = ratio * (N_BUCKETS - BUCKET_MAX_EXACT)
    frac = scaled - np.floor(scaled)
    edge = (d > BUCKET_MAX_EXACT) & (d < BUCKET_MAX_DIST) & ((frac < 1e-4) | (frac > 1 - 1e-4))
    assert not edge.any(), "bucket boundary too close to an integer distance"
    large = np.minimum(BUCKET_MAX_EXACT + np.floor(scaled + 1e-9).astype(np.int64), N_BUCKETS - 1)
    return np.where(d < BUCKET_MAX_EXACT, d, large).astype(np.int32)


_NEAR_BUCKET_T = _bucket_np(np.arange(QBLOCK)[None, :] + QBLOCK - np.arange(2 * QBLOCK)[:, None])
_FAR_BUCKET = int(_bucket_np(np.array([BUCKET_MAX_DIST]))[0])
assert (_bucket_np(np.arange(BUCKET_MAX_DIST, 4096)) == _FAR_BUCKET).all()

INT_MIN = -2 ** 31


COL_SLAB = 64


def _col_reduce(x, op):
    rows = x.shape[0]
    if rows > COL_SLAB and rows % COL_SLAB == 0:
        x = op(x.reshape(rows // COL_SLAB, COL_SLAB, x.shape[1]), axis=0)
    return op(x, axis=0, keepdims=True)


ROW_VARIANTS = 8


def _run_causal_variant(run, i, s):
    per = (s // QBLOCK) // ROW_VARIANTS
    for v in range(ROW_VARIANTS):
        pl.when((i >= v * per) & (i < (v + 1) * per))(functools.partial(run, (v + 1) * per * QBLOCK))


def _sigmoid(x):
    return 0.5 * jnp.tanh(0.5 * x) + 0.5


def _silu(x):
    return x * _sigmoid(x)


def _bias_lookup(table, buckets):
    idx = jnp.asarray(buckets)[..., None]
    out = jnp.zeros(idx.shape[:-1] + (table.shape[1],), table.dtype)
    for bkt in range(N_BUCKETS):
        out = jnp.where(idx == bkt, table[bkt], out)
    return out


A_PRE_TM = 512


def _a_pre_kernel(ckv_ref, q_ref, gkv_ref, wukv_ref, gq_ref, gk_ref, hm_ref, kn_ref, vt_ref, qn_ref):
    c = ckv_ref[...]
    c = c * lax.rsqrt(jnp.mean(c * c, axis=-1, keepdims=True) + RMS_EPS) * gkv_ref[...]
    kv = jnp.dot(c.astype(jnp.bfloat16), wukv_ref[...], preferred_element_type=jnp.float32)
    hm = hm_ref[...]

    def head_rms(x, g):
        ms = jnp.dot(x * x, hm, precision=lax.Precision.HIGHEST, preferred_element_type=jnp.float32)
        return x * lax.rsqrt(ms + RMS_EPS) * g

    kn_ref[...] = head_rms(kv[:, :A_WIDTH], gk_ref[...])
    qn_ref[...] = head_rms(q_ref[...], gq_ref[...]) * (A_HEAD_DIM ** -0.5)
    vt_ref[0] = kv[:, A_WIDTH:].T


def a_pre(proj, b, s, gkv, wukv_bf16, gq, gk):
    n = b * s
    nt = s // A_PRE_TM
    hm = jnp.asarray(np.kron(np.eye(A_HEADS), np.ones((A_HEAD_DIM, A_HEAD_DIM))) / A_HEAD_DIM, jnp.float32)
    row = lambda bb, j: (bb * nt + j, 0)
    full = lambda a: pl.BlockSpec(a.shape, lambda bb, j: (0,) * a.ndim)
    gq = jnp.tile(gq, A_HEADS).reshape(1, A_WIDTH)
    gk = jnp.tile(gk, A_HEADS).reshape(1, A_WIDTH)
    gkv = gkv.reshape(1, A_KV_LATENT)
    return pl.pallas_call(
        _a_pre_kernel,
        grid=(b, nt),
        in_specs=[
            pl.BlockSpec((A_PRE_TM, A_KV_LATENT), lambda bb, j: (bb * nt + j, OFF["a_ckv"] // A_KV_LATENT)),
            pl.BlockSpec((A_PRE_TM, A_WIDTH), lambda bb, j: (bb * nt + j, OFF["a_q"] // A_WIDTH)),
            full(gkv), full(wukv_bf16), full(gq), full(gk), full(hm),
        ],
        out_specs=[
            pl.BlockSpec((A_PRE_TM, A_WIDTH), row),
            pl.BlockSpec((1, A_WIDTH, A_PRE_TM), lambda bb, j: (bb, 0, j)),
            pl.BlockSpec((A_PRE_TM, A_WIDTH), row),
        ],
        out_shape=[
            jax.ShapeDtypeStruct((n, A_WIDTH), jnp.float32),
            jax.ShapeDtypeStruct((b, A_WIDTH, s), jnp.float32),
            jax.ShapeDtypeStruct((n, A_WIDTH), jnp.float32),
        ],
        compiler_params=pltpu.CompilerParams(
            dimension_semantics=("arbitrary", "arbitrary"), vmem_limit_bytes=VMEM_LIMIT),
        name="a_pre",
    )(proj, proj, gkv, wukv_bf16, gq, gk, hm)


def _a_main_kernel(far_ref, qn_ref, qi_ref, z_ref, misct_ref, kn_ref, vt_ref, ki_ref, near_ref, o_ref,
                   key_scr, lg_scr, j_scr):
    s = kn_ref.shape[0]
    i = pl.program_id(1)
    t0 = pl.multiple_of(i * QBLOCK, QBLOCK)
    f32, bf16 = jnp.float32, jnp.bfloat16
    nt = (((1,), (1,)), ((), ()))
    kf = float(TOPK_MAX)

    def run(nrows):

        qi = qi_ref[...] * (IDX_DIM ** -0.5)
        qstack = jnp.concatenate([qi[:, h * IDX_DIM:(h + 1) * IDX_DIM] for h in range(IDX_HEADS)], axis=0)
        ki = ki_ref[0:nrows, 0:IDX_DIM]
        sc = lax.dot_general(ki.astype(bf16), qstack.astype(bf16), nt, preferred_element_type=f32)
        wt = misct_ref[0, A_WI_LANE:A_WI_LANE + IDX_HEADS, :] * (IDX_HEADS ** -0.5)
        score = jnp.zeros((nrows, QBLOCK), f32)
        for h in range(IDX_HEADS):
            score = score + wt[h:h + 1, :] * jnp.maximum(sc[:, h * QBLOCK:(h + 1) * QBLOCK], 0.0)
        score = score + 0.0
        kpos = lax.broadcasted_iota(jnp.int32, (nrows, QBLOCK), 0)
        tpos = t0 + lax.broadcasted_iota(jnp.int32, (nrows, QBLOCK), 1)
        causal = kpos <= tpos
        bits = pltpu.bitcast(score, jnp.int32)
        key = jnp.where(bits < 0, bits ^ jnp.int32(0x7FFFFFFF), bits)
        key_scr[0:nrows, :] = jnp.where(causal, key, jnp.int32(INT_MIN))

        def count_ge(cand):
            return _col_reduce(jnp.where(key_scr[0:nrows, :] >= cand, 1.0, 0.0), jnp.sum)

        def bisect(it, thr):
            cand = thr + lax.shift_left(jnp.int32(1), 31 - it)
            return jnp.where(count_ge(cand) >= kf, cand, thr)

        thr = lax.fori_loop(0, 32, bisect, jnp.full((1, QBLOCK), INT_MIN, jnp.int32))
        n_ge = count_ge(thr)
        need = kf - count_ge(thr + 1)

        j_scr[...] = jnp.full((1, QBLOCK), nrows - 1, jnp.int32)
        surplus = jnp.where((n_ge > kf) & (thr > INT_MIN), 1.0, 0.0)

        @pl.when(jnp.max(surplus) > 0.0)
        def _():
            def bisect_idx(it, lohi):
                lo, hi = lohi
                mid = lax.shift_right_arithmetic(lo + hi, 1)
                k = key_scr[0:nrows, :]
                kp = lax.broadcasted_iota(jnp.int32, (nrows, QBLOCK), 0)
                c = _col_reduce(jnp.where((k == thr) & (kp <= mid), 1.0, 0.0), jnp.sum)
                ok = c >= need
                return jnp.where(ok, lo, mid), jnp.where(ok, mid, hi)

            lo0 = jnp.full((1, QBLOCK), -1, jnp.int32)
            hi0 = jnp.full((1, QBLOCK), nrows - 1, jnp.int32)
            _, hi = lax.fori_loop(0, 11, bisect_idx, (lo0, hi0))
            j_scr[...] = hi

        key = key_scr[0:nrows, :]
        sel = ((key > thr) | ((key == thr) & (kpos <= j_scr[...]))) & causal

        qn = qn_ref[...]
        lane = lax.broadcasted_iota(jnp.int32, (QBLOCK, A_WIDTH), 1)
        qblk = jnp.concatenate(
            [jnp.where((lane >= h * A_HEAD_DIM) & (lane < (h + 1) * A_HEAD_DIM), qn, 0.0) for h in range(A_HEADS)],
            axis=0)
        lg_scr[0:QBLOCK, :] = jnp.zeros((QBLOCK, A_HEADS * QBLOCK), f32)
        lg_scr[QBLOCK:QBLOCK + nrows, :] = lax.dot_general(kn_ref[0:nrows, :].astype(bf16), qblk.astype(bf16), nt,
                                                           preferred_element_type=f32)
        outs = []
        for h in range(A_HEADS):
            cols = slice(h * QBLOCK, (h + 1) * QBLOCK)
            lg_scr[pl.ds(t0, 2 * QBLOCK), cols] += near_ref[h] - far_ref[h]
            l = jnp.where(sel, lg_scr[QBLOCK:QBLOCK + nrows, cols], NEG_INF)
            m = _col_reduce(l, jnp.max)
            p = jnp.exp(l - m)
            den = _col_reduce(p, jnp.sum)
            vt = vt_ref[0, h * A_HEAD_DIM:(h + 1) * A_HEAD_DIM, 0:nrows]
            o_t = jnp.dot(vt.astype(bf16), p.astype(bf16), preferred_element_type=f32)
            outs.append(o_t / den)
        o = jnp.concatenate(outs, axis=0).T
        o_ref[...] = o * _silu(z_ref[...])

    _run_causal_variant(run, i, s)


def a_main(proj, misct, qn, kn, vt, near_t, far, b, s):
    n = b * s
    nq = s // QBLOCK
    row = lambda bb, i: bb * nq + i
    return pl.pallas_call(
        _a_main_kernel,
        grid=(b, nq),
        in_specs=[
            pl.BlockSpec(memory_space=pltpu.SMEM),
            pl.BlockSpec((QBLOCK, A_WIDTH), lambda bb, i: (row(bb, i), 0)),
            pl.BlockSpec((QBLOCK, IDX_HEADS * IDX_DIM), lambda bb, i: (row(bb, i), OFF["a_qi"] // (IDX_HEADS * IDX_DIM))),
            pl.BlockSpec((QBLOCK, A_WIDTH), lambda bb, i: (row(bb, i), OFF["a_z"] // A_WIDTH)),
            pl.BlockSpec((1, 128, QBLOCK), lambda bb, i: (bb, 0, i)),
            pl.BlockSpec((s, A_WIDTH), lambda bb, i: (bb, 0)),
            pl.BlockSpec((1, A_WIDTH, s), lambda bb, i: (bb, 0, 0)),
            pl.BlockSpec((s, 128), lambda bb, i: (bb, MISC_OFF // 128)),
            pl.BlockSpec((A_HEADS, 2 * QBLOCK, QBLOCK), lambda bb, i: (0, 0, 0)),
        ],
        out_specs=pl.BlockSpec((QBLOCK, A_WIDTH), lambda bb, i: (row(bb, i), 0)),
        out_shape=jax.ShapeDtypeStruct((n, A_WIDTH), jnp.float32),
        scratch_shapes=[
            pltpu.VMEM((s, QBLOCK), jnp.int32),
            pltpu.VMEM((s + QBLOCK, A_HEADS * QBLOCK), jnp.float32),
            pltpu.VMEM((1, QBLOCK), jnp.int32),
        ],
        compiler_params=pltpu.CompilerParams(
            dimension_semantics=("arbitrary", "arbitrary"), vmem_limit_bytes=VMEM_LIMIT),
        name="a_main",
    )(far, qn, proj, proj, misct, kn, vt, proj, near_t)


def mixer_a(proj, misct, b, s, gkv, wukv, gq, gk, bias_a):
    kn, vt, qn = a_pre(proj, b, s, gkv, wukv.astype(jnp.bfloat16), gq, gk)
    near_t = jnp.transpose(_bias_lookup(bias_a, _NEAR_BUCKET_T), (2, 0, 1))
    far = bias_a[_FAR_BUCKET]
    return a_main(proj, misct, qn, kn, vt, near_t, far, b, s)


N_CMP_PAD = 128
N_SB = 32
WIN_KEYS = WINDOW + QBLOCK
CMP_GROUPS = CMP_BLOCK // CMP_STRIDE

_CMP_BUCKET_T = _bucket_np((np.arange(16)[:, None, None] * QBLOCK + np.arange(QBLOCK)[None, None, :])
                           - (np.arange(N_CMP_PAD)[None, :, None] * CMP_STRIDE + CMP_BLOCK - 1))
_WIN_BUCKET_T = _bucket_np(np.arange(QBLOCK)[None, :] + WINDOW - np.arange(WIN_KEYS)[:, None])
_OVERLAP_T = np.array([[1.0 if (n * CMP_STRIDE < j * SEL_BLOCK + SEL_BLOCK and n * CMP_STRIDE + CMP_BLOCK > j * SEL_BLOCK
                              and n < N_CMP_PAD - 1) else 0.0 for n in range(N_CMP_PAD)] for j in range(N_SB)], np.float32)
_SB_EXPAND = (np.arange(2048)[:, None] // SEL_BLOCK == np.arange(N_SB)[None, :]).astype(np.float32)


def _c_pre_kernel(cmp_ref, sel_ref, win_ref, kn_ref, pos_ref, w1_ref, w2_ref,
                  kvc_ref, ksel_ref, vselt_ref, kwin_ref, vwin_ref):
    f32, bf16 = jnp.float32, jnp.bfloat16
    d = C_HEAD_DIM

    def rms(x, g):
        return x * lax.rsqrt(jnp.mean(x * x, axis=-1, keepdims=True) + RMS_EPS) * g

    acc = [jnp.zeros((N_CMP_PAD, 2 * PHI_HIDDEN), f32) for _ in range(CMP_GROUPS)]
    for j in range(CMP_STRIDE):
        xs = cmp_ref[pl.ds(j, N_CMP_PAD, stride=CMP_STRIDE), :]
        for half in range(CMP_GROUPS):
            jj = half * CMP_STRIDE + j
            acc[half] = acc[half] + jnp.dot((xs + pos_ref[jj:jj + 1, :]).astype(bf16), w1_ref[jj],
                                            preferred_element_type=f32)
    hid = acc[0] + pltpu.roll(acc[1], N_CMP_PAD - 1, axis=0)
    kv = jnp.dot(_silu(hid).astype(bf16), w2_ref[...], preferred_element_type=f32)
    kvc_ref[0, :, 0:d] = rms(kv[:, 0:d], kn_ref[0:1, :])
    kvc_ref[0, :, d:2 * d] = kv[:, d:2 * d]

    sel = sel_ref[...]
    ksel_ref[...] = rms(sel[:, 0:d], kn_ref[1:2, :])
    vselt_ref[0] = sel.T[d:2 * d, :]
    win = win_ref[...]
    kwin_ref[0, 0:WINDOW, :] = jnp.zeros((WINDOW, d), f32)
    vwin_ref[0, 0:WINDOW, :] = jnp.zeros((WINDOW, d), f32)
    kwin_ref[0, WINDOW:, :] = rms(win[:, 0:d], kn_ref[2:3, :])
    vwin_ref[0, WINDOW:, :] = win[:, d:2 * d]


def c_pre(proj, b, s, k_norm, cmp_pos, phi_w1, phi_w2):
    d = C_HEAD_DIM
    kv0 = OFF["c_kv"] // 128
    pos = jnp.concatenate([cmp_pos[0], cmp_pos[1]], axis=-1)
    w1 = phi_w1.reshape(2, CMP_BLOCK, d, PHI_HIDDEN)
    zero = jnp.zeros((CMP_BLOCK, d, PHI_HIDDEN), jnp.float32)
    w1c = jnp.concatenate([jnp.concatenate([w1[0], zero], axis=-1),
                           jnp.concatenate([zero, w1[1]], axis=-1)], axis=1).astype(jnp.bfloat16)
    z2 = jnp.zeros((PHI_HIDDEN, d), jnp.float32)
    w2c = jnp.concatenate([jnp.concatenate([phi_w2[0], z2], axis=-1),
                           jnp.concatenate([z2, phi_w2[1]], axis=-1)], axis=0).astype(jnp.bfloat16)
    full = lambda a: pl.BlockSpec(a.shape, lambda bb: (0,) * a.ndim)
    return pl.pallas_call(
        _c_pre_kernel,
        grid=(b,),
        in_specs=[
            pl.BlockSpec((s, 128), lambda bb: (bb, kv0)),
            pl.BlockSpec((s, 128), lambda bb: (bb, kv0 + 1)),
            pl.BlockSpec((s, 128), lambda bb: (bb, kv0 + 2)),
            full(k_norm), full(pos), full(w1c), full(w2c),
        ],
        out_specs=[
            pl.BlockSpec((1, N_CMP_PAD, 128), lambda bb: (bb, 0, 0)),
            pl.BlockSpec((s, d), lambda bb: (bb, 0)),
            pl.BlockSpec((1, d, s), lambda bb: (bb, 0, 0)),
            pl.BlockSpec((1, s + WINDOW, d), lambda bb: (bb, 0, 0)),
            pl.BlockSpec((1, s + WINDOW, d), lambda bb: (bb, 0, 0)),
        ],
        out_shape=[
            jax.ShapeDtypeStruct((b, N_CMP_PAD, 128), jnp.float32),
            jax.ShapeDtypeStruct((b * s, d), jnp.float32),
            jax.ShapeDtypeStruct((b, d, s), jnp.float32),
            jax.ShapeDtypeStruct((b, s + WINDOW, d), jnp.float32),
            jax.ShapeDtypeStruct((b, s + WINDOW, d), jnp.float32),
        ],
        compiler_params=pltpu.CompilerParams(dimension_semantics=("arbitrary",), vmem_limit_bytes=VMEM_LIMIT),
        name="c_pre",
    )(proj, proj, proj, k_norm, pos, w1c, w2c)


def _c_main_kernel(cq_ref, z_ref, misct_ref, gq_ref, kvc_ref, ksel_ref, vselt_ref, kwin_ref, vwin_ref,
                   cmptab_ref, near_ref, farrow_ref, wintab_ref, ovt_ref, exp_ref, o_ref, ls_scr):
    s = ksel_ref.shape[0]
    d = C_HEAD_DIM
    hq = C_HEADS * QBLOCK
    i = pl.program_id(1)
    t0 = pl.multiple_of(i * QBLOCK, QBLOCK)
    f32, bf16 = jnp.float32, jnp.bfloat16
    nt = (((1,), (1,)), ((), ()))
    tn = (((0,), (0,)), ((), ()))

    cq = cq_ref[...]
    qs = jnp.concatenate([cq[:, h * d:(h + 1) * d] for h in range(C_HEADS)], axis=0)
    qs = qs * lax.rsqrt(jnp.mean(qs * qs, axis=-1, keepdims=True) + RMS_EPS) * gq_ref[...] * (d ** -0.5)
    qs = qs.astype(bf16)

    def softmax_rows(l, valid):
        l = jnp.where(valid, l, NEG_INF)
        m = _col_reduce(l, jnp.max)
        p = jnp.where(valid, jnp.exp(l - m), 0.0)
        den = _col_reduce(p, jnp.sum)
        return p, den

    kvc = kvc_ref[0]
    lc = lax.dot_general(kvc[:, 0:d].astype(bf16), qs, nt, preferred_element_type=f32) + cmptab_ref[0]
    n_idx = lax.broadcasted_iota(jnp.int32, (N_CMP_PAD, hq), 0)
    t_c = t0 + (lax.broadcasted_iota(jnp.int32, (N_CMP_PAD, hq), 1) & (QBLOCK - 1))
    cmp_valid = n_idx * CMP_STRIDE + (CMP_BLOCK - 1) <= t_c
    pc, den_c = softmax_rows(lc, cmp_valid)
    pc = pc * jnp.where(den_c > 0.0, 1.0 / den_c, 0.0)
    o_cmp = lax.dot_general(kvc[:, d:2 * d].astype(bf16), pc.astype(bf16), tn, preferred_element_type=f32)

    psum = pc[:, 0:QBLOCK]
    for h in range(1, C_HEADS):
        psum = psum + pc[:, h * QBLOCK:(h + 1) * QBLOCK]
    p_hi = psum.astype(bf16)
    p_lo = (psum - p_hi.astype(f32)).astype(bf16)
    ovt = ovt_ref[...]
    imp = jnp.dot(ovt, p_hi, preferred_element_type=f32) + jnp.dot(ovt, p_lo, preferred_element_type=f32)
    j_idx = lax.broadcasted_iota(jnp.int32, (N_SB, QBLOCK), 0)
    t_b = t0 + lax.broadcasted_iota(jnp.int32, (N_SB, QBLOCK), 1)
    cur = lax.shift_right_arithmetic(t_b, 6)
    forced = (j_idx == 0) | (j_idx == cur) | (j_idx == jnp.maximum(cur - 1, 0))
    imp = jnp.where(j_idx * SEL_BLOCK <= t_b, jnp.where(forced, FORCE_SCORE, imp), NEG_INF)
    rank = jnp.zeros((N_SB, QBLOCK), f32)
    for r in range(N_SB):
        row = imp[r:r + 1, :]
        rank = rank + jnp.where((row > imp) | ((row == imp) & (j_idx > r)), 1.0, 0.0)
    picked = jnp.where(rank < float(N_SEL), 1.0, 0.0).astype(bf16)

    kw = kwin_ref[0, pl.ds(t0, WIN_KEYS), :]
    vw = vwin_ref[0, pl.ds(t0, WIN_KEYS), :]
    lw = lax.dot_general(kw.astype(bf16), qs, nt, preferred_element_type=f32) + wintab_ref[...]
    r_idx = lax.broadcasted_iota(jnp.int32, (WIN_KEYS, hq), 0)
    dist = (lax.broadcasted_iota(jnp.int32, (WIN_KEYS, hq), 1) & (QBLOCK - 1)) + WINDOW - r_idx
    win_valid = (dist >= 0) & (dist < WINDOW) & (r_idx + t0 >= WINDOW)
    pw, den_w = softmax_rows(lw, win_valid)
    o_win = lax.dot_general(vw.astype(bf16), pw.astype(bf16), tn, preferred_element_type=f32) / den_w

    g = _sigmoid(misct_ref[0, C_G_LANE:C_G_LANE + 3 * C_HEADS, :])

    def run(nrows):
        kpos = lax.broadcasted_iota(jnp.int32, (nrows, QBLOCK), 0)
        tpos = t0 + lax.broadcasted_iota(jnp.int32, (nrows, QBLOCK), 1)
        sel_valid = (jnp.dot(exp_ref[0:nrows, :], picked, preferred_element_type=f32) > 0.5) & (kpos <= tpos)
        ls_scr[0:QBLOCK, :] = jnp.zeros((QBLOCK, hq), f32)
        ls_scr[QBLOCK:QBLOCK + nrows, :] = lax.dot_general(ksel_ref[0:nrows, :].astype(bf16), qs, nt,
                                                           preferred_element_type=f32)
        ls_scr[pl.ds(t0, 2 * QBLOCK), :] += near_ref[...] - farrow_ref[...]
        outs = []
        for h in range(C_HEADS):
            cols = slice(h * QBLOCK, (h + 1) * QBLOCK)
            l = jnp.where(sel_valid, ls_scr[QBLOCK:QBLOCK + nrows, cols], NEG_INF)
            m = _col_reduce(l, jnp.max)
            p = jnp.exp(l - m)
            den = _col_reduce(p, jnp.sum)
            o_sel = jnp.dot(vselt_ref[0, :, 0:nrows].astype(bf16), p.astype(bf16), preferred_element_type=f32) / den
            outs.append(g[3 * h:3 * h + 1, :] * o_cmp[:, cols] + g[3 * h + 1:3 * h + 2, :] * o_sel
                        + g[3 * h + 2:3 * h + 3, :] * o_win[:, cols])
        o_ref[...] = jnp.concatenate(outs, axis=0).T * _silu(z_ref[...])

    _run_causal_variant(run, i, s)


def c_main(proj, misct, gq, kvc, ksel, vselt, kwin, vwin, cmptab, near, farrow, wintab, b, s):
    n = b * s
    nq = s // QBLOCK
    d = C_HEAD_DIM
    hq = C_HEADS * QBLOCK
    row = lambda bb, i: bb * nq + i
    ovt = jnp.asarray(_OVERLAP_T, jnp.bfloat16)
    expand = jnp.asarray(_SB_EXPAND, jnp.bfloat16)
    const = lambda a: pl.BlockSpec(a.shape, lambda bb, i: (0,) * a.ndim)
    return pl.pallas_call(
        _c_main_kernel,
        grid=(b, nq),
        in_specs=[
            pl.BlockSpec((QBLOCK, C_WIDTH), lambda bb, i: (row(bb, i), OFF["c_q"] // C_WIDTH)),
            pl.BlockSpec((QBLOCK, C_WIDTH), lambda bb, i: (row(bb, i), OFF["c_z"] // C_WIDTH)),
            pl.BlockSpec((1, 128, QBLOCK), lambda bb, i: (bb, 0, i)),
            const(gq),
            pl.BlockSpec((1, N_CMP_PAD, 128), lambda bb, i: (bb, 0, 0)),
            pl.BlockSpec((s, d), lambda bb, i: (bb, 0)),
            pl.BlockSpec((1, d, s), lambda bb, i: (bb, 0, 0)),
            pl.BlockSpec((1, s + WINDOW, d), lambda bb, i: (bb, 0, 0)),
            pl.BlockSpec((1, s + WINDOW, d), lambda bb, i: (bb, 0, 0)),
            pl.BlockSpec((1, N_CMP_PAD, hq), lambda bb, i: (i, 0, 0)),
            const(near), const(farrow), const(wintab), const(ovt), const(expand),
        ],
        out_specs=pl.BlockSpec((QBLOCK, C_WIDTH), lambda bb, i: (row(bb, i), 0)),
        out_shape=jax.ShapeDtypeStruct((n, C_WIDTH), jnp.float32),
        scratch_shapes=[pltpu.VMEM((s + QBLOCK, hq), jnp.float32)],
        compiler_params=pltpu.CompilerParams(
            dimension_semantics=("arbitrary", "arbitrary"), vmem_limit_bytes=VMEM_LIMIT),
        name="c_main",
    )(proj, proj, misct, gq, kvc, ksel, vselt, kwin, vwin, cmptab, near, farrow, wintab, ovt, expand)


def _head_cols(tab):
    return jnp.moveaxis(tab, -1, -2).reshape(*tab.shape[:-2], tab.shape[-1] * tab.shape[-2])


def mixer_c(proj, misct, b, s, gq, k_norm, cmp_pos, phi_w1, phi_w2, bias_c):
    kvc, ksel, vselt, kwin, vwin = c_pre(proj, b, s, k_norm, cmp_pos, phi_w1, phi_w2)
    cmptab = _head_cols(_bias_lookup(bias_c, _CMP_BUCKET_T))
    near = _head_cols(_bias_lookup(bias_c, _NEAR_BUCKET_T))
    wintab = _head_cols(_bias_lookup(bias_c, _WIN_BUCKET_T))
    farrow = jnp.repeat(bias_c[_FAR_BUCKET], QBLOCK).reshape(1, C_HEADS * QBLOCK)
    return c_main(proj, misct, gq.reshape(1, C_HEAD_DIM), kvc, ksel, vselt, kwin, vwin, cmptab, near, farrow,
                  wintab, b, s)


B_HPS = 2
B_GROUP = 4


def _b_kernel(alog_ref, dtb_ref, q_ref, k_ref, v_ref, z_ref, misc_ref, arow_ref, cwq_ref, cwk_ref, cwv_ref,
              gon_ref, o_ref, qs, ks, vs, ws, at, gtok, btok, gcrow):
    s = q_ref.shape[0]
    dh = B_HEAD_DIM
    c_len = GDN_CHUNK
    j = pl.program_id(1)
    f32, bf16 = jnp.float32, jnp.bfloat16
    hi = lax.Precision.HIGHEST
    nt = (((1,), (1,)), ((), ()))
    tn = (((0,), (0,)), ((), ()))

    rows = lax.broadcasted_iota(jnp.int32, (s, B_HPS * dh), 0)

    def conv_silu(x_ref, w_ref):
        x = x_ref[...]
        acc = x * w_ref[CONV_WIDTH - 1:CONV_WIDTH, :]
        for k in range(1, CONV_WIDTH):
            shifted = jnp.where(rows >= k, pltpu.roll(x, k, axis=0), 0.0)
            acc = acc + shifted * w_ref[CONV_WIDTH - 1 - k:CONV_WIDTH - k, :]
        return _silu(acc)

    def l2n(x):
        return x * lax.rsqrt(jnp.sum(x * x, axis=-1, keepdims=True) + RMS_EPS)

    qc = conv_silu(q_ref, cwq_ref)
    kc = conv_silu(k_ref, cwk_ref)
    vs[...] = conv_silu(v_ref, cwv_ref)
    misc = misc_ref[...]
    lane = lax.broadcasted_iota(jnp.int32, misc.shape, 1)
    ri = lax.broadcasted_iota(jnp.int32, (c_len, c_len), 0)
    ci = lax.broadcasted_iota(jnp.int32, (c_len, c_len), 1)
    lower = ci <= ri
    strict = ci < ri
    tri_u = jnp.where(ri <= ci, 1.0, 0.0)
    row_in_chunk = lax.broadcasted_iota(jnp.int32, (s, dh), 0) & (c_len - 1)
    for hh in range(B_HPS):
        sl = slice(hh * dh, (hh + 1) * dh)
        qs[:, sl] = l2n(qc[:, sl]) * (dh ** -0.5)
        ks[:, sl] = l2n(kc[:, sl])
        h = B_HPS * j + hh
        neg_a = -jnp.exp(alog_ref[h])
        dtb = dtb_ref[h]
        a_tok = jnp.sum(jnp.where(lane == B_A_LANE + h, misc, 0.0), axis=-1, keepdims=True)
        b_tok = jnp.sum(jnp.where(lane == B_B_LANE + h, misc, 0.0), axis=-1, keepdims=True)
        g = jnp.broadcast_to(neg_a * jax.nn.softplus(a_tok + dtb), (s, dh))
        for sh in (1, 2, 4, 8, 16, 32):
            g = g + jnp.where(row_in_chunk >= sh, pltpu.roll(g, sh, axis=0), 0.0)
        gtok[hh] = g
        btok[hh] = jnp.broadcast_to(_sigmoid(b_tok), (s, dh))
        g_row = neg_a * jax.nn.softplus(arow_ref[0, hh] + dtb)
        gcrow[hh] = jnp.dot(g_row, tri_u, precision=hi, preferred_element_type=f32)

    lo_lanes = (lax.broadcasted_iota(jnp.int32, (c_len, 4 * c_len), 1) & c_len) != 0

    def split_lhs(p):
        p4 = jnp.concatenate([p, p, p, p], axis=1)
        hi4 = p4.astype(bf16).astype(f32)
        return jnp.where(lo_lanes, p4 - hi4, hi4).astype(bf16)

    def split_rhs(x):
        xh = x.astype(bf16)
        xl = (x - xh.astype(f32)).astype(bf16)
        return jnp.concatenate([xh, xh, xl, xl], axis=0)

    def mm(p_split, x):
        return jnp.dot(p_split, split_rhs(x), preferred_element_type=f32)

    def intra_pair(i, carry):
        where, q_c, k_c, v_c, gb, beta, g_row = [], [], [], [], [], [], []
        for c in [B_GROUP * i + cc for cc in range(B_GROUP)]:
            rs = pl.ds(pl.multiple_of(c * c_len, c_len), c_len)
            for hh in range(B_HPS):
                sl = slice(hh * dh, (hh + 1) * dh)
                where.append((rs, hh, sl))
                q_c.append(qs[rs, sl])
                k_c.append(ks[rs, sl])
                v_c.append(vs[rs, sl])
                gb.append(gtok[hh, rs, :])
                beta.append(btok[hh, rs, :])
                g_row.append(gcrow[hh, pl.ds(c, 1), :])
        nch = range(len(where))
        decay = [jnp.where(lower, jnp.exp(jnp.where(lower, gb[n][:, 0:c_len] - g_row[n], 0.0)), 0.0) for n in nch]
        kb = [k_c[n] * beta[n] for n in nch]
        kk = [lax.dot_general(kb[n].astype(bf16), k_c[n].astype(bf16), nt, preferred_element_type=f32) for n in nch]
        p = [-jnp.where(strict, kk[n] * decay[n], 0.0) for n in nch]
        eg = [jnp.exp(gb[n]) for n in nch]
        x = [jnp.concatenate([v_c[n] * beta[n], kb[n] * eg[n]], axis=1) for n in nch]
        ps = [split_lhs(p[n]) for n in nch]
        x = [x[n] + mm(ps[n], x[n]) for n in nch]
        for _ in range(5):
            p = [mm(ps[n], p[n]) for n in nch]
            ps = [split_lhs(p[n]) for n in nch]
            x = [x[n] + mm(ps[n], x[n]) for n in nch]
        attn = [jnp.where(lower, lax.dot_general(q_c[n].astype(bf16), k_c[n].astype(bf16), nt,
                                                 preferred_element_type=f32) * decay[n], 0.0) for n in nch]
        for n, (rs, hh, sl) in enumerate(where):
            g_last = g_row[n][:, c_len - 1:c_len]
            vs[rs, sl] = x[n][:, 0:dh]
            ws[rs, sl] = x[n][:, dh:2 * dh]
            qs[rs, sl] = q_c[n] * eg[n]
            ks[rs, sl] = k_c[n] * jnp.exp(g_last - gb[n])
            at[rs, hh * c_len:(hh + 1) * c_len] = attn[n]
        return carry

    lax.fori_loop(0, s // (B_GROUP * c_len), intra_pair, 0)

    def recur(c, states):
        rs = pl.ds(pl.multiple_of(c * c_len, c_len), c_len)
        hs = range(B_HPS)
        sls = [slice(hh * dh, (hh + 1) * dh) for hh in hs]
        st_b = [states[hh].astype(bf16) for hh in hs]
        v_new = [vs[rs, sls[hh]] - jnp.dot(ws[rs, sls[hh]].astype(bf16), st_b[hh], preferred_element_type=f32)
                 for hh in hs]
        v_nb = [v_new[hh].astype(bf16) for hh in hs]
        o_st = [jnp.dot(qs[rs, sls[hh]].astype(bf16), st_b[hh], preferred_element_type=f32) for hh in hs]
        new_states = [states[hh] * jnp.exp(gcrow[hh, pl.ds(c, 1), c_len - 1:c_len])
                      + lax.dot_general(ks[rs, sls[hh]].astype(bf16), v_nb[hh], tn, preferred_element_type=f32)
                      for hh in hs]
        for hh in hs:
            o_ref[rs, sls[hh]] = o_st[hh] + jnp.dot(at[rs, hh * c_len:(hh + 1) * c_len].astype(bf16), v_nb[hh],
                                                    preferred_element_type=f32)
        return tuple(new_states)

    lax.fori_loop(0, s // c_len, recur, tuple(jnp.zeros((dh, dh), f32) for _ in range(B_HPS)))

    for hh in range(B_HPS):
        sl = slice(hh * dh, (hh + 1) * dh)
        o = o_ref[:, sl]
        o = o * lax.rsqrt(jnp.mean(o * o, axis=-1, keepdims=True) + RMS_EPS) * gon_ref[...]
        o_ref[:, sl] = o * _silu(z_ref[:, sl])


def mixer_b(proj, misct, b, s, conv_w, a_log, dt_bias, out_norm):
    n = b * s
    w2 = B_HPS * B_HEAD_DIM
    nc = s // GDN_CHUNK
    arow = misct[:, B_A_LANE:B_A_LANE + B_HEADS, :].reshape(b, B_HEADS, nc, GDN_CHUNK)
    qkv0 = OFF["b_qkv"] // w2
    kstep = B_WIDTH // w2
    smem = pl.BlockSpec(memory_space=pltpu.SMEM)
    return pl.pallas_call(
        _b_kernel,
        grid=(b, B_HEADS // B_HPS),
        in_specs=[
            smem, smem,
            pl.BlockSpec((s, w2), lambda bb, j: (bb, qkv0 + j)),
            pl.BlockSpec((s, w2), lambda bb, j: (bb, qkv0 + kstep + j)),
            pl.BlockSpec((s, w2), lambda bb, j: (bb, qkv0 + 2 * kstep + j)),
            pl.BlockSpec((s, w2), lambda bb, j: (bb, OFF["b_z"] // w2 + j)),
            pl.BlockSpec((s, 128), lambda bb, j: (bb, MISC_OFF // 128)),
            pl.BlockSpec((1, B_HPS, nc, GDN_CHUNK), lambda bb, j: (bb, j, 0, 0)),
            pl.BlockSpec((CONV_WIDTH, w2), lambda bb, j: (0, j)),
            pl.BlockSpec((CONV_WIDTH, w2), lambda bb, j: (0, kstep + j)),
            pl.BlockSpec((CONV_WIDTH, w2), lambda bb, j: (0, 2 * kstep + j)),
            pl.BlockSpec((1, B_HEAD_DIM), lambda bb, j: (0, 0)),
        ],
        out_specs=pl.BlockSpec((s, w2), lambda bb, j: (bb, j)),
        out_shape=jax.ShapeDtypeStruct((n, B_WIDTH), jnp.float32),
        scratch_shapes=[
            pltpu.VMEM((s, w2), jnp.float32), pltpu.VMEM((s, w2), jnp.float32), pltpu.VMEM((s, w2), jnp.float32),
            pltpu.VMEM((s, w2), jnp.float32), pltpu.VMEM((s, B_HPS * GDN_CHUNK), jnp.float32),
            pltpu.VMEM((B_HPS, s, B_HEAD_DIM), jnp.float32), pltpu.VMEM((B_HPS, s, B_HEAD_DIM), jnp.float32),
            pltpu.VMEM((B_HPS, nc, GDN_CHUNK), jnp.float32),
        ],
        compiler_params=pltpu.CompilerParams(
            dimension_semantics=("arbitrary", "arbitrary"), vmem_limit_bytes=VMEM_LIMIT),
        name="gdn",
    )(a_log, dt_bias, proj, proj, proj, proj, proj, arow, conv_w, conv_w, conv_w, out_norm.reshape(1, B_HEAD_DIM))


def _rms_norm(x, g):
    y = x * lax.rsqrt(jnp.mean(x * x, axis=-1, keepdims=True) + RMS_EPS)
    return y * g


def _l2norm(x):
    return x * lax.rsqrt(jnp.sum(x * x, axis=-1, keepdims=True) + RMS_EPS)


def _masked_softmax(logits, mask):
    logits = jnp.where(mask, logits, NEG_INF)
    return jnp.where(mask, jax.nn.softmax(logits, axis=-1), 0.0)


def _t5_bucket(dist):
    dist = jnp.maximum(dist, 0)
    log_ratio = jnp.log(jnp.maximum(dist, 1).astype(jnp.float32) / BUCKET_MAX_EXACT) / math.log(BUCKET_MAX_DIST / BUCKET_MAX_EXACT)
    large = BUCKET_MAX_EXACT + (log_ratio * (N_BUCKETS - BUCKET_MAX_EXACT)).astype(jnp.int32)
    large = jnp.minimum(large, N_BUCKETS - 1)
    return jnp.where(dist < BUCKET_MAX_EXACT, dist, large)


def _to_qblocks(a):
    b, s = a.shape[:2]
    return jnp.moveaxis(a.reshape(b, s // QBLOCK, QBLOCK, *a.shape[2:]), 1, 0)


def _from_qblocks(a):
    nb, b = a.shape[:2]
    return jnp.moveaxis(a, 0, 1).reshape(b, nb * QBLOCK, *a.shape[3:])


def _dsa_mixer(q, k, v, q_idx, k_idx, w_idx, bias_tab):
    b, s = q.shape[:2]
    topk = min(TOPK_MAX, s // 4)
    bidx = jnp.arange(b)[:, None, None]
    key_pos = jnp.arange(s)
    scale = A_HEAD_DIM ** -0.5

    def block(args):
        qb, qib, wb, start = args
        t = start + jnp.arange(QBLOCK)
        idx_logits = jnp.einsum('bqhd,bsd->bqhs', qib, k_idx) * (IDX_DIM ** -0.5)
        score = jnp.einsum('bqh,bqhs->bqs', wb * (IDX_HEADS ** -0.5), jax.nn.relu(idx_logits))
        score = jnp.where(key_pos[None, None, :] <= t[None, :, None], score, NEG_INF)
        _, sel = lax.top_k(score, topk)
        kg = k[bidx, sel]
        vg = v[bidx, sel]
        dist = t[None, :, None] - sel
        bias = jnp.transpose(bias_tab[_t5_bucket(dist)], (0, 3, 1, 2))
        logits = jnp.einsum('bqhd,bqkhd->bhqk', qb, kg) * scale + bias
        p = _masked_softmax(logits, (dist >= 0)[:, None])
        return jnp.einsum('bhqk,bqkhd->bqhd', p, vg)

    starts = jnp.arange(s // QBLOCK, dtype=jnp.int32) * QBLOCK
    out = lax.map(block, (_to_qblocks(q), _to_qblocks(q_idx), _to_qblocks(w_idx), starts))
    return _from_qblocks(out)


def _causal_depthwise_conv(x, w):
    c = x.shape[-1]
    return lax.conv_general_dilated(x, w[:, None, :], window_strides=(1,), padding=[(CONV_WIDTH - 1, 0)],
                                    dimension_numbers=('NWC', 'WIO', 'NWC'), feature_group_count=c)


def _gdn_mixer(qkv, a_in, b_in, conv_w, a_log, dt_bias):
    b, s = qkv.shape[:2]
    n = s // GDN_CHUNK
    qkv = jax.nn.silu(_causal_depthwise_conv(qkv, conv_w))
    q, k, v = jnp.split(qkv, 3, axis=-1)
    sh = lambda t: t.reshape(b, s, B_HEADS, B_HEAD_DIM)
    q = _l2norm(sh(q)) * (B_HEAD_DIM ** -0.5)
    k = _l2norm(sh(k))
    v = sh(v)
    beta = jax.nn.sigmoid(b_in)
    g = -jnp.exp(a_log) * jax.nn.softplus(a_in + dt_bias)

    def chunk(t):
        return jnp.moveaxis(t.reshape(b, n, GDN_CHUNK, *t.shape[2:]), 3, 2)

    q, k, v, beta, g = (chunk(t) for t in (q, k, v, beta, g))
    g_cum = jnp.cumsum(g, axis=-1)
    g_last = g_cum[..., -1]
    lower = jnp.tril(jnp.ones((GDN_CHUNK, GDN_CHUNK), dtype=bool))
    strict = jnp.tril(jnp.ones((GDN_CHUNK, GDN_CHUNK), dtype=bool), -1)
    diff = g_cum[..., :, None] - g_cum[..., None, :]
    decay = jnp.where(lower, jnp.exp(jnp.where(lower, diff, 0.0)), 0.0)
    k_beta = k * beta[..., None]
    a_mat = jnp.where(strict, jnp.einsum('bnhid,bnhjd->bnhij', k_beta, k) * decay, 0.0)
    eye = jnp.eye(GDN_CHUNK, dtype=jnp.float32)
    t_inv = lax.linalg.triangular_solve(eye + a_mat, jnp.broadcast_to(eye, a_mat.shape), left_side=True, lower=True)
    u = jnp.einsum('bnhij,bnhjd->bnhid', t_inv, v * beta[..., None])
    w = jnp.einsum('bnhij,bnhjd->bnhid', t_inv, k_beta * jnp.exp(g_cum)[..., None])
    attn = jnp.einsum('bnhid,bnhjd->bnhij', q, k) * decay
    q_dec = q * jnp.exp(g_cum)[..., None]
    k_dec = k * jnp.exp(g_last[..., None] - g_cum)[..., None]

    def step(state, xs):
        u_c, w_c, q_c, k_c, attn_c, gl_c = xs
        v_new = u_c - jnp.einsum('bhik,bhkv->bhiv', w_c, state)
        o = jnp.einsum('bhik,bhkv->bhiv', q_c, state) + jnp.einsum('bhij,bhjv->bhiv', attn_c, v_new)
        state = state * jnp.exp(gl_c)[..., None, None] + jnp.einsum('bhik,bhiv->bhkv', k_c, v_new)
        return state, o

    state0 = jnp.zeros((b, B_HEADS, B_HEAD_DIM, B_HEAD_DIM), jnp.float32)
    xs = tuple(jnp.moveaxis(t, 1, 0) for t in (u, w, q_dec, k_dec, attn, g_last))
    _, o = lax.scan(step, state0, xs)
    o = jnp.moveaxis(jnp.moveaxis(o, 0, 1), 2, 3)
    return o.reshape(b, s, B_HEADS, B_HEAD_DIM)


def _compress_blocks(tok, pos, w1, w2):
    b, s, d = tok.shape
    n_cmp = (s - CMP_BLOCK) // CMP_STRIDE + 1
    idx = jnp.arange(n_cmp)[:, None] * CMP_STRIDE + jnp.arange(CMP_BLOCK)[None, :]
    blocks = tok[:, idx] + pos
    return jax.nn.silu(blocks.reshape(b, n_cmp, CMP_BLOCK * d) @ w1) @ w2


def _nsa_mixer(q, kv, gates, k_norm, cmp_pos, phi_w1, phi_w2, bias_tab):
    b, s = q.shape[:2]
    scale = C_HEAD_DIM ** -0.5
    t_all = jnp.arange(s)
    k_cmp, v_cmp, k_sel, v_sel, k_win, v_win = (kv[:, :, i] for i in range(6))
    k_sel = _rms_norm(k_sel, k_norm[1])
    k_win = _rms_norm(k_win, k_norm[2])
    kc = _rms_norm(_compress_blocks(k_cmp, cmp_pos[0], phi_w1[0], phi_w2[0]), k_norm[0])
    vc = _compress_blocks(v_cmp, cmp_pos[1], phi_w1[1], phi_w2[1])
    n_cmp = kc.shape[1]
    cmp_start = jnp.arange(n_cmp) * CMP_STRIDE
    cmp_end = cmp_start + CMP_BLOCK - 1
    cmp_valid = cmp_end[None, :] <= t_all[:, None]
    cmp_bias = jnp.transpose(bias_tab[_t5_bucket(t_all[:, None] - cmp_end[None, :])], (2, 0, 1))
    logits = jnp.einsum('bqhd,bnd->bhqn', q, kc) * scale + cmp_bias
    p_cmp = _masked_softmax(logits, cmp_valid[None, None])
    o_cmp = jnp.einsum('bhqn,bnd->bqhd', p_cmp, vc)
    n_sb = s // SEL_BLOCK
    n_pick = min(N_SEL, n_sb)
    sb_start = jnp.arange(n_sb) * SEL_BLOCK
    overlap = ((cmp_start[:, None] < sb_start[None, :] + SEL_BLOCK) & (cmp_start[:, None] + CMP_BLOCK > sb_start[None, :])).astype(jnp.float32)
    importance = jnp.einsum('bhqn,nj->bqj', p_cmp, overlap)
    cur = t_all // SEL_BLOCK
    blk = jnp.arange(n_sb)
    forced = (blk[None, :] == 0) | (blk[None, :] == cur[:, None]) | (blk[None, :] == jnp.maximum(cur[:, None] - 1, 0))
    admissible = sb_start[None, :] <= t_all[:, None]
    importance = jnp.where(admissible[None], jnp.where(forced[None], FORCE_SCORE, importance), NEG_INF)
    _, sel = lax.top_k(importance, n_pick)
    k_blocks = k_sel.reshape(b, n_sb, SEL_BLOCK, C_HEAD_DIM)
    v_blocks = v_sel.reshape(b, n_sb, SEL_BLOCK, C_HEAD_DIM)
    k_win_pad = jnp.pad(k_win, ((0, 0), (WINDOW, 0), (0, 0)))
    v_win_pad = jnp.pad(v_win, ((0, 0), (WINDOW, 0), (0, 0)))
    bidx = jnp.arange(b)[:, None, None]
    n_keys = n_pick * SEL_BLOCK

    def block(args):
        qb, selb, start = args
        t = start + jnp.arange(QBLOCK)
        kg = k_blocks[bidx, selb].reshape(b, QBLOCK, n_keys, C_HEAD_DIM)
        vg = v_blocks[bidx, selb].reshape(b, QBLOCK, n_keys, C_HEAD_DIM)
        s_pos = (selb[..., None] * SEL_BLOCK + jnp.arange(SEL_BLOCK)).reshape(b, QBLOCK, n_keys)
        dist = t[None, :, None] - s_pos
        bias = jnp.transpose(bias_tab[_t5_bucket(dist)], (0, 3, 1, 2))
        lg = jnp.einsum('bqhd,bqkd->bhqk', qb, kg) * scale + bias
        p = _masked_softmax(lg, (dist >= 0)[:, None])
        o_sel = jnp.einsum('bhqk,bqkd->bqhd', p, vg)
        kw = lax.dynamic_slice_in_dim(k_win_pad, start, QBLOCK + WINDOW, axis=1)
        vw = lax.dynamic_slice_in_dim(v_win_pad, start, QBLOCK + WINDOW, axis=1)
        s_w = start - WINDOW + jnp.arange(QBLOCK + WINDOW)
        dist_w = t[:, None] - s_w[None, :]
        valid_w = (s_w[None, :] >= 0) & (dist_w >= 0) & (dist_w < WINDOW)
        bias_w = jnp.transpose(bias_tab[_t5_bucket(dist_w)], (2, 0, 1))
        lw = jnp.einsum('bqhd,bsd->bhqs', qb, kw) * scale + bias_w
        pw = _masked_softmax(lw, valid_w[None, None])
        o_win = jnp.einsum('bhqs,bsd->bqhd', pw, vw)
        return o_sel, o_win

    starts = jnp.arange(s // QBLOCK, dtype=jnp.int32) * QBLOCK
    o_sel, o_win = lax.map(block, (_to_qblocks(q), _to_qblocks(sel), starts))
    o_sel = _from_qblocks(o_sel)
    o_win = _from_qblocks(o_win)
    g = jax.nn.sigmoid(gates).reshape(b, s, C_HEADS, 3)
    return g[..., 0:1] * o_cmp + g[..., 1:2] * o_sel + g[..., 2:3] * o_win


def _field(proj3, name):
    return proj3[..., OFF[name]:OFF[name] + _ORIG[name][1]]


def kernel(x, norm_g, w_in, a_kv_norm, a_w_ukv, a_q_norm, a_k_norm, b_conv, b_a_log, b_dt_bias, b_out_norm, c_q_norm, c_k_norm, c_cmp_pos, c_phi_w1, c_phi_w2, w_branch, w_out, rel_bias):
    b, s, _ = x.shape
    n = b * s
    bias_a = rel_bias[:, :A_HEADS]
    bias_c = rel_bias[:, A_HEADS:]
    xf = x.reshape(n, D_MODEL)
    perm = jnp.asarray(_PERM_IDX)
    valid = jnp.asarray(_PERM_VALID)
    sh = lambda t, h, d: t.reshape(b, s, h, d)
    for l in range(DEPTH):
        w_p = jnp.where(valid[None, :], w_in[l][:, perm], 0.0).astype(jnp.bfloat16)
        proj = in_proj(xf, norm_g[l], w_p)
        p3 = proj.reshape(b, s, N_PAD)

        misct = jnp.transpose(p3[..., MISC_OFF:], (0, 2, 1))
        y_a = mixer_a(proj, misct, b, s, a_kv_norm[l], a_w_ukv[l], a_q_norm[l], a_k_norm[l], bias_a)

        y_b = mixer_b(proj, misct, b, s, b_conv[l], b_a_log[l], b_dt_bias[l], b_out_norm[l])
        y_c = mixer_c(proj, misct, b, s, c_q_norm[l], c_k_norm[l], c_cmp_pos[l], c_phi_w1[l], c_phi_w2[l], bias_c)

        wbr = w_branch[l].astype(jnp.bfloat16)
        xf = merge(xf, proj, y_a, y_b, y_c, wbr[:A_WIDTH], wbr[A_WIDTH:A_WIDTH + B_WIDTH],
                   wbr[A_WIDTH + B_WIDTH:], w_out[l].astype(jnp.bfloat16))
    return xf.reshape(b, s, D_MODEL)
```

```python
import functools
import math

import jax
import jax.numpy as jnp
import numpy as np
from jax import lax
from jax.experimental import pallas as pl
from jax.experimental.pallas import tpu as pltpu

D_MODEL = 1024
DEPTH = 4
QBLOCK = 128
NEG_INF = -1e30
FORCE_SCORE = 1e9
RMS_EPS = 1e-6

A_HEADS = 4
A_HEAD_DIM = 64
A_WIDTH = A_HEADS * A_HEAD_DIM
A_KV_LATENT = 128
IDX_HEADS = 8
IDX_DIM = 64
TOPK_MAX = 256

B_HEADS = 4
B_HEAD_DIM = 128
B_WIDTH = B_HEADS * B_HEAD_DIM
CONV_WIDTH = 4
GDN_CHUNK = 64

C_HEADS = 4
C_HEAD_DIM = 64
C_WIDTH = C_HEADS * C_HEAD_DIM
CMP_BLOCK = 32
CMP_STRIDE = 16
SEL_BLOCK = 64
N_SEL = 16
WINDOW = 512
PHI_HIDDEN = 256

N_BUCKETS = 32
BUCKET_MAX_EXACT = 16
BUCKET_MAX_DIST = 128

VMEM_LIMIT = 48 * 1024 * 1024

_ORIG = {}
_o = 0
for _name, _w in (("a_q", 256), ("a_ckv", 128), ("a_qi", 512), ("a_ki", 64), ("a_wi", 8), ("a_z", 256),
                  ("b_qkv", 1536), ("b_a", 4), ("b_b", 4), ("b_z", 512),
                  ("c_q", 256), ("c_kv", 384), ("c_g", 12), ("c_z", 256), ("g", 3072)):
    _ORIG[_name] = (_o, _w)
    _o += _w
N_IN = _o

_NEW_ORDER = ("g", "b_qkv", "b_z", "a_qi", "a_q", "a_z", "c_q", "c_z", "c_kv", "a_ckv",
              "a_ki", "a_wi", "b_a", "b_b", "c_g")
OFF = {}
_o = 0
for _name in _NEW_ORDER:
    OFF[_name] = _o
    _o += _ORIG[_name][1]
MISC_OFF = OFF["a_ki"]
N_PAD = 7296
assert MISC_OFF == 7168 and _o <= N_PAD
A_WI_LANE = OFF["a_wi"] - MISC_OFF
B_A_LANE = OFF["b_a"] - MISC_OFF
B_B_LANE = OFF["b_b"] - MISC_OFF
C_G_LANE = OFF["c_g"] - MISC_OFF


def _perm_indices():
    idx = np.zeros((N_PAD,), np.int32)
    valid = np.zeros((N_PAD,), bool)
    for name in _NEW_ORDER:
        o_old, w = _ORIG[name]
        idx[OFF[name]:OFF[name] + w] = np.arange(o_old, o_old + w)
        valid[OFF[name]:OFF[name] + w] = True
    return idx, valid


_PERM_IDX, _PERM_VALID = _perm_indices()


IN_TM = 512
IN_TN = N_PAD // 3


def _in_proj_kernel(x_ref, g_ref, w_ref, o_ref):
    x = x_ref[...]
    ms = jnp.mean(x * x, axis=-1, keepdims=True)
    h = (x * lax.rsqrt(ms + RMS_EPS)) * g_ref[...]
    o_ref[...] = jnp.dot(h.astype(jnp.bfloat16), w_ref[...], preferred_element_type=jnp.float32)


def in_proj(xf, g, w_bf16):
    n = xf.shape[0]
    return pl.pallas_call(
        _in_proj_kernel,
        grid=(N_PAD // IN_TN, n // IN_TM),
        in_specs=[
            pl.BlockSpec((IN_TM, D_MODEL), lambda j, i: (i, 0)),
            pl.BlockSpec((1, D_MODEL), lambda j, i: (0, 0)),
            pl.BlockSpec((D_MODEL, IN_TN), lambda j, i: (0, j)),
        ],
        out_specs=pl.BlockSpec((IN_TM, IN_TN), lambda j, i: (i, j)),
        out_shape=jax.ShapeDtypeStruct((n, N_PAD), jnp.float32),
        compiler_params=pltpu.CompilerParams(
            dimension_semantics=("arbitrary", "arbitrary"), vmem_limit_bytes=VMEM_LIMIT),
        name="in_proj",
    )(xf, g.reshape(1, D_MODEL), w_bf16)


MG_TM = 256


def _merge_kernel(x_ref, g_ref, ya_ref, yb_ref, yc_ref, wa_ref, wb_ref, wc_ref, wo_ref, o_ref):
    def branch(y_ref, w_ref, k):
        p = jnp.dot(y_ref[...].astype(jnp.bfloat16), w_ref[...], preferred_element_type=jnp.float32)
        return _sigmoid(g_ref[:, k * D_MODEL:(k + 1) * D_MODEL]) * p

    merged = branch(ya_ref, wa_ref, 0) + branch(yb_ref, wb_ref, 1) + branch(yc_ref, wc_ref, 2)
    o_ref[...] = x_ref[...] + jnp.dot(merged.astype(jnp.bfloat16), wo_ref[...],
                                      preferred_element_type=jnp.float32)


def merge(xf, proj, ya, yb, yc, wa, wb, wc, wo):
    n = xf.shape[0]
    row = lambda w: pl.BlockSpec((MG_TM, w), lambda i: (i, 0))
    full = lambda a: pl.BlockSpec(a.shape, lambda i: (0, 0))
    return pl.pallas_call(
        _merge_kernel,
        grid=(n // MG_TM,),
        in_specs=[row(D_MODEL), row(3 * D_MODEL), row(A_WIDTH), row(B_WIDTH), row(C_WIDTH),
                  full(wa), full(wb), full(wc), full(wo)],
        out_specs=row(D_MODEL),
        out_shape=jax.ShapeDtypeStruct((n, D_MODEL), jnp.float32),
        compiler_params=pltpu.CompilerParams(
            dimension_semantics=("arbitrary",), vmem_limit_bytes=VMEM_LIMIT),
        name="merge",
    )(xf, proj, ya, yb, yc, wa, wb, wc, wo)


def _bucket_np(dist):
    d = np.maximum(np.asarray(dist, np.int64), 0)
    ratio = np.log(np.maximum(d, 1).astype(np.float64) / BUCKET_MAX_EXACT) / math.log(BUCKET_MAX_DIST / BUCKET_MAX_EXACT)
    scaled = ratio * (N_BUCKETS - BUCKET_MAX_EXACT)
    frac = scaled - np.floor(scaled)
    edge = (d > BUCKET_MAX_EXACT) & (d < BUCKET_MAX_DIST) & ((frac < 1e-4) | (frac > 1 - 1e-4))
    assert not edge.any(), "bucket boundary too close to an integer distance"
    large = np.minimum(BUCKET_MAX_EXACT + np.floor(scaled + 1e-9).astype(np.int64), N_BUCKETS - 1)
    return np.where(d < BUCKET_MAX_EXACT, d, large).astype(np.int32)


_NEAR_BUCKET_T = _bucket_np(np.arange(QBLOCK)[None, :] + QBLOCK - np.arange(2 * QBLOCK)[:, None])
_FAR_BUCKET = int(_bucket_np(np.array([BUCKET_MAX_DIST]))[0])
assert (_bucket_np(np.arange(BUCKET_MAX_DIST, 4096)) == _FAR_BUCKET).all()

INT_MIN = -2 ** 31


COL_SLAB = 64


def _col_reduce(x, op):
    rows = x.shape[0]
    if rows > COL_SLAB and rows % COL_SLAB == 0:
        x = op(x.reshape(rows // COL_SLAB, COL_SLAB, x.shape[1]), axis=0)
    return op(x, axis=0, keepdims=True)


ROW_VARIANTS = 8


def _run_causal_variant(run, i, s):
    per = (s // QBLOCK) // ROW_VARIANTS
    for v in range(ROW_VARIANTS):
        pl.when((i >= v * per) & (i < (v + 1) * per))(functools.partial(run, (v + 1) * per * QBLOCK))


def _sigmoid(x):
    return 0.5 * jnp.tanh(0.5 * x) + 0.5


def _silu(x):
    return x * _sigmoid(x)


def _bias_lookup(table, buckets):
    idx = jnp.asarray(buckets)[..., None]
    out = jnp.zeros(idx.shape[:-1] + (table.shape[1],), table.dtype)
    for bkt in range(N_BUCKETS):
        out = jnp.where(idx == bkt, table[bkt], out)
    return out


A_PRE_TM = 512


def _a_pre_kernel(ckv_ref, q_ref, gkv_ref, wukv_ref, gq_ref, gk_ref, hm_ref, kn_ref, vt_ref, qn_ref):
    c = ckv_ref[...]
    c = c * lax.rsqrt(jnp.mean(c * c, axis=-1, keepdims=True) + RMS_EPS) * gkv_ref[...]
    kv = jnp.dot(c.astype(jnp.bfloat16), wukv_ref[...], preferred_element_type=jnp.float32)
    hm = hm_ref[...]

    def head_rms(x, g):
        ms = jnp.dot(x * x, hm, precision=lax.Precision.HIGHEST, preferred_element_type=jnp.float32)
        return x * lax.rsqrt(ms + RMS_EPS) * g

    kn_ref[...] = head_rms(kv[:, :A_WIDTH], gk_ref[...])
    qn_ref[...] = head_rms(q_ref[...], gq_ref[...]) * (A_HEAD_DIM ** -0.5)
    vt_ref[0] = kv[:, A_WIDTH:].T


def a_pre(proj, b, s, gkv, wukv_bf16, gq, gk):
    n = b * s
    nt = s // A_PRE_TM
    hm = jnp.asarray(np.kron(np.eye(A_HEADS), np.ones((A_HEAD_DIM, A_HEAD_DIM))) / A_HEAD_DIM, jnp.float32)
    row = lambda bb, j: (bb * nt + j, 0)
    full = lambda a: pl.BlockSpec(a.shape, lambda bb, j: (0,) * a.ndim)
    gq = jnp.tile(gq, A_HEADS).reshape(1, A_WIDTH)
    gk = jnp.tile(gk, A_HEADS).reshape(1, A_WIDTH)
    gkv = gkv.reshape(1, A_KV_LATENT)
    return pl.pallas_call(
        _a_pre_kernel,
        grid=(b, nt),
        in_specs=[
            pl.BlockSpec((A_PRE_TM, A_KV_LATENT), lambda bb, j: (bb * nt + j, OFF["a_ckv"] // A_KV_LATENT)),
            pl.BlockSpec((A_PRE_TM, A_WIDTH), lambda bb, j: (bb * nt + j, OFF["a_q"] // A_WIDTH)),
            full(gkv), full(wukv_bf16), full(gq), full(gk), full(hm),
        ],
        out_specs=[
            pl.BlockSpec((A_PRE_TM, A_WIDTH), row),
            pl.BlockSpec((1, A_WIDTH, A_PRE_TM), lambda bb, j: (bb, 0, j)),
            pl.BlockSpec((A_PRE_TM, A_WIDTH), row),
        ],
        out_shape=[
            jax.ShapeDtypeStruct((n, A_WIDTH), jnp.float32),
            jax.ShapeDtypeStruct((b, A_WIDTH, s), jnp.float32),
            jax.ShapeDtypeStruct((n, A_WIDTH), jnp.float32),
        ],
        compiler_params=pltpu.CompilerParams(
            dimension_semantics=("arbitrary", "arbitrary"), vmem_limit_bytes=VMEM_LIMIT),
        name="a_pre",
    )(proj, proj, gkv, wukv_bf16, gq, gk, hm)


def _a_main_kernel(far_ref, qn_ref, qi_ref, z_ref, misct_ref, kn_ref, vt_ref, ki_ref, near_ref, o_ref,
                   key_scr, lg_scr, j_scr):
    s = kn_ref.shape[0]
    i = pl.program_id(1)
    t0 = pl.multiple_of(i * QBLOCK, QBLOCK)
    f32, bf16 = jnp.float32, jnp.bfloat16
    nt = (((1,), (1,)), ((), ()))
    kf = float(TOPK_MAX)

    def run(nrows):

        qi = qi_ref[...] * (IDX_DIM ** -0.5)
        qstack = jnp.concatenate([qi[:, h * IDX_DIM:(h + 1) * IDX_DIM] for h in range(IDX_HEADS)], axis=0)
        ki = ki_ref[0:nrows, 0:IDX_DIM]
        sc = lax.dot_general(ki.astype(bf16), qstack.astype(bf16), nt, preferred_element_type=f32)
        wt = misct_ref[0, A_WI_LANE:A_WI_LANE + IDX_HEADS, :] * (IDX_HEADS ** -0.5)
        score = jnp.zeros((nrows, QBLOCK), f32)
        for h in range(IDX_HEADS):
            score = score + wt[h:h + 1, :] * jnp.maximum(sc[:, h * QBLOCK:(h + 1) * QBLOCK], 0.0)
        score = score + 0.0
        kpos = lax.broadcasted_iota(jnp.int32, (nrows, QBLOCK), 0)
        tpos = t0 + lax.broadcasted_iota(jnp.int32, (nrows, QBLOCK), 1)
        causal = kpos <= tpos
        bits = pltpu.bitcast(score, jnp.int32)
        key = jnp.where(bits < 0, bits ^ jnp.int32(0x7FFFFFFF), bits)
        key_scr[0:nrows, :] = jnp.where(causal, key, jnp.int32(INT_MIN))

        def count_ge(cand):
            return _col_reduce(jnp.where(key_scr[0:nrows, :] >= cand, 1.0, 0.0), jnp.sum)

        def bisect(it, thr):
            cand = thr + lax.shift_left(jnp.int32(1), 31 - it)
            return jnp.where(count_ge(cand) >= kf, cand, thr)

        thr = lax.fori_loop(0, 32, bisect, jnp.full((1, QBLOCK), INT_MIN, jnp.int32))
        n_ge = count_ge(thr)
        need = kf - count_ge(thr + 1)

        j_scr[...] = jnp.full((1, QBLOCK), nrows - 1, jnp.int32)
        surplus = jnp.where((n_ge > kf) & (thr > INT_MIN), 1.0, 0.0)

        @pl.when(jnp.max(surplus) > 0.0)
        def _():
            def bisect_idx(it, lohi):
                lo, hi = lohi
                mid = lax.shift_right_arithmetic(lo + hi, 1)
                k = key_scr[0:nrows, :]
                kp = lax.broadcasted_iota(jnp.int32, (nrows, QBLOCK), 0)
                c = _col_reduce(jnp.where((k == thr) & (kp <= mid), 1.0, 0.0), jnp.sum)
                ok = c >= need
                return jnp.where(ok, lo, mid), jnp.where(ok, mid, hi)

            lo0 = jnp.full((1, QBLOCK), -1, jnp.int32)
            hi0 = jnp.full((1, QBLOCK), nrows - 1, jnp.int32)
            _, hi = lax.fori_loop(0, 11, bisect_idx, (lo0, hi0))
            j_scr[...] = hi

        key = key_scr[0:nrows, :]
        sel = ((key > thr) | ((key == thr) & (kpos <= j_scr[...]))) & causal

        qn = qn_ref[...]
        lane = lax.broadcasted_iota(jnp.int32, (QBLOCK, A_WIDTH), 1)
        qblk = jnp.concatenate(
            [jnp.where((lane >= h * A_HEAD_DIM) & (lane < (h + 1) * A_HEAD_DIM), qn, 0.0) for h in range(A_HEADS)],
            axis=0)
        lg_scr[0:QBLOCK, :] = jnp.zeros((QBLOCK, A_HEADS * QBLOCK), f32)
        lg_scr[QBLOCK:QBLOCK + nrows, :] = lax.dot_general(kn_ref[0:nrows, :].astype(bf16), qblk.astype(bf16), nt,
                                                           preferred_element_type=f32)
        outs = []
        for h in range(A_HEADS):
            cols = slice(h * QBLOCK, (h + 1) * QBLOCK)
            lg_scr[pl.ds(t0, 2 * QBLOCK), cols] += near_ref[h] - far_ref[h]
            l = jnp.where(sel, lg_scr[QBLOCK:QBLOCK + nrows, cols], NEG_INF)
            m = _col_reduce(l, jnp.max)
            p = jnp.exp(l - m)
            den = _col_reduce(p, jnp.sum)
            vt = vt_ref[0, h * A_HEAD_DIM:(h + 1) * A_HEAD_DIM, 0:nrows]
            o_t = jnp.dot(vt.astype(bf16), p.astype(bf16), preferred_element_type=f32)
            outs.append(o_t / den)
        o = jnp.concatenate(outs, axis=0).T
        o_ref[...] = o * _silu(z_ref[...])

    _run_causal_variant(run, i, s)


A_QPS = 2


def _a_pair_kernel(far_ref, qn_ref, qi_ref, z_ref, misct_ref, kn_ref, vt_ref, ki_ref, near_ref, o_ref,
                   key_scr, lg_scr, j_scr):
    s = kn_ref.shape[0]
    f32, bf16 = jnp.float32, jnp.bfloat16
    nt = (((1,), (1,)), ((), ()))
    kf = float(TOPK_MAX)
    groups = range(A_QPS)

    def run(step):
        t0 = [(A_QPS * step + g) * QBLOCK for g in groups]
        nr = [t + QBLOCK for t in t0]
        qrows = [slice(g * QBLOCK, (g + 1) * QBLOCK) for g in groups]

        for g in groups:
            qi = qi_ref[qrows[g], :] * (IDX_DIM ** -0.5)
            qstack = jnp.concatenate([qi[:, h * IDX_DIM:(h + 1) * IDX_DIM] for h in range(IDX_HEADS)], axis=0)
            ki = ki_ref[0:nr[g], 0:IDX_DIM]
            sc = lax.dot_general(ki.astype(bf16), qstack.astype(bf16), nt, preferred_element_type=f32)
            wt = misct_ref[0, A_WI_LANE:A_WI_LANE + IDX_HEADS, qrows[g]] * (IDX_HEADS ** -0.5)
            score = jnp.zeros((nr[g], QBLOCK), f32)
            for h in range(IDX_HEADS):
                score = score + wt[h:h + 1, :] * jnp.maximum(sc[:, h * QBLOCK:(h + 1) * QBLOCK], 0.0)
            score = score + 0.0
            kpos = lax.broadcasted_iota(jnp.int32, (nr[g], QBLOCK), 0)
            tpos = t0[g] + lax.broadcasted_iota(jnp.int32, (nr[g], QBLOCK), 1)
            bits = pltpu.bitcast(score, jnp.int32)
            key = jnp.where(bits < 0, bits ^ jnp.int32(0x7FFFFFFF), bits)
            key_scr[g, 0:nr[g], :] = jnp.where(kpos <= tpos, key, jnp.int32(INT_MIN))

        def count_ge(g, cand):
            return _col_reduce(jnp.where(key_scr[g, 0:nr[g], :] >= cand, 1.0, 0.0), jnp.sum)

        def bisect(it, thrs):
            bit = lax.shift_left(jnp.int32(1), 31 - it)
            cnt = [count_ge(g, thrs[g] + bit) for g in groups]
            return tuple(jnp.where(cnt[g] >= kf, thrs[g] + bit, thrs[g]) for g in groups)

        thrs = lax.fori_loop(0, 32, bisect, tuple(jnp.full((1, QBLOCK), INT_MIN, jnp.int32) for _ in groups))

        for g in groups:
            nrows, thr = nr[g], thrs[g]
            n_ge = count_ge(g, thr)
            need = kf - count_ge(g, thr + 1)
            kpos = lax.broadcasted_iota(jnp.int32, (nrows, QBLOCK), 0)

            j_scr[g] = jnp.full((1, QBLOCK), nrows - 1, jnp.int32)
            surplus = jnp.where((n_ge > kf) & (thr > INT_MIN), 1.0, 0.0)

            @pl.when(jnp.max(surplus) > 0.0)
            def _():
                def bisect_idx(it, lohi):
                    lo, hi = lohi
                    mid = lax.shift_right_arithmetic(lo + hi, 1)
                    k = key_scr[g, 0:nrows, :]
                    c = _col_reduce(jnp.where((k == thr) & (kpos <= mid), 1.0, 0.0), jnp.sum)
                    ok = c >= need
                    return jnp.where(ok, lo, mid), jnp.where(ok, mid, hi)

                lo0 = jnp.full((1, QBLOCK), -1, jnp.int32)
                hi0 = jnp.full((1, QBLOCK), nrows - 1, jnp.int32)
                _, hi = lax.fori_loop(0, 11, bisect_idx, (lo0, hi0))
                j_scr[g] = hi

            key = key_scr[g, 0:nrows, :]
            tpos = t0[g] + lax.broadcasted_iota(jnp.int32, (nrows, QBLOCK), 1)
            sel = ((key > thr) | ((key == thr) & (kpos <= j_scr[g]))) & (kpos <= tpos)

            qn = qn_ref[qrows[g], :]
            lane = lax.broadcasted_iota(jnp.int32, (QBLOCK, A_WIDTH), 1)
            qblk = jnp.concatenate(
                [jnp.where((lane >= h * A_HEAD_DIM) & (lane < (h + 1) * A_HEAD_DIM), qn, 0.0)
                 for h in range(A_HEADS)], axis=0)
            lg_scr[0:QBLOCK, :] = jnp.zeros((QBLOCK, A_HEADS * QBLOCK), f32)
            lg_scr[QBLOCK:QBLOCK + nrows, :] = lax.dot_general(kn_ref[0:nrows, :].astype(bf16), qblk.astype(bf16),
                                                               nt, preferred_element_type=f32)
            outs = []
            for h in range(A_HEADS):
                cols = slice(h * QBLOCK, (h + 1) * QBLOCK)
                lg_scr[t0[g]:t0[g] + 2 * QBLOCK, cols] += near_ref[h] - far_ref[h]
                l = jnp.where(sel, lg_scr[QBLOCK:QBLOCK + nrows, cols], NEG_INF)
                m = _col_reduce(l, jnp.max)
                p = jnp.exp(l - m)
                den = _col_reduce(p, jnp.sum)
                vt = vt_ref[0, h * A_HEAD_DIM:(h + 1) * A_HEAD_DIM, 0:nrows]
                o_t = jnp.dot(vt.astype(bf16), p.astype(bf16), preferred_element_type=f32)
                outs.append(o_t / den)
            o = jnp.concatenate(outs, axis=0).T
            o_ref[qrows[g], :] = o * _silu(z_ref[qrows[g], :])

    step = pl.program_id(1)
    for v in range(s // (A_QPS * QBLOCK)):
        pl.when(step == v)(functools.partial(run, v))


def a_main(proj, misct, qn, kn, vt, near_t, far, b, s):
    n = b * s
    tq = A_QPS * QBLOCK
    nq = s // tq
    row = lambda bb, i: bb * nq + i
    return pl.pallas_call(
        _a_pair_kernel,
        grid=(b, nq),
        in_specs=[
            pl.BlockSpec(memory_space=pltpu.SMEM),
            pl.BlockSpec((tq, A_WIDTH), lambda bb, i: (row(bb, i), 0)),
            pl.BlockSpec((tq, IDX_HEADS * IDX_DIM), lambda bb, i: (row(bb, i), OFF["a_qi"] // (IDX_HEADS * IDX_DIM))),
            pl.BlockSpec((tq, A_WIDTH), lambda bb, i: (row(bb, i), OFF["a_z"] // A_WIDTH)),
            pl.BlockSpec((1, 128, tq), lambda bb, i: (bb, 0, i)),
            pl.BlockSpec((s, A_WIDTH), lambda bb, i: (bb, 0)),
            pl.BlockSpec((1, A_WIDTH, s), lambda bb, i: (bb, 0, 0)),
            pl.BlockSpec((s, 128), lambda bb, i: (bb, MISC_OFF // 128)),
            pl.BlockSpec((A_HEADS, 2 * QBLOCK, QBLOCK), lambda bb, i: (0, 0, 0)),
        ],
        out_specs=pl.BlockSpec((tq, A_WIDTH), lambda bb, i: (row(bb, i), 0)),
        out_shape=jax.ShapeDtypeStruct((n, A_WIDTH), jnp.float32),
        scratch_shapes=[
            pltpu.VMEM((A_QPS, s, QBLOCK), jnp.int32),
            pltpu.VMEM((s + QBLOCK, A_HEADS * QBLOCK), jnp.float32),
            pltpu.VMEM((A_QPS, 1, QBLOCK), jnp.int32),
        ],
        compiler_params=pltpu.CompilerParams(
            dimension_semantics=("arbitrary", "arbitrary"), vmem_limit_bytes=VMEM_LIMIT),
        name="a_main",
    )(far, qn, proj, proj, misct, kn, vt, proj, near_t)


def mixer_a(proj, misct, b, s, gkv, wukv, gq, gk, bias_a):
    kn, vt, qn = a_pre(proj, b, s, gkv, wukv.astype(jnp.bfloat16), gq, gk)
    near_t = jnp.transpose(_bias_lookup(bias_a, _NEAR_BUCKET_T), (2, 0, 1))
    far = bias_a[_FAR_BUCKET]
    return a_main(proj, misct, qn, kn, vt, near_t, far, b, s)


N_CMP_PAD = 128
N_SB = 32
WIN_KEYS = WINDOW + QBLOCK
CMP_GROUPS = CMP_BLOCK // CMP_STRIDE

_CMP_BUCKET_T = _bucket_np((np.arange(16)[:, None, None] * QBLOCK + np.arange(QBLOCK)[None, None, :])
                           - (np.arange(N_CMP_PAD)[None, :, None] * CMP_STRIDE + CMP_BLOCK - 1))
_WIN_BUCKET_T = _bucket_np(np.arange(QBLOCK)[None, :] + WINDOW - np.arange(WIN_KEYS)[:, None])
_OVERLAP_T = np.array([[1.0 if (n * CMP_STRIDE < j * SEL_BLOCK + SEL_BLOCK and n * CMP_STRIDE + CMP_BLOCK > j * SEL_BLOCK
                              and n < N_CMP_PAD - 1) else 0.0 for n in range(N_CMP_PAD)] for j in range(N_SB)], np.float32)
_SB_EXPAND = (np.arange(2048)[:, None] // SEL_BLOCK == np.arange(N_SB)[None, :]).astype(np.float32)


def _c_pre_kernel(cmp_ref, sel_ref, win_ref, kn_ref, pos_ref, w1_ref, w2_ref,
                  kvc_ref, ksel_ref, vselt_ref, kwin_ref, vwin_ref):
    f32, bf16 = jnp.float32, jnp.bfloat16
    d = C_HEAD_DIM

    def rms(x, g):
        return x * lax.rsqrt(jnp.mean(x * x, axis=-1, keepdims=True) + RMS_EPS) * g

    acc = [jnp.zeros((N_CMP_PAD, 2 * PHI_HIDDEN), f32) for _ in range(CMP_GROUPS)]
    for j in range(CMP_STRIDE):
        xs = cmp_ref[pl.ds(j, N_CMP_PAD, stride=CMP_STRIDE), :]
        for half in range(CMP_GROUPS):
            jj = half * CMP_STRIDE + j
            acc[half] = acc[half] + jnp.dot((xs + pos_ref[jj:jj + 1, :]).astype(bf16), w1_ref[jj],
                                            preferred_element_type=f32)
    hid = acc[0] + pltpu.roll(acc[1], N_CMP_PAD - 1, axis=0)
    kv = jnp.dot(_silu(hid).astype(bf16), w2_ref[...], preferred_element_type=f32)
    kvc_ref[0, :, 0:d] = rms(kv[:, 0:d], kn_ref[0:1, :])
    kvc_ref[0, :, d:2 * d] = kv[:, d:2 * d]

    sel = sel_ref[...]
    ksel_ref[...] = rms(sel[:, 0:d], kn_ref[1:2, :])
    vselt_ref[0] = sel.T[d:2 * d, :]
    win = win_ref[...]
    kwin_ref[0, 0:WINDOW, :] = jnp.zeros((WINDOW, d), f32)
    vwin_ref[0, 0:WINDOW, :] = jnp.zeros((WINDOW, d), f32)
    kwin_ref[0, WINDOW:, :] = rms(win[:, 0:d], kn_ref[2:3, :])
    vwin_ref[0, WINDOW:, :] = win[:, d:2 * d]


def c_pre(proj, b, s, k_norm, cmp_pos, phi_w1, phi_w2):
    d = C_HEAD_DIM
    kv0 = OFF["c_kv"] // 128
    pos = jnp.concatenate([cmp_pos[0], cmp_pos[1]], axis=-1)
    w1 = phi_w1.reshape(2, CMP_BLOCK, d, PHI_HIDDEN)
    zero = jnp.zeros((CMP_BLOCK, d, PHI_HIDDEN), jnp.float32)
    w1c = jnp.concatenate([jnp.concatenate([w1[0], zero], axis=-1),
                           jnp.concatenate([zero, w1[1]], axis=-1)], axis=1).astype(jnp.bfloat16)
    z2 = jnp.zeros((PHI_HIDDEN, d), jnp.float32)
    w2c = jnp.concatenate([jnp.concatenate([phi_w2[0], z2], axis=-1),
                           jnp.concatenate([z2, phi_w2[1]], axis=-1)], axis=0).astype(jnp.bfloat16)
    full = lambda a: pl.BlockSpec(a.shape, lambda bb: (0,) * a.ndim)
    return pl.pallas_call(
        _c_pre_kernel,
        grid=(b,),
        in_specs=[
            pl.BlockSpec((s, 128), lambda bb: (bb, kv0)),
            pl.BlockSpec((s, 128), lambda bb: (bb, kv0 + 1)),
            pl.BlockSpec((s, 128), lambda bb: (bb, kv0 + 2)),
            full(k_norm), full(pos), full(w1c), full(w2c),
        ],
        out_specs=[
            pl.BlockSpec((1, N_CMP_PAD, 128), lambda bb: (bb, 0, 0)),
            pl.BlockSpec((s, d), lambda bb: (bb, 0)),
            pl.BlockSpec((1, d, s), lambda bb: (bb, 0, 0)),
            pl.BlockSpec((1, s + WINDOW, d), lambda bb: (bb, 0, 0)),
            pl.BlockSpec((1, s + WINDOW, d), lambda bb: (bb, 0, 0)),
        ],
        out_shape=[
            jax.ShapeDtypeStruct((b, N_CMP_PAD, 128), jnp.float32),
            jax.ShapeDtypeStruct((b * s, d), jnp.float32),
            jax.ShapeDtypeStruct((b, d, s), jnp.float32),
            jax.ShapeDtypeStruct((b, s + WINDOW, d), jnp.float32),
            jax.ShapeDtypeStruct((b, s + WINDOW, d), jnp.float32),
        ],
        compiler_params=pltpu.CompilerParams(dimension_semantics=("arbitrary",), vmem_limit_bytes=VMEM_LIMIT),
        name="c_pre",
    )(proj, proj, proj, k_norm, pos, w1c, w2c)


def _c_main_kernel(cq_ref, z_ref, misct_ref, gq_ref, kvc_ref, ksel_ref, vselt_ref, kwin_ref, vwin_ref,
                   cmptab_ref, near_ref, farrow_ref, wintab_ref, ovt_ref, exp_ref, o_ref, ls_scr):
    s = ksel_ref.shape[0]
    d = C_HEAD_DIM
    hq = C_HEADS * QBLOCK
    i = pl.program_id(1)
    t0 = pl.multiple_of(i * QBLOCK, QBLOCK)
    f32, bf16 = jnp.float32, jnp.bfloat16
    nt = (((1,), (1,)), ((), ()))
    tn = (((0,), (0,)), ((), ()))

    cq = cq_ref[...]
    qs = jnp.concatenate([cq[:, h * d:(h + 1) * d] for h in range(C_HEADS)], axis=0)
    qs = qs * lax.rsqrt(jnp.mean(qs * qs, axis=-1, keepdims=True) + RMS_EPS) * gq_ref[...] * (d ** -0.5)
    qs = qs.astype(bf16)

    def softmax_rows(l, valid):
        l = jnp.where(valid, l, NEG_INF)
        m = _col_reduce(l, jnp.max)
        p = jnp.where(valid, jnp.exp(l - m), 0.0)
        den = _col_reduce(p, jnp.sum)
        return p, den

    kvc = kvc_ref[0]
    lc = lax.dot_general(kvc[:, 0:d].astype(bf16), qs, nt, preferred_element_type=f32) + cmptab_ref[0]
    n_idx = lax.broadcasted_iota(jnp.int32, (N_CMP_PAD, hq), 0)
    t_c = t0 + (lax.broadcasted_iota(jnp.int32, (N_CMP_PAD, hq), 1) & (QBLOCK - 1))
    cmp_valid = n_idx * CMP_STRIDE + (CMP_BLOCK - 1) <= t_c
    pc, den_c = softmax_rows(lc, cmp_valid)
    pc = pc * jnp.where(den_c > 0.0, 1.0 / den_c, 0.0)
    o_cmp = lax.dot_general(kvc[:, d:2 * d].astype(bf16), pc.astype(bf16), tn, preferred_element_type=f32)

    psum = pc[:, 0:QBLOCK]
    for h in range(1, C_HEADS):
        psum = psum + pc[:, h * QBLOCK:(h + 1) * QBLOCK]
    p_hi = psum.astype(bf16)
    p_lo = (psum - p_hi.astype(f32)).astype(bf16)
    ovt = ovt_ref[...]
    imp = jnp.dot(ovt, p_hi, preferred_element_type=f32) + jnp.dot(ovt, p_lo, preferred_element_type=f32)
    j_idx = lax.broadcasted_iota(jnp.int32, (N_SB, QBLOCK), 0)
    t_b = t0 + lax.broadcasted_iota(jnp.int32, (N_SB, QBLOCK), 1)
    cur = lax.shift_right_arithmetic(t_b, 6)
    forced = (j_idx == 0) | (j_idx == cur) | (j_idx == jnp.maximum(cur - 1, 0))
    imp = jnp.where(j_idx * SEL_BLOCK <= t_b, jnp.where(forced, FORCE_SCORE, imp), NEG_INF)
    rank = jnp.zeros((N_SB, QBLOCK), f32)
    for r in range(N_SB):
        row = imp[r:r + 1, :]
        rank = rank + jnp.where((row > imp) | ((row == imp) & (j_idx > r)), 1.0, 0.0)
    picked = jnp.where(rank < float(N_SEL), 1.0, 0.0).astype(bf16)

    kw = kwin_ref[0, pl.ds(t0, WIN_KEYS), :]
    vw = vwin_ref[0, pl.ds(t0, WIN_KEYS), :]
    lw = lax.dot_general(kw.astype(bf16), qs, nt, preferred_element_type=f32) + wintab_ref[...]
    r_idx = lax.broadcasted_iota(jnp.int32, (WIN_KEYS, hq), 0)
    dist = (lax.broadcasted_iota(jnp.int32, (WIN_KEYS, hq), 1) & (QBLOCK - 1)) + WINDOW - r_idx
    win_valid = (dist >= 0) & (dist < WINDOW) & (r_idx + t0 >= WINDOW)
    pw, den_w = softmax_rows(lw, win_valid)
    o_win = lax.dot_general(vw.astype(bf16), pw.astype(bf16), tn, preferred_element_type=f32) / den_w

    g = _sigmoid(misct_ref[0, C_G_LANE:C_G_LANE + 3 * C_HEADS, :])

    def run(nrows):
        kpos = lax.broadcasted_iota(jnp.int32, (nrows, QBLOCK), 0)
        tpos = t0 + lax.broadcasted_iota(jnp.int32, (nrows, QBLOCK), 1)
        sel_valid = (jnp.dot(exp_ref[0:nrows, :], picked, preferred_element_type=f32) > 0.5) & (kpos <= tpos)
        ls_scr[0:QBLOCK, :] = jnp.zeros((QBLOCK, hq), f32)
        ls_scr[QBLOCK:QBLOCK + nrows, :] = lax.dot_general(ksel_ref[0:nrows, :].astype(bf16), qs, nt,
                                                           preferred_element_type=f32)
        ls_scr[pl.ds(t0, 2 * QBLOCK), :] += near_ref[...] - farrow_ref[...]
        outs = []
        for h in range(C_HEADS):
            cols = slice(h * QBLOCK, (h + 1) * QBLOCK)
            l = jnp.where(sel_valid, ls_scr[QBLOCK:QBLOCK + nrows, cols], NEG_INF)
            m = _col_reduce(l, jnp.max)
            p = jnp.exp(l - m)
            den = _col_reduce(p, jnp.sum)
            o_sel = jnp.dot(vselt_ref[0, :, 0:nrows].astype(bf16), p.astype(bf16), preferred_element_type=f32) / den
            outs.append(g[3 * h:3 * h + 1, :] * o_cmp[:, cols] + g[3 * h + 1:3 * h + 2, :] * o_sel
                        + g[3 * h + 2:3 * h + 3, :] * o_win[:, cols])
        o_ref[...] = jnp.concatenate(outs, axis=0).T * _silu(z_ref[...])

    _run_causal_variant(run, i, s)


def c_main(proj, misct, gq, kvc, ksel, vselt, kwin, vwin, cmptab, near, farrow, wintab, b, s):
    n = b * s
    nq = s // QBLOCK
    d = C_HEAD_DIM
    hq = C_HEADS * QBLOCK
    row = lambda bb, i: bb * nq + i
    ovt = jnp.asarray(_OVERLAP_T, jnp.bfloat16)
    expand = jnp.asarray(_SB_EXPAND, jnp.bfloat16)
    const = lambda a: pl.BlockSpec(a.shape, lambda bb, i: (0,) * a.ndim)
    return pl.pallas_call(
        _c_main_kernel,
        grid=(b, nq),
        in_specs=[
            pl.BlockSpec((QBLOCK, C_WIDTH), lambda bb, i: (row(bb, i), OFF["c_q"] // C_WIDTH)),
            pl.BlockSpec((QBLOCK, C_WIDTH), lambda bb, i: (row(bb, i), OFF["c_z"] // C_WIDTH)),
            pl.BlockSpec((1, 128, QBLOCK), lambda bb, i: (bb, 0, i)),
            const(gq),
            pl.BlockSpec((1, N_CMP_PAD, 128), lambda bb, i: (bb, 0, 0)),
            pl.BlockSpec((s, d), lambda bb, i: (bb, 0)),
            pl.BlockSpec((1, d, s), lambda bb, i: (bb, 0, 0)),
            pl.BlockSpec((1, s + WINDOW, d), lambda bb, i: (bb, 0, 0)),
            pl.BlockSpec((1, s + WINDOW, d), lambda bb, i: (bb, 0, 0)),
            pl.BlockSpec((1, N_CMP_PAD, hq), lambda bb, i: (i, 0, 0)),
            const(near), const(farrow), const(wintab), const(ovt), const(expand),
        ],
        out_specs=pl.BlockSpec((QBLOCK, C_WIDTH), lambda bb, i: (row(bb, i), 0)),
        out_shape=jax.ShapeDtypeStruct((n, C_WIDTH), jnp.float32),
        scratch_shapes=[pltpu.VMEM((s + QBLOCK, hq), jnp.float32)],
        compiler_params=pltpu.CompilerParams(
            dimension_semantics=("arbitrary", "arbitrary"), vmem_limit_bytes=VMEM_LIMIT),
        name="c_main",
    )(proj, proj, misct, gq, kvc, ksel, vselt, kwin, vwin, cmptab, near, farrow, wintab, ovt, expand)


def _head_cols(tab):
    return jnp.moveaxis(tab, -1, -2).reshape(*tab.shape[:-2], tab.shape[-1] * tab.shape[-2])


def mixer_c(proj, misct, b, s, gq, k_norm, cmp_pos, phi_w1, phi_w2, bias_c):
    kvc, ksel, vselt, kwin, vwin = c_pre(proj, b, s, k_norm, cmp_pos, phi_w1, phi_w2)
    cmptab = _head_cols(_bias_lookup(bias_c, _CMP_BUCKET_T))
    near = _head_cols(_bias_lookup(bias_c, _NEAR_BUCKET_T))
    wintab = _head_cols(_bias_lookup(bias_c, _WIN_BUCKET_T))
    farrow = jnp.repeat(bias_c[_FAR_BUCKET], QBLOCK).reshape(1, C_HEADS * QBLOCK)
    return c_main(proj, misct, gq.reshape(1, C_HEAD_DIM), kvc, ksel, vselt, kwin, vwin, cmptab, near, farrow,
                  wintab, b, s)


B_HPS = 2
B_GROUP = 4


def _b_kernel(alog_ref, dtb_ref, q_ref, k_ref, v_ref, z_ref, misc_ref, arow_ref, cwq_ref, cwk_ref, cwv_ref,
              gon_ref, o_ref, qs, ks, vs, ws, at, gtok, btok, gcrow):
    s = q_ref.shape[0]
    dh = B_HEAD_DIM
    c_len = GDN_CHUNK
    j = pl.program_id(1)
    f32, bf16 = jnp.float32, jnp.bfloat16
    hi = lax.Precision.HIGHEST
    nt = (((1,), (1,)), ((), ()))
    tn = (((0,), (0,)), ((), ()))

    rows = lax.broadcasted_iota(jnp.int32, (s, B_HPS * dh), 0)

    def conv_silu(x_ref, w_ref):
        x = x_ref[...]
        acc = x * w_ref[CONV_WIDTH - 1:CONV_WIDTH, :]
        for k in range(1, CONV_WIDTH):
            shifted = jnp.where(rows >= k, pltpu.roll(x, k, axis=0), 0.0)
            acc = acc + shifted * w_ref[CONV_WIDTH - 1 - k:CONV_WIDTH - k, :]
        return _silu(acc)

    def l2n(x):
        return x * lax.rsqrt(jnp.sum(x * x, axis=-1, keepdims=True) + RMS_EPS)

    qc = conv_silu(q_ref, cwq_ref)
    kc = conv_silu(k_ref, cwk_ref)
    vs[...] = conv_silu(v_ref, cwv_ref)
    misc = misc_ref[...]
    lane = lax.broadcasted_iota(jnp.int32, misc.shape, 1)
    ri = lax.broadcasted_iota(jnp.int32, (c_len, c_len), 0)
    ci = lax.broadcasted_iota(jnp.int32, (c_len, c_len), 1)
    lower = ci <= ri
    strict = ci < ri
    tri_u = jnp.where(ri <= ci, 1.0, 0.0)
    row_in_chunk = lax.broadcasted_iota(jnp.int32, (s, dh), 0) & (c_len - 1)
    for hh in range(B_HPS):
        sl = slice(hh * dh, (hh + 1) * dh)
        qs[:, sl] = l2n(qc[:, sl]) * (dh ** -0.5)
        ks[:, sl] = l2n(kc[:, sl])
        h = B_HPS * j + hh
        neg_a = -jnp.exp(alog_ref[h])
        dtb = dtb_ref[h]
        a_tok = jnp.sum(jnp.where(lane == B_A_LANE + h, misc, 0.0), axis=-1, keepdims=True)
        b_tok = jnp.sum(jnp.where(lane == B_B_LANE + h, misc, 0.0), axis=-1, keepdims=True)
        g = jnp.broadcast_to(neg_a * jax.nn.softplus(a_tok + dtb), (s, dh))
        for sh in (1, 2, 4, 8, 16, 32):
            g = g + jnp.where(row_in_chunk >= sh, pltpu.roll(g, sh, axis=0), 0.0)
        gtok[hh] = g
        btok[hh] = jnp.broadcast_to(_sigmoid(b_tok), (s, dh))
        g_row = neg_a * jax.nn.softplus(arow_ref[0, hh] + dtb)
        gcrow[hh] = jnp.dot(g_row, tri_u, precision=hi, preferred_element_type=f32)

    lane3 = lax.broadcasted_iota(jnp.int32, (c_len, 3 * c_len), 1)
    lo_lanes = (lane3 >= c_len) & (lane3 < 2 * c_len)

    def split_lhs(p):
        p4 = jnp.concatenate([p, p, p], axis=1)
        hi4 = p4.astype(bf16).astype(f32)
        return jnp.where(lo_lanes, p4 - hi4, hi4).astype(bf16)

    def split_rhs(x):
        xh = x.astype(bf16)
        xl = (x - xh.astype(f32)).astype(bf16)
        return jnp.concatenate([xh, xh, xl], axis=0)

    def mm(p_split, x):
        return jnp.dot(p_split, split_rhs(x), preferred_element_type=f32)

    def intra_pair(i, carry):
        where, q_c, k_c, v_c, gb, beta, g_row = [], [], [], [], [], [], []
        for c in [B_GROUP * i + cc for cc in range(B_GROUP)]:
            rs = pl.ds(pl.multiple_of(c * c_len, c_len), c_len)
            for hh in range(B_HPS):
                sl = slice(hh * dh, (hh + 1) * dh)
                where.append((rs, hh, sl))
                q_c.append(qs[rs, sl])
                k_c.append(ks[rs, sl])
                v_c.append(vs[rs, sl])
                gb.append(gtok[hh, rs, :])
                beta.append(btok[hh, rs, :])
                g_row.append(gcrow[hh, pl.ds(c, 1), :])
        nch = range(len(where))
        decay = [jnp.where(lower, jnp.exp(jnp.where(lower, gb[n][:, 0:c_len] - g_row[n], 0.0)), 0.0) for n in nch]
        kb = [k_c[n] * beta[n] for n in nch]
        kk = [lax.dot_general(kb[n].astype(bf16), k_c[n].astype(bf16), nt, preferred_element_type=f32) for n in nch]
        p = [-jnp.where(strict, kk[n] * decay[n], 0.0) for n in nch]
        eg = [jnp.exp(gb[n]) for n in nch]
        x = [jnp.concatenate([v_c[n] * beta[n], kb[n] * eg[n]], axis=1) for n in nch]
        ps = [split_lhs(p[n]) for n in nch]
        x = [x[n] + mm(ps[n], x[n]) for n in nch]
        for _ in range(5):
            p = [mm(ps[n], p[n]) for n in nch]
            ps = [split_lhs(p[n]) for n in nch]
            x = [x[n] + mm(ps[n], x[n]) for n in nch]
        attn = [jnp.where(lower, lax.dot_general(q_c[n].astype(bf16), k_c[n].astype(bf16), nt,
                                                 preferred_element_type=f32) * decay[n], 0.0) for n in nch]
        for n, (rs, hh, sl) in enumerate(where):
            g_last = g_row[n][:, c_len - 1:c_len]
            vs[rs, sl] = x[n][:, 0:dh]
            ws[rs, sl] = x[n][:, dh:2 * dh]
            qs[rs, sl] = q_c[n] * eg[n]
            ks[rs, sl] = k_c[n] * jnp.exp(g_last - gb[n])
            at[rs, hh * c_len:(hh + 1) * c_len] = attn[n]
        return carry

    lax.fori_loop(0, s // (B_GROUP * c_len), intra_pair, 0)

    def recur(c, states):
        rs = pl.ds(pl.multiple_of(c * c_len, c_len), c_len)
        hs = range(B_HPS)
        sls = [slice(hh * dh, (hh + 1) * dh) for hh in hs]
        st_b = [states[hh].astype(bf16) for hh in hs]
        v_new = [vs[rs, sls[hh]] - jnp.dot(ws[rs, sls[hh]].astype(bf16), st_b[hh], preferred_element_type=f32)
                 for hh in hs]
        v_nb = [v_new[hh].astype(bf16) for hh in hs]
        o_st = [jnp.dot(qs[rs, sls[hh]].astype(bf16), st_b[hh], preferred_element_type=f32) for hh in hs]
        new_states = [states[hh] * jnp.exp(gcrow[hh, pl.ds(c, 1), c_len - 1:c_len])
                      + lax.dot_general(ks[rs, sls[hh]].astype(bf16), v_nb[hh], tn, preferred_element_type=f32)
                      for hh in hs]
        for hh in hs:
            o_ref[rs, sls[hh]] = o_st[hh] + jnp.dot(at[rs, hh * c_len:(hh + 1) * c_len].astype(bf16), v_nb[hh],
                                                    preferred_element_type=f32)
        return tuple(new_states)

    lax.fori_loop(0, s // c_len, recur, tuple(jnp.zeros((dh, dh), f32) for _ in range(B_HPS)))

    for hh in range(B_HPS):
        sl = slice(hh * dh, (hh + 1) * dh)
        o = o_ref[:, sl]
        o = o * lax.rsqrt(jnp.mean(o * o, axis=-1, keepdims=True) + RMS_EPS) * gon_ref[...]
        o_ref[:, sl] = o * _silu(z_ref[:, sl])


def mixer_b(proj, misct, b, s, conv_w, a_log, dt_bias, out_norm):
    n = b * s
    w2 = B_HPS * B_HEAD_DIM
    nc = s // GDN_CHUNK
    arow = misct[:, B_A_LANE:B_A_LANE + B_HEADS, :].reshape(b, B_HEADS, nc, GDN_CHUNK)
    qkv0 = OFF["b_qkv"] // w2
    kstep = B_WIDTH // w2
    smem = pl.BlockSpec(memory_space=pltpu.SMEM)
    return pl.pallas_call(
        _b_kernel,
        grid=(b, B_HEADS // B_HPS),
        in_specs=[
            smem, smem,
            pl.BlockSpec((s, w2), lambda bb, j: (bb, qkv0 + j)),
            pl.BlockSpec((s, w2), lambda bb, j: (bb, qkv0 + kstep + j)),
            pl.BlockSpec((s, w2), lambda bb, j: (bb, qkv0 + 2 * kstep + j)),
            pl.BlockSpec((s, w2), lambda bb, j: (bb, OFF["b_z"] // w2 + j)),
            pl.BlockSpec((s, 128), lambda bb, j: (bb, MISC_OFF // 128)),
            pl.BlockSpec((1, B_HPS, nc, GDN_CHUNK), lambda bb, j: (bb, j, 0, 0)),
            pl.BlockSpec((CONV_WIDTH, w2), lambda bb, j: (0, j)),
            pl.BlockSpec((CONV_WIDTH, w2), lambda bb, j: (0, kstep + j)),
            pl.BlockSpec((CONV_WIDTH, w2), lambda bb, j: (0, 2 * kstep + j)),
            pl.BlockSpec((1, B_HEAD_DIM), lambda bb, j: (0, 0)),
        ],
        out_specs=pl.BlockSpec((s, w2), lambda bb, j: (bb, j)),
        out_shape=jax.ShapeDtypeStruct((n, B_WIDTH), jnp.float32),
        scratch_shapes=[
            pltpu.VMEM((s, w2), jnp.float32), pltpu.VMEM((s, w2), jnp.float32), pltpu.VMEM((s, w2), jnp.float32),
            pltpu.VMEM((s, w2), jnp.float32), pltpu.VMEM((s, B_HPS * GDN_CHUNK), jnp.float32),
            pltpu.VMEM((B_HPS, s, B_HEAD_DIM), jnp.float32), pltpu.VMEM((B_HPS, s, B_HEAD_DIM), jnp.float32),
            pltpu.VMEM((B_HPS, nc, GDN_CHUNK), jnp.float32),
        ],
        compiler_params=pltpu.CompilerParams(
            dimension_semantics=("arbitrary", "arbitrary"), vmem_limit_bytes=VMEM_LIMIT),
        name="gdn",
    )(a_log, dt_bias, proj, proj, proj, proj, proj, arow, conv_w, conv_w, conv_w, out_norm.reshape(1, B_HEAD_DIM))


def _rms_norm(x, g):
    y = x * lax.rsqrt(jnp.mean(x * x, axis=-1, keepdims=True) + RMS_EPS)
    return y * g


def _l2norm(x):
    return x * lax.rsqrt(jnp.sum(x * x, axis=-1, keepdims=True) + RMS_EPS)


def _masked_softmax(logits, mask):
    logits = jnp.where(mask, logits, NEG_INF)
    return jnp.where(mask, jax.nn.softmax(logits, axis=-1), 0.0)


def _t5_bucket(dist):
    dist = jnp.maximum(dist, 0)
    log_ratio = jnp.log(jnp.maximum(dist, 1).astype(jnp.float32) / BUCKET_MAX_EXACT) / math.log(BUCKET_MAX_DIST / BUCKET_MAX_EXACT)
    large = BUCKET_MAX_EXACT + (log_ratio * (N_BUCKETS - BUCKET_MAX_EXACT)).astype(jnp.int32)
    large = jnp.minimum(large, N_BUCKETS - 1)
    return jnp.where(dist < BUCKET_MAX_EXACT, dist, large)


def _to_qblocks(a):
    b, s = a.shape[:2]
    return jnp.moveaxis(a.reshape(b, s // QBLOCK, QBLOCK, *a.shape[2:]), 1, 0)


def _from_qblocks(a):
    nb, b = a.shape[:2]
    return jnp.moveaxis(a, 0, 1).reshape(b, nb * QBLOCK, *a.shape[3:])


def _dsa_mixer(q, k, v, q_idx, k_idx, w_idx, bias_tab):
    b, s = q.shape[:2]
    topk = min(TOPK_MAX, s // 4)
    bidx = jnp.arange(b)[:, None, None]
    key_pos = jnp.arange(s)
    scale = A_HEAD_DIM ** -0.5

    def block(args):
        qb, qib, wb, start = args
        t = start + jnp.arange(QBLOCK)
        idx_logits = jnp.einsum('bqhd,bsd->bqhs', qib, k_idx) * (IDX_DIM ** -0.5)
        score = jnp.einsum('bqh,bqhs->bqs', wb * (IDX_HEADS ** -0.5), jax.nn.relu(idx_logits))
        score = jnp.where(key_pos[None, None, :] <= t[None, :, None], score, NEG_INF)
        _, sel = lax.top_k(score, topk)
        kg = k[bidx, sel]
        vg = v[bidx, sel]
        dist = t[None, :, None] - sel
        bias = jnp.transpose(bias_tab[_t5_bucket(dist)], (0, 3, 1, 2))
        logits = jnp.einsum('bqhd,bqkhd->bhqk', qb, kg) * scale + bias
        p = _masked_softmax(logits, (dist >= 0)[:, None])
        return jnp.einsum('bhqk,bqkhd->bqhd', p, vg)

    starts = jnp.arange(s // QBLOCK, dtype=jnp.int32) * QBLOCK
    out = lax.map(block, (_to_qblocks(q), _to_qblocks(q_idx), _to_qblocks(w_idx), starts))
    return _from_qblocks(out)


def _causal_depthwise_conv(x, w):
    c = x.shape[-1]
    return lax.conv_general_dilated(x, w[:, None, :], window_strides=(1,), padding=[(CONV_WIDTH - 1, 0)],
                                    dimension_numbers=('NWC', 'WIO', 'NWC'), feature_group_count=c)


def _gdn_mixer(qkv, a_in, b_in, conv_w, a_log, dt_bias):
    b, s = qkv.shape[:2]
    n = s // GDN_CHUNK
    qkv = jax.nn.silu(_causal_depthwise_conv(qkv, conv_w))
    q, k, v = jnp.split(qkv, 3, axis=-1)
    sh = lambda t: t.reshape(b, s, B_HEADS, B_HEAD_DIM)
    q = _l2norm(sh(q)) * (B_HEAD_DIM ** -0.5)
    k = _l2norm(sh(k))
    v = sh(v)
    beta = jax.nn.sigmoid(b_in)
    g = -jnp.exp(a_log) * jax.nn.softplus(a_in + dt_bias)

    def chunk(t):
        return jnp.moveaxis(t.reshape(b, n, GDN_CHUNK, *t.shape[2:]), 3, 2)

    q, k, v, beta, g = (chunk(t) for t in (q, k, v, beta, g))
    g_cum = jnp.cumsum(g, axis=-1)
    g_last = g_cum[..., -1]
    lower = jnp.tril(jnp.ones((GDN_CHUNK, GDN_CHUNK), dtype=bool))
    strict = jnp.tril(jnp.ones((GDN_CHUNK, GDN_CHUNK), dtype=bool), -1)
    diff = g_cum[..., :, None] - g_cum[..., None, :]
    decay = jnp.where(lower, jnp.exp(jnp.where(lower, diff, 0.0)), 0.0)
    k_beta = k * beta[..., None]
    a_mat = jnp.where(strict, jnp.einsum('bnhid,bnhjd->bnhij', k_beta, k) * decay, 0.0)
    eye = jnp.eye(GDN_CHUNK, dtype=jnp.float32)
    t_inv = lax.linalg.triangular_solve(eye + a_mat, jnp.broadcast_to(eye, a_mat.shape), left_side=True, lower=True)
    u = jnp.einsum('bnhij,bnhjd->bnhid', t_inv, v * beta[..., None])
    w = jnp.einsum('bnhij,bnhjd->bnhid', t_inv, k_beta * jnp.exp(g_cum)[..., None])
    attn = jnp.einsum('bnhid,bnhjd->bnhij', q, k) * decay
    q_dec = q * jnp.exp(g_cum)[..., None]
    k_dec = k * jnp.exp(g_last[..., None] - g_cum)[..., None]

    def step(state, xs):
        u_c, w_c, q_c, k_c, attn_c, gl_c = xs
        v_new = u_c - jnp.einsum('bhik,bhkv->bhiv', w_c, state)
        o = jnp.einsum('bhik,bhkv->bhiv', q_c, state) + jnp.einsum('bhij,bhjv->bhiv', attn_c, v_new)
        state = state * jnp.exp(gl_c)[..., None, None] + jnp.einsum('bhik,bhiv->bhkv', k_c, v_new)
        return state, o

    state0 = jnp.zeros((b, B_HEADS, B_HEAD_DIM, B_HEAD_DIM), jnp.float32)
    xs = tuple(jnp.moveaxis(t, 1, 0) for t in (u, w, q_dec, k_dec, attn, g_last))
    _, o = lax.scan(step, state0, xs)
    o = jnp.moveaxis(jnp.moveaxis(o, 0, 1), 2, 3)
    return o.reshape(b, s, B_HEADS, B_HEAD_DIM)


def _compress_blocks(tok, pos, w1, w2):
    b, s, d = tok.shape
    n_cmp = (s - CMP_BLOCK) // CMP_STRIDE + 1
    idx = jnp.arange(n_cmp)[:, None] * CMP_STRIDE + jnp.arange(CMP_BLOCK)[None, :]
    blocks = tok[:, idx] + pos
    return jax.nn.silu(blocks.reshape(b, n_cmp, CMP_BLOCK * d) @ w1) @ w2


def _nsa_mixer(q, kv, gates, k_norm, cmp_pos, phi_w1, phi_w2, bias_tab):
    b, s = q.shape[:2]
    scale = C_HEAD_DIM ** -0.5
    t_all = jnp.arange(s)
    k_cmp, v_cmp, k_sel, v_sel, k_win, v_win = (kv[:, :, i] for i in range(6))
    k_sel = _rms_norm(k_sel, k_norm[1])
    k_win = _rms_norm(k_win, k_norm[2])
    kc = _rms_norm(_compress_blocks(k_cmp, cmp_pos[0], phi_w1[0], phi_w2[0]), k_norm[0])
    vc = _compress_blocks(v_cmp, cmp_pos[1], phi_w1[1], phi_w2[1])
    n_cmp = kc.shape[1]
    cmp_start = jnp.arange(n_cmp) * CMP_STRIDE
    cmp_end = cmp_start + CMP_BLOCK - 1
    cmp_valid = cmp_end[None, :] <= t_all[:, None]
    cmp_bias = jnp.transpose(bias_tab[_t5_bucket(t_all[:, None] - cmp_end[None, :])], (2, 0, 1))
    logits = jnp.einsum('bqhd,bnd->bhqn', q, kc) * scale + cmp_bias
    p_cmp = _masked_softmax(logits, cmp_valid[None, None])
    o_cmp = jnp.einsum('bhqn,bnd->bqhd', p_cmp, vc)
    n_sb = s // SEL_BLOCK
    n_pick = min(N_SEL, n_sb)
    sb_start = jnp.arange(n_sb) * SEL_BLOCK
    overlap = ((cmp_start[:, None] < sb_start[None, :] + SEL_BLOCK) & (cmp_start[:, None] + CMP_BLOCK > sb_start[None, :])).astype(jnp.float32)
    importance = jnp.einsum('bhqn,nj->bqj', p_cmp, overlap)
    cur = t_all // SEL_BLOCK
    blk = jnp.arange(n_sb)
    forced = (blk[None, :] == 0) | (blk[None, :] == cur[:, None]) | (blk[None, :] == jnp.maximum(cur[:, None] - 1, 0))
    admissible = sb_start[None, :] <= t_all[:, None]
    importance = jnp.where(admissible[None], jnp.where(forced[None], FORCE_SCORE, importance), NEG_INF)
    _, sel = lax.top_k(importance, n_pick)
    k_blocks = k_sel.reshape(b, n_sb, SEL_BLOCK, C_HEAD_DIM)
    v_blocks = v_sel.reshape(b, n_sb, SEL_BLOCK, C_HEAD_DIM)
    k_win_pad = jnp.pad(k_win, ((0, 0), (WINDOW, 0), (0, 0)))
    v_win_pad = jnp.pad(v_win, ((0, 0), (WINDOW, 0), (0, 0)))
    bidx = jnp.arange(b)[:, None, None]
    n_keys = n_pick * SEL_BLOCK

    def block(args):
        qb, selb, start = args
        t = start + jnp.arange(QBLOCK)
        kg = k_blocks[bidx, selb].reshape(b, QBLOCK, n_keys, C_HEAD_DIM)
        vg = v_blocks[bidx, selb].reshape(b, QBLOCK, n_keys, C_HEAD_DIM)
        s_pos = (selb[..., None] * SEL_BLOCK + jnp.arange(SEL_BLOCK)).reshape(b, QBLOCK, n_keys)
        dist = t[None, :, None] - s_pos
        bias = jnp.transpose(bias_tab[_t5_bucket(dist)], (0, 3, 1, 2))
        lg = jnp.einsum('bqhd,bqkd->bhqk', qb, kg) * scale + bias
        p = _masked_softmax(lg, (dist >= 0)[:, None])
        o_sel = jnp.einsum('bhqk,bqkd->bqhd', p, vg)
        kw = lax.dynamic_slice_in_dim(k_win_pad, start, QBLOCK + WINDOW, axis=1)
        vw = lax.dynamic_slice_in_dim(v_win_pad, start, QBLOCK + WINDOW, axis=1)
        s_w = start - WINDOW + jnp.arange(QBLOCK + WINDOW)
        dist_w = t[:, None] - s_w[None, :]
        valid_w = (s_w[None, :] >= 0) & (dist_w >= 0) & (dist_w < WINDOW)
        bias_w = jnp.transpose(bias_tab[_t5_bucket(dist_w)], (2, 0, 1))
        lw = jnp.einsum('bqhd,bsd->bhqs', qb, kw) * scale + bias_w
        pw = _masked_softmax(lw, valid_w[None, None])
        o_win = jnp.einsum('bhqs,bsd->bqhd', pw, vw)
        return o_sel, o_win

    starts = jnp.arange(s // QBLOCK, dtype=jnp.int32) * QBLOCK
    o_sel, o_win = lax.map(block, (_to_qblocks(q), _to_qblocks(sel), starts))
    o_sel = _from_qblocks(o_sel)
    o_win = _from_qblocks(o_win)
    g = jax.nn.sigmoid(gates).reshape(b, s, C_HEADS, 3)
    return g[..., 0:1] * o_cmp + g[..., 1:2] * o_sel + g[..., 2:3] * o_win


def _field(proj3, name):
    return proj3[..., OFF[name]:OFF[name] + _ORIG[name][1]]


def kernel(x, norm_g, w_in, a_kv_norm, a_w_ukv, a_q_norm, a_k_norm, b_conv, b_a_log, b_dt_bias, b_out_norm, c_q_norm, c_k_norm, c_cmp_pos, c_phi_w1, c_phi_w2, w_branch, w_out, rel_bias):
    b, s, _ = x.shape
    n = b * s
    bias_a = rel_bias[:, :A_HEADS]
    bias_c = rel_bias[:, A_HEADS:]
    xf = x.reshape(n, D_MODEL)
    perm = jnp.asarray(_PERM_IDX)
    valid = jnp.asarray(_PERM_VALID)
    sh = lambda t, h, d: t.reshape(b, s, h, d)
    for l in range(DEPTH):
        w_p = jnp.where(valid[None, :], w_in[l][:, perm], 0.0).astype(jnp.bfloat16)
        proj = in_proj(xf, norm_g[l], w_p)
        p3 = proj.reshape(b, s, N_PAD)

        misct = jnp.transpose(p3[..., MISC_OFF:], (0, 2, 1))
        y_a = mixer_a(proj, misct, b, s, a_kv_norm[l], a_w_ukv[l], a_q_norm[l], a_k_norm[l], bias_a)

        y_b = mixer_b(proj, misct, b, s, b_conv[l], b_a_log[l], b_dt_bias[l], b_out_norm[l])
        y_c = mixer_c(proj, misct, b, s, c_q_norm[l], c_k_norm[l], c_cmp_pos[l], c_phi_w1[l], c_phi_w2[l], bias_c)

        wbr = w_branch[l].astype(jnp.bfloat16)
        xf = merge(xf, proj, y_a, y_b, y_c, wbr[:A_WIDTH], wbr[A_WIDTH:A_WIDTH + B_WIDTH],
                   wbr[A_WIDTH + B_WIDTH:], w_out[l].astype(jnp.bfloat16))
    return xf.reshape(b, s, D_MODEL)
```

```python
import functools
import math

import jax
import jax.numpy as jnp
import numpy as np
from jax import lax
from jax.experimental import pallas as pl
from jax.experimental.pallas import tpu as pltpu

D_MODEL = 1024
DEPTH = 4
QBLOCK = 128
NEG_INF = -1e30
FORCE_SCORE = 1e9
RMS_EPS = 1e-6

A_HEADS = 4
A_HEAD_DIM = 64
A_WIDTH = A_HEADS * A_HEAD_DIM
A_KV_LATENT = 128
IDX_HEADS = 8
IDX_DIM = 64
TOPK_MAX = 256

B_HEADS = 4
B_HEAD_DIM = 128
B_WIDTH = B_HEADS * B_HEAD_DIM
CONV_WIDTH = 4
GDN_CHUNK = 64

C_HEADS = 4
C_HEAD_DIM = 64
C_WIDTH = C_HEADS * C_HEAD_DIM
CMP_BLOCK = 32
CMP_STRIDE = 16
SEL_BLOCK = 64
N_SEL = 16
WINDOW = 512
PHI_HIDDEN = 256

N_BUCKETS = 32
BUCKET_MAX_EXACT = 16
BUCKET_MAX_DIST = 128

VMEM_LIMIT = 48 * 1024 * 1024

_ORIG = {}
_o = 0
for _name, _w in (("a_q", 256), ("a_ckv", 128), ("a_qi", 512), ("a_ki", 64), ("a_wi", 8), ("a_z", 256),
                  ("b_qkv", 1536), ("b_a", 4), ("b_b", 4), ("b_z", 512),
                  ("c_q", 256), ("c_kv", 384), ("c_g", 12), ("c_z", 256), ("g", 3072)):
    _ORIG[_name] = (_o, _w)
    _o += _w
N_IN = _o

_NEW_ORDER = ("g", "b_qkv", "b_z", "a_qi", "a_q", "a_z", "c_q", "c_z", "c_kv", "a_ckv",
              "a_ki", "a_wi", "b_a", "b_b", "c_g")
OFF = {}
_o = 0
for _name in _NEW_ORDER:
    OFF[_name] = _o
    _o += _ORIG[_name][1]
MISC_OFF = OFF["a_ki"]
N_PAD = 7296
assert MISC_OFF == 7168 and _o <= N_PAD
A_WI_LANE = OFF["a_wi"] - MISC_OFF
B_A_LANE = OFF["b_a"] - MISC_OFF
B_B_LANE = OFF["b_b"] - MISC_OFF
C_G_LANE = OFF["c_g"] - MISC_OFF


def _perm_indices():
    idx = np.zeros((N_PAD,), np.int32)
    valid = np.zeros((N_PAD,), bool)
    for name in _NEW_ORDER:
        o_old, w = _ORIG[name]
        idx[OFF[name]:OFF[name] + w] = np.arange(o_old, o_old + w)
        valid[OFF[name]:OFF[name] + w] = True
    return idx, valid


_PERM_IDX, _PERM_VALID = _perm_indices()


IN_TM = 512
IN_TN = N_PAD // 3


def _in_proj_kernel(x_ref, g_ref, w_ref, o_ref):
    x = x_ref[...]
    ms = jnp.mean(x * x, axis=-1, keepdims=True)
    h = (x * lax.rsqrt(ms + RMS_EPS)) * g_ref[...]
    o_ref[...] = jnp.dot(h.astype(jnp.bfloat16), w_ref[...], preferred_element_type=jnp.float32)


def in_proj(xf, g, w_bf16):
    n = xf.shape[0]
    return pl.pallas_call(
        _in_proj_kernel,
        grid=(N_PAD // IN_TN, n // IN_TM),
        in_specs=[
            pl.BlockSpec((IN_TM, D_MODEL), lambda j, i: (i, 0)),
            pl.BlockSpec((1, D_MODEL), lambda j, i: (0, 0)),
            pl.BlockSpec((D_MODEL, IN_TN), lambda j, i: (0, j)),
        ],
        out_specs=pl.BlockSpec((IN_TM, IN_TN), lambda j, i: (i, j)),
        out_shape=jax.ShapeDtypeStruct((n, N_PAD), jnp.float32),
        compiler_params=pltpu.CompilerParams(
            dimension_semantics=("arbitrary", "arbitrary"), vmem_limit_bytes=VMEM_LIMIT),
        name="in_proj",
    )(xf, g.reshape(1, D_MODEL), w_bf16)


MG_TM = 256


def _merge_kernel(x_ref, g_ref, ya_ref, yb_ref, yc_ref, wa_ref, wb_ref, wc_ref, wo_ref, o_ref):
    def branch(y_ref, w_ref, k):
        p = jnp.dot(y_ref[...].astype(jnp.bfloat16), w_ref[...], preferred_element_type=jnp.float32)
        return _sigmoid(g_ref[:, k * D_MODEL:(k + 1) * D_MODEL]) * p

    merged = branch(ya_ref, wa_ref, 0) + branch(yb_ref, wb_ref, 1) + branch(yc_ref, wc_ref, 2)
    o_ref[...] = x_ref[...] + jnp.dot(merged.astype(jnp.bfloat16), wo_ref[...],
                                      preferred_element_type=jnp.float32)


def merge(xf, proj, ya, yb, yc, wa, wb, wc, wo):
    n = xf.shape[0]
    row = lambda w: pl.BlockSpec((MG_TM, w), lambda i: (i, 0))
    full = lambda a: pl.BlockSpec(a.shape, lambda i: (0, 0))
    return pl.pallas_call(
        _merge_kernel,
        grid=(n // MG_TM,),
        in_specs=[row(D_MODEL), row(3 * D_MODEL), row(A_WIDTH), row(B_WIDTH), row(C_WIDTH),
                  full(wa), full(wb), full(wc), full(wo)],
        out_specs=row(D_MODEL),
        out_shape=jax.ShapeDtypeStruct((n, D_MODEL), jnp.float32),
        compiler_params=pltpu.CompilerParams(
            dimension_semantics=("arbitrary",), vmem_limit_bytes=VMEM_LIMIT),
        name="merge",
    )(xf, proj, ya, yb, yc, wa, wb, wc, wo)


def _bucket_np(dist):
    d = np.maximum(np.asarray(dist, np.int64), 0)
    ratio = np.log(np.maximum(d, 1).astype(np.float64) / BUCKET_MAX_EXACT) / math.log(BUCKET_MAX_DIST / BUCKET_MAX_EXACT)
    scaled = ratio * (N_BUCKETS - BUCKET_MAX_EXACT)
    frac = scaled - np.floor(scaled)
    edge = (d > BUCKET_MAX_EXACT) & (d < BUCKET_MAX_DIST) & ((frac < 1e-4) | (frac > 1 - 1e-4))
    assert not edge.any(), "bucket boundary too close to an integer distance"
    large = np.minimum(BUCKET_MAX_EXACT + np.floor(scaled + 1e-9).astype(np.int64), N_BUCKETS - 1)
    return np.where(d < BUCKET_MAX_EXACT, d, large).astype(np.int32)


_NEAR_BUCKET_T = _bucket_np(np.arange(QBLOCK)[None, :] + QBLOCK - np.arange(2 * QBLOCK)[:, None])
_FAR_BUCKET = int(_bucket_np(np.array([BUCKET_MAX_DIST]))[0])
assert (_bucket_np(np.arange(BUCKET_MAX_DIST, 4096)) == _FAR_BUCKET).all()

INT_MIN = -2 ** 31


COL_SLAB = 64


def _col_reduce(x, op):
    rows = x.shape[0]
    if rows > COL_SLAB and rows % COL_SLAB == 0:
        x = op(x.reshape(rows // COL_SLAB, COL_SLAB, x.shape[1]), axis=0)
    return op(x, axis=0, keepdims=True)


ROW_VARIANTS = 8


def _run_causal_variant(run, i, s):
    per = (s // QBLOCK) // ROW_VARIANTS
    for v in range(ROW_VARIANTS):
        pl.when((i >= v * per) & (i < (v + 1) * per))(functools.partial(run, (v + 1) * per * QBLOCK))


def _sigmoid(x):
    return 0.5 * jnp.tanh(0.5 * x) + 0.5


def _silu(x):
    return x * _sigmoid(x)


def _bias_lookup(table, buckets):
    idx = jnp.asarray(buckets)[..., None]
    out = jnp.zeros(idx.shape[:-1] + (table.shape[1],), table.dtype)
    for bkt in range(N_BUCKETS):
        out = jnp.where(idx == bkt, table[bkt], out)
    return out


A_PRE_TM = 512


def _a_pre_kernel(ckv_ref, q_ref, gkv_ref, wukv_ref, gq_ref, gk_ref, hm_ref, kn_ref, vt_ref, qn_ref):
    c = ckv_ref[...]
    c = c * lax.rsqrt(jnp.mean(c * c, axis=-1, keepdims=True) + RMS_EPS) * gkv_ref[...]
    kv = jnp.dot(c.astype(jnp.bfloat16), wukv_ref[...], preferred_element_type=jnp.float32)
    hm = hm_ref[...]

    def head_rms(x, g):
        ms = jnp.dot(x * x, hm, precision=lax.Precision.HIGHEST, preferred_element_type=jnp.float32)
        return x * lax.rsqrt(ms + RMS_EPS) * g

    kn_ref[...] = head_rms(kv[:, :A_WIDTH], gk_ref[...])
    qn_ref[...] = head_rms(q_ref[...], gq_ref[...]) * (A_HEAD_DIM ** -0.5)
    vt_ref[0] = kv[:, A_WIDTH:].T


def a_pre(proj, b, s, gkv, wukv_bf16, gq, gk):
    n = b * s
    nt = s // A_PRE_TM
    hm = jnp.asarray(np.kron(np.eye(A_HEADS), np.ones((A_HEAD_DIM, A_HEAD_DIM))) / A_HEAD_DIM, jnp.float32)
    row = lambda bb, j: (bb * nt + j, 0)
    full = lambda a: pl.BlockSpec(a.shape, lambda bb, j: (0,) * a.ndim)
    gq = jnp.tile(gq, A_HEADS).reshape(1, A_WIDTH)
    gk = jnp.tile(gk, A_HEADS).reshape(1, A_WIDTH)
    gkv = gkv.reshape(1, A_KV_LATENT)
    return pl.pallas_call(
        _a_pre_kernel,
        grid=(b, nt),
        in_specs=[
            pl.BlockSpec((A_PRE_TM, A_KV_LATENT), lambda bb, j: (bb * nt + j, OFF["a_ckv"] // A_KV_LATENT)),
            pl.BlockSpec((A_PRE_TM, A_WIDTH), lambda bb, j: (bb * nt + j, OFF["a_q"] // A_WIDTH)),
            full(gkv), full(wukv_bf16), full(gq), full(gk), full(hm),
        ],
        out_specs=[
            pl.BlockSpec((A_PRE_TM, A_WIDTH), row),
            pl.BlockSpec((1, A_WIDTH, A_PRE_TM), lambda bb, j: (bb, 0, j)),
            pl.BlockSpec((A_PRE_TM, A_WIDTH), row),
        ],
        out_shape=[
            jax.ShapeDtypeStruct((n, A_WIDTH), jnp.float32),
            jax.ShapeDtypeStruct((b, A_WIDTH, s), jnp.float32),
            jax.ShapeDtypeStruct((n, A_WIDTH), jnp.float32),
        ],
        compiler_params=pltpu.CompilerParams(
            dimension_semantics=("arbitrary", "arbitrary"), vmem_limit_bytes=VMEM_LIMIT),
        name="a_pre",
    )(proj, proj, gkv, wukv_bf16, gq, gk, hm)


def _a_main_kernel(far_ref, qn_ref, qi_ref, z_ref, misct_ref, kn_ref, vt_ref, ki_ref, near_ref, o_ref,
                   key_scr, lg_scr, j_scr):
    s = kn_ref.shape[0]
    i = pl.program_id(1)
    t0 = pl.multiple_of(i * QBLOCK, QBLOCK)
    f32, bf16 = jnp.float32, jnp.bfloat16
    nt = (((1,), (1,)), ((), ()))
    kf = float(TOPK_MAX)

    def run(nrows):

        qi = qi_ref[...] * (IDX_DIM ** -0.5)
        qstack = jnp.concatenate([qi[:, h * IDX_DIM:(h + 1) * IDX_DIM] for h in range(IDX_HEADS)], axis=0)
        ki = ki_ref[0:nrows, 0:IDX_DIM]
        sc = lax.dot_general(ki.astype(bf16), qstack.astype(bf16), nt, preferred_element_type=f32)
        wt = misct_ref[0, A_WI_LANE:A_WI_LANE + IDX_HEADS, :] * (IDX_HEADS ** -0.5)
        score = jnp.zeros((nrows, QBLOCK), f32)
        for h in range(IDX_HEADS):
            score = score + wt[h:h + 1, :] * jnp.maximum(sc[:, h * QBLOCK:(h + 1) * QBLOCK], 0.0)
        score = score + 0.0
        kpos = lax.broadcasted_iota(jnp.int32, (nrows, QBLOCK), 0)
        tpos = t0 + lax.broadcasted_iota(jnp.int32, (nrows, QBLOCK), 1)
        causal = kpos <= tpos
        bits = pltpu.bitcast(score, jnp.int32)
        key = jnp.where(bits < 0, bits ^ jnp.int32(0x7FFFFFFF), bits)
        key_scr[0:nrows, :] = jnp.where(causal, key, jnp.int32(INT_MIN))

        def count_ge(cand):
            return _col_reduce(jnp.where(key_scr[0:nrows, :] >= cand, 1.0, 0.0), jnp.sum)

        def bisect(it, thr):
            cand = thr + lax.shift_left(jnp.int32(1), 31 - it)
            return jnp.where(count_ge(cand) >= kf, cand, thr)

        thr = lax.fori_loop(0, 32, bisect, jnp.full((1, QBLOCK), INT_MIN, jnp.int32))
        n_ge = count_ge(thr)
        need = kf - count_ge(thr + 1)

        j_scr[...] = jnp.full((1, QBLOCK), nrows - 1, jnp.int32)
        surplus = jnp.where((n_ge > kf) & (thr > INT_MIN), 1.0, 0.0)

        @pl.when(jnp.max(surplus) > 0.0)
        def _():
            def bisect_idx(it, lohi):
                lo, hi = lohi
                mid = lax.shift_right_arithmetic(lo + hi, 1)
                k = key_scr[0:nrows, :]
                kp = lax.broadcasted_iota(jnp.int32, (nrows, QBLOCK), 0)
                c = _col_reduce(jnp.where((k == thr) & (kp <= mid), 1.0, 0.0), jnp.sum)
                ok = c >= need
                return jnp.where(ok, lo, mid), jnp.where(ok, mid, hi)

            lo0 = jnp.full((1, QBLOCK), -1, jnp.int32)
            hi0 = jnp.full((1, QBLOCK), nrows - 1, jnp.int32)
            _, hi = lax.fori_loop(0, 11, bisect_idx, (lo0, hi0))
            j_scr[...] = hi

        key = key_scr[0:nrows, :]
        sel = ((key > thr) | ((key == thr) & (kpos <= j_scr[...]))) & causal

        qn = qn_ref[...]
        lane = lax.broadcasted_iota(jnp.int32, (QBLOCK, A_WIDTH), 1)
        qblk = jnp.concatenate(
            [jnp.where((lane >= h * A_HEAD_DIM) & (lane < (h + 1) * A_HEAD_DIM), qn, 0.0) for h in range(A_HEADS)],
            axis=0)
        lg_scr[0:QBLOCK, :] = jnp.zeros((QBLOCK, A_HEADS * QBLOCK), f32)
        lg_scr[QBLOCK:QBLOCK + nrows, :] = lax.dot_general(kn_ref[0:nrows, :].astype(bf16), qblk.astype(bf16), nt,
                                                           preferred_element_type=f32)
        outs = []
        for h in range(A_HEADS):
            cols = slice(h * QBLOCK, (h + 1) * QBLOCK)
            lg_scr[pl.ds(t0, 2 * QBLOCK), cols] += near_ref[h] - far_ref[h]
            l = jnp.where(sel, lg_scr[QBLOCK:QBLOCK + nrows, cols], NEG_INF)
            m = _col_reduce(l, jnp.max)
            p = jnp.exp(l - m)
            den = _col_reduce(p, jnp.sum)
            vt = vt_ref[0, h * A_HEAD_DIM:(h + 1) * A_HEAD_DIM, 0:nrows]
            o_t = jnp.dot(vt.astype(bf16), p.astype(bf16), preferred_element_type=f32)
            outs.append(o_t / den)
        o = jnp.concatenate(outs, axis=0).T
        o_ref[...] = o * _silu(z_ref[...])

    _run_causal_variant(run, i, s)


A_QPS = 2


def _a_pair_kernel(far_ref, qn_ref, qi_ref, z_ref, misct_ref, kn_ref, vt_ref, ki_ref, near_ref, o_ref,
                   key_scr, lg_scr, j_scr):
    s = kn_ref.shape[0]
    f32, bf16 = jnp.float32, jnp.bfloat16
    nt = (((1,), (1,)), ((), ()))
    kf = float(TOPK_MAX)
    groups = range(A_QPS)

    def run(step):
        t0 = [(A_QPS * step + g) * QBLOCK for g in groups]
        nr = [t + QBLOCK for t in t0]
        qrows = [slice(g * QBLOCK, (g + 1) * QBLOCK) for g in groups]

        for g in groups:
            qi = qi_ref[qrows[g], :] * (IDX_DIM ** -0.5)
            qstack = jnp.concatenate([qi[:, h * IDX_DIM:(h + 1) * IDX_DIM] for h in range(IDX_HEADS)], axis=0)
            ki = ki_ref[0:nr[g], 0:IDX_DIM]
            sc = lax.dot_general(ki.astype(bf16), qstack.astype(bf16), nt, preferred_element_type=f32)
            wt = misct_ref[0, A_WI_LANE:A_WI_LANE + IDX_HEADS, qrows[g]] * (IDX_HEADS ** -0.5)
            score = jnp.zeros((nr[g], QBLOCK), f32)
            for h in range(IDX_HEADS):
                score = score + wt[h:h + 1, :] * jnp.maximum(sc[:, h * QBLOCK:(h + 1) * QBLOCK], 0.0)
            score = score + 0.0
            kpos = lax.broadcasted_iota(jnp.int32, (nr[g], QBLOCK), 0)
            tpos = t0[g] + lax.broadcasted_iota(jnp.int32, (nr[g], QBLOCK), 1)
            bits = pltpu.bitcast(score, jnp.int32)
            key = jnp.where(bits < 0, bits ^ jnp.int32(0x7FFFFFFF), bits)
            key_scr[g, 0:nr[g], :] = jnp.where(kpos <= tpos, key, jnp.int32(INT_MIN))

        def count_ge(g, cand):
            return _col_reduce(jnp.where(key_scr[g, 0:nr[g], :] >= cand, 1.0, 0.0), jnp.sum)

        def bisect(it, thrs):
            bit = lax.shift_left(jnp.int32(1), 31 - it)
            cnt = [count_ge(g, thrs[g] + bit) for g in groups]
            return tuple(jnp.where(cnt[g] >= kf, thrs[g] + bit, thrs[g]) for g in groups)

        thrs = lax.fori_loop(0, 32, bisect, tuple(jnp.full((1, QBLOCK), INT_MIN, jnp.int32) for _ in groups))

        for g in groups:
            nrows, thr = nr[g], thrs[g]
            n_ge = count_ge(g, thr)
            need = kf - count_ge(g, thr + 1)
            kpos = lax.broadcasted_iota(jnp.int32, (nrows, QBLOCK), 0)

            j_scr[g] = jnp.full((1, QBLOCK), nrows - 1, jnp.int32)
            surplus = jnp.where((n_ge > kf) & (thr > INT_MIN), 1.0, 0.0)

            @pl.when(jnp.max(surplus) > 0.0)
            def _():
                def bisect_idx(it, lohi):
                    lo, hi = lohi
                    mid = lax.shift_right_arithmetic(lo + hi, 1)
                    k = key_scr[g, 0:nrows, :]
                    c = _col_reduce(jnp.where((k == thr) & (kpos <= mid), 1.0, 0.0), jnp.sum)
                    ok = c >= need
                    return jnp.where(ok, lo, mid), jnp.where(ok, mid, hi)

                lo0 = jnp.full((1, QBLOCK), -1, jnp.int32)
                hi0 = jnp.full((1, QBLOCK), nrows - 1, jnp.int32)
                _, hi = lax.fori_loop(0, 11, bisect_idx, (lo0, hi0))
                j_scr[g] = hi

            key = key_scr[g, 0:nrows, :]
            tpos = t0[g] + lax.broadcasted_iota(jnp.int32, (nrows, QBLOCK), 1)
            sel = ((key > thr) | ((key == thr) & (kpos <= j_scr[g]))) & (kpos <= tpos)

            qn = qn_ref[qrows[g], :]
            lane = lax.broadcasted_iota(jnp.int32, (QBLOCK, A_WIDTH), 1)
            qblk = jnp.concatenate(
                [jnp.where((lane >= h * A_HEAD_DIM) & (lane < (h + 1) * A_HEAD_DIM), qn, 0.0)
                 for h in range(A_HEADS)], axis=0)
            lg_scr[0:QBLOCK, :] = jnp.zeros((QBLOCK, A_HEADS * QBLOCK), f32)
            lg_scr[QBLOCK:QBLOCK + nrows, :] = lax.dot_general(kn_ref[0:nrows, :].astype(bf16), qblk.astype(bf16),
                                                               nt, preferred_element_type=f32)
            outs = []
            for h in range(A_HEADS):
                cols = slice(h * QBLOCK, (h + 1) * QBLOCK)
                lg_scr[t0[g]:t0[g] + 2 * QBLOCK, cols] += near_ref[h] - far_ref[h]
                l = jnp.where(sel, lg_scr[QBLOCK:QBLOCK + nrows, cols], NEG_INF)
                m = _col_reduce(l, jnp.max)
                p = jnp.exp(l - m)
                den = _col_reduce(p, jnp.sum)
                vt = vt_ref[0, h * A_HEAD_DIM:(h + 1) * A_HEAD_DIM, 0:nrows]
                o_t = jnp.dot(vt.astype(bf16), p.astype(bf16), preferred_element_type=f32)
                outs.append(o_t / den)
            o = jnp.concatenate(outs, axis=0).T
            o_ref[qrows[g], :] = o * _silu(z_ref[qrows[g], :])

    step = pl.program_id(1)
    for v in range(s // (A_QPS * QBLOCK)):
        pl.when(step == v)(functools.partial(run, v))


def a_main(proj, misct, qn, kn, vt, near_t, far, b, s):
    n = b * s
    tq = A_QPS * QBLOCK
    nq = s // tq
    row = lambda bb, i: bb * nq + i
    return pl.pallas_call(
        _a_pair_kernel,
        grid=(b, nq),
        in_specs=[
            pl.BlockSpec(memory_space=pltpu.SMEM),
            pl.BlockSpec((tq, A_WIDTH), lambda bb, i: (row(bb, i), 0)),
            pl.BlockSpec((tq, IDX_HEADS * IDX_DIM), lambda bb, i: (row(bb, i), OFF["a_qi"] // (IDX_HEADS * IDX_DIM))),
            pl.BlockSpec((tq, A_WIDTH), lambda bb, i: (row(bb, i), OFF["a_z"] // A_WIDTH)),
            pl.BlockSpec((1, 128, tq), lambda bb, i: (bb, 0, i)),
            pl.BlockSpec((s, A_WIDTH), lambda bb, i: (bb, 0)),
            pl.BlockSpec((1, A_WIDTH, s), lambda bb, i: (bb, 0, 0)),
            pl.BlockSpec((s, 128), lambda bb, i: (bb, MISC_OFF // 128)),
            pl.BlockSpec((A_HEADS, 2 * QBLOCK, QBLOCK), lambda bb, i: (0, 0, 0)),
        ],
        out_specs=pl.BlockSpec((tq, A_WIDTH), lambda bb, i: (row(bb, i), 0)),
        out_shape=jax.ShapeDtypeStruct((n, A_WIDTH), jnp.float32),
        scratch_shapes=[
            pltpu.VMEM((A_QPS, s, QBLOCK), jnp.int32),
            pltpu.VMEM((s + QBLOCK, A_HEADS * QBLOCK), jnp.float32),
            pltpu.VMEM((A_QPS, 1, QBLOCK), jnp.int32),
        ],
        compiler_params=pltpu.CompilerParams(
            dimension_semantics=("arbitrary", "arbitrary"), vmem_limit_bytes=VMEM_LIMIT),
        name="a_main",
    )(far, qn, proj, proj, misct, kn, vt, proj, near_t)


def mixer_a(proj, misct, b, s, gkv, wukv, gq, gk, bias_a):
    kn, vt, qn = a_pre(proj, b, s, gkv, wukv.astype(jnp.bfloat16), gq, gk)
    near_t = jnp.transpose(_bias_lookup(bias_a, _NEAR_BUCKET_T), (2, 0, 1))
    far = bias_a[_FAR_BUCKET]
    return a_main(proj, misct, qn, kn, vt, near_t, far, b, s)


N_CMP_PAD = 128
N_SB = 32
WIN_KEYS = WINDOW + QBLOCK
CMP_GROUPS = CMP_BLOCK // CMP_STRIDE

_CMP_BUCKET_T = _bucket_np((np.arange(16)[:, None, None] * QBLOCK + np.arange(QBLOCK)[None, None, :])
                           - (np.arange(N_CMP_PAD)[None, :, None] * CMP_STRIDE + CMP_BLOCK - 1))
_WIN_BUCKET_T = _bucket_np(np.arange(QBLOCK)[None, :] + WINDOW - np.arange(WIN_KEYS)[:, None])
_OVERLAP_T = np.array([[1.0 if (n * CMP_STRIDE < j * SEL_BLOCK + SEL_BLOCK and n * CMP_STRIDE + CMP_BLOCK > j * SEL_BLOCK
                              and n < N_CMP_PAD - 1) else 0.0 for n in range(N_CMP_PAD)] for j in range(N_SB)], np.float32)
_SB_EXPAND = (np.arange(2048)[:, None] // SEL_BLOCK == np.arange(N_SB)[None, :]).astype(np.float32)


def _c_pre_kernel(cmp_ref, sel_ref, win_ref, kn_ref, pos_ref, w1_ref, w2_ref,
                  kvc_ref, ksel_ref, vselt_ref, kwin_ref, vwin_ref):
    f32, bf16 = jnp.float32, jnp.bfloat16
    d = C_HEAD_DIM

    def rms(x, g):
        return x * lax.rsqrt(jnp.mean(x * x, axis=-1, keepdims=True) + RMS_EPS) * g

    acc = [jnp.zeros((N_CMP_PAD, 2 * PHI_HIDDEN), f32) for _ in range(CMP_GROUPS)]
    for j in range(CMP_STRIDE):
        xs = cmp_ref[pl.ds(j, N_CMP_PAD, stride=CMP_STRIDE), :]
        for half in range(CMP_GROUPS):
            jj = half * CMP_STRIDE + j
            acc[half] = acc[half] + jnp.dot((xs + pos_ref[jj:jj + 1, :]).astype(bf16), w1_ref[jj],
                                            preferred_element_type=f32)
    hid = acc[0] + pltpu.roll(acc[1], N_CMP_PAD - 1, axis=0)
    kv = jnp.dot(_silu(hid).astype(bf16), w2_ref[...], preferred_element_type=f32)
    kvc_ref[0, :, 0:d] = rms(kv[:, 0:d], kn_ref[0:1, :])
    kvc_ref[0, :, d:2 * d] = kv[:, d:2 * d]

    sel = sel_ref[...]
    ksel_ref[...] = rms(sel[:, 0:d], kn_ref[1:2, :])
    vselt_ref[0] = sel.T[d:2 * d, :]
    win = win_ref[...]
    kwin_ref[0, 0:WINDOW, :] = jnp.zeros((WINDOW, d), f32)
    vwin_ref[0, 0:WINDOW, :] = jnp.zeros((WINDOW, d), f32)
    kwin_ref[0, WINDOW:, :] = rms(win[:, 0:d], kn_ref[2:3, :])
    vwin_ref[0, WINDOW:, :] = win[:, d:2 * d]


def c_pre(proj, b, s, k_norm, cmp_pos, phi_w1, phi_w2):
    d = C_HEAD_DIM
    kv0 = OFF["c_kv"] // 128
    pos = jnp.concatenate([cmp_pos[0], cmp_pos[1]], axis=-1)
    w1 = phi_w1.reshape(2, CMP_BLOCK, d, PHI_HIDDEN)
    zero = jnp.zeros((CMP_BLOCK, d, PHI_HIDDEN), jnp.float32)
    w1c = jnp.concatenate([jnp.concatenate([w1[0], zero], axis=-1),
                           jnp.concatenate([zero, w1[1]], axis=-1)], axis=1).astype(jnp.bfloat16)
    z2 = jnp.zeros((PHI_HIDDEN, d), jnp.float32)
    w2c = jnp.concatenate([jnp.concatenate([phi_w2[0], z2], axis=-1),
                           jnp.concatenate([z2, phi_w2[1]], axis=-1)], axis=0).astype(jnp.bfloat16)
    full = lambda a: pl.BlockSpec(a.shape, lambda bb: (0,) * a.ndim)
    return pl.pallas_call(
        _c_pre_kernel,
        grid=(b,),
        in_specs=[
            pl.BlockSpec((s, 128), lambda bb: (bb, kv0)),
            pl.BlockSpec((s, 128), lambda bb: (bb, kv0 + 1)),
            pl.BlockSpec((s, 128), lambda bb: (bb, kv0 + 2)),
            full(k_norm), full(pos), full(w1c), full(w2c),
        ],
        out_specs=[
            pl.BlockSpec((1, N_CMP_PAD, 128), lambda bb: (bb, 0, 0)),
            pl.BlockSpec((s, d), lambda bb: (bb, 0)),
            pl.BlockSpec((1, d, s), lambda bb: (bb, 0, 0)),
            pl.BlockSpec((1, s + WINDOW, d), lambda bb: (bb, 0, 0)),
            pl.BlockSpec((1, s + WINDOW, d), lambda bb: (bb, 0, 0)),
        ],
        out_shape=[
            jax.ShapeDtypeStruct((b, N_CMP_PAD, 128), jnp.float32),
            jax.ShapeDtypeStruct((b * s, d), jnp.float32),
            jax.ShapeDtypeStruct((b, d, s), jnp.float32),
            jax.ShapeDtypeStruct((b, s + WINDOW, d), jnp.float32),
            jax.ShapeDtypeStruct((b, s + WINDOW, d), jnp.float32),
        ],
        compiler_params=pltpu.CompilerParams(dimension_semantics=("arbitrary",), vmem_limit_bytes=VMEM_LIMIT),
        name="c_pre",
    )(proj, proj, proj, k_norm, pos, w1c, w2c)


def _c_main_kernel(cq_ref, z_ref, misct_ref, gq_ref, kvc_ref, ksel_ref, vselt_ref, kwin_ref, vwin_ref,
                   cmptab_ref, near_ref, farrow_ref, wintab_ref, ovt_ref, exp_ref, o_ref, ls_scr):
    s = ksel_ref.shape[0]
    d = C_HEAD_DIM
    hq = C_HEADS * QBLOCK
    i = pl.program_id(1)
    t0 = pl.multiple_of(i * QBLOCK, QBLOCK)
    f32, bf16 = jnp.float32, jnp.bfloat16
    nt = (((1,), (1,)), ((), ()))
    tn = (((0,), (0,)), ((), ()))

    cq = cq_ref[...]
    qs = jnp.concatenate([cq[:, h * d:(h + 1) * d] for h in range(C_HEADS)], axis=0)
    qs = qs * lax.rsqrt(jnp.mean(qs * qs, axis=-1, keepdims=True) + RMS_EPS) * gq_ref[...] * (d ** -0.5)
    qs = qs.astype(bf16)

    def softmax_rows(l, valid):
        l = jnp.where(valid, l, NEG_INF)
        m = _col_reduce(l, jnp.max)
        p = jnp.where(valid, jnp.exp(l - m), 0.0)
        den = _col_reduce(p, jnp.sum)
        return p, den

    kvc = kvc_ref[0]
    lc = lax.dot_general(kvc[:, 0:d].astype(bf16), qs, nt, preferred_element_type=f32) + cmptab_ref[0]
    n_idx = lax.broadcasted_iota(jnp.int32, (N_CMP_PAD, hq), 0)
    t_c = t0 + (lax.broadcasted_iota(jnp.int32, (N_CMP_PAD, hq), 1) & (QBLOCK - 1))
    cmp_valid = n_idx * CMP_STRIDE + (CMP_BLOCK - 1) <= t_c
    pc, den_c = softmax_rows(lc, cmp_valid)
    pc = pc * jnp.where(den_c > 0.0, 1.0 / den_c, 0.0)
    o_cmp = lax.dot_general(kvc[:, d:2 * d].astype(bf16), pc.astype(bf16), tn, preferred_element_type=f32)

    psum = pc[:, 0:QBLOCK]
    for h in range(1, C_HEADS):
        psum = psum + pc[:, h * QBLOCK:(h + 1) * QBLOCK]
    p_hi = psum.astype(bf16)
    p_lo = (psum - p_hi.astype(f32)).astype(bf16)
    ovt = ovt_ref[...]
    imp = jnp.dot(ovt, p_hi, preferred_element_type=f32) + jnp.dot(ovt, p_lo, preferred_element_type=f32)
    j_idx = lax.broadcasted_iota(jnp.int32, (N_SB, QBLOCK), 0)
    t_b = t0 + lax.broadcasted_iota(jnp.int32, (N_SB, QBLOCK), 1)
    cur = lax.shift_right_arithmetic(t_b, 6)
    forced = (j_idx == 0) | (j_idx == cur) | (j_idx == jnp.maximum(cur - 1, 0))
    imp = jnp.where(j_idx * SEL_BLOCK <= t_b, jnp.where(forced, FORCE_SCORE, imp), NEG_INF)
    rank = jnp.zeros((N_SB, QBLOCK), f32)
    for r in range(N_SB):
        row = imp[r:r + 1, :]
        rank = rank + jnp.where((row > imp) | ((row == imp) & (j_idx > r)), 1.0, 0.0)
    picked = jnp.where(rank < float(N_SEL), 1.0, 0.0).astype(bf16)

    kw = kwin_ref[0, pl.ds(t0, WIN_KEYS), :]
    vw = vwin_ref[0, pl.ds(t0, WIN_KEYS), :]
    lw = lax.dot_general(kw.astype(bf16), qs, nt, preferred_element_type=f32) + wintab_ref[...]
    r_idx = lax.broadcasted_iota(jnp.int32, (WIN_KEYS, hq), 0)
    dist = (lax.broadcasted_iota(jnp.int32, (WIN_KEYS, hq), 1) & (QBLOCK - 1)) + WINDOW - r_idx
    win_valid = (dist >= 0) & (dist < WINDOW) & (r_idx + t0 >= WINDOW)
    pw, den_w = softmax_rows(lw, win_valid)
    o_win = lax.dot_general(vw.astype(bf16), pw.astype(bf16), tn, preferred_element_type=f32) / den_w

    g = _sigmoid(misct_ref[0, C_G_LANE:C_G_LANE + 3 * C_HEADS, :])

    def run(nrows):
        kpos = lax.broadcasted_iota(jnp.int32, (nrows, QBLOCK), 0)
        tpos = t0 + lax.broadcasted_iota(jnp.int32, (nrows, QBLOCK), 1)
        sel_valid = (jnp.dot(exp_ref[0:nrows, :], picked, preferred_element_type=f32) > 0.5) & (kpos <= tpos)
        ls_scr[0:QBLOCK, :] = jnp.zeros((QBLOCK, hq), f32)
        ls_scr[QBLOCK:QBLOCK + nrows, :] = lax.dot_general(ksel_ref[0:nrows, :].astype(bf16), qs, nt,
                                                           preferred_element_type=f32)
        ls_scr[pl.ds(t0, 2 * QBLOCK), :] += near_ref[...] - farrow_ref[...]
        outs = []
        for h in range(C_HEADS):
            cols = slice(h * QBLOCK, (h + 1) * QBLOCK)
            l = jnp.where(sel_valid, ls_scr[QBLOCK:QBLOCK + nrows, cols], NEG_INF)
            m = _col_reduce(l, jnp.max)
            p = jnp.exp(l - m)
            den = _col_reduce(p, jnp.sum)
            o_sel = jnp.dot(vselt_ref[0, :, 0:nrows].astype(bf16), p.astype(bf16), preferred_element_type=f32) / den
            outs.append(g[3 * h:3 * h + 1, :] * o_cmp[:, cols] + g[3 * h + 1:3 * h + 2, :] * o_sel
                        + g[3 * h + 2:3 * h + 3, :] * o_win[:, cols])
        o_ref[...] = jnp.concatenate(outs, axis=0).T * _silu(z_ref[...])

    _run_causal_variant(run, i, s)


C_QPS = 2


def _c_pair_kernel(cq_ref, z_ref, misct_ref, gq_ref, kvc_ref, ksel_ref, vselt_ref, kwin_ref, vwin_ref,
                   cmptab_ref, near_ref, farrow_ref, wintab_ref, ovt_ref, exp_ref, o_ref, ls_scr):
    s = ksel_ref.shape[0]
    d = C_HEAD_DIM
    hq = C_HEADS * QBLOCK
    f32, bf16 = jnp.float32, jnp.bfloat16
    nt = (((1,), (1,)), ((), ()))
    tn = (((0,), (0,)), ((), ()))

    def softmax_rows(l, valid):
        l = jnp.where(valid, l, NEG_INF)
        m = _col_reduce(l, jnp.max)
        p = jnp.where(valid, jnp.exp(l - m), 0.0)
        den = _col_reduce(p, jnp.sum)
        return p, den

    def block(g, t0):
        nrows = t0 + QBLOCK
        qrows = slice(g * QBLOCK, (g + 1) * QBLOCK)

        cq = cq_ref[qrows, :]
        qs = jnp.concatenate([cq[:, h * d:(h + 1) * d] for h in range(C_HEADS)], axis=0)
        qs = qs * lax.rsqrt(jnp.mean(qs * qs, axis=-1, keepdims=True) + RMS_EPS) * gq_ref[...] * (d ** -0.5)
        qs = qs.astype(bf16)

        kvc = kvc_ref[0]
        lc = lax.dot_general(kvc[:, 0:d].astype(bf16), qs, nt, preferred_element_type=f32) + cmptab_ref[g]
        n_idx = lax.broadcasted_iota(jnp.int32, (N_CMP_PAD, hq), 0)
        t_c = t0 + (lax.broadcasted_iota(jnp.int32, (N_CMP_PAD, hq), 1) & (QBLOCK - 1))
        cmp_valid = n_idx * CMP_STRIDE + (CMP_BLOCK - 1) <= t_c
        pc, den_c = softmax_rows(lc, cmp_valid)
        pc = pc * jnp.where(den_c > 0.0, 1.0 / den_c, 0.0)
        o_cmp = lax.dot_general(kvc[:, d:2 * d].astype(bf16), pc.astype(bf16), tn, preferred_element_type=f32)

        psum = pc[:, 0:QBLOCK]
        for h in range(1, C_HEADS):
            psum = psum + pc[:, h * QBLOCK:(h + 1) * QBLOCK]
        p_hi = psum.astype(bf16)
        p_lo = (psum - p_hi.astype(f32)).astype(bf16)
        ovt = ovt_ref[...]
        imp = jnp.dot(ovt, p_hi, preferred_element_type=f32) + jnp.dot(ovt, p_lo, preferred_element_type=f32)
        j_idx = lax.broadcasted_iota(jnp.int32, (N_SB, QBLOCK), 0)
        t_b = t0 + lax.broadcasted_iota(jnp.int32, (N_SB, QBLOCK), 1)
        cur = lax.shift_right_arithmetic(t_b, 6)
        forced = (j_idx == 0) | (j_idx == cur) | (j_idx == jnp.maximum(cur - 1, 0))
        imp = jnp.where(j_idx * SEL_BLOCK <= t_b, jnp.where(forced, FORCE_SCORE, imp), NEG_INF)
        rank = jnp.zeros((N_SB, QBLOCK), f32)
        for r in range(N_SB):
            row = imp[r:r + 1, :]
            rank = rank + jnp.where((row > imp) | ((row == imp) & (j_idx > r)), 1.0, 0.0)
        picked = jnp.where(rank < float(N_SEL), 1.0, 0.0).astype(bf16)

        kw = kwin_ref[0, t0:t0 + WIN_KEYS, :]
        vw = vwin_ref[0, t0:t0 + WIN_KEYS, :]
        lw = lax.dot_general(kw.astype(bf16), qs, nt, preferred_element_type=f32) + wintab_ref[...]
        r_idx = lax.broadcasted_iota(jnp.int32, (WIN_KEYS, hq), 0)
        dist = (lax.broadcasted_iota(jnp.int32, (WIN_KEYS, hq), 1) & (QBLOCK - 1)) + WINDOW - r_idx
        win_valid = (dist >= 0) & (dist < WINDOW) & (r_idx + t0 >= WINDOW)
        pw, den_w = softmax_rows(lw, win_valid)
        o_win = lax.dot_general(vw.astype(bf16), pw.astype(bf16), tn, preferred_element_type=f32) / den_w

        gate = _sigmoid(misct_ref[0, C_G_LANE:C_G_LANE + 3 * C_HEADS, qrows])

        kpos = lax.broadcasted_iota(jnp.int32, (nrows, QBLOCK), 0)
        tpos = t0 + lax.broadcasted_iota(jnp.int32, (nrows, QBLOCK), 1)
        sel_valid = (jnp.dot(exp_ref[0:nrows, :], picked, preferred_element_type=f32) > 0.5) & (kpos <= tpos)
        ls_scr[0:QBLOCK, :] = jnp.zeros((QBLOCK, hq), f32)
        ls_scr[QBLOCK:QBLOCK + nrows, :] = lax.dot_general(ksel_ref[0:nrows, :].astype(bf16), qs, nt,
                                                           preferred_element_type=f32)
        ls_scr[t0:t0 + 2 * QBLOCK, :] += near_ref[...] - farrow_ref[...]
        outs = []
        for h in range(C_HEADS):
            cols = slice(h * QBLOCK, (h + 1) * QBLOCK)
            l = jnp.where(sel_valid, ls_scr[QBLOCK:QBLOCK + nrows, cols], NEG_INF)
            m = _col_reduce(l, jnp.max)
            p = jnp.exp(l - m)
            den = _col_reduce(p, jnp.sum)
            o_sel = jnp.dot(vselt_ref[0, :, 0:nrows].astype(bf16), p.astype(bf16), preferred_element_type=f32) / den
            outs.append(gate[3 * h:3 * h + 1, :] * o_cmp[:, cols] + gate[3 * h + 1:3 * h + 2, :] * o_sel
                        + gate[3 * h + 2:3 * h + 3, :] * o_win[:, cols])
        o_ref[qrows, :] = jnp.concatenate(outs, axis=0).T * _silu(z_ref[qrows, :])

    def run(step):
        for g in range(C_QPS):
            block(g, (C_QPS * step + g) * QBLOCK)

    step = pl.program_id(1)
    for v in range(s // (C_QPS * QBLOCK)):
        pl.when(step == v)(functools.partial(run, v))


def c_main(proj, misct, gq, kvc, ksel, vselt, kwin, vwin, cmptab, near, farrow, wintab, b, s):
    n = b * s
    tq = C_QPS * QBLOCK
    nq = s // tq
    d = C_HEAD_DIM
    hq = C_HEADS * QBLOCK
    row = lambda bb, i: bb * nq + i
    ovt = jnp.asarray(_OVERLAP_T, jnp.bfloat16)
    expand = jnp.asarray(_SB_EXPAND, jnp.bfloat16)
    const = lambda a: pl.BlockSpec(a.shape, lambda bb, i: (0,) * a.ndim)
    return pl.pallas_call(
        _c_pair_kernel,
        grid=(b, nq),
        in_specs=[
            pl.BlockSpec((tq, C_WIDTH), lambda bb, i: (row(bb, i), OFF["c_q"] // C_WIDTH)),
            pl.BlockSpec((tq, C_WIDTH), lambda bb, i: (row(bb, i), OFF["c_z"] // C_WIDTH)),
            pl.BlockSpec((1, 128, tq), lambda bb, i: (bb, 0, i)),
            const(gq),
            pl.BlockSpec((1, N_CMP_PAD, 128), lambda bb, i: (bb, 0, 0)),
            pl.BlockSpec((s, d), lambda bb, i: (bb, 0)),
            pl.BlockSpec((1, d, s), lambda bb, i: (bb, 0, 0)),
            pl.BlockSpec((1, s + WINDOW, d), lambda bb, i: (bb, 0, 0)),
            pl.BlockSpec((1, s + WINDOW, d), lambda bb, i: (bb, 0, 0)),
            pl.BlockSpec((C_QPS, N_CMP_PAD, hq), lambda bb, i: (i, 0, 0)),
            const(near), const(farrow), const(wintab), const(ovt), const(expand),
        ],
        out_specs=pl.BlockSpec((tq, C_WIDTH), lambda bb, i: (row(bb, i), 0)),
        out_shape=jax.ShapeDtypeStruct((n, C_WIDTH), jnp.float32),
        scratch_shapes=[pltpu.VMEM((s + QBLOCK, hq), jnp.float32)],
        compiler_params=pltpu.CompilerParams(
            dimension_semantics=("arbitrary", "arbitrary"), vmem_limit_bytes=VMEM_LIMIT),
        name="c_main",
    )(proj, proj, misct, gq, kvc, ksel, vselt, kwin, vwin, cmptab, near, farrow, wintab, ovt, expand)


def _head_cols(tab):
    return jnp.moveaxis(tab, -1, -2).reshape(*tab.shape[:-2], tab.shape[-1] * tab.shape[-2])


def mixer_c(proj, misct, b, s, gq, k_norm, cmp_pos, phi_w1, phi_w2, bias_c):
    kvc, ksel, vselt, kwin, vwin = c_pre(proj, b, s, k_norm, cmp_pos, phi_w1, phi_w2)
    cmptab = _head_cols(_bias_lookup(bias_c, _CMP_BUCKET_T))
    near = _head_cols(_bias_lookup(bias_c, _NEAR_BUCKET_T))
    wintab = _head_cols(_bias_lookup(bias_c, _WIN_BUCKET_T))
    farrow = jnp.repeat(bias_c[_FAR_BUCKET], QBLOCK).reshape(1, C_HEADS * QBLOCK)
    return c_main(proj, misct, gq.reshape(1, C_HEAD_DIM), kvc, ksel, vselt, kwin, vwin, cmptab, near, farrow,
                  wintab, b, s)


B_HPS = 2
B_GROUP = 4


def _b_kernel(alog_ref, dtb_ref, q_ref, k_ref, v_ref, z_ref, misc_ref, arow_ref, cwq_ref, cwk_ref, cwv_ref,
              gon_ref, o_ref, qs, ks, vs, ws, at, gtok, btok, gcrow):
    s = q_ref.shape[0]
    dh = B_HEAD_DIM
    c_len = GDN_CHUNK
    j = pl.program_id(1)
    f32, bf16 = jnp.float32, jnp.bfloat16
    hi = lax.Precision.HIGHEST
    nt = (((1,), (1,)), ((), ()))
    tn = (((0,), (0,)), ((), ()))

    head_rows = lax.broadcasted_iota(jnp.int32, (8, B_HPS * dh), 0)

    def conv_silu(x_ref, w_ref):
        x = x_ref[...]
        acc = x * w_ref[CONV_WIDTH - 1:CONV_WIDTH, :]
        head = acc[0:8]
        for k in range(1, CONV_WIDTH):
            wk = w_ref[CONV_WIDTH - 1 - k:CONV_WIDTH - k, :]
            rolled = pltpu.roll(x, k, axis=0)
            acc = acc + rolled * wk
            head = head + jnp.where(head_rows >= k, rolled[0:8], 0.0) * wk
        return _silu(jnp.concatenate([head, acc[8:]], axis=0))

    def l2n(x):
        return x * lax.rsqrt(jnp.sum(x * x, axis=-1, keepdims=True) + RMS_EPS)

    qc = conv_silu(q_ref, cwq_ref)
    kc = conv_silu(k_ref, cwk_ref)
    vs[...] = conv_silu(v_ref, cwv_ref)
    misc = misc_ref[...]
    lane = lax.broadcasted_iota(jnp.int32, misc.shape, 1)
    ri = lax.broadcasted_iota(jnp.int32, (c_len, c_len), 0)
    ci = lax.broadcasted_iota(jnp.int32, (c_len, c_len), 1)
    lower = ci <= ri
    strict = ci < ri
    tri_u = jnp.where(ri <= ci, 1.0, 0.0)
    row_in_chunk = lax.broadcasted_iota(jnp.int32, (s, dh), 0) & (c_len - 1)
    for hh in range(B_HPS):
        sl = slice(hh * dh, (hh + 1) * dh)
        qs[:, sl] = l2n(qc[:, sl]) * (dh ** -0.5)
        ks[:, sl] = l2n(kc[:, sl])
        h = B_HPS * j + hh
        neg_a = -jnp.exp(alog_ref[h])
        dtb = dtb_ref[h]
        a_tok = jnp.sum(jnp.where(lane == B_A_LANE + h, misc, 0.0), axis=-1, keepdims=True)
        b_tok = jnp.sum(jnp.where(lane == B_B_LANE + h, misc, 0.0), axis=-1, keepdims=True)
        g = jnp.broadcast_to(neg_a * jax.nn.softplus(a_tok + dtb), (s, dh))
        for sh in (1, 2, 4, 8, 16, 32):
            g = g + jnp.where(row_in_chunk >= sh, pltpu.roll(g, sh, axis=0), 0.0)
        gtok[hh] = g
        btok[hh] = jnp.broadcast_to(_sigmoid(b_tok), (s, dh))
        g_row = neg_a * jax.nn.softplus(arow_ref[0, hh] + dtb)
        gcrow[hh] = jnp.dot(g_row, tri_u, precision=hi, preferred_element_type=f32)

    lane3 = lax.broadcasted_iota(jnp.int32, (c_len, 3 * c_len), 1)
    lo_lanes = (lane3 >= c_len) & (lane3 < 2 * c_len)

    def split_lhs(p):
        p4 = jnp.concatenate([p, p, p], axis=1)
        hi4 = p4.astype(bf16).astype(f32)
        return jnp.where(lo_lanes, p4 - hi4, hi4).astype(bf16)

    def split_rhs(x):
        xh = x.astype(bf16)
        xl = (x - xh.astype(f32)).astype(bf16)
        return jnp.concatenate([xh, xh, xl], axis=0)

    def mm(p_split, x):
        return jnp.dot(p_split, split_rhs(x), preferred_element_type=f32)

    def intra_pair(i, carry):
        where, q_c, k_c, v_c, gb, beta, g_row = [], [], [], [], [], [], []
        for c in [B_GROUP * i + cc for cc in range(B_GROUP)]:
            rs = pl.ds(pl.multiple_of(c * c_len, c_len), c_len)
            for hh in range(B_HPS):
                sl = slice(hh * dh, (hh + 1) * dh)
                where.append((rs, hh, sl))
                q_c.append(qs[rs, sl])
                k_c.append(ks[rs, sl])
                v_c.append(vs[rs, sl])
                gb.append(gtok[hh, rs, :])
                beta.append(btok[hh, rs, :])
                g_row.append(gcrow[hh, pl.ds(c, 1), :])
        nch = range(len(where))
        decay = [jnp.where(lower, jnp.exp(jnp.where(lower, gb[n][:, 0:c_len] - g_row[n], 0.0)), 0.0) for n in nch]
        kb = [k_c[n] * beta[n] for n in nch]
        kk = [lax.dot_general(kb[n].astype(bf16), k_c[n].astype(bf16), nt, preferred_element_type=f32) for n in nch]
        p = [-jnp.where(strict, kk[n] * decay[n], 0.0) for n in nch]
        eg = [jnp.exp(gb[n]) for n in nch]
        x = [jnp.concatenate([v_c[n] * beta[n], kb[n] * eg[n]], axis=1) for n in nch]
        ps = [split_lhs(p[n]) for n in nch]
        x = [x[n] + mm(ps[n], x[n]) for n in nch]
        for _ in range(5):
            p = [mm(ps[n], p[n]) for n in nch]
            ps = [split_lhs(p[n]) for n in nch]
            x = [x[n] + mm(ps[n], x[n]) for n in nch]
        attn = [jnp.where(lower, lax.dot_general(q_c[n].astype(bf16), k_c[n].astype(bf16), nt,
                                                 preferred_element_type=f32) * decay[n], 0.0) for n in nch]
        for n, (rs, hh, sl) in enumerate(where):
            g_last = g_row[n][:, c_len - 1:c_len]
            vs[rs, sl] = x[n][:, 0:dh]
            ws[rs, sl] = x[n][:, dh:2 * dh]
            qs[rs, sl] = q_c[n] * eg[n]
            ks[rs, sl] = k_c[n] * jnp.exp(g_last - gb[n])
            at[rs, hh * c_len:(hh + 1) * c_len] = attn[n]
        return carry

    lax.fori_loop(0, s // (B_GROUP * c_len), intra_pair, 0)

    def recur(c, states):
        rs = pl.ds(pl.multiple_of(c * c_len, c_len), c_len)
        hs = range(B_HPS)
        sls = [slice(hh * dh, (hh + 1) * dh) for hh in hs]
        st_b = [states[hh].astype(bf16) for hh in hs]
        v_new = [vs[rs, sls[hh]] - jnp.dot(ws[rs, sls[hh]].astype(bf16), st_b[hh], preferred_element_type=f32)
                 for hh in hs]
        v_nb = [v_new[hh].astype(bf16) for hh in hs]
        o_st = [jnp.dot(qs[rs, sls[hh]].astype(bf16), st_b[hh], preferred_element_type=f32) for hh in hs]
        new_states = [states[hh] * jnp.exp(gcrow[hh, pl.ds(c, 1), c_len - 1:c_len])
                      + lax.dot_general(ks[rs, sls[hh]].astype(bf16), v_nb[hh], tn, preferred_element_type=f32)
                      for hh in hs]
        for hh in hs:
            o_ref[rs, sls[hh]] = o_st[hh] + jnp.dot(at[rs, hh * c_len:(hh + 1) * c_len].astype(bf16), v_nb[hh],
                                                    preferred_element_type=f32)
        return tuple(new_states)

    lax.fori_loop(0, s // c_len, recur, tuple(jnp.zeros((dh, dh), f32) for _ in range(B_HPS)))

    for hh in range(B_HPS):
        sl = slice(hh * dh, (hh + 1) * dh)
        o = o_ref[:, sl]
        o = o * lax.rsqrt(jnp.mean(o * o, axis=-1, keepdims=True) + RMS_EPS) * gon_ref[...]
        o_ref[:, sl] = o * _silu(z_ref[:, sl])


def mixer_b(proj, misct, b, s, conv_w, a_log, dt_bias, out_norm):
    n = b * s
    w2 = B_HPS * B_HEAD_DIM
    nc = s // GDN_CHUNK
    arow = misct[:, B_A_LANE:B_A_LANE + B_HEADS, :].reshape(b, B_HEADS, nc, GDN_CHUNK)
    qkv0 = OFF["b_qkv"] // w2
    kstep = B_WIDTH // w2
    smem = pl.BlockSpec(memory_space=pltpu.SMEM)
    return pl.pallas_call(
        _b_kernel,
        grid=(b, B_HEADS // B_HPS),
        in_specs=[
            smem, smem,
            pl.BlockSpec((s, w2), lambda bb, j: (bb, qkv0 + j)),
            pl.BlockSpec((s, w2), lambda bb, j: (bb, qkv0 + kstep + j)),
            pl.BlockSpec((s, w2), lambda bb, j: (bb, qkv0 + 2 * kstep + j)),
            pl.BlockSpec((s, w2), lambda bb, j: (bb, OFF["b_z"] // w2 + j)),
            pl.BlockSpec((s, 128), lambda bb, j: (bb, MISC_OFF // 128)),
            pl.BlockSpec((1, B_HPS, nc, GDN_CHUNK), lambda bb, j: (bb, j, 0, 0)),
            pl.BlockSpec((CONV_WIDTH, w2), lambda bb, j: (0, j)),
            pl.BlockSpec((CONV_WIDTH, w2), lambda bb, j: (0, kstep + j)),
            pl.BlockSpec((CONV_WIDTH, w2), lambda bb, j: (0, 2 * kstep + j)),
            pl.BlockSpec((1, B_HEAD_DIM), lambda bb, j: (0, 0)),
        ],
        out_specs=pl.BlockSpec((s, w2), lambda bb, j: (bb, j)),
        out_shape=jax.ShapeDtypeStruct((n, B_WIDTH), jnp.float32),
        scratch_shapes=[
            pltpu.VMEM((s, w2), jnp.float32), pltpu.VMEM((s, w2), jnp.float32), pltpu.VMEM((s, w2), jnp.float32),
            pltpu.VMEM((s, w2), jnp.float32), pltpu.VMEM((s, B_HPS * GDN_CHUNK), jnp.float32),
            pltpu.VMEM((B_HPS, s, B_HEAD_DIM), jnp.float32), pltpu.VMEM((B_HPS, s, B_HEAD_DIM), jnp.float32),
            pltpu.VMEM((B_HPS, nc, GDN_CHUNK), jnp.float32),
        ],
        compiler_params=pltpu.CompilerParams(
            dimension_semantics=("arbitrary", "arbitrary"), vmem_limit_bytes=VMEM_LIMIT),
        name="gdn",
    )(a_log, dt_bias, proj, proj, proj, proj, proj, arow, conv_w, conv_w, conv_w, out_norm.reshape(1, B_HEAD_DIM))


def _rms_norm(x, g):
    y = x * lax.rsqrt(jnp.mean(x * x, axis=-1, keepdims=True) + RMS_EPS)
    return y * g


def _l2norm(x):
    return x * lax.rsqrt(jnp.sum(x * x, axis=-1, keepdims=True) + RMS_EPS)


def _masked_softmax(logits, mask):
    logits = jnp.where(mask, logits, NEG_INF)
    return jnp.where(mask, jax.nn.softmax(logits, axis=-1), 0.0)


def _t5_bucket(dist):
    dist = jnp.maximum(dist, 0)
    log_ratio = jnp.log(jnp.maximum(dist, 1).astype(jnp.float32) / BUCKET_MAX_EXACT) / math.log(BUCKET_MAX_DIST / BUCKET_MAX_EXACT)
    large = BUCKET_MAX_EXACT + (log_ratio * (N_BUCKETS - BUCKET_MAX_EXACT)).astype(jnp.int32)
    large = jnp.minimum(large, N_BUCKETS - 1)
    return jnp.where(dist < BUCKET_MAX_EXACT, dist, large)


def _to_qblocks(a):
    b, s = a.shape[:2]
    return jnp.moveaxis(a.reshape(b, s // QBLOCK, QBLOCK, *a.shape[2:]), 1, 0)


def _from_qblocks(a):
    nb, b = a.shape[:2]
    return jnp.moveaxis(a, 0, 1).reshape(b, nb * QBLOCK, *a.shape[3:])


def _dsa_mixer(q, k, v, q_idx, k_idx, w_idx, bias_tab):
    b, s = q.shape[:2]
    topk = min(TOPK_MAX, s // 4)
    bidx = jnp.arange(b)[:, None, None]
    key_pos = jnp.arange(s)
    scale = A_HEAD_DIM ** -0.5

    def block(args):
        qb, qib, wb, start = args
        t = start + jnp.arange(QBLOCK)
        idx_logits = jnp.einsum('bqhd,bsd->bqhs', qib, k_idx) * (IDX_DIM ** -0.5)
        score = jnp.einsum('bqh,bqhs->bqs', wb * (IDX_HEADS ** -0.5), jax.nn.relu(idx_logits))
        score = jnp.where(key_pos[None, None, :] <= t[None, :, None], score, NEG_INF)
        _, sel = lax.top_k(score, topk)
        kg = k[bidx, sel]
        vg = v[bidx, sel]
        dist = t[None, :, None] - sel
        bias = jnp.transpose(bias_tab[_t5_bucket(dist)], (0, 3, 1, 2))
        logits = jnp.einsum('bqhd,bqkhd->bhqk', qb, kg) * scale + bias
        p = _masked_softmax(logits, (dist >= 0)[:, None])
        return jnp.einsum('bhqk,bqkhd->bqhd', p, vg)

    starts = jnp.arange(s // QBLOCK, dtype=jnp.int32) * QBLOCK
    out = lax.map(block, (_to_qblocks(q), _to_qblocks(q_idx), _to_qblocks(w_idx), starts))
    return _from_qblocks(out)


def _causal_depthwise_conv(x, w):
    c = x.shape[-1]
    return lax.conv_general_dilated(x, w[:, None, :], window_strides=(1,), padding=[(CONV_WIDTH - 1, 0)],
                                    dimension_numbers=('NWC', 'WIO', 'NWC'), feature_group_count=c)


def _gdn_mixer(qkv, a_in, b_in, conv_w, a_log, dt_bias):
    b, s = qkv.shape[:2]
    n = s // GDN_CHUNK
    qkv = jax.nn.silu(_causal_depthwise_conv(qkv, conv_w))
    q, k, v = jnp.split(qkv, 3, axis=-1)
    sh = lambda t: t.reshape(b, s, B_HEADS, B_HEAD_DIM)
    q = _l2norm(sh(q)) * (B_HEAD_DIM ** -0.5)
    k = _l2norm(sh(k))
    v = sh(v)
    beta = jax.nn.sigmoid(b_in)
    g = -jnp.exp(a_log) * jax.nn.softplus(a_in + dt_bias)

    def chunk(t):
        return jnp.moveaxis(t.reshape(b, n, GDN_CHUNK, *t.shape[2:]), 3, 2)

    q, k, v, beta, g = (chunk(t) for t in (q, k, v, beta, g))
    g_cum = jnp.cumsum(g, axis=-1)
    g_last = g_cum[..., -1]
    lower = jnp.tril(jnp.ones((GDN_CHUNK, GDN_CHUNK), dtype=bool))
    strict = jnp.tril(jnp.ones((GDN_CHUNK, GDN_CHUNK), dtype=bool), -1)
    diff = g_cum[..., :, None] - g_cum[..., None, :]
    decay = jnp.where(lower, jnp.exp(jnp.where(lower, diff, 0.0)), 0.0)
    k_beta = k * beta[..., None]
    a_mat = jnp.where(strict, jnp.einsum('bnhid,bnhjd->bnhij', k_beta, k) * decay, 0.0)
    eye = jnp.eye(GDN_CHUNK, dtype=jnp.float32)
    t_inv = lax.linalg.triangular_solve(eye + a_mat, jnp.broadcast_to(eye, a_mat.shape), left_side=True, lower=True)
    u = jnp.einsum('bnhij,bnhjd->bnhid', t_inv, v * beta[..., None])
    w = jnp.einsum('bnhij,bnhjd->bnhid', t_inv, k_beta * jnp.exp(g_cum)[..., None])
    attn = jnp.einsum('bnhid,bnhjd->bnhij', q, k) * decay
    q_dec = q * jnp.exp(g_cum)[..., None]
    k_dec = k * jnp.exp(g_last[..., None] - g_cum)[..., None]

    def step(state, xs):
        u_c, w_c, q_c, k_c, attn_c, gl_c = xs
        v_new = u_c - jnp.einsum('bhik,bhkv->bhiv', w_c, state)
        o = jnp.einsum('bhik,bhkv->bhiv', q_c, state) + jnp.einsum('bhij,bhjv->bhiv', attn_c, v_new)
        state = state * jnp.exp(gl_c)[..., None, None] + jnp.einsum('bhik,bhiv->bhkv', k_c, v_new)
        return state, o

    state0 = jnp.zeros((b, B_HEADS, B_HEAD_DIM, B_HEAD_DIM), jnp.float32)
    xs = tuple(jnp.moveaxis(t, 1, 0) for t in (u, w, q_dec, k_dec, attn, g_last))
    _, o = lax.scan(step, state0, xs)
    o = jnp.moveaxis(jnp.moveaxis(o, 0, 1), 2, 3)
    return o.reshape(b, s, B_HEADS, B_HEAD_DIM)


def _compress_blocks(tok, pos, w1, w2):
    b, s, d = tok.shape
    n_cmp = (s - CMP_BLOCK) // CMP_STRIDE + 1
    idx = jnp.arange(n_cmp)[:, None] * CMP_STRIDE + jnp.arange(CMP_BLOCK)[None, :]
    blocks = tok[:, idx] + pos
    return jax.nn.silu(blocks.reshape(b, n_cmp, CMP_BLOCK * d) @ w1) @ w2


def _nsa_mixer(q, kv, gates, k_norm, cmp_pos, phi_w1, phi_w2, bias_tab):
    b, s = q.shape[:2]
    scale = C_HEAD_DIM ** -0.5
    t_all = jnp.arange(s)
    k_cmp, v_cmp, k_sel, v_sel, k_win, v_win = (kv[:, :, i] for i in range(6))
    k_sel = _rms_norm(k_sel, k_norm[1])
    k_win = _rms_norm(k_win, k_norm[2])
    kc = _rms_norm(_compress_blocks(k_cmp, cmp_pos[0], phi_w1[0], phi_w2[0]), k_norm[0])
    vc = _compress_blocks(v_cmp, cmp_pos[1], phi_w1[1], phi_w2[1])
    n_cmp = kc.shape[1]
    cmp_start = jnp.arange(n_cmp) * CMP_STRIDE
    cmp_end = cmp_start + CMP_BLOCK - 1
    cmp_valid = cmp_end[None, :] <= t_all[:, None]
    cmp_bias = jnp.transpose(bias_tab[_t5_bucket(t_all[:, None] - cmp_end[None, :])], (2, 0, 1))
    logits = jnp.einsum('bqhd,bnd->bhqn', q, kc) * scale + cmp_bias
    p_cmp = _masked_softmax(logits, cmp_valid[None, None])
    o_cmp = jnp.einsum('bhqn,bnd->bqhd', p_cmp, vc)
    n_sb = s // SEL_BLOCK
    n_pick = min(N_SEL, n_sb)
    sb_start = jnp.arange(n_sb) * SEL_BLOCK
    overlap = ((cmp_start[:, None] < sb_start[None, :] + SEL_BLOCK) & (cmp_start[:, None] + CMP_BLOCK > sb_start[None, :])).astype(jnp.float32)
    importance = jnp.einsum('bhqn,nj->bqj', p_cmp, overlap)
    cur = t_all // SEL_BLOCK
    blk = jnp.arange(n_sb)
    forced = (blk[None, :] == 0) | (blk[None, :] == cur[:, None]) | (blk[None, :] == jnp.maximum(cur[:, None] - 1, 0))
    admissible = sb_start[None, :] <= t_all[:, None]
    importance = jnp.where(admissible[None], jnp.where(forced[None], FORCE_SCORE, importance), NEG_INF)
    _, sel = lax.top_k(importance, n_pick)
    k_blocks = k_sel.reshape(b, n_sb, SEL_BLOCK, C_HEAD_DIM)
    v_blocks = v_sel.reshape(b, n_sb, SEL_BLOCK, C_HEAD_DIM)
    k_win_pad = jnp.pad(k_win, ((0, 0), (WINDOW, 0), (0, 0)))
    v_win_pad = jnp.pad(v_win, ((0, 0), (WINDOW, 0), (0, 0)))
    bidx = jnp.arange(b)[:, None, None]
    n_keys = n_pick * SEL_BLOCK

    def block(args):
        qb, selb, start = args
        t = start + jnp.arange(QBLOCK)
        kg = k_blocks[bidx, selb].reshape(b, QBLOCK, n_keys, C_HEAD_DIM)
        vg = v_blocks[bidx, selb].reshape(b, QBLOCK, n_keys, C_HEAD_DIM)
        s_pos = (selb[..., None] * SEL_BLOCK + jnp.arange(SEL_BLOCK)).reshape(b, QBLOCK, n_keys)
        dist = t[None, :, None] - s_pos
        bias = jnp.transpose(bias_tab[_t5_bucket(dist)], (0, 3, 1, 2))
        lg = jnp.einsum('bqhd,bqkd->bhqk', qb, kg) * scale + bias
        p = _masked_softmax(lg, (dist >= 0)[:, None])
        o_sel = jnp.einsum('bhqk,bqkd->bqhd', p, vg)
        kw = lax.dynamic_slice_in_dim(k_win_pad, start, QBLOCK + WINDOW, axis=1)
        vw = lax.dynamic_slice_in_dim(v_win_pad, start, QBLOCK + WINDOW, axis=1)
        s_w = start - WINDOW + jnp.arange(QBLOCK + WINDOW)
        dist_w = t[:, None] - s_w[None, :]
        valid_w = (s_w[None, :] >= 0) & (dist_w >= 0) & (dist_w < WINDOW)
        bias_w = jnp.transpose(bias_tab[_t5_bucket(dist_w)], (2, 0, 1))
        lw = jnp.einsum('bqhd,bsd->bhqs', qb, kw) * scale + bias_w
        pw = _masked_softmax(lw, valid_w[None, None])
        o_win = jnp.einsum('bhqs,bsd->bqhd', pw, vw)
        return o_sel, o_win

    starts = jnp.arange(s // QBLOCK, dtype=jnp.int32) * QBLOCK
    o_sel, o_win = lax.map(block, (_to_qblocks(q), _to_qblocks(sel), starts))
    o_sel = _from_qblocks(o_sel)
    o_win = _from_qblocks(o_win)
    g = jax.nn.sigmoid(gates).reshape(b, s, C_HEADS, 3)
    return g[..., 0:1] * o_cmp + g[..., 1:2] * o_sel + g[..., 2:3] * o_win


def _field(proj3, name):
    return proj3[..., OFF[name]:OFF[name] + _ORIG[name][1]]


def kernel(x, norm_g, w_in, a_kv_norm, a_w_ukv, a_q_norm, a_k_norm, b_conv, b_a_log, b_dt_bias, b_out_norm, c_q_norm, c_k_norm, c_cmp_pos, c_phi_w1, c_phi_w2, w_branch, w_out, rel_bias):
    b, s, _ = x.shape
    n = b * s
    bias_a = rel_bias[:, :A_HEADS]
    bias_c = rel_bias[:, A_HEADS:]
    xf = x.reshape(n, D_MODEL)
    fields = [w_in[:, :, _ORIG[name][0]:_ORIG[name][0] + _ORIG[name][1]] for name in _NEW_ORDER]
    pad = jnp.zeros((DEPTH, D_MODEL, N_PAD - N_IN), w_in.dtype)
    w_in_p = jnp.concatenate(fields + [pad], axis=-1).astype(jnp.bfloat16)
    for l in range(DEPTH):
        proj = in_proj(xf, norm_g[l], w_in_p[l])
        p3 = proj.reshape(b, s, N_PAD)

        misct = jnp.transpose(p3[..., MISC_OFF:], (0, 2, 1))
        y_a = mixer_a(proj, misct, b, s, a_kv_norm[l], a_w_ukv[l], a_q_norm[l], a_k_norm[l], bias_a)

        y_b = mixer_b(proj, misct, b, s, b_conv[l], b_a_log[l], b_dt_bias[l], b_out_norm[l])
        y_c = mixer_c(proj, misct, b, s, c_q_norm[l], c_k_norm[l], c_cmp_pos[l], c_phi_w1[l], c_phi_w2[l], bias_c)

        wbr = w_branch[l].astype(jnp.bfloat16)
        xf = merge(xf, proj, y_a, y_b, y_c, wbr[:A_WIDTH], wbr[A_WIDTH:A_WIDTH + B_WIDTH],
                   wbr[A_WIDTH + B_WIDTH:], w_out[l].astype(jnp.bfloat16))
    return xf.reshape(b, s, D_MODEL)
```

```python
import functools
import math

import jax
import jax.numpy as jnp
import numpy as np
from jax import lax
from jax.experimental import pallas as pl
from jax.experimental.pallas import tpu as pltpu

D_MODEL = 1024
DEPTH = 4
QBLOCK = 128
NEG_INF = -1e30
FORCE_SCORE = 1e9
RMS_EPS = 1e-6

A_HEADS = 4
A_HEAD_DIM = 64
A_WIDTH = A_HEADS * A_HEAD_DIM
A_KV_LATENT = 128
IDX_HEADS = 8
IDX_DIM = 64
TOPK_MAX = 256

B_HEADS = 4
B_HEAD_DIM = 128
B_WIDTH = B_HEADS * B_HEAD_DIM
CONV_WIDTH = 4
GDN_CHUNK = 64

C_HEADS = 4
C_HEAD_DIM = 64
C_WIDTH = C_HEADS * C_HEAD_DIM
CMP_BLOCK = 32
CMP_STRIDE = 16
SEL_BLOCK = 64
N_SEL = 16
WINDOW = 512
PHI_HIDDEN = 256

N_BUCKETS = 32
BUCKET_MAX_EXACT = 16
BUCKET_MAX_DIST = 128

VMEM_LIMIT = 48 * 1024 * 1024

_ORIG = {}
_o = 0
for _name, _w in (("a_q", 256), ("a_ckv", 128), ("a_qi", 512), ("a_ki", 64), ("a_wi", 8), ("a_z", 256),
                  ("b_qkv", 1536), ("b_a", 4), ("b_b", 4), ("b_z", 512),
                  ("c_q", 256), ("c_kv", 384), ("c_g", 12), ("c_z", 256), ("g", 3072)):
    _ORIG[_name] = (_o, _w)
    _o += _w
N_IN = _o

_NEW_ORDER = ("g", "b_qkv", "b_z", "a_qi", "a_q", "a_z", "c_q", "c_z", "c_kv", "a_ckv",
              "a_ki", "a_wi", "b_a", "b_b", "c_g")
OFF = {}
_o = 0
for _name in _NEW_ORDER:
    OFF[_name] = _o
    _o += _ORIG[_name][1]
MISC_OFF = OFF["a_ki"]
N_PAD = 7296
assert MISC_OFF == 7168 and _o <= N_PAD
A_WI_LANE = OFF["a_wi"] - MISC_OFF
B_A_LANE = OFF["b_a"] - MISC_OFF
B_B_LANE = OFF["b_b"] - MISC_OFF
C_G_LANE = OFF["c_g"] - MISC_OFF


def _perm_indices():
    idx = np.zeros((N_PAD,), np.int32)
    valid = np.zeros((N_PAD,), bool)
    for name in _NEW_ORDER:
        o_old, w = _ORIG[name]
        idx[OFF[name]:OFF[name] + w] = np.arange(o_old, o_old + w)
        valid[OFF[name]:OFF[name] + w] = True
    return idx, valid


_PERM_IDX, _PERM_VALID = _perm_indices()


IN_TM = 512
IN_TN = N_PAD // 3


def _in_proj_kernel(x_ref, g_ref, w_ref, o_ref):
    x = x_ref[...]
    ms = jnp.mean(x * x, axis=-1, keepdims=True)
    h = (x * lax.rsqrt(ms + RMS_EPS)) * g_ref[...]
    o_ref[...] = jnp.dot(h.astype(jnp.bfloat16), w_ref[...], preferred_element_type=jnp.float32)


def in_proj(xf, g, w_bf16):
    n = xf.shape[0]
    return pl.pallas_call(
        _in_proj_kernel,
        grid=(N_PAD // IN_TN, n // IN_TM),
        in_specs=[
            pl.BlockSpec((IN_TM, D_MODEL), lambda j, i: (i, 0)),
            pl.BlockSpec((1, D_MODEL), lambda j, i: (0, 0)),
            pl.BlockSpec((D_MODEL, IN_TN), lambda j, i: (0, j)),
        ],
        out_specs=pl.BlockSpec((IN_TM, IN_TN), lambda j, i: (i, j)),
        out_shape=jax.ShapeDtypeStruct((n, N_PAD), jnp.float32),
        compiler_params=pltpu.CompilerParams(
            dimension_semantics=("arbitrary", "arbitrary"), vmem_limit_bytes=VMEM_LIMIT),
        name="in_proj",
    )(xf, g.reshape(1, D_MODEL), w_bf16)


MG_TM = 256


def _merge_kernel(x_ref, g_ref, ya_ref, yb_ref, yc_ref, wa_ref, wb_ref, wc_ref, wo_ref, o_ref):
    def branch(y_ref, w_ref, k):
        p = jnp.dot(y_ref[...].astype(jnp.bfloat16), w_ref[...], preferred_element_type=jnp.float32)
        return _sigmoid(g_ref[:, k * D_MODEL:(k + 1) * D_MODEL]) * p

    merged = branch(ya_ref, wa_ref, 0) + branch(yb_ref, wb_ref, 1) + branch(yc_ref, wc_ref, 2)
    o_ref[...] = x_ref[...] + jnp.dot(merged.astype(jnp.bfloat16), wo_ref[...],
                                      preferred_element_type=jnp.float32)


def merge(xf, proj, ya, yb, yc, wa, wb, wc, wo):
    n = xf.shape[0]
    row = lambda w: pl.BlockSpec((MG_TM, w), lambda i: (i, 0))
    full = lambda a: pl.BlockSpec(a.shape, lambda i: (0, 0))
    return pl.pallas_call(
        _merge_kernel,
        grid=(n // MG_TM,),
        in_specs=[row(D_MODEL), row(3 * D_MODEL), row(A_WIDTH), row(B_WIDTH), row(C_WIDTH),
                  full(wa), full(wb), full(wc), full(wo)],
        out_specs=row(D_MODEL),
        out_shape=jax.ShapeDtypeStruct((n, D_MODEL), jnp.float32),
        compiler_params=pltpu.CompilerParams(
            dimension_semantics=("arbitrary",), vmem_limit_bytes=VMEM_LIMIT),
        name="merge",
    )(xf, proj, ya, yb, yc, wa, wb, wc, wo)


def _bucket_np(dist):
    d = np.maximum(np.asarray(dist, np.int64), 0)
    ratio = np.log(np.maximum(d, 1).astype(np.float64) / BUCKET_MAX_EXACT) / math.log(BUCKET_MAX_DIST / BUCKET_MAX_EXACT)
    scaled = ratio * (N_BUCKETS - BUCKET_MAX_EXACT)
    frac = scaled - np.floor(scaled)
    edge = (d > BUCKET_MAX_EXACT) & (d < BUCKET_MAX_DIST) & ((frac < 1e-4) | (frac > 1 - 1e-4))
    assert not edge.any(), "bucket boundary too close to an integer distance"
    large = np.minimum(BUCKET_MAX_EXACT + np.floor(scaled + 1e-9).astype(np.int64), N_BUCKETS - 1)
    return np.where(d < BUCKET_MAX_EXACT, d, large).astype(np.int32)


_NEAR_BUCKET_T = _bucket_np(np.arange(QBLOCK)[None, :] + QBLOCK - np.arange(2 * QBLOCK)[:, None])
_FAR_BUCKET = int(_bucket_np(np.array([BUCKET_MAX_DIST]))[0])
assert (_bucket_np(np.arange(BUCKET_MAX_DIST, 4096)) == _FAR_BUCKET).all()

INT_MIN = -2 ** 31


COL_SLAB = 64


def _col_reduce(x, op):
    rows = x.shape[0]
    if rows > COL_SLAB and rows % COL_SLAB == 0:
        x = op(x.reshape(rows // COL_SLAB, COL_SLAB, x.shape[1]), axis=0)
    return op(x, axis=0, keepdims=True)


ROW_VARIANTS = 8


def _run_causal_variant(run, i, s):
    per = (s // QBLOCK) // ROW_VARIANTS
    for v in range(ROW_VARIANTS):
        pl.when((i >= v * per) & (i < (v + 1) * per))(functools.partial(run, (v + 1) * per * QBLOCK))


def _sigmoid(x):
    return 0.5 * jnp.tanh(0.5 * x) + 0.5


def _silu(x):
    return x * _sigmoid(x)


def _bias_lookup(table, buckets):
    idx = jnp.asarray(buckets)[..., None]
    out = jnp.zeros(idx.shape[:-1] + (table.shape[1],), table.dtype)
    for bkt in range(N_BUCKETS):
        out = jnp.where(idx == bkt, table[bkt], out)
    return out


A_PRE_TM = 512


def _a_pre_kernel(ckv_ref, q_ref, gkv_ref, wukv_ref, gq_ref, gk_ref, hm_ref, kn_ref, vt_ref, qn_ref):
    c = ckv_ref[...]
    c = c * lax.rsqrt(jnp.mean(c * c, axis=-1, keepdims=True) + RMS_EPS) * gkv_ref[...]
    kv = jnp.dot(c.astype(jnp.bfloat16), wukv_ref[...], preferred_element_type=jnp.float32)
    hm = hm_ref[...]

    def head_rms(x, g):
        ms = jnp.dot(x * x, hm, precision=lax.Precision.HIGHEST, preferred_element_type=jnp.float32)
        return x * lax.rsqrt(ms + RMS_EPS) * g

    kn_ref[...] = head_rms(kv[:, :A_WIDTH], gk_ref[...])
    qn_ref[...] = head_rms(q_ref[...], gq_ref[...]) * (A_HEAD_DIM ** -0.5)
    vt_ref[0] = kv[:, A_WIDTH:].T


def a_pre(proj, b, s, gkv, wukv_bf16, gq, gk):
    n = b * s
    nt = s // A_PRE_TM
    hm = jnp.asarray(np.kron(np.eye(A_HEADS), np.ones((A_HEAD_DIM, A_HEAD_DIM))) / A_HEAD_DIM, jnp.float32)
    row = lambda bb, j: (bb * nt + j, 0)
    full = lambda a: pl.BlockSpec(a.shape, lambda bb, j: (0,) * a.ndim)
    gq = jnp.tile(gq, A_HEADS).reshape(1, A_WIDTH)
    gk = jnp.tile(gk, A_HEADS).reshape(1, A_WIDTH)
    gkv = gkv.reshape(1, A_KV_LATENT)
    return pl.pallas_call(
        _a_pre_kernel,
        grid=(b, nt),
        in_specs=[
            pl.BlockSpec((A_PRE_TM, A_KV_LATENT), lambda bb, j: (bb * nt + j, OFF["a_ckv"] // A_KV_LATENT)),
            pl.BlockSpec((A_PRE_TM, A_WIDTH), lambda bb, j: (bb * nt + j, OFF["a_q"] // A_WIDTH)),
            full(gkv), full(wukv_bf16), full(gq), full(gk), full(hm),
        ],
        out_specs=[
            pl.BlockSpec((A_PRE_TM, A_WIDTH), row),
            pl.BlockSpec((1, A_WIDTH, A_PRE_TM), lambda bb, j: (bb, 0, j)),
            pl.BlockSpec((A_PRE_TM, A_WIDTH), row),
        ],
        out_shape=[
            jax.ShapeDtypeStruct((n, A_WIDTH), jnp.float32),
            jax.ShapeDtypeStruct((b, A_WIDTH, s), jnp.float32),
            jax.ShapeDtypeStruct((n, A_WIDTH), jnp.float32),
        ],
        compiler_params=pltpu.CompilerParams(
            dimension_semantics=("arbitrary", "arbitrary"), vmem_limit_bytes=VMEM_LIMIT),
        name="a_pre",
    )(proj, proj, gkv, wukv_bf16, gq, gk, hm)


def _a_main_kernel(far_ref, qn_ref, qi_ref, z_ref, misct_ref, kn_ref, vt_ref, ki_ref, near_ref, o_ref,
                   key_scr, lg_scr, j_scr):
    s = kn_ref.shape[0]
    i = pl.program_id(1)
    t0 = pl.multiple_of(i * QBLOCK, QBLOCK)
    f32, bf16 = jnp.float32, jnp.bfloat16
    nt = (((1,), (1,)), ((), ()))
    kf = float(TOPK_MAX)

    def run(nrows):

        qi = qi_ref[...] * (IDX_DIM ** -0.5)
        qstack = jnp.concatenate([qi[:, h * IDX_DIM:(h + 1) * IDX_DIM] for h in range(IDX_HEADS)], axis=0)
        ki = ki_ref[0:nrows, 0:IDX_DIM]
        sc = lax.dot_general(ki.astype(bf16), qstack.astype(bf16), nt, preferred_element_type=f32)
        wt = misct_ref[0, A_WI_LANE:A_WI_LANE + IDX_HEADS, :] * (IDX_HEADS ** -0.5)
        score = jnp.zeros((nrows, QBLOCK), f32)
        for h in range(IDX_HEADS):
            score = score + wt[h:h + 1, :] * jnp.maximum(sc[:, h * QBLOCK:(h + 1) * QBLOCK], 0.0)
        score = score + 0.0
        kpos = lax.broadcasted_iota(jnp.int32, (nrows, QBLOCK), 0)
        tpos = t0 + lax.broadcasted_iota(jnp.int32, (nrows, QBLOCK), 1)
        causal = kpos <= tpos
        bits = pltpu.bitcast(score, jnp.int32)
        key = jnp.where(bits < 0, bits ^ jnp.int32(0x7FFFFFFF), bits)
        key_scr[0:nrows, :] = jnp.where(causal, key, jnp.int32(INT_MIN))

        def count_ge(cand):
            return _col_reduce(jnp.where(key_scr[0:nrows, :] >= cand, 1.0, 0.0), jnp.sum)

        def bisect(it, thr):
            cand = thr + lax.shift_left(jnp.int32(1), 31 - it)
            return jnp.where(count_ge(cand) >= kf, cand, thr)

        thr = lax.fori_loop(0, 32, bisect, jnp.full((1, QBLOCK), INT_MIN, jnp.int32))
        n_ge = count_ge(thr)
        need = kf - count_ge(thr + 1)

        j_scr[...] = jnp.full((1, QBLOCK), nrows - 1, jnp.int32)
        surplus = jnp.where((n_ge > kf) & (thr > INT_MIN), 1.0, 0.0)

        @pl.when(jnp.max(surplus) > 0.0)
        def _():
            def bisect_idx(it, lohi):
                lo, hi = lohi
                mid = lax.shift_right_arithmetic(lo + hi, 1)
                k = key_scr[0:nrows, :]
                kp = lax.broadcasted_iota(jnp.int32, (nrows, QBLOCK), 0)
                c = _col_reduce(jnp.where((k == thr) & (kp <= mid), 1.0, 0.0), jnp.sum)
                ok = c >= need
                return jnp.where(ok, lo, mid), jnp.where(ok, mid, hi)

            lo0 = jnp.full((1, QBLOCK), -1, jnp.int32)
            hi0 = jnp.full((1, QBLOCK), nrows - 1, jnp.int32)
            _, hi = lax.fori_loop(0, 11, bisect_idx, (lo0, hi0))
            j_scr[...] = hi

        key = key_scr[0:nrows, :]
        sel = ((key > thr) | ((key == thr) & (kpos <= j_scr[...]))) & causal

        qn = qn_ref[...]
        lane = lax.broadcasted_iota(jnp.int32, (QBLOCK, A_WIDTH), 1)
        qblk = jnp.concatenate(
            [jnp.where((lane >= h * A_HEAD_DIM) & (lane < (h + 1) * A_HEAD_DIM), qn, 0.0) for h in range(A_HEADS)],
            axis=0)
        lg_scr[0:QBLOCK, :] = jnp.zeros((QBLOCK, A_HEADS * QBLOCK), f32)
        lg_scr[QBLOCK:QBLOCK + nrows, :] = lax.dot_general(kn_ref[0:nrows, :].astype(bf16), qblk.astype(bf16), nt,
                                                           preferred_element_type=f32)
        outs = []
        for h in range(A_HEADS):
            cols = slice(h * QBLOCK, (h + 1) * QBLOCK)
            lg_scr[pl.ds(t0, 2 * QBLOCK), cols] += near_ref[h] - far_ref[h]
            l = jnp.where(sel, lg_scr[QBLOCK:QBLOCK + nrows, cols], NEG_INF)
            m = _col_reduce(l, jnp.max)
            p = jnp.exp(l - m)
            den = _col_reduce(p, jnp.sum)
            vt = vt_ref[0, h * A_HEAD_DIM:(h + 1) * A_HEAD_DIM, 0:nrows]
            o_t = jnp.dot(vt.astype(bf16), p.astype(bf16), preferred_element_type=f32)
            outs.append(o_t / den)
        o = jnp.concatenate(outs, axis=0).T
        o_ref[...] = o * _silu(z_ref[...])

    _run_causal_variant(run, i, s)


A_QPS = 2


def _a_pair_kernel(far_ref, qn_ref, qi_ref, z_ref, misct_ref, kn_ref, vt_ref, ki_ref, near_ref, o_ref,
                   key_scr, lg_scr, j_scr):
    s = kn_ref.shape[0]
    f32, bf16 = jnp.float32, jnp.bfloat16
    nt = (((1,), (1,)), ((), ()))
    kf = float(TOPK_MAX)
    groups = range(A_QPS)

    def run(step):
        t0 = [(A_QPS * step + g) * QBLOCK for g in groups]
        nr = [t + QBLOCK for t in t0]
        qrows = [slice(g * QBLOCK, (g + 1) * QBLOCK) for g in groups]

        for g in groups:
            qi = qi_ref[qrows[g], :] * (IDX_DIM ** -0.5)
            qstack = jnp.concatenate([qi[:, h * IDX_DIM:(h + 1) * IDX_DIM] for h in range(IDX_HEADS)], axis=0)
            ki = ki_ref[0:nr[g], 0:IDX_DIM]
            sc = lax.dot_general(ki.astype(bf16), qstack.astype(bf16), nt, preferred_element_type=f32)
            wt = misct_ref[0, A_WI_LANE:A_WI_LANE + IDX_HEADS, qrows[g]] * (IDX_HEADS ** -0.5)
            score = jnp.zeros((nr[g], QBLOCK), f32)
            for h in range(IDX_HEADS):
                score = score + wt[h:h + 1, :] * jnp.maximum(sc[:, h * QBLOCK:(h + 1) * QBLOCK], 0.0)
            score = score + 0.0
            kpos = lax.broadcasted_iota(jnp.int32, (nr[g], QBLOCK), 0)
            tpos = t0[g] + lax.broadcasted_iota(jnp.int32, (nr[g], QBLOCK), 1)
            bits = pltpu.bitcast(score, jnp.int32)
            key = jnp.where(bits < 0, bits ^ jnp.int32(0x7FFFFFFF), bits)
            key_scr[g, 0:nr[g], :] = jnp.where(kpos <= tpos, key, jnp.int32(INT_MIN))

        def count_ge(g, cand):
            return _col_reduce(jnp.where(key_scr[g, 0:nr[g], :] >= cand, 1.0, 0.0), jnp.sum)

        def bisect(it, thrs):
            bit = lax.shift_left(jnp.int32(1), 31 - it)
            cnt = [count_ge(g, thrs[g] + bit) for g in groups]
            return tuple(jnp.where(cnt[g] >= kf, thrs[g] + bit, thrs[g]) for g in groups)

        thrs = lax.fori_loop(0, 32, bisect, tuple(jnp.full((1, QBLOCK), INT_MIN, jnp.int32) for _ in groups))

        for g in groups:
            nrows, thr = nr[g], thrs[g]
            n_ge = count_ge(g, thr)
            need = kf - count_ge(g, thr + 1)
            kpos = lax.broadcasted_iota(jnp.int32, (nrows, QBLOCK), 0)

            j_scr[g] = jnp.full((1, QBLOCK), nrows - 1, jnp.int32)
            surplus = jnp.where((n_ge > kf) & (thr > INT_MIN), 1.0, 0.0)

            @pl.when(jnp.max(surplus) > 0.0)
            def _():
                def bisect_idx(it, lohi):
                    lo, hi = lohi
                    mid = lax.shift_right_arithmetic(lo + hi, 1)
                    k = key_scr[g, 0:nrows, :]
                    c = _col_reduce(jnp.where((k == thr) & (kpos <= mid), 1.0, 0.0), jnp.sum)
                    ok = c >= need
                    return jnp.where(ok, lo, mid), jnp.where(ok, mid, hi)

                lo0 = jnp.full((1, QBLOCK), -1, jnp.int32)
                hi0 = jnp.full((1, QBLOCK), nrows - 1, jnp.int32)
                _, hi = lax.fori_loop(0, 11, bisect_idx, (lo0, hi0))
                j_scr[g] = hi

            key = key_scr[g, 0:nrows, :]
            tpos = t0[g] + lax.broadcasted_iota(jnp.int32, (nrows, QBLOCK), 1)
            sel = ((key > thr) | ((key == thr) & (kpos <= j_scr[g]))) & (kpos <= tpos)

            qn = qn_ref[qrows[g], :]
            lane = lax.broadcasted_iota(jnp.int32, (QBLOCK, A_WIDTH), 1)
            qblk = jnp.concatenate(
                [jnp.where((lane >= h * A_HEAD_DIM) & (lane < (h + 1) * A_HEAD_DIM), qn, 0.0)
                 for h in range(A_HEADS)], axis=0)
            lg_scr[0:QBLOCK, :] = jnp.zeros((QBLOCK, A_HEADS * QBLOCK), f32)
            lg_scr[QBLOCK:QBLOCK + nrows, :] = lax.dot_general(kn_ref[0:nrows, :].astype(bf16), qblk.astype(bf16),
                                                               nt, preferred_element_type=f32)
            outs = []
            for h in range(A_HEADS):
                cols = slice(h * QBLOCK, (h + 1) * QBLOCK)
                lg_scr[t0[g]:t0[g] + 2 * QBLOCK, cols] += near_ref[h] - far_ref[h]
                l = jnp.where(sel, lg_scr[QBLOCK:QBLOCK + nrows, cols], NEG_INF)
                m = _col_reduce(l, jnp.max)
                p = jnp.exp(l - m)
                den = _col_reduce(p, jnp.sum)
                vt = vt_ref[0, h * A_HEAD_DIM:(h + 1) * A_HEAD_DIM, 0:nrows]
                o_t = jnp.dot(vt.astype(bf16), p.astype(bf16), preferred_element_type=f32)
                outs.append(o_t / den)
            o = jnp.concatenate(outs, axis=0).T
            o_ref[qrows[g], :] = o * _silu(z_ref[qrows[g], :])

    step = pl.program_id(1)
    for v in range(s // (A_QPS * QBLOCK)):
        pl.when(step == v)(functools.partial(run, v))


def a_main(proj, misct, qn, kn, vt, near_t, far, b, s):
    n = b * s
    tq = A_QPS * QBLOCK
    nq = s // tq
    row = lambda bb, i: bb * nq + i
    return pl.pallas_call(
        _a_pair_kernel,
        grid=(b, nq),
        in_specs=[
            pl.BlockSpec(memory_space=pltpu.SMEM),
            pl.BlockSpec((tq, A_WIDTH), lambda bb, i: (row(bb, i), 0)),
            pl.BlockSpec((tq, IDX_HEADS * IDX_DIM), lambda bb, i: (row(bb, i), OFF["a_qi"] // (IDX_HEADS * IDX_DIM))),
            pl.BlockSpec((tq, A_WIDTH), lambda bb, i: (row(bb, i), OFF["a_z"] // A_WIDTH)),
            pl.BlockSpec((1, 128, tq), lambda bb, i: (bb, 0, i)),
            pl.BlockSpec((s, A_WIDTH), lambda bb, i: (bb, 0)),
            pl.BlockSpec((1, A_WIDTH, s), lambda bb, i: (bb, 0, 0)),
            pl.BlockSpec((s, 128), lambda bb, i: (bb, MISC_OFF // 128)),
            pl.BlockSpec((A_HEADS, 2 * QBLOCK, QBLOCK), lambda bb, i: (0, 0, 0)),
        ],
        out_specs=pl.BlockSpec((tq, A_WIDTH), lambda bb, i: (row(bb, i), 0)),
        out_shape=jax.ShapeDtypeStruct((n, A_WIDTH), jnp.float32),
        scratch_shapes=[
            pltpu.VMEM((A_QPS, s, QBLOCK), jnp.int32),
            pltpu.VMEM((s + QBLOCK, A_HEADS * QBLOCK), jnp.float32),
            pltpu.VMEM((A_QPS, 1, QBLOCK), jnp.int32),
        ],
        compiler_params=pltpu.CompilerParams(
            dimension_semantics=("arbitrary", "arbitrary"), vmem_limit_bytes=VMEM_LIMIT),
        name="a_main",
    )(far, qn, proj, proj, misct, kn, vt, proj, near_t)


def mixer_a(proj, misct, b, s, gkv, wukv, gq, gk, bias_a):
    kn, vt, qn = a_pre(proj, b, s, gkv, wukv.astype(jnp.bfloat16), gq, gk)
    near_t = jnp.transpose(_bias_lookup(bias_a, _NEAR_BUCKET_T), (2, 0, 1))
    far = bias_a[_FAR_BUCKET]
    return a_main(proj, misct, qn, kn, vt, near_t, far, b, s)


N_CMP_PAD = 128
N_SB = 32
WIN_KEYS = WINDOW + QBLOCK
CMP_GROUPS = CMP_BLOCK // CMP_STRIDE

_CMP_BUCKET_T = _bucket_np((np.arange(16)[:, None, None] * QBLOCK + np.arange(QBLOCK)[None, None, :])
                           - (np.arange(N_CMP_PAD)[None, :, None] * CMP_STRIDE + CMP_BLOCK - 1))
_WIN_BUCKET_T = _bucket_np(np.arange(QBLOCK)[None, :] + WINDOW - np.arange(WIN_KEYS)[:, None])
_OVERLAP_T = np.array([[1.0 if (n * CMP_STRIDE < j * SEL_BLOCK + SEL_BLOCK and n * CMP_STRIDE + CMP_BLOCK > j * SEL_BLOCK
                              and n < N_CMP_PAD - 1) else 0.0 for n in range(N_CMP_PAD)] for j in range(N_SB)], np.float32)
_SB_EXPAND = (np.arange(2048)[:, None] // SEL_BLOCK == np.arange(N_SB)[None, :]).astype(np.float32)


def _c_pre_kernel(cmp_ref, sel_ref, win_ref, kn_ref, pos_ref, w1_ref, w2_ref,
                  kvc_ref, ksel_ref, vselt_ref, kwin_ref, vwin_ref):
    f32, bf16 = jnp.float32, jnp.bfloat16
    d = C_HEAD_DIM

    def rms(x, g):
        return x * lax.rsqrt(jnp.mean(x * x, axis=-1, keepdims=True) + RMS_EPS) * g

    acc = [jnp.zeros((N_CMP_PAD, 2 * PHI_HIDDEN), f32) for _ in range(CMP_GROUPS)]
    for j in range(CMP_STRIDE):
        xs = cmp_ref[pl.ds(j, N_CMP_PAD, stride=CMP_STRIDE), :]
        for half in range(CMP_GROUPS):
            jj = half * CMP_STRIDE + j
            acc[half] = acc[half] + jnp.dot((xs + pos_ref[jj:jj + 1, :]).astype(bf16), w1_ref[jj],
                                            preferred_element_type=f32)
    hid = acc[0] + pltpu.roll(acc[1], N_CMP_PAD - 1, axis=0)
    kv = jnp.dot(_silu(hid).astype(bf16), w2_ref[...], preferred_element_type=f32)
    kvc_ref[0, :, 0:d] = rms(kv[:, 0:d], kn_ref[0:1, :])
    kvc_ref[0, :, d:2 * d] = kv[:, d:2 * d]

    sel = sel_ref[...]
    ksel_ref[...] = rms(sel[:, 0:d], kn_ref[1:2, :])
    vselt_ref[0] = sel.T[d:2 * d, :]
    win = win_ref[...]
    kwin_ref[0, 0:WINDOW, :] = jnp.zeros((WINDOW, d), f32)
    vwin_ref[0, 0:WINDOW, :] = jnp.zeros((WINDOW, d), f32)
    kwin_ref[0, WINDOW:, :] = rms(win[:, 0:d], kn_ref[2:3, :])
    vwin_ref[0, WINDOW:, :] = win[:, d:2 * d]


def c_pre(proj, b, s, k_norm, cmp_pos, phi_w1, phi_w2):
    d = C_HEAD_DIM
    kv0 = OFF["c_kv"] // 128
    pos = jnp.concatenate([cmp_pos[0], cmp_pos[1]], axis=-1)
    w1 = phi_w1.reshape(2, CMP_BLOCK, d, PHI_HIDDEN)
    zero = jnp.zeros((CMP_BLOCK, d, PHI_HIDDEN), jnp.float32)
    w1c = jnp.concatenate([jnp.concatenate([w1[0], zero], axis=-1),
                           jnp.concatenate([zero, w1[1]], axis=-1)], axis=1).astype(jnp.bfloat16)
    z2 = jnp.zeros((PHI_HIDDEN, d), jnp.float32)
    w2c = jnp.concatenate([jnp.concatenate([phi_w2[0], z2], axis=-1),
                           jnp.concatenate([z2, phi_w2[1]], axis=-1)], axis=0).astype(jnp.bfloat16)
    full = lambda a: pl.BlockSpec(a.shape, lambda bb: (0,) * a.ndim)
    return pl.pallas_call(
        _c_pre_kernel,
        grid=(b,),
        in_specs=[
            pl.BlockSpec((s, 128), lambda bb: (bb, kv0)),
            pl.BlockSpec((s, 128), lambda bb: (bb, kv0 + 1)),
            pl.BlockSpec((s, 128), lambda bb: (bb, kv0 + 2)),
            full(k_norm), full(pos), full(w1c), full(w2c),
        ],
        out_specs=[
            pl.BlockSpec((1, N_CMP_PAD, 128), lambda bb: (bb, 0, 0)),
            pl.BlockSpec((s, d), lambda bb: (bb, 0)),
            pl.BlockSpec((1, d, s), lambda bb: (bb, 0, 0)),
            pl.BlockSpec((1, s + WINDOW, d), lambda bb: (bb, 0, 0)),
            pl.BlockSpec((1, s + WINDOW, d), lambda bb: (bb, 0, 0)),
        ],
        out_shape=[
            jax.ShapeDtypeStruct((b, N_CMP_PAD, 128), jnp.float32),
            jax.ShapeDtypeStruct((b * s, d), jnp.float32),
            jax.ShapeDtypeStruct((b, d, s), jnp.float32),
            jax.ShapeDtypeStruct((b, s + WINDOW, d), jnp.float32),
            jax.ShapeDtypeStruct((b, s + WINDOW, d), jnp.float32),
        ],
        compiler_params=pltpu.CompilerParams(dimension_semantics=("arbitrary",), vmem_limit_bytes=VMEM_LIMIT),
        name="c_pre",
    )(proj, proj, proj, k_norm, pos, w1c, w2c)


def _c_main_kernel(cq_ref, z_ref, misct_ref, gq_ref, kvc_ref, ksel_ref, vselt_ref, kwin_ref, vwin_ref,
                   cmptab_ref, near_ref, farrow_ref, wintab_ref, ovt_ref, exp_ref, o_ref, ls_scr):
    s = ksel_ref.shape[0]
    d = C_HEAD_DIM
    hq = C_HEADS * QBLOCK
    i = pl.program_id(1)
    t0 = pl.multiple_of(i * QBLOCK, QBLOCK)
    f32, bf16 = jnp.float32, jnp.bfloat16
    nt = (((1,), (1,)), ((), ()))
    tn = (((0,), (0,)), ((), ()))

    cq = cq_ref[...]
    qs = jnp.concatenate([cq[:, h * d:(h + 1) * d] for h in range(C_HEADS)], axis=0)
    qs = qs * lax.rsqrt(jnp.mean(qs * qs, axis=-1, keepdims=True) + RMS_EPS) * gq_ref[...] * (d ** -0.5)
    qs = qs.astype(bf16)

    def softmax_rows(l, valid):
        l = jnp.where(valid, l, NEG_INF)
        m = _col_reduce(l, jnp.max)
        p = jnp.where(valid, jnp.exp(l - m), 0.0)
        den = _col_reduce(p, jnp.sum)
        return p, den

    kvc = kvc_ref[0]
    lc = lax.dot_general(kvc[:, 0:d].astype(bf16), qs, nt, preferred_element_type=f32) + cmptab_ref[0]
    n_idx = lax.broadcasted_iota(jnp.int32, (N_CMP_PAD, hq), 0)
    t_c = t0 + (lax.broadcasted_iota(jnp.int32, (N_CMP_PAD, hq), 1) & (QBLOCK - 1))
    cmp_valid = n_idx * CMP_STRIDE + (CMP_BLOCK - 1) <= t_c
    pc, den_c = softmax_rows(lc, cmp_valid)
    pc = pc * jnp.where(den_c > 0.0, 1.0 / den_c, 0.0)
    o_cmp = lax.dot_general(kvc[:, d:2 * d].astype(bf16), pc.astype(bf16), tn, preferred_element_type=f32)

    psum = pc[:, 0:QBLOCK]
    for h in range(1, C_HEADS):
        psum = psum + pc[:, h * QBLOCK:(h + 1) * QBLOCK]
    p_hi = psum.astype(bf16)
    p_lo = (psum - p_hi.astype(f32)).astype(bf16)
    ovt = ovt_ref[...]
    imp = jnp.dot(ovt, p_hi, preferred_element_type=f32) + jnp.dot(ovt, p_lo, preferred_element_type=f32)
    j_idx = lax.broadcasted_iota(jnp.int32, (N_SB, QBLOCK), 0)
    t_b = t0 + lax.broadcasted_iota(jnp.int32, (N_SB, QBLOCK), 1)
    cur = lax.shift_right_arithmetic(t_b, 6)
    forced = (j_idx == 0) | (j_idx == cur) | (j_idx == jnp.maximum(cur - 1, 0))
    imp = jnp.where(j_idx * SEL_BLOCK <= t_b, jnp.where(forced, FORCE_SCORE, imp), NEG_INF)
    rank = jnp.zeros((N_SB, QBLOCK), f32)
    for r in range(N_SB):
        row = imp[r:r + 1, :]
        rank = rank + jnp.where((row > imp) | ((row == imp) & (j_idx > r)), 1.0, 0.0)
    picked = jnp.where(rank < float(N_SEL), 1.0, 0.0).astype(bf16)

    kw = kwin_ref[0, pl.ds(t0, WIN_KEYS), :]
    vw = vwin_ref[0, pl.ds(t0, WIN_KEYS), :]
    lw = lax.dot_general(kw.astype(bf16), qs, nt, preferred_element_type=f32) + wintab_ref[...]
    r_idx = lax.broadcasted_iota(jnp.int32, (WIN_KEYS, hq), 0)
    dist = (lax.broadcasted_iota(jnp.int32, (WIN_KEYS, hq), 1) & (QBLOCK - 1)) + WINDOW - r_idx
    win_valid = (dist >= 0) & (dist < WINDOW) & (r_idx + t0 >= WINDOW)
    pw, den_w = softmax_rows(lw, win_valid)
    o_win = lax.dot_general(vw.astype(bf16), pw.astype(bf16), tn, preferred_element_type=f32) / den_w

    g = _sigmoid(misct_ref[0, C_G_LANE:C_G_LANE + 3 * C_HEADS, :])

    def run(nrows):
        kpos = lax.broadcasted_iota(jnp.int32, (nrows, QBLOCK), 0)
        tpos = t0 + lax.broadcasted_iota(jnp.int32, (nrows, QBLOCK), 1)
        sel_valid = (jnp.dot(exp_ref[0:nrows, :], picked, preferred_element_type=f32) > 0.5) & (kpos <= tpos)
        ls_scr[0:QBLOCK, :] = jnp.zeros((QBLOCK, hq), f32)
        ls_scr[QBLOCK:QBLOCK + nrows, :] = lax.dot_general(ksel_ref[0:nrows, :].astype(bf16), qs, nt,
                                                           preferred_element_type=f32)
        ls_scr[pl.ds(t0, 2 * QBLOCK), :] += near_ref[...] - farrow_ref[...]
        outs = []
        for h in range(C_HEADS):
            cols = slice(h * QBLOCK, (h + 1) * QBLOCK)
            l = jnp.where(sel_valid, ls_scr[QBLOCK:QBLOCK + nrows, cols], NEG_INF)
            m = _col_reduce(l, jnp.max)
            p = jnp.exp(l - m)
            den = _col_reduce(p, jnp.sum)
            o_sel = jnp.dot(vselt_ref[0, :, 0:nrows].astype(bf16), p.astype(bf16), preferred_element_type=f32) / den
            outs.append(g[3 * h:3 * h + 1, :] * o_cmp[:, cols] + g[3 * h + 1:3 * h + 2, :] * o_sel
                        + g[3 * h + 2:3 * h + 3, :] * o_win[:, cols])
        o_ref[...] = jnp.concatenate(outs, axis=0).T * _silu(z_ref[...])

    _run_causal_variant(run, i, s)


C_QPS = 2


def _c_pair_kernel(cq_ref, z_ref, misct_ref, gq_ref, kvc_ref, ksel_ref, vselt_ref, kwin_ref, vwin_ref,
                   cmptab_ref, near_ref, farrow_ref, wintab_ref, ovt_ref, exp_ref, o_ref, ls_scr):
    s = ksel_ref.shape[0]
    d = C_HEAD_DIM
    hq = C_HEADS * QBLOCK
    f32, bf16 = jnp.float32, jnp.bfloat16
    nt = (((1,), (1,)), ((), ()))
    tn = (((0,), (0,)), ((), ()))

    def softmax_rows(l, valid):
        l = jnp.where(valid, l, NEG_INF)
        m = _col_reduce(l, jnp.max)
        p = jnp.where(valid, jnp.exp(l - m), 0.0)
        den = _col_reduce(p, jnp.sum)
        return p, den

    def block(g, t0):
        nrows = t0 + QBLOCK
        qrows = slice(g * QBLOCK, (g + 1) * QBLOCK)

        cq = cq_ref[qrows, :]
        qs = jnp.concatenate([cq[:, h * d:(h + 1) * d] for h in range(C_HEADS)], axis=0)
        qs = qs * lax.rsqrt(jnp.mean(qs * qs, axis=-1, keepdims=True) + RMS_EPS) * gq_ref[...] * (d ** -0.5)
        qs = qs.astype(bf16)

        kvc = kvc_ref[0]
        lc = lax.dot_general(kvc[:, 0:d].astype(bf16), qs, nt, preferred_element_type=f32) + cmptab_ref[g]
        n_idx = lax.broadcasted_iota(jnp.int32, (N_CMP_PAD, hq), 0)
        t_c = t0 + (lax.broadcasted_iota(jnp.int32, (N_CMP_PAD, hq), 1) & (QBLOCK - 1))
        cmp_valid = n_idx * CMP_STRIDE + (CMP_BLOCK - 1) <= t_c
        pc, den_c = softmax_rows(lc, cmp_valid)
        pc = pc * jnp.where(den_c > 0.0, 1.0 / den_c, 0.0)
        o_cmp = lax.dot_general(kvc[:, d:2 * d].astype(bf16), pc.astype(bf16), tn, preferred_element_type=f32)

        psum = pc[:, 0:QBLOCK]
        for h in range(1, C_HEADS):
            psum = psum + pc[:, h * QBLOCK:(h + 1) * QBLOCK]
        p_hi = psum.astype(bf16)
        p_lo = (psum - p_hi.astype(f32)).astype(bf16)
        ovt = ovt_ref[...]
        imp = jnp.dot(ovt, p_hi, preferred_element_type=f32) + jnp.dot(ovt, p_lo, preferred_element_type=f32)
        j_idx = lax.broadcasted_iota(jnp.int32, (N_SB, QBLOCK), 0)
        t_b = t0 + lax.broadcasted_iota(jnp.int32, (N_SB, QBLOCK), 1)
        cur = lax.shift_right_arithmetic(t_b, 6)
        forced = (j_idx == 0) | (j_idx == cur) | (j_idx == jnp.maximum(cur - 1, 0))
        imp = jnp.where(j_idx * SEL_BLOCK <= t_b, jnp.where(forced, FORCE_SCORE, imp), NEG_INF)
        rank = jnp.zeros((N_SB, QBLOCK), f32)
        for r in range(N_SB):
            row = imp[r:r + 1, :]
            rank = rank + jnp.where((row > imp) | ((row == imp) & (j_idx > r)), 1.0, 0.0)
        picked = jnp.where(rank < float(N_SEL), 1.0, 0.0).astype(bf16)

        kw = kwin_ref[0, t0:t0 + WIN_KEYS, :]
        vw = vwin_ref[0, t0:t0 + WIN_KEYS, :]
        lw = lax.dot_general(kw.astype(bf16), qs, nt, preferred_element_type=f32) + wintab_ref[...]
        r_idx = lax.broadcasted_iota(jnp.int32, (WIN_KEYS, hq), 0)
        dist = (lax.broadcasted_iota(jnp.int32, (WIN_KEYS, hq), 1) & (QBLOCK - 1)) + WINDOW - r_idx
        win_valid = (dist >= 0) & (dist < WINDOW) & (r_idx + t0 >= WINDOW)
        pw, den_w = softmax_rows(lw, win_valid)
        o_win = lax.dot_general(vw.astype(bf16), pw.astype(bf16), tn, preferred_element_type=f32) / den_w

        gate = _sigmoid(misct_ref[0, C_G_LANE:C_G_LANE + 3 * C_HEADS, qrows])

        kpos = lax.broadcasted_iota(jnp.int32, (nrows, QBLOCK), 0)
        tpos = t0 + lax.broadcasted_iota(jnp.int32, (nrows, QBLOCK), 1)
        sel_valid = (jnp.dot(exp_ref[0:nrows, :], picked, preferred_element_type=f32) > 0.5) & (kpos <= tpos)
        ls_scr[0:QBLOCK, :] = jnp.zeros((QBLOCK, hq), f32)
        ls_scr[QBLOCK:QBLOCK + nrows, :] = lax.dot_general(ksel_ref[0:nrows, :].astype(bf16), qs, nt,
                                                           preferred_element_type=f32)
        ls_scr[t0:t0 + 2 * QBLOCK, :] += near_ref[...] - farrow_ref[...]
        outs = []
        for h in range(C_HEADS):
            cols = slice(h * QBLOCK, (h + 1) * QBLOCK)
            l = jnp.where(sel_valid, ls_scr[QBLOCK:QBLOCK + nrows, cols], NEG_INF)
            m = _col_reduce(l, jnp.max)
            p = jnp.exp(l - m)
            den = _col_reduce(p, jnp.sum)
            o_sel = jnp.dot(vselt_ref[0, :, 0:nrows].astype(bf16), p.astype(bf16), preferred_element_type=f32) / den
            outs.append(gate[3 * h:3 * h + 1, :] * o_cmp[:, cols] + gate[3 * h + 1:3 * h + 2, :] * o_sel
                        + gate[3 * h + 2:3 * h + 3, :] * o_win[:, cols])
        o_ref[qrows, :] = jnp.concatenate(outs, axis=0).T * _silu(z_ref[qrows, :])

    def run(step):
        for g in range(C_QPS):
            block(g, (C_QPS * step + g) * QBLOCK)

    step = pl.program_id(1)
    for v in range(s // (C_QPS * QBLOCK)):
        pl.when(step == v)(functools.partial(run, v))


def c_main(proj, misct, gq, kvc, ksel, vselt, kwin, vwin, cmptab, near, farrow, wintab, b, s):
    n = b * s
    tq = C_QPS * QBLOCK
    nq = s // tq
    d = C_HEAD_DIM
    hq = C_HEADS * QBLOCK
    row = lambda bb, i: bb * nq + i
    ovt = jnp.asarray(_OVERLAP_T, jnp.bfloat16)
    expand = jnp.asarray(_SB_EXPAND, jnp.bfloat16)
    const = lambda a: pl.BlockSpec(a.shape, lambda bb, i: (0,) * a.ndim)
    return pl.pallas_call(
        _c_pair_kernel,
        grid=(b, nq),
        in_specs=[
            pl.BlockSpec((tq, C_WIDTH), lambda bb, i: (row(bb, i), OFF["c_q"] // C_WIDTH)),
            pl.BlockSpec((tq, C_WIDTH), lambda bb, i: (row(bb, i), OFF["c_z"] // C_WIDTH)),
            pl.BlockSpec((1, 128, tq), lambda bb, i: (bb, 0, i)),
            const(gq),
            pl.BlockSpec((1, N_CMP_PAD, 128), lambda bb, i: (bb, 0, 0)),
            pl.BlockSpec((s, d), lambda bb, i: (bb, 0)),
            pl.BlockSpec((1, d, s), lambda bb, i: (bb, 0, 0)),
            pl.BlockSpec((1, s + WINDOW, d), lambda bb, i: (bb, 0, 0)),
            pl.BlockSpec((1, s + WINDOW, d), lambda bb, i: (bb, 0, 0)),
            pl.BlockSpec((C_QPS, N_CMP_PAD, hq), lambda bb, i: (i, 0, 0)),
            const(near), const(farrow), const(wintab), const(ovt), const(expand),
        ],
        out_specs=pl.BlockSpec((tq, C_WIDTH), lambda bb, i: (row(bb, i), 0)),
        out_shape=jax.ShapeDtypeStruct((n, C_WIDTH), jnp.float32),
        scratch_shapes=[pltpu.VMEM((s + QBLOCK, hq), jnp.float32)],
        compiler_params=pltpu.CompilerParams(
            dimension_semantics=("arbitrary", "arbitrary"), vmem_limit_bytes=VMEM_LIMIT),
        name="c_main",
    )(proj, proj, misct, gq, kvc, ksel, vselt, kwin, vwin, cmptab, near, farrow, wintab, ovt, expand)


def _head_cols(tab):
    return jnp.moveaxis(tab, -1, -2).reshape(*tab.shape[:-2], tab.shape[-1] * tab.shape[-2])


def mixer_c(proj, misct, b, s, gq, k_norm, cmp_pos, phi_w1, phi_w2, bias_c):
    kvc, ksel, vselt, kwin, vwin = c_pre(proj, b, s, k_norm, cmp_pos, phi_w1, phi_w2)
    cmptab = _head_cols(_bias_lookup(bias_c, _CMP_BUCKET_T))
    near = _head_cols(_bias_lookup(bias_c, _NEAR_BUCKET_T))
    wintab = _head_cols(_bias_lookup(bias_c, _WIN_BUCKET_T))
    farrow = jnp.repeat(bias_c[_FAR_BUCKET], QBLOCK).reshape(1, C_HEADS * QBLOCK)
    return c_main(proj, misct, gq.reshape(1, C_HEAD_DIM), kvc, ksel, vselt, kwin, vwin, cmptab, near, farrow,
                  wintab, b, s)


B_HPS = 2
B_GROUP = 4


def _b_kernel(alog_ref, dtb_ref, q_ref, k_ref, v_ref, z_ref, misc_ref, arow_ref, cwq_ref, cwk_ref, cwv_ref,
              gon_ref, o_ref, qs, ks, vs, ws, at, gtok, btok, gcrow):
    s = q_ref.shape[0]
    dh = B_HEAD_DIM
    c_len = GDN_CHUNK
    j = pl.program_id(1)
    f32, bf16 = jnp.float32, jnp.bfloat16
    hi = lax.Precision.HIGHEST
    nt = (((1,), (1,)), ((), ()))
    tn = (((0,), (0,)), ((), ()))

    head_rows = lax.broadcasted_iota(jnp.int32, (8, B_HPS * dh), 0)

    def conv_silu(x_ref, w_ref):
        x8 = x_ref[0:8, :]
        w_last = w_ref[CONV_WIDTH - 1:CONV_WIDTH, :]
        head = x8 * w_last
        body = x_ref[8:s, :] * w_last
        for k in range(1, CONV_WIDTH):
            wk = w_ref[CONV_WIDTH - 1 - k:CONV_WIDTH - k, :]
            head = head + jnp.where(head_rows >= k, pltpu.roll(x8, k, axis=0), 0.0) * wk
            body = body + x_ref[8 - k:s - k, :] * wk
        return _silu(jnp.concatenate([head, body], axis=0))

    def l2n(x):
        return x * lax.rsqrt(jnp.sum(x * x, axis=-1, keepdims=True) + RMS_EPS)

    qc = conv_silu(q_ref, cwq_ref)
    kc = conv_silu(k_ref, cwk_ref)
    vs[...] = conv_silu(v_ref, cwv_ref)
    misc = misc_ref[...]
    lane = lax.broadcasted_iota(jnp.int32, misc.shape, 1)
    ri = lax.broadcasted_iota(jnp.int32, (c_len, c_len), 0)
    ci = lax.broadcasted_iota(jnp.int32, (c_len, c_len), 1)
    lower = ci <= ri
    strict = ci < ri
    tri_u = jnp.where(ri <= ci, 1.0, 0.0)
    row_in_chunk = lax.broadcasted_iota(jnp.int32, (s, dh), 0) & (c_len - 1)
    for hh in range(B_HPS):
        sl = slice(hh * dh, (hh + 1) * dh)
        qs[:, sl] = l2n(qc[:, sl]) * (dh ** -0.5)
        ks[:, sl] = l2n(kc[:, sl])
        h = B_HPS * j + hh
        neg_a = -jnp.exp(alog_ref[h])
        dtb = dtb_ref[h]
        a_tok = jnp.sum(jnp.where(lane == B_A_LANE + h, misc, 0.0), axis=-1, keepdims=True)
        b_tok = jnp.sum(jnp.where(lane == B_B_LANE + h, misc, 0.0), axis=-1, keepdims=True)
        g = jnp.broadcast_to(neg_a * jax.nn.softplus(a_tok + dtb), (s, dh))
        for sh in (1, 2, 4, 8, 16, 32):
            g = g + jnp.where(row_in_chunk >= sh, pltpu.roll(g, sh, axis=0), 0.0)
        gtok[hh] = g
        btok[hh] = jnp.broadcast_to(_sigmoid(b_tok), (s, dh))
        g_row = neg_a * jax.nn.softplus(arow_ref[0, hh] + dtb)
        gcrow[hh] = jnp.dot(g_row, tri_u, precision=hi, preferred_element_type=f32)

    lane3 = lax.broadcasted_iota(jnp.int32, (c_len, 3 * c_len), 1)
    lo_lanes = (lane3 >= c_len) & (lane3 < 2 * c_len)

    def split_lhs(p):
        p4 = jnp.concatenate([p, p, p], axis=1)
        hi4 = p4.astype(bf16).astype(f32)
        return jnp.where(lo_lanes, p4 - hi4, hi4).astype(bf16)

    def split_rhs(x):
        xh = x.astype(bf16)
        xl = (x - xh.astype(f32)).astype(bf16)
        return jnp.concatenate([xh, xh, xl], axis=0)

    def mm(p_split, x):
        return jnp.dot(p_split, split_rhs(x), preferred_element_type=f32)

    hs = range(B_HPS)
    sls = [slice(hh * dh, (hh + 1) * dh) for hh in hs]
    n_groups = s // (B_GROUP * c_len)

    def chunk_rows(c):
        return pl.ds(pl.multiple_of(c * c_len, c_len), c_len)

    def intra_group(i, between):
        where, q_c, k_c, v_c, gb, beta, g_row = [], [], [], [], [], [], []
        for c in [B_GROUP * i + cc for cc in range(B_GROUP)]:
            rs = chunk_rows(c)
            for hh in hs:
                where.append((rs, hh))
                q_c.append(qs[rs, sls[hh]])
                k_c.append(ks[rs, sls[hh]])
                v_c.append(vs[rs, sls[hh]])
                gb.append(gtok[hh, rs, :])
                beta.append(btok[hh, rs, :])
                g_row.append(gcrow[hh, pl.ds(c, 1), :])
        nch = range(len(where))
        decay = [jnp.where(lower, jnp.exp(jnp.where(lower, gb[n][:, 0:c_len] - g_row[n], 0.0)), 0.0) for n in nch]
        kb = [k_c[n] * beta[n] for n in nch]
        kk = [lax.dot_general(kb[n].astype(bf16), k_c[n].astype(bf16), nt, preferred_element_type=f32) for n in nch]
        between()
        p = [-jnp.where(strict, kk[n] * decay[n], 0.0) for n in nch]
        eg = [jnp.exp(gb[n]) for n in nch]
        x = [jnp.concatenate([v_c[n] * beta[n], kb[n] * eg[n]], axis=1) for n in nch]
        ps = [split_lhs(p[n]) for n in nch]
        x = [x[n] + mm(ps[n], x[n]) for n in nch]
        between()
        for _ in range(5):
            p = [mm(ps[n], p[n]) for n in nch]
            between()
            ps = [split_lhs(p[n]) for n in nch]
            x = [x[n] + mm(ps[n], x[n]) for n in nch]
            between()
        attn = [jnp.where(lower, lax.dot_general(q_c[n].astype(bf16), k_c[n].astype(bf16), nt,
                                                 preferred_element_type=f32) * decay[n], 0.0) for n in nch]
        between()

        def commit():
            for n, (rs, hh) in enumerate(where):
                g_last = g_row[n][:, c_len - 1:c_len]
                vs[rs, sls[hh]] = x[n][:, 0:dh]
                ws[rs, sls[hh]] = x[n][:, dh:2 * dh]
                qs[rs, sls[hh]] = q_c[n] * eg[n]
                ks[rs, sls[hh]] = k_c[n] * jnp.exp(g_last - gb[n])
                at[rs, hh * c_len:(hh + 1) * c_len] = attn[n]

        return commit

    def recur_steps(group, states, result):
        for c in [B_GROUP * group + cc for cc in range(B_GROUP)]:
            rs = chunk_rows(c)
            st_b = [states[hh].astype(bf16) for hh in hs]
            v_new = [vs[rs, sls[hh]] - jnp.dot(ws[rs, sls[hh]].astype(bf16), st_b[hh], preferred_element_type=f32)
                     for hh in hs]
            o_st = [jnp.dot(qs[rs, sls[hh]].astype(bf16), st_b[hh], preferred_element_type=f32) for hh in hs]
            yield
            v_nb = [v_new[hh].astype(bf16) for hh in hs]
            states = tuple(states[hh] * jnp.exp(gcrow[hh, pl.ds(c, 1), c_len - 1:c_len])
                           + lax.dot_general(ks[rs, sls[hh]].astype(bf16), v_nb[hh], tn, preferred_element_type=f32)
                           for hh in hs)
            for hh in hs:
                o_ref[rs, sls[hh]] = o_st[hh] + jnp.dot(at[rs, hh * c_len:(hh + 1) * c_len].astype(bf16), v_nb[hh],
                                                        preferred_element_type=f32)
            yield
        result.append(states)

    def overlapped(g, states):
        result = []
        steps = recur_steps(g - 1, states, result)
        commit = intra_group(g, lambda: next(steps, None))
        for _ in steps:
            pass
        commit()
        return result[0]

    intra_group(0, lambda: None)()
    states = lax.fori_loop(1, n_groups, overlapped, tuple(jnp.zeros((dh, dh), f32) for _ in hs))
    for _ in recur_steps(n_groups - 1, states, []):
        pass

    for hh in range(B_HPS):
        sl = slice(hh * dh, (hh + 1) * dh)
        o = o_ref[:, sl]
        o = o * lax.rsqrt(jnp.mean(o * o, axis=-1, keepdims=True) + RMS_EPS) * gon_ref[...]
        o_ref[:, sl] = o * _silu(z_ref[:, sl])


def mixer_b(proj, misct, b, s, conv_w, a_log, dt_bias, out_norm):
    n = b * s
    w2 = B_HPS * B_HEAD_DIM
    nc = s // GDN_CHUNK
    arow = misct[:, B_A_LANE:B_A_LANE + B_HEADS, :].reshape(b, B_HEADS, nc, GDN_CHUNK)
    qkv0 = OFF["b_qkv"] // w2
    kstep = B_WIDTH // w2
    smem = pl.BlockSpec(memory_space=pltpu.SMEM)
    return pl.pallas_call(
        _b_kernel,
        grid=(b, B_HEADS // B_HPS),
        in_specs=[
            smem, smem,
            pl.BlockSpec((s, w2), lambda bb, j: (bb, qkv0 + j)),
            pl.BlockSpec((s, w2), lambda bb, j: (bb, qkv0 + kstep + j)),
            pl.BlockSpec((s, w2), lambda bb, j: (bb, qkv0 + 2 * kstep + j)),
            pl.BlockSpec((s, w2), lambda bb, j: (bb, OFF["b_z"] // w2 + j)),
            pl.BlockSpec((s, 128), lambda bb, j: (bb, MISC_OFF // 128)),
            pl.BlockSpec((1, B_HPS, nc, GDN_CHUNK), lambda bb, j: (bb, j, 0, 0)),
            pl.BlockSpec((CONV_WIDTH, w2), lambda bb, j: (0, j)),
            pl.BlockSpec((CONV_WIDTH, w2), lambda bb, j: (0, kstep + j)),
            pl.BlockSpec((CONV_WIDTH, w2), lambda bb, j: (0, 2 * kstep + j)),
            pl.BlockSpec((1, B_HEAD_DIM), lambda bb, j: (0, 0)),
        ],
        out_specs=pl.BlockSpec((s, w2), lambda bb, j: (bb, j)),
        out_shape=jax.ShapeDtypeStruct((n, B_WIDTH), jnp.float32),
        scratch_shapes=[
            pltpu.VMEM((s, w2), jnp.float32), pltpu.VMEM((s, w2), jnp.float32), pltpu.VMEM((s, w2), jnp.float32),
            pltpu.VMEM((s, w2), jnp.float32), pltpu.VMEM((s, B_HPS * GDN_CHUNK), jnp.float32),
            pltpu.VMEM((B_HPS, s, B_HEAD_DIM), jnp.float32), pltpu.VMEM((B_HPS, s, B_HEAD_DIM), jnp.float32),
            pltpu.VMEM((B_HPS, nc, GDN_CHUNK), jnp.float32),
        ],
        compiler_params=pltpu.CompilerParams(
            dimension_semantics=("arbitrary", "arbitrary"), vmem_limit_bytes=VMEM_LIMIT),
        name="gdn",
    )(a_log, dt_bias, proj, proj, proj, proj, proj, arow, conv_w, conv_w, conv_w, out_norm.reshape(1, B_HEAD_DIM))


def _rms_norm(x, g):
    y = x * lax.rsqrt(jnp.mean(x * x, axis=-1, keepdims=True) + RMS_EPS)
    return y * g


def _l2norm(x):
    return x * lax.rsqrt(jnp.sum(x * x, axis=-1, keepdims=True) + RMS_EPS)


def _masked_softmax(logits, mask):
    logits = jnp.where(mask, logits, NEG_INF)
    return jnp.where(mask, jax.nn.softmax(logits, axis=-1), 0.0)


def _t5_bucket(dist):
    dist = jnp.maximum(dist, 0)
    log_ratio = jnp.log(jnp.maximum(dist, 1).astype(jnp.float32) / BUCKET_MAX_EXACT) / math.log(BUCKET_MAX_DIST / BUCKET_MAX_EXACT)
    large = BUCKET_MAX_EXACT + (log_ratio * (N_BUCKETS - BUCKET_MAX_EXACT)).astype(jnp.int32)
    large = jnp.minimum(large, N_BUCKETS - 1)
    return jnp.where(dist < BUCKET_MAX_EXACT, dist, large)


def _to_qblocks(a):
    b, s = a.shape[:2]
    return jnp.moveaxis(a.reshape(b, s // QBLOCK, QBLOCK, *a.shape[2:]), 1, 0)


def _from_qblocks(a):
    nb, b = a.shape[:2]
    return jnp.moveaxis(a, 0, 1).reshape(b, nb * QBLOCK, *a.shape[3:])


def _dsa_mixer(q, k, v, q_idx, k_idx, w_idx, bias_tab):
    b, s = q.shape[:2]
    topk = min(TOPK_MAX, s // 4)
    bidx = jnp.arange(b)[:, None, None]
    key_pos = jnp.arange(s)
    scale = A_HEAD_DIM ** -0.5

    def block(args):
        qb, qib, wb, start = args
        t = start + jnp.arange(QBLOCK)
        idx_logits = jnp.einsum('bqhd,bsd->bqhs', qib, k_idx) * (IDX_DIM ** -0.5)
        score = jnp.einsum('bqh,bqhs->bqs', wb * (IDX_HEADS ** -0.5), jax.nn.relu(idx_logits))
        score = jnp.where(key_pos[None, None, :] <= t[None, :, None], score, NEG_INF)
        _, sel = lax.top_k(score, topk)
        kg = k[bidx, sel]
        vg = v[bidx, sel]
        dist = t[None, :, None] - sel
        bias = jnp.transpose(bias_tab[_t5_bucket(dist)], (0, 3, 1, 2))
        logits = jnp.einsum('bqhd,bqkhd->bhqk', qb, kg) * scale + bias
        p = _masked_softmax(logits, (dist >= 0)[:, None])
        return jnp.einsum('bhqk,bqkhd->bqhd', p, vg)

    starts = jnp.arange(s // QBLOCK, dtype=jnp.int32) * QBLOCK
    out = lax.map(block, (_to_qblocks(q), _to_qblocks(q_idx), _to_qblocks(w_idx), starts))
    return _from_qblocks(out)


def _causal_depthwise_conv(x, w):
    c = x.shape[-1]
    return lax.conv_general_dilated(x, w[:, None, :], window_strides=(1,), padding=[(CONV_WIDTH - 1, 0)],
                                    dimension_numbers=('NWC', 'WIO', 'NWC'), feature_group_count=c)


def _gdn_mixer(qkv, a_in, b_in, conv_w, a_log, dt_bias):
    b, s = qkv.shape[:2]
    n = s // GDN_CHUNK
    qkv = jax.nn.silu(_causal_depthwise_conv(qkv, conv_w))
    q, k, v = jnp.split(qkv, 3, axis=-1)
    sh = lambda t: t.reshape(b, s, B_HEADS, B_HEAD_DIM)
    q = _l2norm(sh(q)) * (B_HEAD_DIM ** -0.5)
    k = _l2norm(sh(k))
    v = sh(v)
    beta = jax.nn.sigmoid(b_in)
    g = -jnp.exp(a_log) * jax.nn.softplus(a_in + dt_bias)

    def chunk(t):
        return jnp.moveaxis(t.reshape(b, n, GDN_CHUNK, *t.shape[2:]), 3, 2)

    q, k, v, beta, g = (chunk(t) for t in (q, k, v, beta, g))
    g_cum = jnp.cumsum(g, axis=-1)
    g_last = g_cum[..., -1]
    lower = jnp.tril(jnp.ones((GDN_CHUNK, GDN_CHUNK), dtype=bool))
    strict = jnp.tril(jnp.ones((GDN_CHUNK, GDN_CHUNK), dtype=bool), -1)
    diff = g_cum[..., :, None] - g_cum[..., None, :]
    decay = jnp.where(lower, jnp.exp(jnp.where(lower, diff, 0.0)), 0.0)
    k_beta = k * beta[..., None]
    a_mat = jnp.where(strict, jnp.einsum('bnhid,bnhjd->bnhij', k_beta, k) * decay, 0.0)
    eye = jnp.eye(GDN_CHUNK, dtype=jnp.float32)
    t_inv = lax.linalg.triangular_solve(eye + a_mat, jnp.broadcast_to(eye, a_mat.shape), left_side=True, lower=True)
    u = jnp.einsum('bnhij,bnhjd->bnhid', t_inv, v * beta[..., None])
    w = jnp.einsum('bnhij,bnhjd->bnhid', t_inv, k_beta * jnp.exp(g_cum)[..., None])
    attn = jnp.einsum('bnhid,bnhjd->bnhij', q, k) * decay
    q_dec = q * jnp.exp(g_cum)[..., None]
    k_dec = k * jnp.exp(g_last[..., None] - g_cum)[..., None]

    def step(state, xs):
        u_c, w_c, q_c, k_c, attn_c, gl_c = xs
        v_new = u_c - jnp.einsum('bhik,bhkv->bhiv', w_c, state)
        o = jnp.einsum('bhik,bhkv->bhiv', q_c, state) + jnp.einsum('bhij,bhjv->bhiv', attn_c, v_new)
        state = state * jnp.exp(gl_c)[..., None, None] + jnp.einsum('bhik,bhiv->bhkv', k_c, v_new)
        return state, o

    state0 = jnp.zeros((b, B_HEADS, B_HEAD_DIM, B_HEAD_DIM), jnp.float32)
    xs = tuple(jnp.moveaxis(t, 1, 0) for t in (u, w, q_dec, k_dec, attn, g_last))
    _, o = lax.scan(step, state0, xs)
    o = jnp.moveaxis(jnp.moveaxis(o, 0, 1), 2, 3)
    return o.reshape(b, s, B_HEADS, B_HEAD_DIM)


def _compress_blocks(tok, pos, w1, w2):
    b, s, d = tok.shape
    n_cmp = (s - CMP_BLOCK) // CMP_STRIDE + 1
    idx = jnp.arange(n_cmp)[:, None] * CMP_STRIDE + jnp.arange(CMP_BLOCK)[None, :]
    blocks = tok[:, idx] + pos
    return jax.nn.silu(blocks.reshape(b, n_cmp, CMP_BLOCK * d) @ w1) @ w2


def _nsa_mixer(q, kv, gates, k_norm, cmp_pos, phi_w1, phi_w2, bias_tab):
    b, s = q.shape[:2]
    scale = C_HEAD_DIM ** -0.5
    t_all = jnp.arange(s)
    k_cmp, v_cmp, k_sel, v_sel, k_win, v_win = (kv[:, :, i] for i in range(6))
    k_sel = _rms_norm(k_sel, k_norm[1])
    k_win = _rms_norm(k_win, k_norm[2])
    kc = _rms_norm(_compress_blocks(k_cmp, cmp_pos[0], phi_w1[0], phi_w2[0]), k_norm[0])
    vc = _compress_blocks(v_cmp, cmp_pos[1], phi_w1[1], phi_w2[1])
    n_cmp = kc.shape[1]
    cmp_start = jnp.arange(n_cmp) * CMP_STRIDE
    cmp_end = cmp_start + CMP_BLOCK - 1
    cmp_valid = cmp_end[None, :] <= t_all[:, None]
    cmp_bias = jnp.transpose(bias_tab[_t5_bucket(t_all[:, None] - cmp_end[None, :])], (2, 0, 1))
    logits = jnp.einsum('bqhd,bnd->bhqn', q, kc) * scale + cmp_bias
    p_cmp = _masked_softmax(logits, cmp_valid[None, None])
    o_cmp = jnp.einsum('bhqn,bnd->bqhd', p_cmp, vc)
    n_sb = s // SEL_BLOCK
    n_pick = min(N_SEL, n_sb)
    sb_start = jnp.arange(n_sb) * SEL_BLOCK
    overlap = ((cmp_start[:, None] < sb_start[None, :] + SEL_BLOCK) & (cmp_start[:, None] + CMP_BLOCK > sb_start[None, :])).astype(jnp.float32)
    importance = jnp.einsum('bhqn,nj->bqj', p_cmp, overlap)
    cur = t_all // SEL_BLOCK
    blk = jnp.arange(n_sb)
    forced = (blk[None, :] == 0) | (blk[None, :] == cur[:, None]) | (blk[None, :] == jnp.maximum(cur[:, None] - 1, 0))
    admissible = sb_start[None, :] <= t_all[:, None]
    importance = jnp.where(admissible[None], jnp.where(forced[None], FORCE_SCORE, importance), NEG_INF)
    _, sel = lax.top_k(importance, n_pick)
    k_blocks = k_sel.reshape(b, n_sb, SEL_BLOCK, C_HEAD_DIM)
    v_blocks = v_sel.reshape(b, n_sb, SEL_BLOCK, C_HEAD_DIM)
    k_win_pad = jnp.pad(k_win, ((0, 0), (WINDOW, 0), (0, 0)))
    v_win_pad = jnp.pad(v_win, ((0, 0), (WINDOW, 0), (0, 0)))
    bidx = jnp.arange(b)[:, None, None]
    n_keys = n_pick * SEL_BLOCK

    def block(args):
        qb, selb, start = args
        t = start + jnp.arange(QBLOCK)
        kg = k_blocks[bidx, selb].reshape(b, QBLOCK, n_keys, C_HEAD_DIM)
        vg = v_blocks[bidx, selb].reshape(b, QBLOCK, n_keys, C_HEAD_DIM)
        s_pos = (selb[..., None] * SEL_BLOCK + jnp.arange(SEL_BLOCK)).reshape(b, QBLOCK, n_keys)
        dist = t[None, :, None] - s_pos
        bias = jnp.transpose(bias_tab[_t5_bucket(dist)], (0, 3, 1, 2))
        lg = jnp.einsum('bqhd,bqkd->bhqk', qb, kg) * scale + bias
        p = _masked_softmax(lg, (dist >= 0)[:, None])
        o_sel = jnp.einsum('bhqk,bqkd->bqhd', p, vg)
        kw = lax.dynamic_slice_in_dim(k_win_pad, start, QBLOCK + WINDOW, axis=1)
        vw = lax.dynamic_slice_in_dim(v_win_pad, start, QBLOCK + WINDOW, axis=1)
        s_w = start - WINDOW + jnp.arange(QBLOCK + WINDOW)
        dist_w = t[:, None] - s_w[None, :]
        valid_w = (s_w[None, :] >= 0) & (dist_w >= 0) & (dist_w < WINDOW)
        bias_w = jnp.transpose(bias_tab[_t5_bucket(dist_w)], (2, 0, 1))
        lw = jnp.einsum('bqhd,bsd->bhqs', qb, kw) * scale + bias_w
        pw = _masked_softmax(lw, valid_w[None, None])
        o_win = jnp.einsum('bhqs,bsd->bqhd', pw, vw)
        return o_sel, o_win

    starts = jnp.arange(s // QBLOCK, dtype=jnp.int32) * QBLOCK
    o_sel, o_win = lax.map(block, (_to_qblocks(q), _to_qblocks(sel), starts))
    o_sel = _from_qblocks(o_sel)
    o_win = _from_qblocks(o_win)
    g = jax.nn.sigmoid(gates).reshape(b, s, C_HEADS, 3)
    return g[..., 0:1] * o_cmp + g[..., 1:2] * o_sel + g[..., 2:3] * o_win


def _field(proj3, name):
    return proj3[..., OFF[name]:OFF[name] + _ORIG[name][1]]


def kernel(x, norm_g, w_in, a_kv_norm, a_w_ukv, a_q_norm, a_k_norm, b_conv, b_a_log, b_dt_bias, b_out_norm, c_q_norm, c_k_norm, c_cmp_pos, c_phi_w1, c_phi_w2, w_branch, w_out, rel_bias):
    b, s, _ = x.shape
    n = b * s
    bias_a = rel_bias[:, :A_HEADS]
    bias_c = rel_bias[:, A_HEADS:]
    xf = x.reshape(n, D_MODEL)
    fields = [w_in[:, :, _ORIG[name][0]:_ORIG[name][0] + _ORIG[name][1]] for name in _NEW_ORDER]
    pad = jnp.zeros((DEPTH, D_MODEL, N_PAD - N_IN), w_in.dtype)
    w_in_p = jnp.concatenate(fields + [pad], axis=-1).astype(jnp.bfloat16)
    for l in range(DEPTH):
        proj = in_proj(xf, norm_g[l], w_in_p[l])
        p3 = proj.reshape(b, s, N_PAD)

        misct = jnp.transpose(p3[..., MISC_OFF:], (0, 2, 1))
        y_a = mixer_a(proj, misct, b, s, a_kv_norm[l], a_w_ukv[l], a_q_norm[l], a_k_norm[l], bias_a)

        y_b = mixer_b(proj, misct, b, s, b_conv[l], b_a_log[l], b_dt_bias[l], b_out_norm[l])
        y_c = mixer_c(proj, misct, b, s, c_q_norm[l], c_k_norm[l], c_cmp_pos[l], c_phi_w1[l], c_phi_w2[l], bias_c)

        wbr = w_branch[l].astype(jnp.bfloat16)
        xf = merge(xf, proj, y_a, y_b, y_c, wbr[:A_WIDTH], wbr[A_WIDTH:A_WIDTH + B_WIDTH],
                   wbr[A_WIDTH + B_WIDTH:], w_out[l].astype(jnp.bfloat16))
    return xf.reshape(b, s, D_MODEL)
```

```python
import functools
import math

import jax
import jax.numpy as jnp
import numpy as np
from jax import lax
from jax.experimental import pallas as pl
from jax.experimental.pallas import tpu as pltpu

D_MODEL = 1024
DEPTH = 4
QBLOCK = 128
NEG_INF = -1e30
FORCE_SCORE = 1e9
RMS_EPS = 1e-6

A_HEADS = 4
A_HEAD_DIM = 64
A_WIDTH = A_HEADS * A_HEAD_DIM
A_KV_LATENT = 128
IDX_HEADS = 8
IDX_DIM = 64
TOPK_MAX = 256

B_HEADS = 4
B_HEAD_DIM = 128
B_WIDTH = B_HEADS * B_HEAD_DIM
CONV_WIDTH = 4
GDN_CHUNK = 64

C_HEADS = 4
C_HEAD_DIM = 64
C_WIDTH = C_HEADS * C_HEAD_DIM
CMP_BLOCK = 32
CMP_STRIDE = 16
SEL_BLOCK = 64
N_SEL = 16
WINDOW = 512
PHI_HIDDEN = 256

N_BUCKETS = 32
BUCKET_MAX_EXACT = 16
BUCKET_MAX_DIST = 128

VMEM_LIMIT = 48 * 1024 * 1024

_ORIG = {}
_o = 0
for _name, _w in (("a_q", 256), ("a_ckv", 128), ("a_qi", 512), ("a_ki", 64), ("a_wi", 8), ("a_z", 256),
                  ("b_qkv", 1536), ("b_a", 4), ("b_b", 4), ("b_z", 512),
                  ("c_q", 256), ("c_kv", 384), ("c_g", 12), ("c_z", 256), ("g", 3072)):
    _ORIG[_name] = (_o, _w)
    _o += _w
N_IN = _o

_NEW_ORDER = ("g", "b_qkv", "b_z", "a_qi", "a_q", "a_z", "c_q", "c_z", "c_kv", "a_ckv",
              "a_ki", "a_wi", "b_a", "b_b", "c_g")
OFF = {}
_o = 0
for _name in _NEW_ORDER:
    OFF[_name] = _o
    _o += _ORIG[_name][1]
MISC_OFF = OFF["a_ki"]
N_PAD = 7296
assert MISC_OFF == 7168 and _o <= N_PAD
A_WI_LANE = OFF["a_wi"] - MISC_OFF
B_A_LANE = OFF["b_a"] - MISC_OFF
B_B_LANE = OFF["b_b"] - MISC_OFF
C_G_LANE = OFF["c_g"] - MISC_OFF


IN_TM = 1024
IN_TN = N_PAD // 3


def _in_proj_kernel(x_ref, g_ref, w_ref, o_ref):
    x = x_ref[...]
    ms = jnp.mean(x * x, axis=-1, keepdims=True)
    h = (x * lax.rsqrt(ms + RMS_EPS)) * g_ref[...]
    o_ref[...] = jnp.dot(h.astype(jnp.bfloat16), w_ref[...], preferred_element_type=jnp.float32)


def in_proj(xf, g, w_bf16):
    n = xf.shape[0]
    return pl.pallas_call(
        _in_proj_kernel,
        grid=(N_PAD // IN_TN, n // IN_TM),
        in_specs=[
            pl.BlockSpec((IN_TM, D_MODEL), lambda j, i: (i, 0)),
            pl.BlockSpec((1, D_MODEL), lambda j, i: (0, 0)),
            pl.BlockSpec((D_MODEL, IN_TN), lambda j, i: (0, j)),
        ],
        out_specs=pl.BlockSpec((IN_TM, IN_TN), lambda j, i: (i, j)),
        out_shape=jax.ShapeDtypeStruct((n, N_PAD), jnp.float32),
        compiler_params=pltpu.CompilerParams(
            dimension_semantics=("arbitrary", "arbitrary"), vmem_limit_bytes=VMEM_LIMIT),
        name="in_proj",
    )(xf, g.reshape(1, D_MODEL), w_bf16)


MG_TM = 512


def _merge_kernel(x_ref, g_ref, ya_ref, yb_ref, yc_ref, wa_ref, wb_ref, wc_ref, wo_ref, o_ref):
    def branch(y_ref, w_ref, k):
        p = jnp.dot(y_ref[...].astype(jnp.bfloat16), w_ref[...], preferred_element_type=jnp.float32)
        return _sigmoid(g_ref[:, k * D_MODEL:(k + 1) * D_MODEL]) * p

    merged = branch(ya_ref, wa_ref, 0) + branch(yb_ref, wb_ref, 1) + branch(yc_ref, wc_ref, 2)
    o_ref[...] = x_ref[...] + jnp.dot(merged.astype(jnp.bfloat16), wo_ref[...],
                                      preferred_element_type=jnp.float32)


def merge(xf, proj, ya, yb, yc, wa, wb, wc, wo):
    n = xf.shape[0]
    row = lambda w: pl.BlockSpec((MG_TM, w), lambda i: (i, 0))
    full = lambda a: pl.BlockSpec(a.shape, lambda i: (0, 0))
    return pl.pallas_call(
        _merge_kernel,
        grid=(n // MG_TM,),
        in_specs=[row(D_MODEL), row(3 * D_MODEL), row(A_WIDTH), row(B_WIDTH), row(C_WIDTH),
                  full(wa), full(wb), full(wc), full(wo)],
        out_specs=row(D_MODEL),
        out_shape=jax.ShapeDtypeStruct((n, D_MODEL), jnp.float32),
        compiler_params=pltpu.CompilerParams(
            dimension_semantics=("arbitrary",), vmem_limit_bytes=VMEM_LIMIT),
        name="merge",
    )(xf, proj, ya, yb, yc, wa, wb, wc, wo)


def _bucket_np(dist):
    d = np.maximum(np.asarray(dist, np.int64), 0)
    ratio = np.log(np.maximum(d, 1).astype(np.float64) / BUCKET_MAX_EXACT) / math.log(BUCKET_MAX_DIST / BUCKET_MAX_EXACT)
    scaled = ratio * (N_BUCKETS - BUCKET_MAX_EXACT)
    frac = scaled - np.floor(scaled)
    edge = (d > BUCKET_MAX_EXACT) & (d < BUCKET_MAX_DIST) & ((frac < 1e-4) | (frac > 1 - 1e-4))
    assert not edge.any(), "bucket boundary too close to an integer distance"
    large = np.minimum(BUCKET_MAX_EXACT + np.floor(scaled + 1e-9).astype(np.int64), N_BUCKETS - 1)
    return np.where(d < BUCKET_MAX_EXACT, d, large).astype(np.int32)


_NEAR_BUCKET_T = _bucket_np(np.arange(QBLOCK)[None, :] + QBLOCK - np.arange(2 * QBLOCK)[:, None])
_FAR_BUCKET = int(_bucket_np(np.array([BUCKET_MAX_DIST]))[0])
assert (_bucket_np(np.arange(BUCKET_MAX_DIST, 4096)) == _FAR_BUCKET).all()

INT_MIN = -2 ** 31


COL_SLAB = 64


def _col_reduce(x, op):
    rows = x.shape[0]
    if rows > COL_SLAB and rows % COL_SLAB == 0:
        x = op(x.reshape(rows // COL_SLAB, COL_SLAB, x.shape[1]), axis=0)
    return op(x, axis=0, keepdims=True)


def _sigmoid(x):
    return 0.5 * jnp.tanh(0.5 * x) + 0.5


def _silu(x):
    return x * _sigmoid(x)


def _bias_lookup(table, buckets):
    idx = jnp.asarray(buckets)[..., None]
    out = jnp.zeros(idx.shape[:-1] + (table.shape[1],), table.dtype)
    for bkt in range(N_BUCKETS):
        out = jnp.where(idx == bkt, table[bkt], out)
    return out


A_PRE_TM = 512


def _a_pre_kernel(ckv_ref, q_ref, gkv_ref, wukv_ref, gq_ref, gk_ref, hm_ref, kn_ref, vt_ref, qn_ref):
    c = ckv_ref[...]
    c = c * lax.rsqrt(jnp.mean(c * c, axis=-1, keepdims=True) + RMS_EPS) * gkv_ref[...]
    kv = jnp.dot(c.astype(jnp.bfloat16), wukv_ref[...], preferred_element_type=jnp.float32)
    hm = hm_ref[...]

    def head_rms(x, g):
        ms = jnp.dot(x * x, hm, precision=lax.Precision.HIGHEST, preferred_element_type=jnp.float32)
        return x * lax.rsqrt(ms + RMS_EPS) * g

    kn_ref[...] = head_rms(kv[:, :A_WIDTH], gk_ref[...])
    qn_ref[...] = head_rms(q_ref[...], gq_ref[...]) * (A_HEAD_DIM ** -0.5)
    vt_ref[0] = kv[:, A_WIDTH:].T


def a_pre(proj, b, s, gkv, wukv_bf16, gq, gk):
    n = b * s
    nt = s // A_PRE_TM
    hm = jnp.asarray(np.kron(np.eye(A_HEADS), np.ones((A_HEAD_DIM, A_HEAD_DIM))) / A_HEAD_DIM, jnp.float32)
    row = lambda bb, j: (bb * nt + j, 0)
    full = lambda a: pl.BlockSpec(a.shape, lambda bb, j: (0,) * a.ndim)
    gq = jnp.tile(gq, A_HEADS).reshape(1, A_WIDTH)
    gk = jnp.tile(gk, A_HEADS).reshape(1, A_WIDTH)
    gkv = gkv.reshape(1, A_KV_LATENT)
    return pl.pallas_call(
        _a_pre_kernel,
        grid=(b, nt),
        in_specs=[
            pl.BlockSpec((A_PRE_TM, A_KV_LATENT), lambda bb, j: (bb * nt + j, OFF["a_ckv"] // A_KV_LATENT)),
            pl.BlockSpec((A_PRE_TM, A_WIDTH), lambda bb, j: (bb * nt + j, OFF["a_q"] // A_WIDTH)),
            full(gkv), full(wukv_bf16), full(gq), full(gk), full(hm),
        ],
        out_specs=[
            pl.BlockSpec((A_PRE_TM, A_WIDTH), row),
            pl.BlockSpec((1, A_WIDTH, A_PRE_TM), lambda bb, j: (bb, 0, j)),
            pl.BlockSpec((A_PRE_TM, A_WIDTH), row),
        ],
        out_shape=[
            jax.ShapeDtypeStruct((n, A_WIDTH), jnp.float32),
            jax.ShapeDtypeStruct((b, A_WIDTH, s), jnp.float32),
            jax.ShapeDtypeStruct((n, A_WIDTH), jnp.float32),
        ],
        compiler_params=pltpu.CompilerParams(
            dimension_semantics=("arbitrary", "arbitrary"), vmem_limit_bytes=VMEM_LIMIT),
        name="a_pre",
    )(proj, proj, gkv, wukv_bf16, gq, gk, hm)


A_QPS = 2


def _a_pair_kernel(far_ref, qn_ref, qi_ref, z_ref, misct_ref, kn_ref, vt_ref, ki_ref, near_ref, o_ref,
                   key_scr, lg_scr, j_scr):
    s = kn_ref.shape[0]
    f32, bf16 = jnp.float32, jnp.bfloat16
    nt = (((1,), (1,)), ((), ()))
    kf = float(TOPK_MAX)
    groups = range(A_QPS)

    def run(step):
        t0 = [(A_QPS * step + g) * QBLOCK for g in groups]
        nr = [t + QBLOCK for t in t0]
        qrows = [slice(g * QBLOCK, (g + 1) * QBLOCK) for g in groups]

        for g in groups:
            qi = qi_ref[qrows[g], :] * (IDX_DIM ** -0.5)
            qstack = jnp.concatenate([qi[:, h * IDX_DIM:(h + 1) * IDX_DIM] for h in range(IDX_HEADS)], axis=0)
            ki = ki_ref[0:nr[g], 0:IDX_DIM]
            sc = lax.dot_general(ki.astype(bf16), qstack.astype(bf16), nt, preferred_element_type=f32)
            wt = misct_ref[0, A_WI_LANE:A_WI_LANE + IDX_HEADS, qrows[g]] * (IDX_HEADS ** -0.5)
            score = jnp.zeros((nr[g], QBLOCK), f32)
            for h in range(IDX_HEADS):
                score = score + wt[h:h + 1, :] * jnp.maximum(sc[:, h * QBLOCK:(h + 1) * QBLOCK], 0.0)
            score = score + 0.0
            kpos = lax.broadcasted_iota(jnp.int32, (nr[g], QBLOCK), 0)
            tpos = t0[g] + lax.broadcasted_iota(jnp.int32, (nr[g], QBLOCK), 1)
            bits = pltpu.bitcast(score, jnp.int32)
            key = jnp.where(bits < 0, bits ^ jnp.int32(0x7FFFFFFF), bits)
            key_scr[g, 0:nr[g], :] = jnp.where(kpos <= tpos, key, jnp.int32(INT_MIN))

        def count_ge(g, cand):
            return _col_reduce(jnp.where(key_scr[g, 0:nr[g], :] >= cand, 1.0, 0.0), jnp.sum)

        def bisect(it, thrs):
            bit = lax.shift_left(jnp.int32(1), 31 - it)
            cnt = [count_ge(g, thrs[g] + bit) for g in groups]
            return tuple(jnp.where(cnt[g] >= kf, thrs[g] + bit, thrs[g]) for g in groups)

        thrs = lax.fori_loop(0, 32, bisect, tuple(jnp.full((1, QBLOCK), INT_MIN, jnp.int32) for _ in groups))

        for g in groups:
            nrows, thr = nr[g], thrs[g]
            n_ge = count_ge(g, thr)
            need = kf - count_ge(g, thr + 1)
            kpos = lax.broadcasted_iota(jnp.int32, (nrows, QBLOCK), 0)

            j_scr[g] = jnp.full((1, QBLOCK), nrows - 1, jnp.int32)
            surplus = jnp.where((n_ge > kf) & (thr > INT_MIN), 1.0, 0.0)

            @pl.when(jnp.max(surplus) > 0.0)
            def _():
                def bisect_idx(it, lohi):
                    lo, hi = lohi
                    mid = lax.shift_right_arithmetic(lo + hi, 1)
                    k = key_scr[g, 0:nrows, :]
                    c = _col_reduce(jnp.where((k == thr) & (kpos <= mid), 1.0, 0.0), jnp.sum)
                    ok = c >= need
                    return jnp.where(ok, lo, mid), jnp.where(ok, mid, hi)

                lo0 = jnp.full((1, QBLOCK), -1, jnp.int32)
                hi0 = jnp.full((1, QBLOCK), nrows - 1, jnp.int32)
                _, hi = lax.fori_loop(0, 11, bisect_idx, (lo0, hi0))
                j_scr[g] = hi

            key = key_scr[g, 0:nrows, :]
            thr_c = jnp.maximum(thr, INT_MIN + 1)
            tie_ok = jnp.where(kpos <= j_scr[g], 0.0, NEG_INF)
            sel_bias = jnp.where(key > thr_c, 0.0, jnp.where(key == thr_c, tie_ok, NEG_INF))

            qn = qn_ref[qrows[g], :]
            lane = lax.broadcasted_iota(jnp.int32, (QBLOCK, A_WIDTH), 1)
            qblk = jnp.concatenate(
                [jnp.where((lane >= h * A_HEAD_DIM) & (lane < (h + 1) * A_HEAD_DIM), qn, 0.0)
                 for h in range(A_HEADS)], axis=0)
            lg_scr[0:QBLOCK, :] = jnp.zeros((QBLOCK, A_HEADS * QBLOCK), f32)
            lg_scr[QBLOCK:QBLOCK + nrows, :] = lax.dot_general(kn_ref[0:nrows, :].astype(bf16), qblk.astype(bf16),
                                                               nt, preferred_element_type=f32)
            outs = []
            for h in range(A_HEADS):
                cols = slice(h * QBLOCK, (h + 1) * QBLOCK)
                lg_scr[t0[g]:t0[g] + 2 * QBLOCK, cols] += near_ref[h] - far_ref[h]
                l = lg_scr[QBLOCK:QBLOCK + nrows, cols] + sel_bias
                m = _col_reduce(l, jnp.max)
                p = jnp.exp(l - m)
                den = _col_reduce(p, jnp.sum)
                vt = vt_ref[0, h * A_HEAD_DIM:(h + 1) * A_HEAD_DIM, 0:nrows]
                o_t = jnp.dot(vt.astype(bf16), p.astype(bf16), preferred_element_type=f32)
                outs.append(o_t / den)
            o = jnp.concatenate(outs, axis=0).T
            o_ref[qrows[g], :] = o * _silu(z_ref[qrows[g], :])

    step = pl.program_id(1)
    for v in range(s // (A_QPS * QBLOCK)):
        pl.when(step == v)(functools.partial(run, v))


def a_main(proj, misct, qn, kn, vt, near_t, far, b, s):
    n = b * s
    tq = A_QPS * QBLOCK
    nq = s // tq
    row = lambda bb, i: bb * nq + i
    return pl.pallas_call(
        _a_pair_kernel,
        grid=(b, nq),
        in_specs=[
            pl.BlockSpec(memory_space=pltpu.SMEM),
            pl.BlockSpec((tq, A_WIDTH), lambda bb, i: (row(bb, i), 0)),
            pl.BlockSpec((tq, IDX_HEADS * IDX_DIM), lambda bb, i: (row(bb, i), OFF["a_qi"] // (IDX_HEADS * IDX_DIM))),
            pl.BlockSpec((tq, A_WIDTH), lambda bb, i: (row(bb, i), OFF["a_z"] // A_WIDTH)),
            pl.BlockSpec((1, 128, tq), lambda bb, i: (bb, 0, i)),
            pl.BlockSpec((s, A_WIDTH), lambda bb, i: (bb, 0)),
            pl.BlockSpec((1, A_WIDTH, s), lambda bb, i: (bb, 0, 0)),
            pl.BlockSpec((s, 128), lambda bb, i: (bb, MISC_OFF // 128)),
            pl.BlockSpec((A_HEADS, 2 * QBLOCK, QBLOCK), lambda bb, i: (0, 0, 0)),
        ],
        out_specs=pl.BlockSpec((tq, A_WIDTH), lambda bb, i: (row(bb, i), 0)),
        out_shape=jax.ShapeDtypeStruct((n, A_WIDTH), jnp.float32),
        scratch_shapes=[
            pltpu.VMEM((A_QPS, s, QBLOCK), jnp.int32),
            pltpu.VMEM((s + QBLOCK, A_HEADS * QBLOCK), jnp.float32),
            pltpu.VMEM((A_QPS, 1, QBLOCK), jnp.int32),
        ],
        compiler_params=pltpu.CompilerParams(
            dimension_semantics=("arbitrary", "arbitrary"), vmem_limit_bytes=VMEM_LIMIT),
        name="a_main",
    )(far, qn, proj, proj, misct, kn, vt, proj, near_t)


def mixer_a(proj, misct, b, s, gkv, wukv, gq, gk, bias_a):
    kn, vt, qn = a_pre(proj, b, s, gkv, wukv.astype(jnp.bfloat16), gq, gk)
    near_t = jnp.transpose(_bias_lookup(bias_a, _NEAR_BUCKET_T), (2, 0, 1))
    far = bias_a[_FAR_BUCKET]
    return a_main(proj, misct, qn, kn, vt, near_t, far, b, s)


N_CMP_PAD = 128
N_SB = 32
WIN_KEYS = WINDOW + QBLOCK
CMP_GROUPS = CMP_BLOCK // CMP_STRIDE

_CMP_BUCKET_T = _bucket_np((np.arange(16)[:, None, None] * QBLOCK + np.arange(QBLOCK)[None, None, :])
                           - (np.arange(N_CMP_PAD)[None, :, None] * CMP_STRIDE + CMP_BLOCK - 1))
_WIN_BUCKET_T = _bucket_np(np.arange(QBLOCK)[None, :] + WINDOW - np.arange(WIN_KEYS)[:, None])
_OVERLAP_T = np.array([[1.0 if (n * CMP_STRIDE < j * SEL_BLOCK + SEL_BLOCK and n * CMP_STRIDE + CMP_BLOCK > j * SEL_BLOCK
                              and n < N_CMP_PAD - 1) else 0.0 for n in range(N_CMP_PAD)] for j in range(N_SB)], np.float32)
_SB_EXPAND = (np.arange(2048)[:, None] // SEL_BLOCK == np.arange(N_SB)[None, :]).astype(np.float32)


def _c_pre_kernel(cmp_ref, sel_ref, win_ref, kn_ref, pos_ref, w1_ref, w2_ref,
                  kvc_ref, ksel_ref, vselt_ref, kwin_ref, vwin_ref):
    f32, bf16 = jnp.float32, jnp.bfloat16
    d = C_HEAD_DIM

    def rms(x, g):
        return x * lax.rsqrt(jnp.mean(x * x, axis=-1, keepdims=True) + RMS_EPS) * g

    acc = [jnp.zeros((N_CMP_PAD, 2 * PHI_HIDDEN), f32) for _ in range(CMP_GROUPS)]
    for j in range(CMP_STRIDE):
        xs = cmp_ref[pl.ds(j, N_CMP_PAD, stride=CMP_STRIDE), :]
        for half in range(CMP_GROUPS):
            jj = half * CMP_STRIDE + j
            acc[half] = acc[half] + jnp.dot((xs + pos_ref[jj:jj + 1, :]).astype(bf16), w1_ref[jj],
                                            preferred_element_type=f32)
    hid = acc[0] + pltpu.roll(acc[1], N_CMP_PAD - 1, axis=0)
    kv = jnp.dot(_silu(hid).astype(bf16), w2_ref[...], preferred_element_type=f32)
    kvc_ref[0, :, 0:d] = rms(kv[:, 0:d], kn_ref[0:1, :])
    kvc_ref[0, :, d:2 * d] = kv[:, d:2 * d]

    sel = sel_ref[...]
    ksel_ref[...] = rms(sel[:, 0:d], kn_ref[1:2, :])
    vselt_ref[0] = sel.T[d:2 * d, :]
    win = win_ref[...]
    kwin_ref[0, 0:WINDOW, :] = jnp.zeros((WINDOW, d), f32)
    vwin_ref[0, 0:WINDOW, :] = jnp.zeros((WINDOW, d), f32)
    kwin_ref[0, WINDOW:, :] = rms(win[:, 0:d], kn_ref[2:3, :])
    vwin_ref[0, WINDOW:, :] = win[:, d:2 * d]


def c_pre(proj, b, s, k_norm, cmp_pos, phi_w1, phi_w2):
    d = C_HEAD_DIM
    kv0 = OFF["c_kv"] // 128
    pos = jnp.concatenate([cmp_pos[0], cmp_pos[1]], axis=-1)
    w1 = phi_w1.reshape(2, CMP_BLOCK, d, PHI_HIDDEN)
    zero = jnp.zeros((CMP_BLOCK, d, PHI_HIDDEN), jnp.float32)
    w1c = jnp.concatenate([jnp.concatenate([w1[0], zero], axis=-1),
                           jnp.concatenate([zero, w1[1]], axis=-1)], axis=1).astype(jnp.bfloat16)
    z2 = jnp.zeros((PHI_HIDDEN, d), jnp.float32)
    w2c = jnp.concatenate([jnp.concatenate([phi_w2[0], z2], axis=-1),
                           jnp.concatenate([z2, phi_w2[1]], axis=-1)], axis=0).astype(jnp.bfloat16)
    full = lambda a: pl.BlockSpec(a.shape, lambda bb: (0,) * a.ndim)
    return pl.pallas_call(
        _c_pre_kernel,
        grid=(b,),
        in_specs=[
            pl.BlockSpec((s, 128), lambda bb: (bb, kv0)),
            pl.BlockSpec((s, 128), lambda bb: (bb, kv0 + 1)),
            pl.BlockSpec((s, 128), lambda bb: (bb, kv0 + 2)),
            full(k_norm), full(pos), full(w1c), full(w2c),
        ],
        out_specs=[
            pl.BlockSpec((1, N_CMP_PAD, 128), lambda bb: (bb, 0, 0)),
            pl.BlockSpec((s, d), lambda bb: (bb, 0)),
            pl.BlockSpec((1, d, s), lambda bb: (bb, 0, 0)),
            pl.BlockSpec((1, s + WINDOW, d), lambda bb: (bb, 0, 0)),
            pl.BlockSpec((1, s + WINDOW, d), lambda bb: (bb, 0, 0)),
        ],
        out_shape=[
            jax.ShapeDtypeStruct((b, N_CMP_PAD, 128), jnp.float32),
            jax.ShapeDtypeStruct((b * s, d), jnp.float32),
            jax.ShapeDtypeStruct((b, d, s), jnp.float32),
            jax.ShapeDtypeStruct((b, s + WINDOW, d), jnp.float32),
            jax.ShapeDtypeStruct((b, s + WINDOW, d), jnp.float32),
        ],
        compiler_params=pltpu.CompilerParams(dimension_semantics=("arbitrary",), vmem_limit_bytes=VMEM_LIMIT),
        name="c_pre",
    )(proj, proj, proj, k_norm, pos, w1c, w2c)


C_QPS = 2


def _c_pair_kernel(cq_ref, z_ref, misct_ref, gq_ref, kvc_ref, ksel_ref, vselt_ref, kwin_ref, vwin_ref,
                   cmptab_ref, near_ref, farrow_ref, wintab_ref, ovt_ref, exp_ref, o_ref, ls_scr):
    s = ksel_ref.shape[0]
    d = C_HEAD_DIM
    hq = C_HEADS * QBLOCK
    f32, bf16 = jnp.float32, jnp.bfloat16
    nt = (((1,), (1,)), ((), ()))
    tn = (((0,), (0,)), ((), ()))

    def softmax_rows(l, valid):
        l = jnp.where(valid, l, NEG_INF)
        m = _col_reduce(l, jnp.max)
        p = jnp.where(valid, jnp.exp(l - m), 0.0)
        den = _col_reduce(p, jnp.sum)
        return p, den

    def block(g, t0):
        nrows = t0 + QBLOCK
        qrows = slice(g * QBLOCK, (g + 1) * QBLOCK)

        cq = cq_ref[qrows, :]
        qs = jnp.concatenate([cq[:, h * d:(h + 1) * d] for h in range(C_HEADS)], axis=0)
        qs = qs * lax.rsqrt(jnp.mean(qs * qs, axis=-1, keepdims=True) + RMS_EPS) * gq_ref[...] * (d ** -0.5)
        qs = qs.astype(bf16)

        kvc = kvc_ref[0]
        lc = lax.dot_general(kvc[:, 0:d].astype(bf16), qs, nt, preferred_element_type=f32) + cmptab_ref[g]
        n_idx = lax.broadcasted_iota(jnp.int32, (N_CMP_PAD, hq), 0)
        t_c = t0 + (lax.broadcasted_iota(jnp.int32, (N_CMP_PAD, hq), 1) & (QBLOCK - 1))
        cmp_valid = n_idx * CMP_STRIDE + (CMP_BLOCK - 1) <= t_c
        pc, den_c = softmax_rows(lc, cmp_valid)
        pc = pc * jnp.where(den_c > 0.0, 1.0 / den_c, 0.0)
        o_cmp = lax.dot_general(kvc[:, d:2 * d].astype(bf16), pc.astype(bf16), tn, preferred_element_type=f32)

        psum = pc[:, 0:QBLOCK]
        for h in range(1, C_HEADS):
            psum = psum + pc[:, h * QBLOCK:(h + 1) * QBLOCK]
        p_hi = psum.astype(bf16)
        p_lo = (psum - p_hi.astype(f32)).astype(bf16)
        ovt = ovt_ref[...]
        imp = jnp.dot(ovt, p_hi, preferred_element_type=f32) + jnp.dot(ovt, p_lo, preferred_element_type=f32)
        j_idx = lax.broadcasted_iota(jnp.int32, (N_SB, QBLOCK), 0)
        t_b = t0 + lax.broadcasted_iota(jnp.int32, (N_SB, QBLOCK), 1)
        cur = lax.shift_right_arithmetic(t_b, 6)
        forced = (j_idx == 0) | (j_idx == cur) | (j_idx == jnp.maximum(cur - 1, 0))
        imp = jnp.where(j_idx * SEL_BLOCK <= t_b, jnp.where(forced, FORCE_SCORE, imp), NEG_INF)
        rank = jnp.zeros((N_SB, QBLOCK), f32)
        for r in range(N_SB):
            row = imp[r:r + 1, :]
            rank = rank + jnp.where((row > imp) | ((row == imp) & (j_idx > r)), 1.0, 0.0)
        picked = jnp.where(rank < float(N_SEL), 1.0, 0.0).astype(bf16)

        kw = kwin_ref[0, t0:t0 + WIN_KEYS, :]
        vw = vwin_ref[0, t0:t0 + WIN_KEYS, :]
        lw = lax.dot_general(kw.astype(bf16), qs, nt, preferred_element_type=f32) + wintab_ref[...]
        r_idx = lax.broadcasted_iota(jnp.int32, (WIN_KEYS, hq), 0)
        dist = (lax.broadcasted_iota(jnp.int32, (WIN_KEYS, hq), 1) & (QBLOCK - 1)) + WINDOW - r_idx
        win_valid = (dist >= 0) & (dist < WINDOW) & (r_idx + t0 >= WINDOW)
        pw, den_w = softmax_rows(lw, win_valid)
        o_win = lax.dot_general(vw.astype(bf16), pw.astype(bf16), tn, preferred_element_type=f32) / den_w

        gate = _sigmoid(misct_ref[0, C_G_LANE:C_G_LANE + 3 * C_HEADS, qrows])

        kpos = lax.broadcasted_iota(jnp.int32, (nrows, QBLOCK), 0)
        tpos = t0 + lax.broadcasted_iota(jnp.int32, (nrows, QBLOCK), 1)
        in_picked = jnp.dot(exp_ref[0:nrows, :], picked, preferred_element_type=f32) > 0.5
        sel_bias = jnp.where(in_picked, jnp.where(kpos <= tpos, 0.0, NEG_INF), NEG_INF)
        ls_scr[0:QBLOCK, :] = jnp.zeros((QBLOCK, hq), f32)
        ls_scr[QBLOCK:QBLOCK + nrows, :] = lax.dot_general(ksel_ref[0:nrows, :].astype(bf16), qs, nt,
                                                           preferred_element_type=f32)
        ls_scr[t0:t0 + 2 * QBLOCK, :] += near_ref[...] - farrow_ref[...]
        outs = []
        for h in range(C_HEADS):
            cols = slice(h * QBLOCK, (h + 1) * QBLOCK)
            l = ls_scr[QBLOCK:QBLOCK + nrows, cols] + sel_bias
            m = _col_reduce(l, jnp.max)
            p = jnp.exp(l - m)
            den = _col_reduce(p, jnp.sum)
            o_sel = jnp.dot(vselt_ref[0, :, 0:nrows].astype(bf16), p.astype(bf16), preferred_element_type=f32) / den
            outs.append(gate[3 * h:3 * h + 1, :] * o_cmp[:, cols] + gate[3 * h + 1:3 * h + 2, :] * o_sel
                        + gate[3 * h + 2:3 * h + 3, :] * o_win[:, cols])
        o_ref[qrows, :] = jnp.concatenate(outs, axis=0).T * _silu(z_ref[qrows, :])

    def run(step):
        for g in range(C_QPS):
            block(g, (C_QPS * step + g) * QBLOCK)

    step = pl.program_id(1)
    for v in range(s // (C_QPS * QBLOCK)):
        pl.when(step == v)(functools.partial(run, v))


def c_main(proj, misct, gq, kvc, ksel, vselt, kwin, vwin, cmptab, near, farrow, wintab, b, s):
    n = b * s
    tq = C_QPS * QBLOCK
    nq = s // tq
    d = C_HEAD_DIM
    hq = C_HEADS * QBLOCK
    row = lambda bb, i: bb * nq + i
    ovt = jnp.asarray(_OVERLAP_T, jnp.bfloat16)
    expand = jnp.asarray(_SB_EXPAND, jnp.bfloat16)
    const = lambda a: pl.BlockSpec(a.shape, lambda bb, i: (0,) * a.ndim)
    return pl.pallas_call(
        _c_pair_kernel,
        grid=(b, nq),
        in_specs=[
            pl.BlockSpec((tq, C_WIDTH), lambda bb, i: (row(bb, i), OFF["c_q"] // C_WIDTH)),
            pl.BlockSpec((tq, C_WIDTH), lambda bb, i: (row(bb, i), OFF["c_z"] // C_WIDTH)),
            pl.BlockSpec((1, 128, tq), lambda bb, i: (bb, 0, i)),
            const(gq),
            pl.BlockSpec((1, N_CMP_PAD, 128), lambda bb, i: (bb, 0, 0)),
            pl.BlockSpec((s, d), lambda bb, i: (bb, 0)),
            pl.BlockSpec((1, d, s), lambda bb, i: (bb, 0, 0)),
            pl.BlockSpec((1, s + WINDOW, d), lambda bb, i: (bb, 0, 0)),
            pl.BlockSpec((1, s + WINDOW, d), lambda bb, i: (bb, 0, 0)),
            pl.BlockSpec((C_QPS, N_CMP_PAD, hq), lambda bb, i: (i, 0, 0)),
            const(near), const(farrow), const(wintab), const(ovt), const(expand),
        ],
        out_specs=pl.BlockSpec((tq, C_WIDTH), lambda bb, i: (row(bb, i), 0)),
        out_shape=jax.ShapeDtypeStruct((n, C_WIDTH), jnp.float32),
        scratch_shapes=[pltpu.VMEM((s + QBLOCK, hq), jnp.float32)],
        compiler_params=pltpu.CompilerParams(
            dimension_semantics=("arbitrary", "arbitrary"), vmem_limit_bytes=VMEM_LIMIT),
        name="c_main",
    )(proj, proj, misct, gq, kvc, ksel, vselt, kwin, vwin, cmptab, near, farrow, wintab, ovt, expand)


def _head_cols(tab):
    return jnp.moveaxis(tab, -1, -2).reshape(*tab.shape[:-2], tab.shape[-1] * tab.shape[-2])


def mixer_c(proj, misct, b, s, gq, k_norm, cmp_pos, phi_w1, phi_w2, bias_c):
    kvc, ksel, vselt, kwin, vwin = c_pre(proj, b, s, k_norm, cmp_pos, phi_w1, phi_w2)
    cmptab = _head_cols(_bias_lookup(bias_c, _CMP_BUCKET_T))
    near = _head_cols(_bias_lookup(bias_c, _NEAR_BUCKET_T))
    wintab = _head_cols(_bias_lookup(bias_c, _WIN_BUCKET_T))
    farrow = jnp.repeat(bias_c[_FAR_BUCKET], QBLOCK).reshape(1, C_HEADS * QBLOCK)
    return c_main(proj, misct, gq.reshape(1, C_HEAD_DIM), kvc, ksel, vselt, kwin, vwin, cmptab, near, farrow,
                  wintab, b, s)


B_HPS = 2
B_GROUP = 4


def _b_kernel(alog_ref, dtb_ref, q_ref, k_ref, v_ref, z_ref, misc_ref, arow_ref, cwq_ref, cwk_ref, cwv_ref,
              gon_ref, o_ref, qs, ks, vs, ws, at, gtok, btok, gcrow):
    s = q_ref.shape[0]
    dh = B_HEAD_DIM
    c_len = GDN_CHUNK
    j = pl.program_id(1)
    f32, bf16 = jnp.float32, jnp.bfloat16
    hi = lax.Precision.HIGHEST
    nt = (((1,), (1,)), ((), ()))
    tn = (((0,), (0,)), ((), ()))

    head_rows = lax.broadcasted_iota(jnp.int32, (8, B_HPS * dh), 0)

    def conv_silu(x_ref, w_ref):
        x8 = x_ref[0:8, :]
        w_last = w_ref[CONV_WIDTH - 1:CONV_WIDTH, :]
        head = x8 * w_last
        body = x_ref[8:s, :] * w_last
        for k in range(1, CONV_WIDTH):
            wk = w_ref[CONV_WIDTH - 1 - k:CONV_WIDTH - k, :]
            head = head + jnp.where(head_rows >= k, pltpu.roll(x8, k, axis=0), 0.0) * wk
            body = body + x_ref[8 - k:s - k, :] * wk
        return _silu(jnp.concatenate([head, body], axis=0))

    def l2n(x):
        return x * lax.rsqrt(jnp.sum(x * x, axis=-1, keepdims=True) + RMS_EPS)

    qc = conv_silu(q_ref, cwq_ref)
    kc = conv_silu(k_ref, cwk_ref)
    vs[...] = conv_silu(v_ref, cwv_ref)
    misc = misc_ref[...]
    lane = lax.broadcasted_iota(jnp.int32, misc.shape, 1)
    ri = lax.broadcasted_iota(jnp.int32, (c_len, c_len), 0)
    ci = lax.broadcasted_iota(jnp.int32, (c_len, c_len), 1)
    lower = ci <= ri
    strict = ci < ri
    tri_u = jnp.where(ri <= ci, 1.0, 0.0)
    row_in_chunk = lax.broadcasted_iota(jnp.int32, (s, dh), 0) & (c_len - 1)
    for hh in range(B_HPS):
        sl = slice(hh * dh, (hh + 1) * dh)
        qs[:, sl] = l2n(qc[:, sl]) * (dh ** -0.5)
        ks[:, sl] = l2n(kc[:, sl])
        h = B_HPS * j + hh
        neg_a = -jnp.exp(alog_ref[h])
        dtb = dtb_ref[h]
        a_tok = jnp.sum(jnp.where(lane == B_A_LANE + h, misc, 0.0), axis=-1, keepdims=True)
        b_tok = jnp.sum(jnp.where(lane == B_B_LANE + h, misc, 0.0), axis=-1, keepdims=True)
        g = jnp.broadcast_to(neg_a * jax.nn.softplus(a_tok + dtb), (s, dh))
        for sh in (1, 2, 4, 8, 16, 32):
            g = g + jnp.where(row_in_chunk >= sh, pltpu.roll(g, sh, axis=0), 0.0)
        gtok[hh] = g
        btok[hh] = jnp.broadcast_to(_sigmoid(b_tok), (s, dh))
        g_row = neg_a * jax.nn.softplus(arow_ref[0, hh] + dtb)
        gcrow[hh] = jnp.dot(g_row, tri_u, precision=hi, preferred_element_type=f32)

    lane3 = lax.broadcasted_iota(jnp.int32, (c_len, 3 * c_len), 1)
    lo_lanes = (lane3 >= c_len) & (lane3 < 2 * c_len)

    def split_lhs(p):
        p4 = jnp.concatenate([p, p, p], axis=1)
        hi4 = p4.astype(bf16).astype(f32)
        return jnp.where(lo_lanes, p4 - hi4, hi4).astype(bf16)

    def split_rhs(x):
        xh = x.astype(bf16)
        xl = (x - xh.astype(f32)).astype(bf16)
        return jnp.concatenate([xh, xh, xl], axis=0)

    def mm(p_split, x):
        return jnp.dot(p_split, split_rhs(x), preferred_element_type=f32)

    hs = range(B_HPS)
    sls = [slice(hh * dh, (hh + 1) * dh) for hh in hs]
    n_groups = s // (B_GROUP * c_len)

    def chunk_rows(c):
        return pl.ds(pl.multiple_of(c * c_len, c_len), c_len)

    def intra_group(i, between):
        where, q_c, k_c, v_c, gb, beta, g_row = [], [], [], [], [], [], []
        for c in [B_GROUP * i + cc for cc in range(B_GROUP)]:
            rs = chunk_rows(c)
            for hh in hs:
                where.append((rs, hh))
                q_c.append(qs[rs, sls[hh]])
                k_c.append(ks[rs, sls[hh]])
                v_c.append(vs[rs, sls[hh]])
                gb.append(gtok[hh, rs, :])
                beta.append(btok[hh, rs, :])
                g_row.append(gcrow[hh, pl.ds(c, 1), :])
        nch = range(len(where))
        decay = [jnp.where(lower, jnp.exp(jnp.where(lower, gb[n][:, 0:c_len] - g_row[n], 0.0)), 0.0) for n in nch]
        kb = [k_c[n] * beta[n] for n in nch]
        kk = [lax.dot_general(kb[n].astype(bf16), k_c[n].astype(bf16), nt, preferred_element_type=f32) for n in nch]
        between()
        p = [-jnp.where(strict, kk[n] * decay[n], 0.0) for n in nch]
        eg = [jnp.exp(gb[n]) for n in nch]
        x = [jnp.concatenate([v_c[n] * beta[n], kb[n] * eg[n]], axis=1) for n in nch]
        ps = [split_lhs(p[n]) for n in nch]
        x = [x[n] + mm(ps[n], x[n]) for n in nch]
        between()
        for _ in range(5):
            p = [mm(ps[n], p[n]) for n in nch]
            between()
            ps = [split_lhs(p[n]) for n in nch]
            x = [x[n] + mm(ps[n], x[n]) for n in nch]
            between()
        attn = [jnp.where(lower, lax.dot_general(q_c[n].astype(bf16), k_c[n].astype(bf16), nt,
                                                 preferred_element_type=f32) * decay[n], 0.0) for n in nch]
        between()

        def commit():
            for n, (rs, hh) in enumerate(where):
                g_last = g_row[n][:, c_len - 1:c_len]
                vs[rs, sls[hh]] = x[n][:, 0:dh]
                ws[rs, sls[hh]] = x[n][:, dh:2 * dh]
                qs[rs, sls[hh]] = q_c[n] * eg[n]
                ks[rs, sls[hh]] = k_c[n] * jnp.exp(g_last - gb[n])
                at[rs, hh * c_len:(hh + 1) * c_len] = attn[n]

        return commit

    def recur_steps(group, states, result):
        for c in [B_GROUP * group + cc for cc in range(B_GROUP)]:
            rs = chunk_rows(c)
            st_b = [states[hh].astype(bf16) for hh in hs]
            v_new = [vs[rs, sls[hh]] - jnp.dot(ws[rs, sls[hh]].astype(bf16), st_b[hh], preferred_element_type=f32)
                     for hh in hs]
            o_st = [jnp.dot(qs[rs, sls[hh]].astype(bf16), st_b[hh], preferred_element_type=f32) for hh in hs]
            yield
            v_nb = [v_new[hh].astype(bf16) for hh in hs]
            states = tuple(states[hh] * jnp.exp(gcrow[hh, pl.ds(c, 1), c_len - 1:c_len])
                           + lax.dot_general(ks[rs, sls[hh]].astype(bf16), v_nb[hh], tn, preferred_element_type=f32)
                           for hh in hs)
            for hh in hs:
                o_ref[rs, sls[hh]] = o_st[hh] + jnp.dot(at[rs, hh * c_len:(hh + 1) * c_len].astype(bf16), v_nb[hh],
                                                        preferred_element_type=f32)
            yield
        result.append(states)

    def overlapped(g, states):
        result = []
        steps = recur_steps(g - 1, states, result)
        commit = intra_group(g, lambda: next(steps, None))
        for _ in steps:
            pass
        commit()
        return result[0]

    intra_group(0, lambda: None)()
    states = lax.fori_loop(1, n_groups, overlapped, tuple(jnp.zeros((dh, dh), f32) for _ in hs))
    for _ in recur_steps(n_groups - 1, states, []):
        pass

    for hh in range(B_HPS):
        sl = slice(hh * dh, (hh + 1) * dh)
        o = o_ref[:, sl]
        o = o * lax.rsqrt(jnp.mean(o * o, axis=-1, keepdims=True) + RMS_EPS) * gon_ref[...]
        o_ref[:, sl] = o * _silu(z_ref[:, sl])


def mixer_b(proj, misct, b, s, conv_w, a_log, dt_bias, out_norm):
    n = b * s
    w2 = B_HPS * B_HEAD_DIM
    nc = s // GDN_CHUNK
    arow = misct[:, B_A_LANE:B_A_LANE + B_HEADS, :].reshape(b, B_HEADS, nc, GDN_CHUNK)
    qkv0 = OFF["b_qkv"] // w2
    kstep = B_WIDTH // w2
    smem = pl.BlockSpec(memory_space=pltpu.SMEM)
    return pl.pallas_call(
        _b_kernel,
        grid=(b, B_HEADS // B_HPS),
        in_specs=[
            smem, smem,
            pl.BlockSpec((s, w2), lambda bb, j: (bb, qkv0 + j)),
            pl.BlockSpec((s, w2), lambda bb, j: (bb, qkv0 + kstep + j)),
            pl.BlockSpec((s, w2), lambda bb, j: (bb, qkv0 + 2 * kstep + j)),
            pl.BlockSpec((s, w2), lambda bb, j: (bb, OFF["b_z"] // w2 + j)),
            pl.BlockSpec((s, 128), lambda bb, j: (bb, MISC_OFF // 128)),
            pl.BlockSpec((1, B_HPS, nc, GDN_CHUNK), lambda bb, j: (bb, j, 0, 0)),
            pl.BlockSpec((CONV_WIDTH, w2), lambda bb, j: (0, j)),
            pl.BlockSpec((CONV_WIDTH, w2), lambda bb, j: (0, kstep + j)),
            pl.BlockSpec((CONV_WIDTH, w2), lambda bb, j: (0, 2 * kstep + j)),
            pl.BlockSpec((1, B_HEAD_DIM), lambda bb, j: (0, 0)),
        ],
        out_specs=pl.BlockSpec((s, w2), lambda bb, j: (bb, j)),
        out_shape=jax.ShapeDtypeStruct((n, B_WIDTH), jnp.float32),
        scratch_shapes=[
            pltpu.VMEM((s, w2), jnp.float32), pltpu.VMEM((s, w2), jnp.float32), pltpu.VMEM((s, w2), jnp.float32),
            pltpu.VMEM((s, w2), jnp.float32), pltpu.VMEM((s, B_HPS * GDN_CHUNK), jnp.float32),
            pltpu.VMEM((B_HPS, s, B_HEAD_DIM), jnp.float32), pltpu.VMEM((B_HPS, s, B_HEAD_DIM), jnp.float32),
            pltpu.VMEM((B_HPS, nc, GDN_CHUNK), jnp.float32),
        ],
        compiler_params=pltpu.CompilerParams(
            dimension_semantics=("arbitrary", "arbitrary"), vmem_limit_bytes=VMEM_LIMIT),
        name="gdn",
    )(a_log, dt_bias, proj, proj, proj, proj, proj, arow, conv_w, conv_w, conv_w, out_norm.reshape(1, B_HEAD_DIM))


def kernel(x, norm_g, w_in, a_kv_norm, a_w_ukv, a_q_norm, a_k_norm, b_conv, b_a_log, b_dt_bias, b_out_norm, c_q_norm, c_k_norm, c_cmp_pos, c_phi_w1, c_phi_w2, w_branch, w_out, rel_bias):
    b, s, _ = x.shape
    n = b * s
    bias_a = rel_bias[:, :A_HEADS]
    bias_c = rel_bias[:, A_HEADS:]
    xf = x.reshape(n, D_MODEL)
    fields = [w_in[:, :, _ORIG[name][0]:_ORIG[name][0] + _ORIG[name][1]] for name in _NEW_ORDER]
    pad = jnp.zeros((DEPTH, D_MODEL, N_PAD - N_IN), w_in.dtype)
    w_in_p = jnp.concatenate(fields + [pad], axis=-1).astype(jnp.bfloat16)
    for l in range(DEPTH):
        proj = in_proj(xf, norm_g[l], w_in_p[l])
        p3 = proj.reshape(b, s, N_PAD)

        misct = jnp.transpose(p3[..., MISC_OFF:], (0, 2, 1))
        y_a = mixer_a(proj, misct, b, s, a_kv_norm[l], a_w_ukv[l], a_q_norm[l], a_k_norm[l], bias_a)

        y_b = mixer_b(proj, misct, b, s, b_conv[l], b_a_log[l], b_dt_bias[l], b_out_norm[l])
        y_c = mixer_c(proj, misct, b, s, c_q_norm[l], c_k_norm[l], c_cmp_pos[l], c_phi_w1[l], c_phi_w2[l], bias_c)

        wbr = w_branch[l].astype(jnp.bfloat16)
        xf = merge(xf, proj, y_a, y_b, y_c, wbr[:A_WIDTH], wbr[A_WIDTH:A_WIDTH + B_WIDTH],
                   wbr[A_WIDTH + B_WIDTH:], w_out[l].astype(jnp.bfloat16))
    return xf.reshape(b, s, D_MODEL)
```

```python
import functools
import math

import jax
import jax.numpy as jnp
import numpy as np
from jax import lax
from jax.experimental import pallas as pl
from jax.experimental.pallas import tpu as pltpu

D_MODEL = 1024
DEPTH = 4
QBLOCK = 128
NEG_INF = -1e30
FORCE_SCORE = 1e9
RMS_EPS = 1e-6

A_HEADS = 4
A_HEAD_DIM = 64
A_WIDTH = A_HEADS * A_HEAD_DIM
A_KV_LATENT = 128
IDX_HEADS = 8
IDX_DIM = 64
TOPK_MAX = 256

B_HEADS = 4
B_HEAD_DIM = 128
B_WIDTH = B_HEADS * B_HEAD_DIM
CONV_WIDTH = 4
GDN_CHUNK = 64

C_HEADS = 4
C_HEAD_DIM = 64
C_WIDTH = C_HEADS * C_HEAD_DIM
CMP_BLOCK = 32
CMP_STRIDE = 16
SEL_BLOCK = 64
N_SEL = 16
WINDOW = 512
PHI_HIDDEN = 256

N_BUCKETS = 32
BUCKET_MAX_EXACT = 16
BUCKET_MAX_DIST = 128

VMEM_LIMIT = 48 * 1024 * 1024
SUBLANES = 8
KEY_BITS = 32

_ORIG = {}
_o = 0
for _name, _w in (("a_q", 256), ("a_ckv", 128), ("a_qi", 512), ("a_ki", 64), ("a_wi", 8), ("a_z", 256),
                  ("b_qkv", 1536), ("b_a", 4), ("b_b", 4), ("b_z", 512),
                  ("c_q", 256), ("c_kv", 384), ("c_g", 12), ("c_z", 256), ("g", 3072)):
    _ORIG[_name] = (_o, _w)
    _o += _w
N_IN = _o

_NEW_ORDER = ("g", "b_qkv", "b_z", "a_qi", "a_q", "a_z", "c_q", "c_z", "c_kv", "a_ckv",
              "a_ki", "a_wi", "b_a", "b_b", "c_g")
OFF = {}
_o = 0
for _name in _NEW_ORDER:
    OFF[_name] = _o
    _o += _ORIG[_name][1]
MISC_OFF = OFF["a_ki"]
N_PAD = 7296
assert MISC_OFF == 7168 and _o <= N_PAD
A_WI_LANE = OFF["a_wi"] - MISC_OFF
B_A_LANE = OFF["b_a"] - MISC_OFF
B_B_LANE = OFF["b_b"] - MISC_OFF
C_G_LANE = OFF["c_g"] - MISC_OFF


IN_TM = 1024
IN_TN = N_PAD // 3


def _in_proj_kernel(x_ref, g_ref, w_ref, o_ref):
    x = x_ref[...]
    ms = jnp.mean(x * x, axis=-1, keepdims=True)
    h = (x * lax.rsqrt(ms + RMS_EPS)) * g_ref[...]
    o_ref[...] = jnp.dot(h.astype(jnp.bfloat16), w_ref[...], preferred_element_type=jnp.float32)


def in_proj(xf, g, w_bf16):
    n = xf.shape[0]
    return pl.pallas_call(
        _in_proj_kernel,
        grid=(N_PAD // IN_TN, n // IN_TM),
        in_specs=[
            pl.BlockSpec((IN_TM, D_MODEL), lambda j, i: (i, 0)),
            pl.BlockSpec((1, D_MODEL), lambda j, i: (0, 0)),
            pl.BlockSpec((D_MODEL, IN_TN), lambda j, i: (0, j)),
        ],
        out_specs=pl.BlockSpec((IN_TM, IN_TN), lambda j, i: (i, j)),
        out_shape=jax.ShapeDtypeStruct((n, N_PAD), jnp.float32),
        compiler_params=pltpu.CompilerParams(
            dimension_semantics=("arbitrary", "arbitrary"), vmem_limit_bytes=VMEM_LIMIT),
        name="in_proj",
    )(xf, g.reshape(1, D_MODEL), w_bf16)


MG_TM = 512


def _merge_kernel(x_ref, g_ref, ya_ref, yb_ref, yc_ref, wa_ref, wb_ref, wc_ref, wo_ref, o_ref):
    def branch(y_ref, w_ref, k):
        p = jnp.dot(y_ref[...].astype(jnp.bfloat16), w_ref[...], preferred_element_type=jnp.float32)
        return _sigmoid(g_ref[:, k * D_MODEL:(k + 1) * D_MODEL]) * p

    merged = branch(ya_ref, wa_ref, 0) + branch(yb_ref, wb_ref, 1) + branch(yc_ref, wc_ref, 2)
    o_ref[...] = x_ref[...] + jnp.dot(merged.astype(jnp.bfloat16), wo_ref[...],
                                      preferred_element_type=jnp.float32)


def merge(xf, proj, ya, yb, yc, wa, wb, wc, wo):
    n = xf.shape[0]
    row = lambda w: pl.BlockSpec((MG_TM, w), lambda i: (i, 0))
    full = lambda a: pl.BlockSpec(a.shape, lambda i: (0, 0))
    return pl.pallas_call(
        _merge_kernel,
        grid=(n // MG_TM,),
        in_specs=[row(D_MODEL), row(3 * D_MODEL), row(A_WIDTH), row(B_WIDTH), row(C_WIDTH),
                  full(wa), full(wb), full(wc), full(wo)],
        out_specs=row(D_MODEL),
        out_shape=jax.ShapeDtypeStruct((n, D_MODEL), jnp.float32),
        compiler_params=pltpu.CompilerParams(
            dimension_semantics=("arbitrary",), vmem_limit_bytes=VMEM_LIMIT),
        name="merge",
    )(xf, proj, ya, yb, yc, wa, wb, wc, wo)


def _bucket_np(dist):
    d = np.maximum(np.asarray(dist, np.int64), 0)
    ratio = np.log(np.maximum(d, 1).astype(np.float64) / BUCKET_MAX_EXACT) / math.log(BUCKET_MAX_DIST / BUCKET_MAX_EXACT)
    scaled = ratio * (N_BUCKETS - BUCKET_MAX_EXACT)
    frac = scaled - np.floor(scaled)
    edge = (d > BUCKET_MAX_EXACT) & (d < BUCKET_MAX_DIST) & ((frac < 1e-4) | (frac > 1 - 1e-4))
    assert not edge.any(), "bucket boundary too close to an integer distance"
    large = np.minimum(BUCKET_MAX_EXACT + np.floor(scaled + 1e-9).astype(np.int64), N_BUCKETS - 1)
    return np.where(d < BUCKET_MAX_EXACT, d, large).astype(np.int32)


_NEAR_BUCKET_T = _bucket_np(np.arange(QBLOCK)[None, :] + QBLOCK - np.arange(2 * QBLOCK)[:, None])
_FAR_BUCKET = int(_bucket_np(np.array([BUCKET_MAX_DIST]))[0])
assert (_bucket_np(np.arange(BUCKET_MAX_DIST, 4096)) == _FAR_BUCKET).all()

INT_MIN = -2 ** 31


COL_SLAB = 64


def _col_reduce(x, op):
    rows = x.shape[0]
    if rows > COL_SLAB and rows % COL_SLAB == 0:
        x = op(x.reshape(rows // COL_SLAB, COL_SLAB, x.shape[1]), axis=0)
    return op(x, axis=0, keepdims=True)


def _sigmoid(x):
    return 0.5 * jnp.tanh(0.5 * x) + 0.5


def _silu(x):
    return x * _sigmoid(x)


def _bias_lookup(table, buckets):
    idx = jnp.asarray(buckets)[..., None]
    out = jnp.zeros(idx.shape[:-1] + (table.shape[1],), table.dtype)
    for bkt in range(N_BUCKETS):
        out = jnp.where(idx == bkt, table[bkt], out)
    return out


A_PRE_TM = 512


def _a_pre_kernel(ckv_ref, q_ref, gkv_ref, wukv_ref, gq_ref, gk_ref, hm_ref, kn_ref, vt_ref, qn_ref):
    c = ckv_ref[...]
    c = c * lax.rsqrt(jnp.mean(c * c, axis=-1, keepdims=True) + RMS_EPS) * gkv_ref[...]
    kv = jnp.dot(c.astype(jnp.bfloat16), wukv_ref[...], preferred_element_type=jnp.float32)
    hm = hm_ref[...]

    def head_rms(x, g):
        ms = jnp.dot(x * x, hm, precision=lax.Precision.HIGHEST, preferred_element_type=jnp.float32)
        return x * lax.rsqrt(ms + RMS_EPS) * g

    kn_ref[...] = head_rms(kv[:, :A_WIDTH], gk_ref[...])
    qn_ref[...] = head_rms(q_ref[...], gq_ref[...]) * (A_HEAD_DIM ** -0.5)
    vt_ref[0] = kv[:, A_WIDTH:].T


def a_pre(proj, b, s, gkv, wukv_bf16, gq, gk):
    n = b * s
    nt = s // A_PRE_TM
    hm = jnp.asarray(np.kron(np.eye(A_HEADS), np.ones((A_HEAD_DIM, A_HEAD_DIM))) / A_HEAD_DIM, jnp.float32)
    row = lambda bb, j: (bb * nt + j, 0)
    full = lambda a: pl.BlockSpec(a.shape, lambda bb, j: (0,) * a.ndim)
    gq = jnp.tile(gq, A_HEADS).reshape(1, A_WIDTH)
    gk = jnp.tile(gk, A_HEADS).reshape(1, A_WIDTH)
    gkv = gkv.reshape(1, A_KV_LATENT)
    return pl.pallas_call(
        _a_pre_kernel,
        grid=(b, nt),
        in_specs=[
            pl.BlockSpec((A_PRE_TM, A_KV_LATENT), lambda bb, j: (bb * nt + j, OFF["a_ckv"] // A_KV_LATENT)),
            pl.BlockSpec((A_PRE_TM, A_WIDTH), lambda bb, j: (bb * nt + j, OFF["a_q"] // A_WIDTH)),
            full(gkv), full(wukv_bf16), full(gq), full(gk), full(hm),
        ],
        out_specs=[
            pl.BlockSpec((A_PRE_TM, A_WIDTH), row),
            pl.BlockSpec((1, A_WIDTH, A_PRE_TM), lambda bb, j: (bb, 0, j)),
            pl.BlockSpec((A_PRE_TM, A_WIDTH), row),
        ],
        out_shape=[
            jax.ShapeDtypeStruct((n, A_WIDTH), jnp.float32),
            jax.ShapeDtypeStruct((b, A_WIDTH, s), jnp.float32),
            jax.ShapeDtypeStruct((n, A_WIDTH), jnp.float32),
        ],
        compiler_params=pltpu.CompilerParams(
            dimension_semantics=("arbitrary", "arbitrary"), vmem_limit_bytes=VMEM_LIMIT),
        name="a_pre",
    )(proj, proj, gkv, wukv_bf16, gq, gk, hm)


A_QPS = 4


def _a_pair_kernel(far_ref, qn_ref, qi_ref, z_ref, misct_ref, kn_ref, vt_ref, ki_ref, near_ref, o_ref,
                   key_scr, lg_scr, j_scr):
    s = kn_ref.shape[0]
    f32, bf16 = jnp.float32, jnp.bfloat16
    nt = (((1,), (1,)), ((), ()))
    kf = float(TOPK_MAX)
    groups = range(A_QPS)

    def run(step):
        t0 = [(A_QPS * step + g) * QBLOCK for g in groups]
        nr = [t + QBLOCK for t in t0]
        qrows = [slice(g * QBLOCK, (g + 1) * QBLOCK) for g in groups]

        for g in groups:
            qi = qi_ref[qrows[g], :] * (IDX_DIM ** -0.5)
            qstack = jnp.concatenate([qi[:, h * IDX_DIM:(h + 1) * IDX_DIM] for h in range(IDX_HEADS)], axis=0)
            ki = ki_ref[0:nr[g], 0:IDX_DIM]
            sc = lax.dot_general(ki.astype(bf16), qstack.astype(bf16), nt, preferred_element_type=f32)
            wt = misct_ref[0, A_WI_LANE:A_WI_LANE + IDX_HEADS, qrows[g]] * (IDX_HEADS ** -0.5)
            score = jnp.zeros((nr[g], QBLOCK), f32)
            for h in range(IDX_HEADS):
                score = score + wt[h:h + 1, :] * jnp.maximum(sc[:, h * QBLOCK:(h + 1) * QBLOCK], 0.0)
            score = score + 0.0
            kpos = lax.broadcasted_iota(jnp.int32, (nr[g], QBLOCK), 0)
            tpos = t0[g] + lax.broadcasted_iota(jnp.int32, (nr[g], QBLOCK), 1)
            bits = pltpu.bitcast(score, jnp.int32)
            key = jnp.where(bits < 0, bits ^ jnp.int32(0x7FFFFFFF), bits)
            key_scr[g, 0:nr[g], :] = jnp.where(kpos <= tpos, key, jnp.int32(INT_MIN))

        def count_ge(g, cand):
            return _col_reduce(jnp.where(key_scr[g, 0:nr[g], :] >= cand, 1.0, 0.0), jnp.sum)

        def bisect(it, thrs):
            bit = lax.shift_left(jnp.int32(1), KEY_BITS - 1 - it)
            cnt = [count_ge(g, thrs[g] + bit) for g in groups]
            return tuple(jnp.where(cnt[g] >= kf, thrs[g] + bit, thrs[g]) for g in groups)

        thrs = lax.fori_loop(0, KEY_BITS, bisect, tuple(jnp.full((1, QBLOCK), INT_MIN, jnp.int32) for _ in groups))

        for g in groups:
            nrows, thr = nr[g], thrs[g]
            n_ge = count_ge(g, thr)
            need = kf - count_ge(g, thr + 1)
            kpos = lax.broadcasted_iota(jnp.int32, (nrows, QBLOCK), 0)

            j_scr[g] = jnp.full((1, QBLOCK), nrows - 1, jnp.int32)
            surplus = jnp.where((n_ge > kf) & (thr > INT_MIN), 1.0, 0.0)

            @pl.when(jnp.max(surplus) > 0.0)
            def _():
                def bisect_idx(it, lohi):
                    lo, hi = lohi
                    mid = lax.shift_right_arithmetic(lo + hi, 1)
                    k = key_scr[g, 0:nrows, :]
                    c = _col_reduce(jnp.where((k == thr) & (kpos <= mid), 1.0, 0.0), jnp.sum)
                    ok = c >= need
                    return jnp.where(ok, lo, mid), jnp.where(ok, mid, hi)

                lo0 = jnp.full((1, QBLOCK), -1, jnp.int32)
                hi0 = jnp.full((1, QBLOCK), nrows - 1, jnp.int32)
                _, hi = lax.fori_loop(0, (nrows - 1).bit_length(), bisect_idx, (lo0, hi0))
                j_scr[g] = hi

            key = key_scr[g, 0:nrows, :]
            thr_c = jnp.maximum(thr, INT_MIN + 1)
            tie_ok = jnp.where(kpos <= j_scr[g], 0.0, NEG_INF)
            sel_bias = jnp.where(key > thr_c, 0.0, jnp.where(key == thr_c, tie_ok, NEG_INF))

            qn = qn_ref[qrows[g], :]
            lane = lax.broadcasted_iota(jnp.int32, (QBLOCK, A_WIDTH), 1)
            qblk = jnp.concatenate(
                [jnp.where((lane >= h * A_HEAD_DIM) & (lane < (h + 1) * A_HEAD_DIM), qn, 0.0)
                 for h in range(A_HEADS)], axis=0)
            lg_scr[0:QBLOCK, :] = jnp.zeros((QBLOCK, A_HEADS * QBLOCK), f32)
            lg_scr[QBLOCK:QBLOCK + nrows, :] = lax.dot_general(kn_ref[0:nrows, :].astype(bf16), qblk.astype(bf16),
                                                               nt, preferred_element_type=f32)
            outs = []
            for h in range(A_HEADS):
                cols = slice(h * QBLOCK, (h + 1) * QBLOCK)
                lg_scr[t0[g]:t0[g] + 2 * QBLOCK, cols] += near_ref[h] - far_ref[h]
                l = lg_scr[QBLOCK:QBLOCK + nrows, cols] + sel_bias
                m = _col_reduce(l, jnp.max)
                p = jnp.exp(l - m)
                den = _col_reduce(p, jnp.sum)
                vt = vt_ref[0, h * A_HEAD_DIM:(h + 1) * A_HEAD_DIM, 0:nrows]
                o_t = jnp.dot(vt.astype(bf16), p.astype(bf16), preferred_element_type=f32)
                outs.append(o_t / den)
            o = jnp.concatenate(outs, axis=0).T
            o_ref[qrows[g], :] = o * _silu(z_ref[qrows[g], :])

    step = pl.program_id(1)
    for v in range(s // (A_QPS * QBLOCK)):
        pl.when(step == v)(functools.partial(run, v))


def a_main(proj, misct, qn, kn, vt, near_t, far, b, s):
    n = b * s
    tq = A_QPS * QBLOCK
    nq = s // tq
    row = lambda bb, i: bb * nq + i
    return pl.pallas_call(
        _a_pair_kernel,
        grid=(b, nq),
        in_specs=[
            pl.BlockSpec(memory_space=pltpu.SMEM),
            pl.BlockSpec((tq, A_WIDTH), lambda bb, i: (row(bb, i), 0)),
            pl.BlockSpec((tq, IDX_HEADS * IDX_DIM), lambda bb, i: (row(bb, i), OFF["a_qi"] // (IDX_HEADS * IDX_DIM))),
            pl.BlockSpec((tq, A_WIDTH), lambda bb, i: (row(bb, i), OFF["a_z"] // A_WIDTH)),
            pl.BlockSpec((1, 128, tq), lambda bb, i: (bb, 0, i)),
            pl.BlockSpec((s, A_WIDTH), lambda bb, i: (bb, 0)),
            pl.BlockSpec((1, A_WIDTH, s), lambda bb, i: (bb, 0, 0)),
            pl.BlockSpec((s, 128), lambda bb, i: (bb, MISC_OFF // 128)),
            pl.BlockSpec((A_HEADS, 2 * QBLOCK, QBLOCK), lambda bb, i: (0, 0, 0)),
        ],
        out_specs=pl.BlockSpec((tq, A_WIDTH), lambda bb, i: (row(bb, i), 0)),
        out_shape=jax.ShapeDtypeStruct((n, A_WIDTH), jnp.float32),
        scratch_shapes=[
            pltpu.VMEM((A_QPS, s, QBLOCK), jnp.int32),
            pltpu.VMEM((s + QBLOCK, A_HEADS * QBLOCK), jnp.float32),
            pltpu.VMEM((A_QPS, 1, QBLOCK), jnp.int32),
        ],
        compiler_params=pltpu.CompilerParams(
            dimension_semantics=("arbitrary", "arbitrary"), vmem_limit_bytes=VMEM_LIMIT),
        name="a_main",
    )(far, qn, proj, proj, misct, kn, vt, proj, near_t)


def mixer_a(proj, misct, b, s, gkv, wukv, gq, gk, bias_a):
    kn, vt, qn = a_pre(proj, b, s, gkv, wukv.astype(jnp.bfloat16), gq, gk)
    near_t = jnp.transpose(_bias_lookup(bias_a, _NEAR_BUCKET_T), (2, 0, 1))
    far = bias_a[_FAR_BUCKET]
    return a_main(proj, misct, qn, kn, vt, near_t, far, b, s)


N_CMP_PAD = 128
N_SB = 32
WIN_KEYS = WINDOW + QBLOCK
CMP_GROUPS = CMP_BLOCK // CMP_STRIDE

_CMP_BUCKET_T = _bucket_np((np.arange(16)[:, None, None] * QBLOCK + np.arange(QBLOCK)[None, None, :])
                           - (np.arange(N_CMP_PAD)[None, :, None] * CMP_STRIDE + CMP_BLOCK - 1))
_WIN_BUCKET_T = _bucket_np(np.arange(QBLOCK)[None, :] + WINDOW - np.arange(WIN_KEYS)[:, None])
_OVERLAP_T = np.array([[1.0 if (n * CMP_STRIDE < j * SEL_BLOCK + SEL_BLOCK and n * CMP_STRIDE + CMP_BLOCK > j * SEL_BLOCK
                              and n < N_CMP_PAD - 1) else 0.0 for n in range(N_CMP_PAD)] for j in range(N_SB)], np.float32)
_SB_EXPAND = (np.arange(2048)[:, None] // SEL_BLOCK == np.arange(N_SB)[None, :]).astype(np.float32)


def _c_pre_kernel(cmp_ref, sel_ref, win_ref, kn_ref, pos_ref, w1_ref, w2_ref,
                  kvc_ref, ksel_ref, vselt_ref, kwin_ref, vwin_ref):
    f32, bf16 = jnp.float32, jnp.bfloat16
    d = C_HEAD_DIM

    def rms(x, g):
        return x * lax.rsqrt(jnp.mean(x * x, axis=-1, keepdims=True) + RMS_EPS) * g

    acc = [jnp.zeros((N_CMP_PAD, 2 * PHI_HIDDEN), f32) for _ in range(CMP_GROUPS)]
    for j in range(CMP_STRIDE):
        xs = cmp_ref[pl.ds(j, N_CMP_PAD, stride=CMP_STRIDE), :]
        for half in range(CMP_GROUPS):
            jj = half * CMP_STRIDE + j
            acc[half] = acc[half] + jnp.dot((xs + pos_ref[jj:jj + 1, :]).astype(bf16), w1_ref[jj],
                                            preferred_element_type=f32)
    hid = acc[0] + pltpu.roll(acc[1], N_CMP_PAD - 1, axis=0)
    kv = jnp.dot(_silu(hid).astype(bf16), w2_ref[...], preferred_element_type=f32)
    kvc_ref[0, :, 0:d] = rms(kv[:, 0:d], kn_ref[0:1, :])
    kvc_ref[0, :, d:2 * d] = kv[:, d:2 * d]

    sel = sel_ref[...]
    ksel_ref[...] = rms(sel[:, 0:d], kn_ref[1:2, :])
    vselt_ref[0] = sel.T[d:2 * d, :]
    win = win_ref[...]
    kwin_ref[0, 0:WINDOW, :] = jnp.zeros((WINDOW, d), f32)
    vwin_ref[0, 0:WINDOW, :] = jnp.zeros((WINDOW, d), f32)
    kwin_ref[0, WINDOW:, :] = rms(win[:, 0:d], kn_ref[2:3, :])
    vwin_ref[0, WINDOW:, :] = win[:, d:2 * d]


def c_pre(proj, b, s, k_norm, cmp_pos, phi_w1, phi_w2):
    d = C_HEAD_DIM
    kv0 = OFF["c_kv"] // 128
    pos = jnp.concatenate([cmp_pos[0], cmp_pos[1]], axis=-1)
    w1 = phi_w1.reshape(2, CMP_BLOCK, d, PHI_HIDDEN)
    zero = jnp.zeros((CMP_BLOCK, d, PHI_HIDDEN), jnp.float32)
    w1c = jnp.concatenate([jnp.concatenate([w1[0], zero], axis=-1),
                           jnp.concatenate([zero, w1[1]], axis=-1)], axis=1).astype(jnp.bfloat16)
    z2 = jnp.zeros((PHI_HIDDEN, d), jnp.float32)
    w2c = jnp.concatenate([jnp.concatenate([phi_w2[0], z2], axis=-1),
                           jnp.concatenate([z2, phi_w2[1]], axis=-1)], axis=0).astype(jnp.bfloat16)
    full = lambda a: pl.BlockSpec(a.shape, lambda bb: (0,) * a.ndim)
    return pl.pallas_call(
        _c_pre_kernel,
        grid=(b,),
        in_specs=[
            pl.BlockSpec((s, 128), lambda bb: (bb, kv0)),
            pl.BlockSpec((s, 128), lambda bb: (bb, kv0 + 1)),
            pl.BlockSpec((s, 128), lambda bb: (bb, kv0 + 2)),
            full(k_norm), full(pos), full(w1c), full(w2c),
        ],
        out_specs=[
            pl.BlockSpec((1, N_CMP_PAD, 128), lambda bb: (bb, 0, 0)),
            pl.BlockSpec((s, d), lambda bb: (bb, 0)),
            pl.BlockSpec((1, d, s), lambda bb: (bb, 0, 0)),
            pl.BlockSpec((1, s + WINDOW, d), lambda bb: (bb, 0, 0)),
            pl.BlockSpec((1, s + WINDOW, d), lambda bb: (bb, 0, 0)),
        ],
        out_shape=[
            jax.ShapeDtypeStruct((b, N_CMP_PAD, 128), jnp.float32),
            jax.ShapeDtypeStruct((b * s, d), jnp.float32),
            jax.ShapeDtypeStruct((b, d, s), jnp.float32),
            jax.ShapeDtypeStruct((b, s + WINDOW, d), jnp.float32),
            jax.ShapeDtypeStruct((b, s + WINDOW, d), jnp.float32),
        ],
        compiler_params=pltpu.CompilerParams(dimension_semantics=("arbitrary",), vmem_limit_bytes=VMEM_LIMIT),
        name="c_pre",
    )(proj, proj, proj, k_norm, pos, w1c, w2c)


C_QPS = 2


def _c_pair_kernel(cq_ref, z_ref, misct_ref, gq_ref, kvc_ref, ksel_ref, vselt_ref, kwin_ref, vwin_ref,
                   cmptab_ref, near_ref, farrow_ref, wintab_ref, ovt_ref, exp_ref, o_ref, ls_scr):
    s = ksel_ref.shape[0]
    d = C_HEAD_DIM
    hq = C_HEADS * QBLOCK
    f32, bf16 = jnp.float32, jnp.bfloat16
    nt = (((1,), (1,)), ((), ()))
    tn = (((0,), (0,)), ((), ()))

    def softmax_rows(l, valid):
        l = jnp.where(valid, l, NEG_INF)
        m = _col_reduce(l, jnp.max)
        p = jnp.where(valid, jnp.exp(l - m), 0.0)
        den = _col_reduce(p, jnp.sum)
        return p, den

    def block(g, t0):
        nrows = t0 + QBLOCK
        qrows = slice(g * QBLOCK, (g + 1) * QBLOCK)

        cq = cq_ref[qrows, :]
        qs = jnp.concatenate([cq[:, h * d:(h + 1) * d] for h in range(C_HEADS)], axis=0)
        qs = qs * lax.rsqrt(jnp.mean(qs * qs, axis=-1, keepdims=True) + RMS_EPS) * gq_ref[...] * (d ** -0.5)
        qs = qs.astype(bf16)

        kvc = kvc_ref[0]
        lc = lax.dot_general(kvc[:, 0:d].astype(bf16), qs, nt, preferred_element_type=f32) + cmptab_ref[g]
        n_idx = lax.broadcasted_iota(jnp.int32, (N_CMP_PAD, hq), 0)
        t_c = t0 + (lax.broadcasted_iota(jnp.int32, (N_CMP_PAD, hq), 1) & (QBLOCK - 1))
        cmp_valid = n_idx * CMP_STRIDE + (CMP_BLOCK - 1) <= t_c
        pc, den_c = softmax_rows(lc, cmp_valid)
        pc = pc * jnp.where(den_c > 0.0, 1.0 / den_c, 0.0)
        o_cmp = lax.dot_general(kvc[:, d:2 * d].astype(bf16), pc.astype(bf16), tn, preferred_element_type=f32)

        psum = pc[:, 0:QBLOCK]
        for h in range(1, C_HEADS):
            psum = psum + pc[:, h * QBLOCK:(h + 1) * QBLOCK]
        p_hi = psum.astype(bf16)
        p_lo = (psum - p_hi.astype(f32)).astype(bf16)
        ovt = ovt_ref[...]
        imp = jnp.dot(ovt, p_hi, preferred_element_type=f32) + jnp.dot(ovt, p_lo, preferred_element_type=f32)
        j_idx = lax.broadcasted_iota(jnp.int32, (N_SB, QBLOCK), 0)
        t_b = t0 + lax.broadcasted_iota(jnp.int32, (N_SB, QBLOCK), 1)
        cur = lax.shift_right_arithmetic(t_b, SEL_BLOCK.bit_length() - 1)
        forced = (j_idx == 0) | (j_idx == cur) | (j_idx == jnp.maximum(cur - 1, 0))
        imp = jnp.where(j_idx * SEL_BLOCK <= t_b, jnp.where(forced, FORCE_SCORE, imp), NEG_INF)
        rank = jnp.zeros((N_SB, QBLOCK), f32)
        for r in range(N_SB):
            row = imp[r:r + 1, :]
            rank = rank + jnp.where((row > imp) | ((row == imp) & (j_idx > r)), 1.0, 0.0)
        picked = jnp.where(rank < float(N_SEL), 1.0, 0.0).astype(bf16)

        kw = kwin_ref[0, t0:t0 + WIN_KEYS, :]
        vw = vwin_ref[0, t0:t0 + WIN_KEYS, :]
        lw = lax.dot_general(kw.astype(bf16), qs, nt, preferred_element_type=f32) + wintab_ref[...]
        r_idx = lax.broadcasted_iota(jnp.int32, (WIN_KEYS, hq), 0)
        dist = (lax.broadcasted_iota(jnp.int32, (WIN_KEYS, hq), 1) & (QBLOCK - 1)) + WINDOW - r_idx
        win_valid = (dist >= 0) & (dist < WINDOW) & (r_idx + t0 >= WINDOW)
        pw, den_w = softmax_rows(lw, win_valid)
        o_win = lax.dot_general(vw.astype(bf16), pw.astype(bf16), tn, preferred_element_type=f32) / den_w

        gate = _sigmoid(misct_ref[0, C_G_LANE:C_G_LANE + 3 * C_HEADS, qrows])

        kpos = lax.broadcasted_iota(jnp.int32, (nrows, QBLOCK), 0)
        tpos = t0 + lax.broadcasted_iota(jnp.int32, (nrows, QBLOCK), 1)
        in_picked = jnp.dot(exp_ref[0:nrows, :], picked, preferred_element_type=f32) > 0.5
        sel_bias = jnp.where(in_picked, jnp.where(kpos <= tpos, 0.0, NEG_INF), NEG_INF)
        ls_scr[0:QBLOCK, :] = jnp.zeros((QBLOCK, hq), f32)
        ls_scr[QBLOCK:QBLOCK + nrows, :] = lax.dot_general(ksel_ref[0:nrows, :].astype(bf16), qs, nt,
                                                           preferred_element_type=f32)
        ls_scr[t0:t0 + 2 * QBLOCK, :] += near_ref[...] - farrow_ref[...]
        outs = []
        for h in range(C_HEADS):
            cols = slice(h * QBLOCK, (h + 1) * QBLOCK)
            l = ls_scr[QBLOCK:QBLOCK + nrows, cols] + sel_bias
            m = _col_reduce(l, jnp.max)
            p = jnp.exp(l - m)
            den = _col_reduce(p, jnp.sum)
            o_sel = jnp.dot(vselt_ref[0, :, 0:nrows].astype(bf16), p.astype(bf16), preferred_element_type=f32) / den
            outs.append(gate[3 * h:3 * h + 1, :] * o_cmp[:, cols] + gate[3 * h + 1:3 * h + 2, :] * o_sel
                        + gate[3 * h + 2:3 * h + 3, :] * o_win[:, cols])
        o_ref[qrows, :] = jnp.concatenate(outs, axis=0).T * _silu(z_ref[qrows, :])

    def run(step):
        for g in range(C_QPS):
            block(g, (C_QPS * step + g) * QBLOCK)

    step = pl.program_id(1)
    for v in range(s // (C_QPS * QBLOCK)):
        pl.when(step == v)(functools.partial(run, v))


def c_main(proj, misct, gq, kvc, ksel, vselt, kwin, vwin, cmptab, near, farrow, wintab, b, s):
    n = b * s
    tq = C_QPS * QBLOCK
    nq = s // tq
    d = C_HEAD_DIM
    hq = C_HEADS * QBLOCK
    row = lambda bb, i: bb * nq + i
    ovt = jnp.asarray(_OVERLAP_T, jnp.bfloat16)
    expand = jnp.asarray(_SB_EXPAND, jnp.bfloat16)
    const = lambda a: pl.BlockSpec(a.shape, lambda bb, i: (0,) * a.ndim)
    return pl.pallas_call(
        _c_pair_kernel,
        grid=(b, nq),
        in_specs=[
            pl.BlockSpec((tq, C_WIDTH), lambda bb, i: (row(bb, i), OFF["c_q"] // C_WIDTH)),
            pl.BlockSpec((tq, C_WIDTH), lambda bb, i: (row(bb, i), OFF["c_z"] // C_WIDTH)),
            pl.BlockSpec((1, 128, tq), lambda bb, i: (bb, 0, i)),
            const(gq),
            pl.BlockSpec((1, N_CMP_PAD, 128), lambda bb, i: (bb, 0, 0)),
            pl.BlockSpec((s, d), lambda bb, i: (bb, 0)),
            pl.BlockSpec((1, d, s), lambda bb, i: (bb, 0, 0)),
            pl.BlockSpec((1, s + WINDOW, d), lambda bb, i: (bb, 0, 0)),
            pl.BlockSpec((1, s + WINDOW, d), lambda bb, i: (bb, 0, 0)),
            pl.BlockSpec((C_QPS, N_CMP_PAD, hq), lambda bb, i: (i, 0, 0)),
            const(near), const(farrow), const(wintab), const(ovt), const(expand),
        ],
        out_specs=pl.BlockSpec((tq, C_WIDTH), lambda bb, i: (row(bb, i), 0)),
        out_shape=jax.ShapeDtypeStruct((n, C_WIDTH), jnp.float32),
        scratch_shapes=[pltpu.VMEM((s + QBLOCK, hq), jnp.float32)],
        compiler_params=pltpu.CompilerParams(
            dimension_semantics=("arbitrary", "arbitrary"), vmem_limit_bytes=VMEM_LIMIT),
        name="c_main",
    )(proj, proj, misct, gq, kvc, ksel, vselt, kwin, vwin, cmptab, near, farrow, wintab, ovt, expand)


def _head_cols(tab):
    return jnp.moveaxis(tab, -1, -2).reshape(*tab.shape[:-2], tab.shape[-1] * tab.shape[-2])


def mixer_c(proj, misct, b, s, gq, k_norm, cmp_pos, phi_w1, phi_w2, bias_c):
    kvc, ksel, vselt, kwin, vwin = c_pre(proj, b, s, k_norm, cmp_pos, phi_w1, phi_w2)
    cmptab = _head_cols(_bias_lookup(bias_c, _CMP_BUCKET_T))
    near = _head_cols(_bias_lookup(bias_c, _NEAR_BUCKET_T))
    wintab = _head_cols(_bias_lookup(bias_c, _WIN_BUCKET_T))
    farrow = jnp.repeat(bias_c[_FAR_BUCKET], QBLOCK).reshape(1, C_HEADS * QBLOCK)
    return c_main(proj, misct, gq.reshape(1, C_HEAD_DIM), kvc, ksel, vselt, kwin, vwin, cmptab, near, farrow,
                  wintab, b, s)


B_HPS = 2
B_GROUP = 4


def _b_kernel(alog_ref, dtb_ref, q_ref, k_ref, v_ref, z_ref, misc_ref, arow_ref, cwq_ref, cwk_ref, cwv_ref,
              gon_ref, o_ref, qs, ks, vs, ws, at, gtok, btok, gcrow):
    s = q_ref.shape[0]
    dh = B_HEAD_DIM
    c_len = GDN_CHUNK
    j = pl.program_id(1)
    f32, bf16 = jnp.float32, jnp.bfloat16
    hi = lax.Precision.HIGHEST
    nt = (((1,), (1,)), ((), ()))
    tn = (((0,), (0,)), ((), ()))

    head_rows = lax.broadcasted_iota(jnp.int32, (SUBLANES, B_HPS * dh), 0)

    def conv_silu(x_ref, w_ref):
        x8 = x_ref[0:SUBLANES, :]
        w_last = w_ref[CONV_WIDTH - 1:CONV_WIDTH, :]
        head = x8 * w_last
        body = x_ref[SUBLANES:s, :] * w_last
        for k in range(1, CONV_WIDTH):
            wk = w_ref[CONV_WIDTH - 1 - k:CONV_WIDTH - k, :]
            head = head + jnp.where(head_rows >= k, pltpu.roll(x8, k, axis=0), 0.0) * wk
            body = body + x_ref[SUBLANES - k:s - k, :] * wk
        return _silu(jnp.concatenate([head, body], axis=0))

    def l2n(x):
        return x * lax.rsqrt(jnp.sum(x * x, axis=-1, keepdims=True) + RMS_EPS)

    qc = conv_silu(q_ref, cwq_ref)
    kc = conv_silu(k_ref, cwk_ref)
    vs[...] = conv_silu(v_ref, cwv_ref)
    misc = misc_ref[...]
    lane = lax.broadcasted_iota(jnp.int32, misc.shape, 1)
    ri = lax.broadcasted_iota(jnp.int32, (c_len, c_len), 0)
    ci = lax.broadcasted_iota(jnp.int32, (c_len, c_len), 1)
    lower = ci <= ri
    strict = ci < ri
    tri_u = jnp.where(ri <= ci, 1.0, 0.0)
    row_in_chunk = lax.broadcasted_iota(jnp.int32, (s, dh), 0) & (c_len - 1)
    heads = [B_HPS * j + hh for hh in range(B_HPS)]
    neg_a = [-jnp.exp(alog_ref[h]) for h in heads]
    lane_row = lax.broadcasted_iota(jnp.int32, (1, misc.shape[1]), 1)
    dtb_row = jnp.zeros((1, misc.shape[1]), f32)
    nega_row = jnp.zeros((1, misc.shape[1]), f32)
    for hh, h in enumerate(heads):
        dtb_row = jnp.where(lane_row == B_A_LANE + h, dtb_ref[h], dtb_row)
        nega_row = jnp.where(lane_row == B_A_LANE + h, neg_a[hh], nega_row)
    g_all = nega_row * jax.nn.softplus(misc + dtb_row)
    for sh in (1, 2, 4, 8, 16, 32):
        g_all = g_all + jnp.where(row_in_chunk >= sh, pltpu.roll(g_all, sh, axis=0), 0.0)
    beta_all = _sigmoid(misc)
    for hh, h in enumerate(heads):
        sl = slice(hh * dh, (hh + 1) * dh)
        qs[:, sl] = l2n(qc[:, sl]) * (dh ** -0.5)
        ks[:, sl] = l2n(kc[:, sl])
        g_tok = jnp.sum(jnp.where(lane == B_A_LANE + h, g_all, 0.0), axis=-1, keepdims=True)
        b_tok = jnp.sum(jnp.where(lane == B_B_LANE + h, beta_all, 0.0), axis=-1, keepdims=True)
        gtok[hh] = jnp.broadcast_to(g_tok, (s, dh))
        btok[hh] = jnp.broadcast_to(b_tok, (s, dh))
        g_row = neg_a[hh] * jax.nn.softplus(arow_ref[0, hh] + dtb_ref[h])
        gcrow[hh] = jnp.dot(g_row, tri_u, precision=hi, preferred_element_type=f32)

    lane3 = lax.broadcasted_iota(jnp.int32, (c_len, 3 * c_len), 1)
    lo_lanes = (lane3 >= c_len) & (lane3 < 2 * c_len)

    def split_lhs(p):
        p4 = jnp.concatenate([p, p, p], axis=1)
        hi4 = p4.astype(bf16).astype(f32)
        return jnp.where(lo_lanes, p4 - hi4, hi4).astype(bf16)

    def split_rhs(x):
        xh = x.astype(bf16)
        xl = (x - xh.astype(f32)).astype(bf16)
        return jnp.concatenate([xh, xh, xl], axis=0)

    def mm(p_split, x):
        return jnp.dot(p_split, split_rhs(x), preferred_element_type=f32)

    hs = range(B_HPS)
    sls = [slice(hh * dh, (hh + 1) * dh) for hh in hs]
    n_groups = s // (B_GROUP * c_len)

    def chunk_rows(c):
        return pl.ds(pl.multiple_of(c * c_len, c_len), c_len)

    def intra_group(i, between):
        where, q_c, k_c, v_c, gb, beta, g_row = [], [], [], [], [], [], []
        for c in [B_GROUP * i + cc for cc in range(B_GROUP)]:
            rs = chunk_rows(c)
            for hh in hs:
                where.append((rs, hh))
                q_c.append(qs[rs, sls[hh]])
                k_c.append(ks[rs, sls[hh]])
                v_c.append(vs[rs, sls[hh]])
                gb.append(gtok[hh, rs, :])
                beta.append(btok[hh, rs, :])
                g_row.append(gcrow[hh, pl.ds(c, 1), :])
        nch = range(len(where))
        decay = [jnp.where(lower, jnp.exp(jnp.where(lower, gb[n][:, 0:c_len] - g_row[n], 0.0)), 0.0) for n in nch]
        kb = [k_c[n] * beta[n] for n in nch]
        kk = [lax.dot_general(kb[n].astype(bf16), k_c[n].astype(bf16), nt, preferred_element_type=f32) for n in nch]
        between()
        p = [-jnp.where(strict, kk[n] * decay[n], 0.0) for n in nch]
        eg = [jnp.exp(gb[n]) for n in nch]
        x = [jnp.concatenate([v_c[n] * beta[n], kb[n] * eg[n]], axis=1) for n in nch]
        ps = [split_lhs(p[n]) for n in nch]
        x = [x[n] + mm(ps[n], x[n]) for n in nch]
        between()
        for _ in range(5):
            p = [mm(ps[n], p[n]) for n in nch]
            between()
            ps = [split_lhs(p[n]) for n in nch]
            x = [x[n] + mm(ps[n], x[n]) for n in nch]
            between()
        attn = [jnp.where(lower, lax.dot_general(q_c[n].astype(bf16), k_c[n].astype(bf16), nt,
                                                 preferred_element_type=f32) * decay[n], 0.0) for n in nch]
        between()

        def commit():
            for n, (rs, hh) in enumerate(where):
                g_last = g_row[n][:, c_len - 1:c_len]
                vs[rs, sls[hh]] = x[n][:, 0:dh]
                ws[rs, sls[hh]] = x[n][:, dh:2 * dh]
                qs[rs, sls[hh]] = q_c[n] * eg[n]
                ks[rs, sls[hh]] = k_c[n] * jnp.exp(g_last - gb[n])
                at[rs, hh * c_len:(hh + 1) * c_len] = attn[n]

        return commit

    def recur_steps(group, states, result):
        for c in [B_GROUP * group + cc for cc in range(B_GROUP)]:
            rs = chunk_rows(c)
            st_b = [states[hh].astype(bf16) for hh in hs]
            v_new = [vs[rs, sls[hh]] - jnp.dot(ws[rs, sls[hh]].astype(bf16), st_b[hh], preferred_element_type=f32)
                     for hh in hs]
            o_st = [jnp.dot(qs[rs, sls[hh]].astype(bf16), st_b[hh], preferred_element_type=f32) for hh in hs]
            yield
            v_nb = [v_new[hh].astype(bf16) for hh in hs]
            states = tuple(states[hh] * jnp.exp(gcrow[hh, pl.ds(c, 1), c_len - 1:c_len])
                           + lax.dot_general(ks[rs, sls[hh]].astype(bf16), v_nb[hh], tn, preferred_element_type=f32)
                           for hh in hs)
            for hh in hs:
                o_ref[rs, sls[hh]] = o_st[hh] + jnp.dot(at[rs, hh * c_len:(hh + 1) * c_len].astype(bf16), v_nb[hh],
                                                        preferred_element_type=f32)
            yield
        result.append(states)

    def overlapped(g, states):
        result = []
        steps = recur_steps(g - 1, states, result)
        commit = intra_group(g, lambda: next(steps, None))
        for _ in steps:
            pass
        commit()
        return result[0]

    intra_group(0, lambda: None)()
    states = lax.fori_loop(1, n_groups, overlapped, tuple(jnp.zeros((dh, dh), f32) for _ in hs))
    for _ in recur_steps(n_groups - 1, states, []):
        pass

    for hh in range(B_HPS):
        sl = slice(hh * dh, (hh + 1) * dh)
        o = o_ref[:, sl]
        o = o * lax.rsqrt(jnp.mean(o * o, axis=-1, keepdims=True) + RMS_EPS) * gon_ref[...]
        o_ref[:, sl] = o * _silu(z_ref[:, sl])


def mixer_b(proj, misct, b, s, conv_w, a_log, dt_bias, out_norm):
    n = b * s
    w2 = B_HPS * B_HEAD_DIM
    nc = s // GDN_CHUNK
    arow = misct[:, B_A_LANE:B_A_LANE + B_HEADS, :].reshape(b, B_HEADS, nc, GDN_CHUNK)
    qkv0 = OFF["b_qkv"] // w2
    kstep = B_WIDTH // w2
    smem = pl.BlockSpec(memory_space=pltpu.SMEM)
    return pl.pallas_call(
        _b_kernel,
        grid=(b, B_HEADS // B_HPS),
        in_specs=[
            smem, smem,
            pl.BlockSpec((s, w2), lambda bb, j: (bb, qkv0 + j)),
            pl.BlockSpec((s, w2), lambda bb, j: (bb, qkv0 + kstep + j)),
            pl.BlockSpec((s, w2), lambda bb, j: (bb, qkv0 + 2 * kstep + j)),
            pl.BlockSpec((s, w2), lambda bb, j: (bb, OFF["b_z"] // w2 + j)),
            pl.BlockSpec((s, 128), lambda bb, j: (bb, MISC_OFF // 128)),
            pl.BlockSpec((1, B_HPS, nc, GDN_CHUNK), lambda bb, j: (bb, j, 0, 0)),
            pl.BlockSpec((CONV_WIDTH, w2), lambda bb, j: (0, j)),
            pl.BlockSpec((CONV_WIDTH, w2), lambda bb, j: (0, kstep + j)),
            pl.BlockSpec((CONV_WIDTH, w2), lambda bb, j: (0, 2 * kstep + j)),
            pl.BlockSpec((1, B_HEAD_DIM), lambda bb, j: (0, 0)),
        ],
        out_specs=pl.BlockSpec((s, w2), lambda bb, j: (bb, j)),
        out_shape=jax.ShapeDtypeStruct((n, B_WIDTH), jnp.float32),
        scratch_shapes=[
            pltpu.VMEM((s, w2), jnp.float32), pltpu.VMEM((s, w2), jnp.float32), pltpu.VMEM((s, w2), jnp.float32),
            pltpu.VMEM((s, w2), jnp.float32), pltpu.VMEM((s, B_HPS * GDN_CHUNK), jnp.float32),
            pltpu.VMEM((B_HPS, s, B_HEAD_DIM), jnp.float32), pltpu.VMEM((B_HPS, s, B_HEAD_DIM), jnp.float32),
            pltpu.VMEM((B_HPS, nc, GDN_CHUNK), jnp.float32),
        ],
        compiler_params=pltpu.CompilerParams(
            dimension_semantics=("arbitrary", "arbitrary"), vmem_limit_bytes=VMEM_LIMIT),
        name="gdn",
    )(a_log, dt_bias, proj, proj, proj, proj, proj, arow, conv_w, conv_w, conv_w, out_norm.reshape(1, B_HEAD_DIM))


def kernel(x, norm_g, w_in, a_kv_norm, a_w_ukv, a_q_norm, a_k_norm, b_conv, b_a_log, b_dt_bias, b_out_norm, c_q_norm, c_k_norm, c_cmp_pos, c_phi_w1, c_phi_w2, w_branch, w_out, rel_bias):
    b, s, _ = x.shape
    n = b * s
    bias_a = rel_bias[:, :A_HEADS]
    bias_c = rel_bias[:, A_HEADS:]
    xf = x.reshape(n, D_MODEL)
    fields = [w_in[:, :, _ORIG[name][0]:_ORIG[name][0] + _ORIG[name][1]] for name in _NEW_ORDER]
    pad = jnp.zeros((DEPTH, D_MODEL, N_PAD - N_IN), w_in.dtype)
    w_in_p = jnp.concatenate(fields + [pad], axis=-1).astype(jnp.bfloat16)
    for l in range(DEPTH):
        proj = in_proj(xf, norm_g[l], w_in_p[l])
        p3 = proj.reshape(b, s, N_PAD)

        misct = jnp.transpose(p3[..., MISC_OFF:], (0, 2, 1))
        y_a = mixer_a(proj, misct, b, s, a_kv_norm[l], a_w_ukv[l], a_q_norm[l], a_k_norm[l], bias_a)

        y_b = mixer_b(proj, misct, b, s, b_conv[l], b_a_log[l], b_dt_bias[l], b_out_norm[l])
        y_c = mixer_c(proj, misct, b, s, c_q_norm[l], c_k_norm[l], c_cmp_pos[l], c_phi_w1[l], c_phi_w2[l], bias_c)

        wbr = w_branch[l].astype(jnp.bfloat16)
        xf = merge(xf, proj, y_a, y_b, y_c, wbr[:A_WIDTH], wbr[A_WIDTH:A_WIDTH + B_WIDTH],
                   wbr[A_WIDTH + B_WIDTH:], w_out[l].astype(jnp.bfloat16))
    return xf.reshape(b, s, D_MODEL)
```

```python
import functools
import math

import jax
import jax.numpy as jnp
import numpy as np
from jax import lax
from jax.experimental import pallas as pl
from jax.experimental.pallas import tpu as pltpu

D_MODEL = 1024
DEPTH = 4
QBLOCK = 128
NEG_INF = -1e30
FORCE_SCORE = 1e9
RMS_EPS = 1e-6

A_HEADS = 4
A_HEAD_DIM = 64
A_WIDTH = A_HEADS * A_HEAD_DIM
A_KV_LATENT = 128
IDX_HEADS = 8
IDX_DIM = 64
TOPK_MAX = 256

B_HEADS = 4
B_HEAD_DIM = 128
B_WIDTH = B_HEADS * B_HEAD_DIM
CONV_WIDTH = 4
GDN_CHUNK = 64

C_HEADS = 4
C_HEAD_DIM = 64
C_WIDTH = C_HEADS * C_HEAD_DIM
CMP_BLOCK = 32
CMP_STRIDE = 16
SEL_BLOCK = 64
N_SEL = 16
WINDOW = 512
PHI_HIDDEN = 256

N_BUCKETS = 32
BUCKET_MAX_EXACT = 16
BUCKET_MAX_DIST = 128

VMEM_LIMIT = 48 * 1024 * 1024
SUBLANES = 8
KEY_BITS = 32

_ORIG = {}
_o = 0
for _name, _w in (("a_q", 256), ("a_ckv", 128), ("a_qi", 512), ("a_ki", 64), ("a_wi", 8), ("a_z", 256),
                  ("b_qkv", 1536), ("b_a", 4), ("b_b", 4), ("b_z", 512),
                  ("c_q", 256), ("c_kv", 384), ("c_g", 12), ("c_z", 256), ("g", 3072)):
    _ORIG[_name] = (_o, _w)
    _o += _w
N_IN = _o

_NEW_ORDER = ("g", "b_qkv", "b_z", "a_qi", "a_q", "a_z", "c_q", "c_z", "c_kv", "a_ckv",
              "a_ki", "a_wi", "b_a", "b_b", "c_g")
OFF = {}
_o = 0
for _name in _NEW_ORDER:
    OFF[_name] = _o
    _o += _ORIG[_name][1]
MISC_OFF = OFF["a_ki"]
N_PAD = 7296
assert MISC_OFF == 7168 and _o <= N_PAD
A_WI_LANE = OFF["a_wi"] - MISC_OFF
B_A_LANE = OFF["b_a"] - MISC_OFF
B_B_LANE = OFF["b_b"] - MISC_OFF
C_G_LANE = OFF["c_g"] - MISC_OFF


IN_TM = 1024
IN_TN = N_PAD // 3


def _in_proj_kernel(x_ref, g_ref, w_ref, o_ref):
    x = x_ref[...]
    ms = jnp.mean(x * x, axis=-1, keepdims=True)
    h = (x * lax.rsqrt(ms + RMS_EPS)) * g_ref[...]
    o_ref[...] = jnp.dot(h.astype(jnp.bfloat16), w_ref[...], preferred_element_type=jnp.float32)


def in_proj(xf, g, w_bf16):
    n = xf.shape[0]
    return pl.pallas_call(
        _in_proj_kernel,
        grid=(N_PAD // IN_TN, n // IN_TM),
        in_specs=[
            pl.BlockSpec((IN_TM, D_MODEL), lambda j, i: (i, 0)),
            pl.BlockSpec((1, D_MODEL), lambda j, i: (0, 0)),
            pl.BlockSpec((D_MODEL, IN_TN), lambda j, i: (0, j)),
        ],
        out_specs=pl.BlockSpec((IN_TM, IN_TN), lambda j, i: (i, j)),
        out_shape=jax.ShapeDtypeStruct((n, N_PAD), jnp.float32),
        compiler_params=pltpu.CompilerParams(
            dimension_semantics=("arbitrary", "arbitrary"), vmem_limit_bytes=VMEM_LIMIT),
        name="in_proj",
    )(xf, g.reshape(1, D_MODEL), w_bf16)


MG_TM = 512


def _merge_kernel(x_ref, g_ref, ya_ref, yb_ref, yc_ref, wa_ref, wb_ref, wc_ref, wo_ref, o_ref):
    def branch(y_ref, w_ref, k):
        p = jnp.dot(y_ref[...].astype(jnp.bfloat16), w_ref[...], preferred_element_type=jnp.float32)
        return _sigmoid(g_ref[:, k * D_MODEL:(k + 1) * D_MODEL]) * p

    merged = branch(ya_ref, wa_ref, 0) + branch(yb_ref, wb_ref, 1) + branch(yc_ref, wc_ref, 2)
    o_ref[...] = x_ref[...] + jnp.dot(merged.astype(jnp.bfloat16), wo_ref[...],
                                      preferred_element_type=jnp.float32)


def merge(xf, proj, ya, yb, yc, wa, wb, wc, wo):
    n = xf.shape[0]
    row = lambda w: pl.BlockSpec((MG_TM, w), lambda i: (i, 0))
    full = lambda a: pl.BlockSpec(a.shape, lambda i: (0, 0))
    return pl.pallas_call(
        _merge_kernel,
        grid=(n // MG_TM,),
        in_specs=[row(D_MODEL), row(3 * D_MODEL), row(A_WIDTH), row(B_WIDTH), row(C_WIDTH),
                  full(wa), full(wb), full(wc), full(wo)],
        out_specs=row(D_MODEL),
        out_shape=jax.ShapeDtypeStruct((n, D_MODEL), jnp.float32),
        compiler_params=pltpu.CompilerParams(
            dimension_semantics=("arbitrary",), vmem_limit_bytes=VMEM_LIMIT),
        name="merge",
    )(xf, proj, ya, yb, yc, wa, wb, wc, wo)


def _bucket_np(dist):
    d = np.maximum(np.asarray(dist, np.int64), 0)
    ratio = np.log(np.maximum(d, 1).astype(np.float64) / BUCKET_MAX_EXACT) / math.log(BUCKET_MAX_DIST / BUCKET_MAX_EXACT)
    scaled = ratio * (N_BUCKETS - BUCKET_MAX_EXACT)
    frac = scaled - np.floor(scaled)
    edge = (d > BUCKET_MAX_EXACT) & (d < BUCKET_MAX_DIST) & ((frac < 1e-4) | (frac > 1 - 1e-4))
    assert not edge.any(), "bucket boundary too close to an integer distance"
    large = np.minimum(BUCKET_MAX_EXACT + np.floor(scaled + 1e-9).astype(np.int64), N_BUCKETS - 1)
    return np.where(d < BUCKET_MAX_EXACT, d, large).astype(np.int32)


_NEAR_BUCKET_T = _bucket_np(np.arange(QBLOCK)[None, :] + QBLOCK - np.arange(2 * QBLOCK)[:, None])
_FAR_BUCKET = int(_bucket_np(np.array([BUCKET_MAX_DIST]))[0])
assert (_bucket_np(np.arange(BUCKET_MAX_DIST, 4096)) == _FAR_BUCKET).all()

INT_MIN = -2 ** 31


COL_SLAB = 64


def _col_reduce(x, op):
    rows = x.shape[0]
    if rows > COL_SLAB and rows % COL_SLAB == 0:
        x = op(x.reshape(rows // COL_SLAB, COL_SLAB, x.shape[1]), axis=0)
    return op(x, axis=0, keepdims=True)


def _sigmoid(x):
    return 0.5 * jnp.tanh(0.5 * x) + 0.5


def _silu(x):
    return x * _sigmoid(x)


def _bias_lookup(table, buckets):
    idx = jnp.asarray(buckets)[..., None]
    out = jnp.zeros(idx.shape[:-1] + (table.shape[1],), table.dtype)
    for bkt in range(N_BUCKETS):
        out = jnp.where(idx == bkt, table[bkt], out)
    return out


A_PRE_TM = 512


def _a_pre_kernel(ckv_ref, q_ref, gkv_ref, wukv_ref, gq_ref, gk_ref, hm_ref, kn_ref, vt_ref, qn_ref):
    c = ckv_ref[...]
    c = c * lax.rsqrt(jnp.mean(c * c, axis=-1, keepdims=True) + RMS_EPS) * gkv_ref[...]
    kv = jnp.dot(c.astype(jnp.bfloat16), wukv_ref[...], preferred_element_type=jnp.float32)
    hm = hm_ref[...]

    def head_rms(x, g):
        ms = jnp.dot(x * x, hm, precision=lax.Precision.HIGHEST, preferred_element_type=jnp.float32)
        return x * lax.rsqrt(ms + RMS_EPS) * g

    kn_ref[...] = head_rms(kv[:, :A_WIDTH], gk_ref[...])
    qn_ref[...] = head_rms(q_ref[...], gq_ref[...]) * (A_HEAD_DIM ** -0.5)
    vt_ref[0] = kv[:, A_WIDTH:].T


def a_pre(proj, b, s, gkv, wukv_bf16, gq, gk):
    n = b * s
    nt = s // A_PRE_TM
    hm = jnp.asarray(np.kron(np.eye(A_HEADS), np.ones((A_HEAD_DIM, A_HEAD_DIM))) / A_HEAD_DIM, jnp.float32)
    row = lambda bb, j: (bb * nt + j, 0)
    full = lambda a: pl.BlockSpec(a.shape, lambda bb, j: (0,) * a.ndim)
    gq = jnp.tile(gq, A_HEADS).reshape(1, A_WIDTH)
    gk = jnp.tile(gk, A_HEADS).reshape(1, A_WIDTH)
    gkv = gkv.reshape(1, A_KV_LATENT)
    return pl.pallas_call(
        _a_pre_kernel,
        grid=(b, nt),
        in_specs=[
            pl.BlockSpec((A_PRE_TM, A_KV_LATENT), lambda bb, j: (bb * nt + j, OFF["a_ckv"] // A_KV_LATENT)),
            pl.BlockSpec((A_PRE_TM, A_WIDTH), lambda bb, j: (bb * nt + j, OFF["a_q"] // A_WIDTH)),
            full(gkv), full(wukv_bf16), full(gq), full(gk), full(hm),
        ],
        out_specs=[
            pl.BlockSpec((A_PRE_TM, A_WIDTH), row),
            pl.BlockSpec((1, A_WIDTH, A_PRE_TM), lambda bb, j: (bb, 0, j)),
            pl.BlockSpec((A_PRE_TM, A_WIDTH), row),
        ],
        out_shape=[
            jax.ShapeDtypeStruct((n, A_WIDTH), jnp.float32),
            jax.ShapeDtypeStruct((b, A_WIDTH, s), jnp.float32),
            jax.ShapeDtypeStruct((n, A_WIDTH), jnp.float32),
        ],
        compiler_params=pltpu.CompilerParams(
            dimension_semantics=("arbitrary", "arbitrary"), vmem_limit_bytes=VMEM_LIMIT),
        name="a_pre",
    )(proj, proj, gkv, wukv_bf16, gq, gk, hm)


A_QPS = 4


def _a_pair_kernel(far_ref, qn_ref, qi_ref, z_ref, misct_ref, kn_ref, vt_ref, ki_ref, near_ref, o_ref,
                   key_scr, lg_scr, j_scr):
    s = kn_ref.shape[0]
    f32, bf16 = jnp.float32, jnp.bfloat16
    nt = (((1,), (1,)), ((), ()))
    kf = float(TOPK_MAX)
    groups = range(A_QPS)

    def run(step):
        t0 = [(A_QPS * step + g) * QBLOCK for g in groups]
        nr = [t + QBLOCK for t in t0]
        qrows = [slice(g * QBLOCK, (g + 1) * QBLOCK) for g in groups]

        for g in groups:
            qi = qi_ref[qrows[g], :] * (IDX_DIM ** -0.5)
            qstack = jnp.concatenate([qi[:, h * IDX_DIM:(h + 1) * IDX_DIM] for h in range(IDX_HEADS)], axis=0)
            ki = ki_ref[0:nr[g], 0:IDX_DIM]
            sc = lax.dot_general(ki.astype(bf16), qstack.astype(bf16), nt, preferred_element_type=f32)
            wt = misct_ref[0, A_WI_LANE:A_WI_LANE + IDX_HEADS, qrows[g]] * (IDX_HEADS ** -0.5)
            score = jnp.zeros((nr[g], QBLOCK), f32)
            for h in range(IDX_HEADS):
                score = score + wt[h:h + 1, :] * jnp.maximum(sc[:, h * QBLOCK:(h + 1) * QBLOCK], 0.0)
            score = score + 0.0
            bits = pltpu.bitcast(score, jnp.int32)
            key = jnp.where(bits < 0, bits ^ jnp.int32(0x7FFFFFFF), bits)
            upper = (lax.broadcasted_iota(jnp.int32, (QBLOCK, QBLOCK), 0)
                     <= lax.broadcasted_iota(jnp.int32, (QBLOCK, QBLOCK), 1))
            if t0[g]:
                key_scr[g, 0:t0[g], :] = key[0:t0[g]]
            key_scr[g, t0[g]:nr[g], :] = jnp.where(upper, key[t0[g]:nr[g]], jnp.int32(INT_MIN))

        def count_ge(g, cand):
            return _col_reduce(jnp.where(key_scr[g, 0:nr[g], :] >= cand, 1.0, 0.0), jnp.sum)

        def bisect(it, thrs):
            bit = lax.shift_left(jnp.int32(1), KEY_BITS - 1 - it)
            cnt = [count_ge(g, thrs[g] + bit) for g in groups]
            return tuple(jnp.where(cnt[g] >= kf, thrs[g] + bit, thrs[g]) for g in groups)

        thrs = lax.fori_loop(0, KEY_BITS, bisect, tuple(jnp.full((1, QBLOCK), INT_MIN, jnp.int32) for _ in groups))

        for g in groups:
            nrows, thr = nr[g], thrs[g]
            n_ge = count_ge(g, thr)
            need = kf - count_ge(g, thr + 1)
            kpos = lax.broadcasted_iota(jnp.int32, (nrows, QBLOCK), 0)

            j_scr[g] = jnp.full((1, QBLOCK), nrows - 1, jnp.int32)
            surplus = jnp.where((n_ge > kf) & (thr > INT_MIN), 1.0, 0.0)

            @pl.when(jnp.max(surplus) > 0.0)
            def _():
                def bisect_idx(it, lohi):
                    lo, hi = lohi
                    mid = lax.shift_right_arithmetic(lo + hi, 1)
                    k = key_scr[g, 0:nrows, :]
                    c = _col_reduce(jnp.where((k == thr) & (kpos <= mid), 1.0, 0.0), jnp.sum)
                    ok = c >= need
                    return jnp.where(ok, lo, mid), jnp.where(ok, mid, hi)

                lo0 = jnp.full((1, QBLOCK), -1, jnp.int32)
                hi0 = jnp.full((1, QBLOCK), nrows - 1, jnp.int32)
                _, hi = lax.fori_loop(0, (nrows - 1).bit_length(), bisect_idx, (lo0, hi0))
                j_scr[g] = hi

            key = key_scr[g, 0:nrows, :]
            thr_c = jnp.maximum(thr, INT_MIN + 1)
            tie_ok = jnp.where(kpos <= j_scr[g], 0.0, NEG_INF)
            sel_bias = jnp.where(key > thr_c, 0.0, jnp.where(key == thr_c, tie_ok, NEG_INF))

            qn = qn_ref[qrows[g], :]
            lane = lax.broadcasted_iota(jnp.int32, (QBLOCK, A_WIDTH), 1)
            qblk = jnp.concatenate(
                [jnp.where((lane >= h * A_HEAD_DIM) & (lane < (h + 1) * A_HEAD_DIM), qn, 0.0)
                 for h in range(A_HEADS)], axis=0)
            lg_scr[0:QBLOCK, :] = jnp.zeros((QBLOCK, A_HEADS * QBLOCK), f32)
            lg_scr[QBLOCK:QBLOCK + nrows, :] = lax.dot_general(kn_ref[0:nrows, :].astype(bf16), qblk.astype(bf16),
                                                               nt, preferred_element_type=f32)
            outs = []
            for h in range(A_HEADS):
                cols = slice(h * QBLOCK, (h + 1) * QBLOCK)
                lg_scr[t0[g]:t0[g] + 2 * QBLOCK, cols] += near_ref[h] - far_ref[h]
                l = lg_scr[QBLOCK:QBLOCK + nrows, cols] + sel_bias
                m = _col_reduce(l, jnp.max)
                p = jnp.exp(l - m)
                den = _col_reduce(p, jnp.sum)
                vt = vt_ref[0, h * A_HEAD_DIM:(h + 1) * A_HEAD_DIM, 0:nrows]
                o_t = jnp.dot(vt.astype(bf16), p.astype(bf16), preferred_element_type=f32)
                outs.append(o_t / den)
            o = jnp.concatenate(outs, axis=0).T
            o_ref[qrows[g], :] = o * _silu(z_ref[qrows[g], :])

    step = pl.program_id(1)
    for v in range(s // (A_QPS * QBLOCK)):
        pl.when(step == v)(functools.partial(run, v))


def a_main(proj, misct, qn, kn, vt, near_t, far, b, s):
    n = b * s
    tq = A_QPS * QBLOCK
    nq = s // tq
    row = lambda bb, i: bb * nq + i
    return pl.pallas_call(
        _a_pair_kernel,
        grid=(b, nq),
        in_specs=[
            pl.BlockSpec(memory_space=pltpu.SMEM),
            pl.BlockSpec((tq, A_WIDTH), lambda bb, i: (row(bb, i), 0)),
            pl.BlockSpec((tq, IDX_HEADS * IDX_DIM), lambda bb, i: (row(bb, i), OFF["a_qi"] // (IDX_HEADS * IDX_DIM))),
            pl.BlockSpec((tq, A_WIDTH), lambda bb, i: (row(bb, i), OFF["a_z"] // A_WIDTH)),
            pl.BlockSpec((1, 128, tq), lambda bb, i: (bb, 0, i)),
            pl.BlockSpec((s, A_WIDTH), lambda bb, i: (bb, 0)),
            pl.BlockSpec((1, A_WIDTH, s), lambda bb, i: (bb, 0, 0)),
            pl.BlockSpec((s, 128), lambda bb, i: (bb, MISC_OFF // 128)),
            pl.BlockSpec((A_HEADS, 2 * QBLOCK, QBLOCK), lambda bb, i: (0, 0, 0)),
        ],
        out_specs=pl.BlockSpec((tq, A_WIDTH), lambda bb, i: (row(bb, i), 0)),
        out_shape=jax.ShapeDtypeStruct((n, A_WIDTH), jnp.float32),
        scratch_shapes=[
            pltpu.VMEM((A_QPS, s, QBLOCK), jnp.int32),
            pltpu.VMEM((s + QBLOCK, A_HEADS * QBLOCK), jnp.float32),
            pltpu.VMEM((A_QPS, 1, QBLOCK), jnp.int32),
        ],
        compiler_params=pltpu.CompilerParams(
            dimension_semantics=("arbitrary", "arbitrary"), vmem_limit_bytes=VMEM_LIMIT),
        name="a_main",
    )(far, qn, proj, proj, misct, kn, vt, proj, near_t)


def mixer_a(proj, misct, b, s, gkv, wukv, gq, gk, bias_a):
    kn, vt, qn = a_pre(proj, b, s, gkv, wukv.astype(jnp.bfloat16), gq, gk)
    near_t = jnp.transpose(_bias_lookup(bias_a, _NEAR_BUCKET_T), (2, 0, 1))
    far = bias_a[_FAR_BUCKET]
    return a_main(proj, misct, qn, kn, vt, near_t, far, b, s)


N_CMP_PAD = 128
N_SB = 32
WIN_KEYS = WINDOW + QBLOCK
CMP_GROUPS = CMP_BLOCK // CMP_STRIDE

_CMP_BUCKET_T = _bucket_np((np.arange(16)[:, None, None] * QBLOCK + np.arange(QBLOCK)[None, None, :])
                           - (np.arange(N_CMP_PAD)[None, :, None] * CMP_STRIDE + CMP_BLOCK - 1))
_WIN_DIST_T = np.arange(QBLOCK)[None, :] + WINDOW - np.arange(WIN_KEYS)[:, None]
_WIN_BUCKET_T = _bucket_np(_WIN_DIST_T)
_CMP_MASK_T = np.where((np.arange(N_CMP_PAD)[None, :, None] * CMP_STRIDE + CMP_BLOCK - 1)
                       <= (np.arange(16)[:, None, None] * QBLOCK + np.arange(QBLOCK)[None, None, :]),
                       0.0, NEG_INF).astype(np.float32)
_WIN_MASK_T = np.where((_WIN_DIST_T >= 0) & (_WIN_DIST_T < WINDOW), 0.0, NEG_INF).astype(np.float32)
_OVERLAP_T = np.array([[1.0 if (n * CMP_STRIDE < j * SEL_BLOCK + SEL_BLOCK and n * CMP_STRIDE + CMP_BLOCK > j * SEL_BLOCK
                              and n < N_CMP_PAD - 1) else 0.0 for n in range(N_CMP_PAD)] for j in range(N_SB)], np.float32)
_SB_EXPAND = (np.arange(2048)[:, None] // SEL_BLOCK == np.arange(N_SB)[None, :]).astype(np.float32)


def _c_pre_kernel(cmp_ref, sel_ref, win_ref, kn_ref, pos_ref, w1_ref, w2_ref,
                  kvc_ref, ksel_ref, vselt_ref, kwin_ref, vwin_ref):
    f32, bf16 = jnp.float32, jnp.bfloat16
    d = C_HEAD_DIM

    def rms(x, g):
        return x * lax.rsqrt(jnp.mean(x * x, axis=-1, keepdims=True) + RMS_EPS) * g

    acc = [jnp.zeros((N_CMP_PAD, 2 * PHI_HIDDEN), f32) for _ in range(CMP_GROUPS)]
    for j in range(CMP_STRIDE):
        xs = cmp_ref[pl.ds(j, N_CMP_PAD, stride=CMP_STRIDE), :]
        for half in range(CMP_GROUPS):
            jj = half * CMP_STRIDE + j
            acc[half] = acc[half] + jnp.dot((xs + pos_ref[jj:jj + 1, :]).astype(bf16), w1_ref[jj],
                                            preferred_element_type=f32)
    hid = acc[0] + pltpu.roll(acc[1], N_CMP_PAD - 1, axis=0)
    kv = jnp.dot(_silu(hid).astype(bf16), w2_ref[...], preferred_element_type=f32)
    kvc_ref[0, :, 0:d] = rms(kv[:, 0:d], kn_ref[0:1, :])
    kvc_ref[0, :, d:2 * d] = kv[:, d:2 * d]

    sel = sel_ref[...]
    ksel_ref[...] = rms(sel[:, 0:d], kn_ref[1:2, :])
    vselt_ref[0] = sel.T[d:2 * d, :]
    win = win_ref[...]
    kwin_ref[0, 0:WINDOW, :] = jnp.zeros((WINDOW, d), f32)
    vwin_ref[0, 0:WINDOW, :] = jnp.zeros((WINDOW, d), f32)
    kwin_ref[0, WINDOW:, :] = rms(win[:, 0:d], kn_ref[2:3, :])
    vwin_ref[0, WINDOW:, :] = win[:, d:2 * d]


def c_pre(proj, b, s, k_norm, cmp_pos, phi_w1, phi_w2):
    d = C_HEAD_DIM
    kv0 = OFF["c_kv"] // 128
    pos = jnp.concatenate([cmp_pos[0], cmp_pos[1]], axis=-1)
    w1 = phi_w1.reshape(2, CMP_BLOCK, d, PHI_HIDDEN)
    zero = jnp.zeros((CMP_BLOCK, d, PHI_HIDDEN), jnp.float32)
    w1c = jnp.concatenate([jnp.concatenate([w1[0], zero], axis=-1),
                           jnp.concatenate([zero, w1[1]], axis=-1)], axis=1).astype(jnp.bfloat16)
    z2 = jnp.zeros((PHI_HIDDEN, d), jnp.float32)
    w2c = jnp.concatenate([jnp.concatenate([phi_w2[0], z2], axis=-1),
                           jnp.concatenate([z2, phi_w2[1]], axis=-1)], axis=0).astype(jnp.bfloat16)
    full = lambda a: pl.BlockSpec(a.shape, lambda bb: (0,) * a.ndim)
    return pl.pallas_call(
        _c_pre_kernel,
        grid=(b,),
        in_specs=[
            pl.BlockSpec((s, 128), lambda bb: (bb, kv0)),
            pl.BlockSpec((s, 128), lambda bb: (bb, kv0 + 1)),
            pl.BlockSpec((s, 128), lambda bb: (bb, kv0 + 2)),
            full(k_norm), full(pos), full(w1c), full(w2c),
        ],
        out_specs=[
            pl.BlockSpec((1, N_CMP_PAD, 128), lambda bb: (bb, 0, 0)),
            pl.BlockSpec((s, d), lambda bb: (bb, 0)),
            pl.BlockSpec((1, d, s), lambda bb: (bb, 0, 0)),
            pl.BlockSpec((1, s + WINDOW, d), lambda bb: (bb, 0, 0)),
            pl.BlockSpec((1, s + WINDOW, d), lambda bb: (bb, 0, 0)),
        ],
        out_shape=[
            jax.ShapeDtypeStruct((b, N_CMP_PAD, 128), jnp.float32),
            jax.ShapeDtypeStruct((b * s, d), jnp.float32),
            jax.ShapeDtypeStruct((b, d, s), jnp.float32),
            jax.ShapeDtypeStruct((b, s + WINDOW, d), jnp.float32),
            jax.ShapeDtypeStruct((b, s + WINDOW, d), jnp.float32),
        ],
        compiler_params=pltpu.CompilerParams(dimension_semantics=("arbitrary",), vmem_limit_bytes=VMEM_LIMIT),
        name="c_pre",
    )(proj, proj, proj, k_norm, pos, w1c, w2c)


C_QPS = 2


def _c_pair_kernel(cq_ref, z_ref, misct_ref, gq_ref, kvc_ref, ksel_ref, vselt_ref, kwin_ref, vwin_ref,
                   cmptab_ref, near_ref, farrow_ref, wintab_ref, ovt_ref, exp_ref, o_ref, ls_scr):
    s = ksel_ref.shape[0]
    d = C_HEAD_DIM
    hq = C_HEADS * QBLOCK
    f32, bf16 = jnp.float32, jnp.bfloat16
    nt = (((1,), (1,)), ((), ()))
    tn = (((0,), (0,)), ((), ()))

    def softmax_rows(l, valid):
        l = jnp.where(valid, l, NEG_INF)
        m = _col_reduce(l, jnp.max)
        p = jnp.where(valid, jnp.exp(l - m), 0.0)
        den = _col_reduce(p, jnp.sum)
        return p, den

    def block(g, t0):
        nrows = t0 + QBLOCK
        qrows = slice(g * QBLOCK, (g + 1) * QBLOCK)

        cq = cq_ref[qrows, :]
        qs = jnp.concatenate([cq[:, h * d:(h + 1) * d] for h in range(C_HEADS)], axis=0)
        qs = qs * lax.rsqrt(jnp.mean(qs * qs, axis=-1, keepdims=True) + RMS_EPS) * gq_ref[...] * (d ** -0.5)
        qs = qs.astype(bf16)

        kvc = kvc_ref[0]
        lc = lax.dot_general(kvc[:, 0:d].astype(bf16), qs, nt, preferred_element_type=f32) + cmptab_ref[g]
        if t0 < CMP_BLOCK:
            n_idx = lax.broadcasted_iota(jnp.int32, (N_CMP_PAD, hq), 0)
            t_c = t0 + (lax.broadcasted_iota(jnp.int32, (N_CMP_PAD, hq), 1) & (QBLOCK - 1))
            pc, den_c = softmax_rows(lc, n_idx * CMP_STRIDE + (CMP_BLOCK - 1) <= t_c)
            pc = pc * jnp.where(den_c > 0.0, 1.0 / den_c, 0.0)
        else:
            pc = jnp.exp(lc - _col_reduce(lc, jnp.max))
            pc = pc * (1.0 / _col_reduce(pc, jnp.sum))
        o_cmp = lax.dot_general(kvc[:, d:2 * d].astype(bf16), pc.astype(bf16), tn, preferred_element_type=f32)

        psum = pc[:, 0:QBLOCK]
        for h in range(1, C_HEADS):
            psum = psum + pc[:, h * QBLOCK:(h + 1) * QBLOCK]
        p_hi = psum.astype(bf16)
        p_lo = (psum - p_hi.astype(f32)).astype(bf16)
        ovt = ovt_ref[...]
        imp = jnp.dot(ovt, p_hi, preferred_element_type=f32) + jnp.dot(ovt, p_lo, preferred_element_type=f32)
        j_idx = lax.broadcasted_iota(jnp.int32, (N_SB, QBLOCK), 0)
        t_b = t0 + lax.broadcasted_iota(jnp.int32, (N_SB, QBLOCK), 1)
        cur = lax.shift_right_arithmetic(t_b, SEL_BLOCK.bit_length() - 1)
        forced = (j_idx == 0) | (j_idx == cur) | (j_idx == jnp.maximum(cur - 1, 0))
        imp = jnp.where(j_idx * SEL_BLOCK <= t_b, jnp.where(forced, FORCE_SCORE, imp), NEG_INF)
        rank = jnp.zeros((N_SB, QBLOCK), f32)
        for r in range(N_SB):
            row = imp[r:r + 1, :]
            rank = rank + jnp.where((row > imp) | ((row == imp) & (j_idx > r)), 1.0, 0.0)
        picked = jnp.where(rank < float(N_SEL), 1.0, 0.0).astype(bf16)

        kw = kwin_ref[0, t0:t0 + WIN_KEYS, :]
        vw = vwin_ref[0, t0:t0 + WIN_KEYS, :]
        lw = lax.dot_general(kw.astype(bf16), qs, nt, preferred_element_type=f32) + wintab_ref[...]
        if t0 < WINDOW:
            r_idx = lax.broadcasted_iota(jnp.int32, (WIN_KEYS, hq), 0)
            lw = jnp.where(r_idx >= WINDOW - t0, lw, NEG_INF)
        pw = jnp.exp(lw - _col_reduce(lw, jnp.max))
        den_w = _col_reduce(pw, jnp.sum)
        o_win = lax.dot_general(vw.astype(bf16), pw.astype(bf16), tn, preferred_element_type=f32) / den_w

        gate = _sigmoid(misct_ref[0, C_G_LANE:C_G_LANE + 3 * C_HEADS, qrows])

        in_picked = jnp.dot(exp_ref[0:nrows, :], picked, preferred_element_type=f32) > 0.5
        sel_bias = jnp.where(in_picked, 0.0, NEG_INF)
        tri = jnp.where(lax.broadcasted_iota(jnp.int32, (QBLOCK, QBLOCK), 0)
                        <= lax.broadcasted_iota(jnp.int32, (QBLOCK, QBLOCK), 1), 0.0, NEG_INF)
        tail = sel_bias[t0:nrows] + tri
        sel_bias = jnp.concatenate([sel_bias[0:t0], tail], axis=0) if t0 else tail
        ls_scr[0:QBLOCK, :] = jnp.zeros((QBLOCK, hq), f32)
        ls_scr[QBLOCK:QBLOCK + nrows, :] = lax.dot_general(ksel_ref[0:nrows, :].astype(bf16), qs, nt,
                                                           preferred_element_type=f32)
        ls_scr[t0:t0 + 2 * QBLOCK, :] += near_ref[...] - farrow_ref[...]
        outs = []
        for h in range(C_HEADS):
            cols = slice(h * QBLOCK, (h + 1) * QBLOCK)
            l = ls_scr[QBLOCK:QBLOCK + nrows, cols] + sel_bias
            m = _col_reduce(l, jnp.max)
            p = jnp.exp(l - m)
            den = _col_reduce(p, jnp.sum)
            o_sel = jnp.dot(vselt_ref[0, :, 0:nrows].astype(bf16), p.astype(bf16), preferred_element_type=f32) / den
            outs.append(gate[3 * h:3 * h + 1, :] * o_cmp[:, cols] + gate[3 * h + 1:3 * h + 2, :] * o_sel
                        + gate[3 * h + 2:3 * h + 3, :] * o_win[:, cols])
        o_ref[qrows, :] = jnp.concatenate(outs, axis=0).T * _silu(z_ref[qrows, :])

    def run(step):
        for g in range(C_QPS):
            block(g, (C_QPS * step + g) * QBLOCK)

    step = pl.program_id(1)
    for v in range(s // (C_QPS * QBLOCK)):
        pl.when(step == v)(functools.partial(run, v))


def c_main(proj, misct, gq, kvc, ksel, vselt, kwin, vwin, cmptab, near, farrow, wintab, b, s):
    n = b * s
    tq = C_QPS * QBLOCK
    nq = s // tq
    d = C_HEAD_DIM
    hq = C_HEADS * QBLOCK
    row = lambda bb, i: bb * nq + i
    ovt = jnp.asarray(_OVERLAP_T, jnp.bfloat16)
    expand = jnp.asarray(_SB_EXPAND, jnp.bfloat16)
    const = lambda a: pl.BlockSpec(a.shape, lambda bb, i: (0,) * a.ndim)
    return pl.pallas_call(
        _c_pair_kernel,
        grid=(b, nq),
        in_specs=[
            pl.BlockSpec((tq, C_WIDTH), lambda bb, i: (row(bb, i), OFF["c_q"] // C_WIDTH)),
            pl.BlockSpec((tq, C_WIDTH), lambda bb, i: (row(bb, i), OFF["c_z"] // C_WIDTH)),
            pl.BlockSpec((1, 128, tq), lambda bb, i: (bb, 0, i)),
            const(gq),
            pl.BlockSpec((1, N_CMP_PAD, 128), lambda bb, i: (bb, 0, 0)),
            pl.BlockSpec((s, d), lambda bb, i: (bb, 0)),
            pl.BlockSpec((1, d, s), lambda bb, i: (bb, 0, 0)),
            pl.BlockSpec((1, s + WINDOW, d), lambda bb, i: (bb, 0, 0)),
            pl.BlockSpec((1, s + WINDOW, d), lambda bb, i: (bb, 0, 0)),
            pl.BlockSpec((C_QPS, N_CMP_PAD, hq), lambda bb, i: (i, 0, 0)),
            const(near), const(farrow), const(wintab), const(ovt), const(expand),
        ],
        out_specs=pl.BlockSpec((tq, C_WIDTH), lambda bb, i: (row(bb, i), 0)),
        out_shape=jax.ShapeDtypeStruct((n, C_WIDTH), jnp.float32),
        scratch_shapes=[pltpu.VMEM((s + QBLOCK, hq), jnp.float32)],
        compiler_params=pltpu.CompilerParams(
            dimension_semantics=("arbitrary", "arbitrary"), vmem_limit_bytes=VMEM_LIMIT),
        name="c_main",
    )(proj, proj, misct, gq, kvc, ksel, vselt, kwin, vwin, cmptab, near, farrow, wintab, ovt, expand)


def _head_cols(tab):
    return jnp.moveaxis(tab, -1, -2).reshape(*tab.shape[:-2], tab.shape[-1] * tab.shape[-2])


def mixer_c(proj, misct, b, s, gq, k_norm, cmp_pos, phi_w1, phi_w2, bias_c):
    kvc, ksel, vselt, kwin, vwin = c_pre(proj, b, s, k_norm, cmp_pos, phi_w1, phi_w2)
    cmptab = _head_cols(_bias_lookup(bias_c, _CMP_BUCKET_T) + jnp.asarray(_CMP_MASK_T)[..., None])
    near = _head_cols(_bias_lookup(bias_c, _NEAR_BUCKET_T))
    wintab = _head_cols(_bias_lookup(bias_c, _WIN_BUCKET_T) + jnp.asarray(_WIN_MASK_T)[..., None])
    farrow = jnp.repeat(bias_c[_FAR_BUCKET], QBLOCK).reshape(1, C_HEADS * QBLOCK)
    return c_main(proj, misct, gq.reshape(1, C_HEAD_DIM), kvc, ksel, vselt, kwin, vwin, cmptab, near, farrow,
                  wintab, b, s)


B_HPS = 2
B_GROUP = 4


def _b_kernel(alog_ref, dtb_ref, q_ref, k_ref, v_ref, z_ref, misc_ref, arow_ref, cwq_ref, cwk_ref, cwv_ref,
              gon_ref, o_ref, qs, ks, vs, ws, at, gtok, btok, gcrow):
    s = q_ref.shape[0]
    dh = B_HEAD_DIM
    c_len = GDN_CHUNK
    j = pl.program_id(1)
    f32, bf16 = jnp.float32, jnp.bfloat16
    hi = lax.Precision.HIGHEST
    nt = (((1,), (1,)), ((), ()))
    tn = (((0,), (0,)), ((), ()))

    head_rows = lax.broadcasted_iota(jnp.int32, (SUBLANES, B_HPS * dh), 0)

    def conv_silu(x_ref, w_ref):
        x8 = x_ref[0:SUBLANES, :]
        w_last = w_ref[CONV_WIDTH - 1:CONV_WIDTH, :]
        head = x8 * w_last
        body = x_ref[SUBLANES:s, :] * w_last
        for k in range(1, CONV_WIDTH):
            wk = w_ref[CONV_WIDTH - 1 - k:CONV_WIDTH - k, :]
            head = head + jnp.where(head_rows >= k, pltpu.roll(x8, k, axis=0), 0.0) * wk
            body = body + x_ref[SUBLANES - k:s - k, :] * wk
        return _silu(jnp.concatenate([head, body], axis=0))

    def l2n(x):
        return x * lax.rsqrt(jnp.sum(x * x, axis=-1, keepdims=True) + RMS_EPS)

    qc = conv_silu(q_ref, cwq_ref)
    kc = conv_silu(k_ref, cwk_ref)
    vs[...] = conv_silu(v_ref, cwv_ref)
    misc = misc_ref[...]
    lane = lax.broadcasted_iota(jnp.int32, misc.shape, 1)
    ri = lax.broadcasted_iota(jnp.int32, (c_len, c_len), 0)
    ci = lax.broadcasted_iota(jnp.int32, (c_len, c_len), 1)
    lower = ci <= ri
    strict = ci < ri
    tri_u = jnp.where(ri <= ci, 1.0, 0.0)
    row_in_chunk = lax.broadcasted_iota(jnp.int32, (s, dh), 0) & (c_len - 1)
    heads = [B_HPS * j + hh for hh in range(B_HPS)]
    neg_a = [-jnp.exp(alog_ref[h]) for h in heads]
    lane_row = lax.broadcasted_iota(jnp.int32, (1, misc.shape[1]), 1)
    dtb_row = jnp.zeros((1, misc.shape[1]), f32)
    nega_row = jnp.zeros((1, misc.shape[1]), f32)
    for hh, h in enumerate(heads):
        dtb_row = jnp.where(lane_row == B_A_LANE + h, dtb_ref[h], dtb_row)
        nega_row = jnp.where(lane_row == B_A_LANE + h, neg_a[hh], nega_row)
    g_all = nega_row * jax.nn.softplus(misc + dtb_row)
    for sh in (1, 2, 4, 8, 16, 32):
        g_all = g_all + jnp.where(row_in_chunk >= sh, pltpu.roll(g_all, sh, axis=0), 0.0)
    beta_all = _sigmoid(misc)
    for hh, h in enumerate(heads):
        sl = slice(hh * dh, (hh + 1) * dh)
        qs[:, sl] = l2n(qc[:, sl]) * (dh ** -0.5)
        ks[:, sl] = l2n(kc[:, sl])
        g_tok = jnp.sum(jnp.where(lane == B_A_LANE + h, g_all, 0.0), axis=-1, keepdims=True)
        b_tok = jnp.sum(jnp.where(lane == B_B_LANE + h, beta_all, 0.0), axis=-1, keepdims=True)
        gtok[hh] = jnp.broadcast_to(g_tok, (s, dh))
        btok[hh] = jnp.broadcast_to(b_tok, (s, dh))
        g_row = neg_a[hh] * jax.nn.softplus(arow_ref[0, hh] + dtb_ref[h])
        gcrow[hh] = jnp.dot(g_row, tri_u, precision=hi, preferred_element_type=f32)

    lane3 = lax.broadcasted_iota(jnp.int32, (c_len, 3 * c_len), 1)
    lo_lanes = (lane3 >= c_len) & (lane3 < 2 * c_len)

    def split_lhs(p):
        p4 = jnp.concatenate([p, p, p], axis=1)
        hi4 = p4.astype(bf16).astype(f32)
        return jnp.where(lo_lanes, p4 - hi4, hi4).astype(bf16)

    def split_rhs(x):
        xh = x.astype(bf16)
        xl = (x - xh.astype(f32)).astype(bf16)
        return jnp.concatenate([xh, xh, xl], axis=0)

    def mm(p_split, x):
        return jnp.dot(p_split, split_rhs(x), preferred_element_type=f32)

    hs = range(B_HPS)
    sls = [slice(hh * dh, (hh + 1) * dh) for hh in hs]
    n_groups = s // (B_GROUP * c_len)

    def chunk_rows(c):
        return pl.ds(pl.multiple_of(c * c_len, c_len), c_len)

    def intra_group(i, between):
        where, q_c, k_c, v_c, gb, beta, g_row = [], [], [], [], [], [], []
        for c in [B_GROUP * i + cc for cc in range(B_GROUP)]:
            rs = chunk_rows(c)
            for hh in hs:
                where.append((rs, hh))
                q_c.append(qs[rs, sls[hh]])
                k_c.append(ks[rs, sls[hh]])
                v_c.append(vs[rs, sls[hh]])
                gb.append(gtok[hh, rs, :])
                beta.append(btok[hh, rs, :])
                g_row.append(gcrow[hh, pl.ds(c, 1), :])
        nch = range(len(where))
        decay = [jnp.where(lower, jnp.exp(jnp.where(lower, gb[n][:, 0:c_len] - g_row[n], 0.0)), 0.0) for n in nch]
        kb = [k_c[n] * beta[n] for n in nch]
        kk = [lax.dot_general(kb[n].astype(bf16), k_c[n].astype(bf16), nt, preferred_element_type=f32) for n in nch]
        between()
        p = [-jnp.where(strict, kk[n] * decay[n], 0.0) for n in nch]
        eg = [jnp.exp(gb[n]) for n in nch]
        x = [jnp.concatenate([v_c[n] * beta[n], kb[n] * eg[n]], axis=1) for n in nch]
        ps = [split_lhs(p[n]) for n in nch]
        x = [x[n] + mm(ps[n], x[n]) for n in nch]
        between()
        for _ in range(5):
            p = [mm(ps[n], p[n]) for n in nch]
            between()
            ps = [split_lhs(p[n]) for n in nch]
            x = [x[n] + mm(ps[n], x[n]) for n in nch]
            between()
        attn = [jnp.where(lower, lax.dot_general(q_c[n].astype(bf16), k_c[n].astype(bf16), nt,
                                                 preferred_element_type=f32) * decay[n], 0.0) for n in nch]
        between()

        def commit():
            for n, (rs, hh) in enumerate(where):
                g_last = g_row[n][:, c_len - 1:c_len]
                vs[rs, sls[hh]] = x[n][:, 0:dh]
                ws[rs, sls[hh]] = x[n][:, dh:2 * dh]
                qs[rs, sls[hh]] = q_c[n] * eg[n]
                ks[rs, sls[hh]] = k_c[n] * jnp.exp(g_last - gb[n])
                at[rs, hh * c_len:(hh + 1) * c_len] = attn[n]

        return commit

    def recur_steps(group, states, result):
        for c in [B_GROUP * group + cc for cc in range(B_GROUP)]:
            rs = chunk_rows(c)
            st_b = [states[hh].astype(bf16) for hh in hs]
            v_new = [vs[rs, sls[hh]] - jnp.dot(ws[rs, sls[hh]].astype(bf16), st_b[hh], preferred_element_type=f32)
                     for hh in hs]
            o_st = [jnp.dot(qs[rs, sls[hh]].astype(bf16), st_b[hh], preferred_element_type=f32) for hh in hs]
            yield
            v_nb = [v_new[hh].astype(bf16) for hh in hs]
            states = tuple(states[hh] * jnp.exp(gcrow[hh, pl.ds(c, 1), c_len - 1:c_len])
                           + lax.dot_general(ks[rs, sls[hh]].astype(bf16), v_nb[hh], tn, preferred_element_type=f32)
                           for hh in hs)
            for hh in hs:
                o_ref[rs, sls[hh]] = o_st[hh] + jnp.dot(at[rs, hh * c_len:(hh + 1) * c_len].astype(bf16), v_nb[hh],
                                                        preferred_element_type=f32)
            yield
        result.append(states)

    def overlapped(g, states):
        result = []
        steps = recur_steps(g - 1, states, result)
        commit = intra_group(g, lambda: next(steps, None))
        for _ in steps:
            pass
        commit()
        return result[0]

    intra_group(0, lambda: None)()
    states = lax.fori_loop(1, n_groups, overlapped, tuple(jnp.zeros((dh, dh), f32) for _ in hs))
    for _ in recur_steps(n_groups - 1, states, []):
        pass

    for hh in range(B_HPS):
        sl = slice(hh * dh, (hh + 1) * dh)
        o = o_ref[:, sl]
        o = o * lax.rsqrt(jnp.mean(o * o, axis=-1, keepdims=True) + RMS_EPS) * gon_ref[...]
        o_ref[:, sl] = o * _silu(z_ref[:, sl])


def mixer_b(proj, misct, b, s, conv_w, a_log, dt_bias, out_norm):
    n = b * s
    w2 = B_HPS * B_HEAD_DIM
    nc = s // GDN_CHUNK
    arow = misct[:, B_A_LANE:B_A_LANE + B_HEADS, :].reshape(b, B_HEADS, nc, GDN_CHUNK)
    qkv0 = OFF["b_qkv"] // w2
    kstep = B_WIDTH // w2
    smem = pl.BlockSpec(memory_space=pltpu.SMEM)
    return pl.pallas_call(
        _b_kernel,
        grid=(b, B_HEADS // B_HPS),
        in_specs=[
            smem, smem,
            pl.BlockSpec((s, w2), lambda bb, j: (bb, qkv0 + j)),
            pl.BlockSpec((s, w2), lambda bb, j: (bb, qkv0 + kstep + j)),
            pl.BlockSpec((s, w2), lambda bb, j: (bb, qkv0 + 2 * kstep + j)),
            pl.BlockSpec((s, w2), lambda bb, j: (bb, OFF["b_z"] // w2 + j)),
            pl.BlockSpec((s, 128), lambda bb, j: (bb, MISC_OFF // 128)),
            pl.BlockSpec((1, B_HPS, nc, GDN_CHUNK), lambda bb, j: (bb, j, 0, 0)),
            pl.BlockSpec((CONV_WIDTH, w2), lambda bb, j: (0, j)),
            pl.BlockSpec((CONV_WIDTH, w2), lambda bb, j: (0, kstep + j)),
            pl.BlockSpec((CONV_WIDTH, w2), lambda bb, j: (0, 2 * kstep + j)),
            pl.BlockSpec((1, B_HEAD_DIM), lambda bb, j: (0, 0)),
        ],
        out_specs=pl.BlockSpec((s, w2), lambda bb, j: (bb, j)),
        out_shape=jax.ShapeDtypeStruct((n, B_WIDTH), jnp.float32),
        scratch_shapes=[
            pltpu.VMEM((s, w2), jnp.float32), pltpu.VMEM((s, w2), jnp.float32), pltpu.VMEM((s, w2), jnp.float32),
            pltpu.VMEM((s, w2), jnp.float32), pltpu.VMEM((s, B_HPS * GDN_CHUNK), jnp.float32),
            pltpu.VMEM((B_HPS, s, B_HEAD_DIM), jnp.float32), pltpu.VMEM((B_HPS, s, B_HEAD_DIM), jnp.float32),
            pltpu.VMEM((B_HPS, nc, GDN_CHUNK), jnp.float32),
        ],
        compiler_params=pltpu.CompilerParams(
            dimension_semantics=("arbitrary", "arbitrary"), vmem_limit_bytes=VMEM_LIMIT),
        name="gdn",
    )(a_log, dt_bias, proj, proj, proj, proj, proj, arow, conv_w, conv_w, conv_w, out_norm.reshape(1, B_HEAD_DIM))


def kernel(x, norm_g, w_in, a_kv_norm, a_w_ukv, a_q_norm, a_k_norm, b_conv, b_a_log, b_dt_bias, b_out_norm, c_q_norm, c_k_norm, c_cmp_pos, c_phi_w1, c_phi_w2, w_branch, w_out, rel_bias):
    b, s, _ = x.shape
    n = b * s
    bias_a = rel_bias[:, :A_HEADS]
    bias_c = rel_bias[:, A_HEADS:]
    xf = x.reshape(n, D_MODEL)
    fields = [w_in[:, :, _ORIG[name][0]:_ORIG[name][0] + _ORIG[name][1]] for name in _NEW_ORDER]
    pad = jnp.zeros((DEPTH, D_MODEL, N_PAD - N_IN), w_in.dtype)
    w_in_p = jnp.concatenate(fields + [pad], axis=-1).astype(jnp.bfloat16)
    for l in range(DEPTH):
        proj = in_proj(xf, norm_g[l], w_in_p[l])
        p3 = proj.reshape(b, s, N_PAD)

        misct = jnp.transpose(p3[..., MISC_OFF:], (0, 2, 1))
        y_a = mixer_a(proj, misct, b, s, a_kv_norm[l], a_w_ukv[l], a_q_norm[l], a_k_norm[l], bias_a)

        y_b = mixer_b(proj, misct, b, s, b_conv[l], b_a_log[l], b_dt_bias[l], b_out_norm[l])
        y_c = mixer_c(proj, misct, b, s, c_q_norm[l], c_k_norm[l], c_cmp_pos[l], c_phi_w1[l], c_phi_w2[l], bias_c)

        wbr = w_branch[l].astype(jnp.bfloat16)
        xf = merge(xf, proj, y_a, y_b, y_c, wbr[:A_WIDTH], wbr[A_WIDTH:A_WIDTH + B_WIDTH],
                   wbr[A_WIDTH + B_WIDTH:], w_out[l].astype(jnp.bfloat16))
    return xf.reshape(b, s, D_MODEL)
```

```python
import functools
import math

import jax
import jax.numpy as jnp
import numpy as np
from jax import lax
from jax.experimental import pallas as pl
from jax.experimental.pallas import tpu as pltpu

D_MODEL = 1024
DEPTH = 4
QBLOCK = 128
NEG_INF = -1e30
FORCE_SCORE = 1e9
RMS_EPS = 1e-6

A_HEADS = 4
A_HEAD_DIM = 64
A_WIDTH = A_HEADS * A_HEAD_DIM
A_KV_LATENT = 128
IDX_HEADS = 8
IDX_DIM = 64
TOPK_MAX = 256

B_HEADS = 4
B_HEAD_DIM = 128
B_WIDTH = B_HEADS * B_HEAD_DIM
CONV_WIDTH = 4
GDN_CHUNK = 64

C_HEADS = 4
C_HEAD_DIM = 64
C_WIDTH = C_HEADS * C_HEAD_DIM
CMP_BLOCK = 32
CMP_STRIDE = 16
SEL_BLOCK = 64
N_SEL = 16
WINDOW = 512
PHI_HIDDEN = 256

N_BUCKETS = 32
BUCKET_MAX_EXACT = 16
BUCKET_MAX_DIST = 128

VMEM_LIMIT = 48 * 1024 * 1024
SUBLANES = 8
KEY_BITS = 32

_ORIG = {}
_o = 0
for _name, _w in (("a_q", 256), ("a_ckv", 128), ("a_qi", 512), ("a_ki", 64), ("a_wi", 8), ("a_z", 256),
                  ("b_qkv", 1536), ("b_a", 4), ("b_b", 4), ("b_z", 512),
                  ("c_q", 256), ("c_kv", 384), ("c_g", 12), ("c_z", 256), ("g", 3072)):
    _ORIG[_name] = (_o, _w)
    _o += _w
N_IN = _o

_NEW_ORDER = ("g", "b_qkv", "b_z", "a_qi", "a_q", "a_z", "c_q", "c_z", "c_kv", "a_ckv",
              "a_ki", "a_wi", "b_a", "b_b", "c_g")
OFF = {}
_o = 0
for _name in _NEW_ORDER:
    OFF[_name] = _o
    _o += _ORIG[_name][1]
MISC_OFF = OFF["a_ki"]
N_PAD = 7296
assert MISC_OFF == 7168 and _o <= N_PAD
A_WI_LANE = OFF["a_wi"] - MISC_OFF
B_A_LANE = OFF["b_a"] - MISC_OFF
B_B_LANE = OFF["b_b"] - MISC_OFF
C_G_LANE = OFF["c_g"] - MISC_OFF


IN_TM = 1024
IN_TN = N_PAD // 3


def _in_proj_kernel(x_ref, g_ref, w_ref, o_ref):
    x = x_ref[...]
    ms = jnp.mean(x * x, axis=-1, keepdims=True)
    h = (x * lax.rsqrt(ms + RMS_EPS)) * g_ref[...]
    o_ref[...] = jnp.dot(h.astype(jnp.bfloat16), w_ref[...], preferred_element_type=jnp.float32)


def in_proj(xf, g, w_bf16):
    n = xf.shape[0]
    return pl.pallas_call(
        _in_proj_kernel,
        grid=(N_PAD // IN_TN, n // IN_TM),
        in_specs=[
            pl.BlockSpec((IN_TM, D_MODEL), lambda j, i: (i, 0)),
            pl.BlockSpec((1, D_MODEL), lambda j, i: (0, 0)),
            pl.BlockSpec((D_MODEL, IN_TN), lambda j, i: (0, j)),
        ],
        out_specs=pl.BlockSpec((IN_TM, IN_TN), lambda j, i: (i, j)),
        out_shape=jax.ShapeDtypeStruct((n, N_PAD), jnp.float32),
        compiler_params=pltpu.CompilerParams(
            dimension_semantics=("arbitrary", "arbitrary"), vmem_limit_bytes=VMEM_LIMIT),
        name="in_proj",
    )(xf, g.reshape(1, D_MODEL), w_bf16)


MG_TM = 512


def _merge_kernel(x_ref, g_ref, ya_ref, yb_ref, yc_ref, wa_ref, wb_ref, wc_ref, wo_ref, o_ref):
    def branch(y_ref, w_ref, k):
        p = jnp.dot(y_ref[...].astype(jnp.bfloat16), w_ref[...], preferred_element_type=jnp.float32)
        return _sigmoid(g_ref[:, k * D_MODEL:(k + 1) * D_MODEL]) * p

    merged = branch(ya_ref, wa_ref, 0) + branch(yb_ref, wb_ref, 1) + branch(yc_ref, wc_ref, 2)
    o_ref[...] = x_ref[...] + jnp.dot(merged.astype(jnp.bfloat16), wo_ref[...],
                                      preferred_element_type=jnp.float32)


def merge(xf, proj, ya, yb, yc, wa, wb, wc, wo):
    n = xf.shape[0]
    row = lambda w: pl.BlockSpec((MG_TM, w), lambda i: (i, 0))
    full = lambda a: pl.BlockSpec(a.shape, lambda i: (0, 0))
    return pl.pallas_call(
        _merge_kernel,
        grid=(n // MG_TM,),
        in_specs=[row(D_MODEL), row(3 * D_MODEL), row(A_WIDTH), row(B_WIDTH), row(C_WIDTH),
                  full(wa), full(wb), full(wc), full(wo)],
        out_specs=row(D_MODEL),
        out_shape=jax.ShapeDtypeStruct((n, D_MODEL), jnp.float32),
        compiler_params=pltpu.CompilerParams(
            dimension_semantics=("arbitrary",), vmem_limit_bytes=VMEM_LIMIT),
        name="merge",
    )(xf, proj, ya, yb, yc, wa, wb, wc, wo)


def _bucket_np(dist):
    d = np.maximum(np.asarray(dist, np.int64), 0)
    ratio = np.log(np.maximum(d, 1).astype(np.float64) / BUCKET_MAX_EXACT) / math.log(BUCKET_MAX_DIST / BUCKET_MAX_EXACT)
    scaled = ratio * (N_BUCKETS - BUCKET_MAX_EXACT)
    frac = scaled - np.floor(scaled)
    edge = (d > BUCKET_MAX_EXACT) & (d < BUCKET_MAX_DIST) & ((frac < 1e-4) | (frac > 1 - 1e-4))
    assert not edge.any(), "bucket boundary too close to an integer distance"
    large = np.minimum(BUCKET_MAX_EXACT + np.floor(scaled + 1e-9).astype(np.int64), N_BUCKETS - 1)
    return np.where(d < BUCKET_MAX_EXACT, d, large).astype(np.int32)


_NEAR_BUCKET_T = _bucket_np(np.arange(QBLOCK)[None, :] + QBLOCK - np.arange(2 * QBLOCK)[:, None])
_FAR_BUCKET = int(_bucket_np(np.array([BUCKET_MAX_DIST]))[0])
assert (_bucket_np(np.arange(BUCKET_MAX_DIST, 4096)) == _FAR_BUCKET).all()

INT_MIN = -2 ** 31


COL_SLAB = 64


def _col_reduce(x, op):
    rows = x.shape[0]
    if rows > COL_SLAB and rows % COL_SLAB == 0:
        x = op(x.reshape(rows // COL_SLAB, COL_SLAB, x.shape[1]), axis=0)
    return op(x, axis=0, keepdims=True)


def _sigmoid(x):
    return 0.5 * jnp.tanh(0.5 * x) + 0.5


def _silu(x):
    return x * _sigmoid(x)


def _bias_lookup(table, buckets):
    idx = jnp.asarray(buckets)[..., None]
    out = jnp.zeros(idx.shape[:-1] + (table.shape[1],), table.dtype)
    for bkt in range(N_BUCKETS):
        out = jnp.where(idx == bkt, table[bkt], out)
    return out


A_PRE_TM = 512


def _a_pre_kernel(ckv_ref, q_ref, gkv_ref, wukv_ref, gq_ref, gk_ref, hm_ref, kn_ref, vt_ref, qn_ref):
    c = ckv_ref[...]
    c = c * lax.rsqrt(jnp.mean(c * c, axis=-1, keepdims=True) + RMS_EPS) * gkv_ref[...]
    kv = jnp.dot(c.astype(jnp.bfloat16), wukv_ref[...], preferred_element_type=jnp.float32)
    hm = hm_ref[...]

    def head_rms(x, g):
        ms = jnp.dot(x * x, hm, precision=lax.Precision.HIGHEST, preferred_element_type=jnp.float32)
        return x * lax.rsqrt(ms + RMS_EPS) * g

    kn_ref[...] = head_rms(kv[:, :A_WIDTH], gk_ref[...])
    qn_ref[...] = head_rms(q_ref[...], gq_ref[...]) * (A_HEAD_DIM ** -0.5)
    vt_ref[0] = kv[:, A_WIDTH:].T


def a_pre(proj, b, s, gkv, wukv_bf16, gq, gk):
    n = b * s
    nt = s // A_PRE_TM
    hm = jnp.asarray(np.kron(np.eye(A_HEADS), np.ones((A_HEAD_DIM, A_HEAD_DIM))) / A_HEAD_DIM, jnp.float32)
    row = lambda bb, j: (bb * nt + j, 0)
    full = lambda a: pl.BlockSpec(a.shape, lambda bb, j: (0,) * a.ndim)
    gq = jnp.tile(gq, A_HEADS).reshape(1, A_WIDTH)
    gk = jnp.tile(gk, A_HEADS).reshape(1, A_WIDTH)
    gkv = gkv.reshape(1, A_KV_LATENT)
    return pl.pallas_call(
        _a_pre_kernel,
        grid=(b, nt),
        in_specs=[
            pl.BlockSpec((A_PRE_TM, A_KV_LATENT), lambda bb, j: (bb * nt + j, OFF["a_ckv"] // A_KV_LATENT)),
            pl.BlockSpec((A_PRE_TM, A_WIDTH), lambda bb, j: (bb * nt + j, OFF["a_q"] // A_WIDTH)),
            full(gkv), full(wukv_bf16), full(gq), full(gk), full(hm),
        ],
        out_specs=[
            pl.BlockSpec((A_PRE_TM, A_WIDTH), row),
            pl.BlockSpec((1, A_WIDTH, A_PRE_TM), lambda bb, j: (bb, 0, j)),
            pl.BlockSpec((A_PRE_TM, A_WIDTH), row),
        ],
        out_shape=[
            jax.ShapeDtypeStruct((n, A_WIDTH), jnp.float32),
            jax.ShapeDtypeStruct((b, A_WIDTH, s), jnp.float32),
            jax.ShapeDtypeStruct((n, A_WIDTH), jnp.float32),
        ],
        compiler_params=pltpu.CompilerParams(
            dimension_semantics=("arbitrary", "arbitrary"), vmem_limit_bytes=VMEM_LIMIT),
        name="a_pre",
    )(proj, proj, gkv, wukv_bf16, gq, gk, hm)


A_QPS = 8


def _a_pair_kernel(far_ref, qn_ref, qi_ref, z_ref, misct_ref, kn_ref, vt_ref, ki_ref, near_ref, o_ref,
                   key_scr, lg_scr, j_scr):
    s = kn_ref.shape[0]
    f32, bf16 = jnp.float32, jnp.bfloat16
    nt = (((1,), (1,)), ((), ()))
    kf = float(TOPK_MAX)
    groups = range(A_QPS)

    def run(step):
        t0 = [(A_QPS * step + g) * QBLOCK for g in groups]
        nr = [t + QBLOCK for t in t0]
        qrows = [slice(g * QBLOCK, (g + 1) * QBLOCK) for g in groups]

        for g in groups:
            qi = qi_ref[qrows[g], :] * (IDX_DIM ** -0.5)
            qstack = jnp.concatenate([qi[:, h * IDX_DIM:(h + 1) * IDX_DIM] for h in range(IDX_HEADS)], axis=0)
            ki = ki_ref[0:nr[g], 0:IDX_DIM]
            sc = lax.dot_general(ki.astype(bf16), qstack.astype(bf16), nt, preferred_element_type=f32)
            wt = misct_ref[0, A_WI_LANE:A_WI_LANE + IDX_HEADS, qrows[g]] * (IDX_HEADS ** -0.5)
            score = jnp.zeros((nr[g], QBLOCK), f32)
            for h in range(IDX_HEADS):
                score = score + wt[h:h + 1, :] * jnp.maximum(sc[:, h * QBLOCK:(h + 1) * QBLOCK], 0.0)
            score = score + 0.0
            bits = pltpu.bitcast(score, jnp.int32)
            key = jnp.where(bits < 0, bits ^ jnp.int32(0x7FFFFFFF), bits)
            upper = (lax.broadcasted_iota(jnp.int32, (QBLOCK, QBLOCK), 0)
                     <= lax.broadcasted_iota(jnp.int32, (QBLOCK, QBLOCK), 1))
            if t0[g]:
                key_scr[g, 0:t0[g], :] = key[0:t0[g]]
            key_scr[g, t0[g]:nr[g], :] = jnp.where(upper, key[t0[g]:nr[g]], jnp.int32(INT_MIN))

        def count_ge(g, cand):
            return _col_reduce(jnp.where(key_scr[g, 0:nr[g], :] >= cand, 1.0, 0.0), jnp.sum)

        def bisect(it, thrs):
            bit = lax.shift_left(jnp.int32(1), KEY_BITS - 1 - it)
            cnt = [count_ge(g, thrs[g] + bit) for g in groups]
            return tuple(jnp.where(cnt[g] >= kf, thrs[g] + bit, thrs[g]) for g in groups)

        thrs = lax.fori_loop(0, KEY_BITS, bisect, tuple(jnp.full((1, QBLOCK), INT_MIN, jnp.int32) for _ in groups))

        for g in groups:
            nrows, thr = nr[g], thrs[g]
            n_ge = count_ge(g, thr)
            need = kf - count_ge(g, thr + 1)
            kpos = lax.broadcasted_iota(jnp.int32, (nrows, QBLOCK), 0)

            j_scr[g] = jnp.full((1, QBLOCK), nrows - 1, jnp.int32)
            surplus = jnp.where((n_ge > kf) & (thr > INT_MIN), 1.0, 0.0)

            @pl.when(jnp.max(surplus) > 0.0)
            def _():
                def bisect_idx(it, lohi):
                    lo, hi = lohi
                    mid = lax.shift_right_arithmetic(lo + hi, 1)
                    k = key_scr[g, 0:nrows, :]
                    c = _col_reduce(jnp.where((k == thr) & (kpos <= mid), 1.0, 0.0), jnp.sum)
                    ok = c >= need
                    return jnp.where(ok, lo, mid), jnp.where(ok, mid, hi)

                lo0 = jnp.full((1, QBLOCK), -1, jnp.int32)
                hi0 = jnp.full((1, QBLOCK), nrows - 1, jnp.int32)
                _, hi = lax.fori_loop(0, (nrows - 1).bit_length(), bisect_idx, (lo0, hi0))
                j_scr[g] = hi

            key = key_scr[g, 0:nrows, :]
            thr_c = jnp.maximum(thr, INT_MIN + 1)
            tie_ok = jnp.where(kpos <= j_scr[g], 0.0, NEG_INF)
            sel_bias = jnp.where(key > thr_c, 0.0, jnp.where(key == thr_c, tie_ok, NEG_INF))

            qn = qn_ref[qrows[g], :]
            lane = lax.broadcasted_iota(jnp.int32, (QBLOCK, A_WIDTH), 1)
            qblk = jnp.concatenate(
                [jnp.where((lane >= h * A_HEAD_DIM) & (lane < (h + 1) * A_HEAD_DIM), qn, 0.0)
                 for h in range(A_HEADS)], axis=0)
            lg_scr[0:QBLOCK, :] = jnp.zeros((QBLOCK, A_HEADS * QBLOCK), f32)
            lg_scr[QBLOCK:QBLOCK + nrows, :] = lax.dot_general(kn_ref[0:nrows, :].astype(bf16), qblk.astype(bf16),
                                                               nt, preferred_element_type=f32)
            outs = []
            for h in range(A_HEADS):
                cols = slice(h * QBLOCK, (h + 1) * QBLOCK)
                lg_scr[t0[g]:t0[g] + 2 * QBLOCK, cols] += near_ref[h] - far_ref[h]
                l = lg_scr[QBLOCK:QBLOCK + nrows, cols] + sel_bias
                m = _col_reduce(l, jnp.max)
                p = jnp.exp(l - m)
                den = _col_reduce(p, jnp.sum)
                vt = vt_ref[0, h * A_HEAD_DIM:(h + 1) * A_HEAD_DIM, 0:nrows]
                o_t = jnp.dot(vt.astype(bf16), p.astype(bf16), preferred_element_type=f32)
                outs.append(o_t / den)
            o = jnp.concatenate(outs, axis=0).T
            o_ref[qrows[g], :] = o * _silu(z_ref[qrows[g], :])

    step = pl.program_id(1)
    for v in range(s // (A_QPS * QBLOCK)):
        pl.when(step == v)(functools.partial(run, v))


def a_main(proj, misct, qn, kn, vt, near_t, far, b, s):
    n = b * s
    tq = A_QPS * QBLOCK
    nq = s // tq
    row = lambda bb, i: bb * nq + i
    return pl.pallas_call(
        _a_pair_kernel,
        grid=(b, nq),
        in_specs=[
            pl.BlockSpec(memory_space=pltpu.SMEM),
            pl.BlockSpec((tq, A_WIDTH), lambda bb, i: (row(bb, i), 0)),
            pl.BlockSpec((tq, IDX_HEADS * IDX_DIM), lambda bb, i: (row(bb, i), OFF["a_qi"] // (IDX_HEADS * IDX_DIM))),
            pl.BlockSpec((tq, A_WIDTH), lambda bb, i: (row(bb, i), OFF["a_z"] // A_WIDTH)),
            pl.BlockSpec((1, 128, tq), lambda bb, i: (bb, 0, i)),
            pl.BlockSpec((s, A_WIDTH), lambda bb, i: (bb, 0)),
            pl.BlockSpec((1, A_WIDTH, s), lambda bb, i: (bb, 0, 0)),
            pl.BlockSpec((s, 128), lambda bb, i: (bb, MISC_OFF // 128)),
            pl.BlockSpec((A_HEADS, 2 * QBLOCK, QBLOCK), lambda bb, i: (0, 0, 0)),
        ],
        out_specs=pl.BlockSpec((tq, A_WIDTH), lambda bb, i: (row(bb, i), 0)),
        out_shape=jax.ShapeDtypeStruct((n, A_WIDTH), jnp.float32),
        scratch_shapes=[
            pltpu.VMEM((A_QPS, s, QBLOCK), jnp.int32),
            pltpu.VMEM((s + QBLOCK, A_HEADS * QBLOCK), jnp.float32),
            pltpu.VMEM((A_QPS, 1, QBLOCK), jnp.int32),
        ],
        compiler_params=pltpu.CompilerParams(
            dimension_semantics=("arbitrary", "arbitrary"), vmem_limit_bytes=VMEM_LIMIT),
        name="a_main",
    )(far, qn, proj, proj, misct, kn, vt, proj, near_t)


def mixer_a(proj, misct, b, s, gkv, wukv, gq, gk, bias_a):
    kn, vt, qn = a_pre(proj, b, s, gkv, wukv.astype(jnp.bfloat16), gq, gk)
    near_t = jnp.transpose(_bias_lookup(bias_a, _NEAR_BUCKET_T), (2, 0, 1))
    far = bias_a[_FAR_BUCKET]
    return a_main(proj, misct, qn, kn, vt, near_t, far, b, s)


N_CMP_PAD = 128
N_SB = 32
WIN_KEYS = WINDOW + QBLOCK
CMP_GROUPS = CMP_BLOCK // CMP_STRIDE

_CMP_BUCKET_T = _bucket_np((np.arange(16)[:, None, None] * QBLOCK + np.arange(QBLOCK)[None, None, :])
                           - (np.arange(N_CMP_PAD)[None, :, None] * CMP_STRIDE + CMP_BLOCK - 1))
_WIN_DIST_T = np.arange(QBLOCK)[None, :] + WINDOW - np.arange(WIN_KEYS)[:, None]
_WIN_BUCKET_T = _bucket_np(_WIN_DIST_T)
_CMP_MASK_T = np.where((np.arange(N_CMP_PAD)[None, :, None] * CMP_STRIDE + CMP_BLOCK - 1)
                       <= (np.arange(16)[:, None, None] * QBLOCK + np.arange(QBLOCK)[None, None, :]),
                       0.0, NEG_INF).astype(np.float32)
_WIN_MASK_T = np.where((_WIN_DIST_T >= 0) & (_WIN_DIST_T < WINDOW), 0.0, NEG_INF).astype(np.float32)
_OVERLAP_T = np.array([[1.0 if (n * CMP_STRIDE < j * SEL_BLOCK + SEL_BLOCK and n * CMP_STRIDE + CMP_BLOCK > j * SEL_BLOCK
                              and n < N_CMP_PAD - 1) else 0.0 for n in range(N_CMP_PAD)] for j in range(N_SB)], np.float32)
_SB_EXPAND = (np.arange(2048)[:, None] // SEL_BLOCK == np.arange(N_SB)[None, :]).astype(np.float32)


def _c_pre_kernel(cmp_ref, sel_ref, win_ref, kn_ref, pos_ref, w1_ref, w2_ref,
                  kvc_ref, ksel_ref, vselt_ref, kwin_ref, vwin_ref):
    f32, bf16 = jnp.float32, jnp.bfloat16
    d = C_HEAD_DIM

    def rms(x, g):
        return x * lax.rsqrt(jnp.mean(x * x, axis=-1, keepdims=True) + RMS_EPS) * g

    acc = [jnp.zeros((N_CMP_PAD, 2 * PHI_HIDDEN), f32) for _ in range(CMP_GROUPS)]
    for j in range(CMP_STRIDE):
        xs = cmp_ref[pl.ds(j, N_CMP_PAD, stride=CMP_STRIDE), :]
        for half in range(CMP_GROUPS):
            jj = half * CMP_STRIDE + j
            acc[half] = acc[half] + jnp.dot((xs + pos_ref[jj:jj + 1, :]).astype(bf16), w1_ref[jj],
                                            preferred_element_type=f32)
    hid = acc[0] + pltpu.roll(acc[1], N_CMP_PAD - 1, axis=0)
    kv = jnp.dot(_silu(hid).astype(bf16), w2_ref[...], preferred_element_type=f32)
    kvc_ref[0, :, 0:d] = rms(kv[:, 0:d], kn_ref[0:1, :])
    kvc_ref[0, :, d:2 * d] = kv[:, d:2 * d]

    sel = sel_ref[...]
    ksel_ref[...] = rms(sel[:, 0:d], kn_ref[1:2, :])
    vselt_ref[0] = sel.T[d:2 * d, :]
    win = win_ref[...]
    kwin_ref[0, 0:WINDOW, :] = jnp.zeros((WINDOW, d), f32)
    vwin_ref[0, 0:WINDOW, :] = jnp.zeros((WINDOW, d), f32)
    kwin_ref[0, WINDOW:, :] = rms(win[:, 0:d], kn_ref[2:3, :])
    vwin_ref[0, WINDOW:, :] = win[:, d:2 * d]


def c_pre(proj, b, s, k_norm, cmp_pos, phi_w1, phi_w2):
    d = C_HEAD_DIM
    kv0 = OFF["c_kv"] // 128
    pos = jnp.concatenate([cmp_pos[0], cmp_pos[1]], axis=-1)
    w1 = phi_w1.reshape(2, CMP_BLOCK, d, PHI_HIDDEN)
    zero = jnp.zeros((CMP_BLOCK, d, PHI_HIDDEN), jnp.float32)
    w1c = jnp.concatenate([jnp.concatenate([w1[0], zero], axis=-1),
                           jnp.concatenate([zero, w1[1]], axis=-1)], axis=1).astype(jnp.bfloat16)
    z2 = jnp.zeros((PHI_HIDDEN, d), jnp.float32)
    w2c = jnp.concatenate([jnp.concatenate([phi_w2[0], z2], axis=-1),
                           jnp.concatenate([z2, phi_w2[1]], axis=-1)], axis=0).astype(jnp.bfloat16)
    full = lambda a: pl.BlockSpec(a.shape, lambda bb: (0,) * a.ndim)
    return pl.pallas_call(
        _c_pre_kernel,
        grid=(b,),
        in_specs=[
            pl.BlockSpec((s, 128), lambda bb: (bb, kv0)),
            pl.BlockSpec((s, 128), lambda bb: (bb, kv0 + 1)),
            pl.BlockSpec((s, 128), lambda bb: (bb, kv0 + 2)),
            full(k_norm), full(pos), full(w1c), full(w2c),
        ],
        out_specs=[
            pl.BlockSpec((1, N_CMP_PAD, 128), lambda bb: (bb, 0, 0)),
            pl.BlockSpec((s, d), lambda bb: (bb, 0)),
            pl.BlockSpec((1, d, s), lambda bb: (bb, 0, 0)),
            pl.BlockSpec((1, s + WINDOW, d), lambda bb: (bb, 0, 0)),
            pl.BlockSpec((1, s + WINDOW, d), lambda bb: (bb, 0, 0)),
        ],
        out_shape=[
            jax.ShapeDtypeStruct((b, N_CMP_PAD, 128), jnp.float32),
            jax.ShapeDtypeStruct((b * s, d), jnp.float32),
            jax.ShapeDtypeStruct((b, d, s), jnp.float32),
            jax.ShapeDtypeStruct((b, s + WINDOW, d), jnp.float32),
            jax.ShapeDtypeStruct((b, s + WINDOW, d), jnp.float32),
        ],
        compiler_params=pltpu.CompilerParams(dimension_semantics=("arbitrary",), vmem_limit_bytes=VMEM_LIMIT),
        name="c_pre",
    )(proj, proj, proj, k_norm, pos, w1c, w2c)


C_QPS = 4


def _c_pair_kernel(cq_ref, z_ref, misct_ref, gq_ref, kvc_ref, ksel_ref, vselt_ref, kwin_ref, vwin_ref,
                   cmptab_ref, near_ref, farrow_ref, wintab_ref, ovt_ref, exp_ref, o_ref, ls_scr):
    s = ksel_ref.shape[0]
    d = C_HEAD_DIM
    hq = C_HEADS * QBLOCK
    f32, bf16 = jnp.float32, jnp.bfloat16
    nt = (((1,), (1,)), ((), ()))
    tn = (((0,), (0,)), ((), ()))

    def softmax_rows(l, valid):
        l = jnp.where(valid, l, NEG_INF)
        m = _col_reduce(l, jnp.max)
        p = jnp.where(valid, jnp.exp(l - m), 0.0)
        den = _col_reduce(p, jnp.sum)
        return p, den

    def block(g, t0):
        nrows = t0 + QBLOCK
        qrows = slice(g * QBLOCK, (g + 1) * QBLOCK)

        cq = cq_ref[qrows, :]
        qs = jnp.concatenate([cq[:, h * d:(h + 1) * d] for h in range(C_HEADS)], axis=0)
        qs = qs * lax.rsqrt(jnp.mean(qs * qs, axis=-1, keepdims=True) + RMS_EPS) * gq_ref[...] * (d ** -0.5)
        qs = qs.astype(bf16)

        kvc = kvc_ref[0]
        lc = lax.dot_general(kvc[:, 0:d].astype(bf16), qs, nt, preferred_element_type=f32) + cmptab_ref[g]
        if t0 < CMP_BLOCK:
            n_idx = lax.broadcasted_iota(jnp.int32, (N_CMP_PAD, hq), 0)
            t_c = t0 + (lax.broadcasted_iota(jnp.int32, (N_CMP_PAD, hq), 1) & (QBLOCK - 1))
            pc, den_c = softmax_rows(lc, n_idx * CMP_STRIDE + (CMP_BLOCK - 1) <= t_c)
            pc = pc * jnp.where(den_c > 0.0, 1.0 / den_c, 0.0)
        else:
            pc = jnp.exp(lc - _col_reduce(lc, jnp.max))
            pc = pc * (1.0 / _col_reduce(pc, jnp.sum))
        o_cmp = lax.dot_general(kvc[:, d:2 * d].astype(bf16), pc.astype(bf16), tn, preferred_element_type=f32)

        psum = pc[:, 0:QBLOCK]
        for h in range(1, C_HEADS):
            psum = psum + pc[:, h * QBLOCK:(h + 1) * QBLOCK]
        p_hi = psum.astype(bf16)
        p_lo = (psum - p_hi.astype(f32)).astype(bf16)
        ovt = ovt_ref[...]
        imp = jnp.dot(ovt, p_hi, preferred_element_type=f32) + jnp.dot(ovt, p_lo, preferred_element_type=f32)
        j_idx = lax.broadcasted_iota(jnp.int32, (N_SB, QBLOCK), 0)
        t_b = t0 + lax.broadcasted_iota(jnp.int32, (N_SB, QBLOCK), 1)
        cur = lax.shift_right_arithmetic(t_b, SEL_BLOCK.bit_length() - 1)
        forced = (j_idx == 0) | (j_idx == cur) | (j_idx == jnp.maximum(cur - 1, 0))
        imp = jnp.where(j_idx * SEL_BLOCK <= t_b, jnp.where(forced, FORCE_SCORE, imp), NEG_INF)
        rank = jnp.zeros((N_SB, QBLOCK), f32)
        for r in range(N_SB):
            row = imp[r:r + 1, :]
            rank = rank + jnp.where((row > imp) | ((row == imp) & (j_idx > r)), 1.0, 0.0)
        picked = jnp.where(rank < float(N_SEL), 1.0, 0.0).astype(bf16)

        kw = kwin_ref[0, t0:t0 + WIN_KEYS, :]
        vw = vwin_ref[0, t0:t0 + WIN_KEYS, :]
        lw = lax.dot_general(kw.astype(bf16), qs, nt, preferred_element_type=f32) + wintab_ref[...]
        if t0 < WINDOW:
            r_idx = lax.broadcasted_iota(jnp.int32, (WIN_KEYS, hq), 0)
            lw = jnp.where(r_idx >= WINDOW - t0, lw, NEG_INF)
        pw = jnp.exp(lw - _col_reduce(lw, jnp.max))
        den_w = _col_reduce(pw, jnp.sum)
        o_win = lax.dot_general(vw.astype(bf16), pw.astype(bf16), tn, preferred_element_type=f32) / den_w

        gate = _sigmoid(misct_ref[0, C_G_LANE:C_G_LANE + 3 * C_HEADS, qrows])

        in_picked = jnp.dot(exp_ref[0:nrows, :], picked, preferred_element_type=f32) > 0.5
        sel_bias = jnp.where(in_picked, 0.0, NEG_INF)
        tri = jnp.where(lax.broadcasted_iota(jnp.int32, (QBLOCK, QBLOCK), 0)
                        <= lax.broadcasted_iota(jnp.int32, (QBLOCK, QBLOCK), 1), 0.0, NEG_INF)
        tail = sel_bias[t0:nrows] + tri
        sel_bias = jnp.concatenate([sel_bias[0:t0], tail], axis=0) if t0 else tail
        ls_scr[0:QBLOCK, :] = jnp.zeros((QBLOCK, hq), f32)
        ls_scr[QBLOCK:QBLOCK + nrows, :] = lax.dot_general(ksel_ref[0:nrows, :].astype(bf16), qs, nt,
                                                           preferred_element_type=f32)
        ls_scr[t0:t0 + 2 * QBLOCK, :] += near_ref[...] - farrow_ref[...]
        outs = []
        for h in range(C_HEADS):
            cols = slice(h * QBLOCK, (h + 1) * QBLOCK)
            l = ls_scr[QBLOCK:QBLOCK + nrows, cols] + sel_bias
            m = _col_reduce(l, jnp.max)
            p = jnp.exp(l - m)
            den = _col_reduce(p, jnp.sum)
            o_sel = jnp.dot(vselt_ref[0, :, 0:nrows].astype(bf16), p.astype(bf16), preferred_element_type=f32) / den
            outs.append(gate[3 * h:3 * h + 1, :] * o_cmp[:, cols] + gate[3 * h + 1:3 * h + 2, :] * o_sel
                        + gate[3 * h + 2:3 * h + 3, :] * o_win[:, cols])
        o_ref[qrows, :] = jnp.concatenate(outs, axis=0).T * _silu(z_ref[qrows, :])

    def run(step):
        for g in range(C_QPS):
            block(g, (C_QPS * step + g) * QBLOCK)

    step = pl.program_id(1)
    for v in range(s // (C_QPS * QBLOCK)):
        pl.when(step == v)(functools.partial(run, v))


def c_main(proj, misct, gq, kvc, ksel, vselt, kwin, vwin, cmptab, near, farrow, wintab, b, s):
    n = b * s
    tq = C_QPS * QBLOCK
    nq = s // tq
    d = C_HEAD_DIM
    hq = C_HEADS * QBLOCK
    row = lambda bb, i: bb * nq + i
    ovt = jnp.asarray(_OVERLAP_T, jnp.bfloat16)
    expand = jnp.asarray(_SB_EXPAND, jnp.bfloat16)
    const = lambda a: pl.BlockSpec(a.shape, lambda bb, i: (0,) * a.ndim)
    return pl.pallas_call(
        _c_pair_kernel,
        grid=(b, nq),
        in_specs=[
            pl.BlockSpec((tq, C_WIDTH), lambda bb, i: (row(bb, i), OFF["c_q"] // C_WIDTH)),
            pl.BlockSpec((tq, C_WIDTH), lambda bb, i: (row(bb, i), OFF["c_z"] // C_WIDTH)),
            pl.BlockSpec((1, 128, tq), lambda bb, i: (bb, 0, i)),
            const(gq),
            pl.BlockSpec((1, N_CMP_PAD, 128), lambda bb, i: (bb, 0, 0)),
            pl.BlockSpec((s, d), lambda bb, i: (bb, 0)),
            pl.BlockSpec((1, d, s), lambda bb, i: (bb, 0, 0)),
            pl.BlockSpec((1, s + WINDOW, d), lambda bb, i: (bb, 0, 0)),
            pl.BlockSpec((1, s + WINDOW, d), lambda bb, i: (bb, 0, 0)),
            pl.BlockSpec((C_QPS, N_CMP_PAD, hq), lambda bb, i: (i, 0, 0)),
            const(near), const(farrow), const(wintab), const(ovt), const(expand),
        ],
        out_specs=pl.BlockSpec((tq, C_WIDTH), lambda bb, i: (row(bb, i), 0)),
        out_shape=jax.ShapeDtypeStruct((n, C_WIDTH), jnp.float32),
        scratch_shapes=[pltpu.VMEM((s + QBLOCK, hq), jnp.float32)],
        compiler_params=pltpu.CompilerParams(
            dimension_semantics=("arbitrary", "arbitrary"), vmem_limit_bytes=VMEM_LIMIT),
        name="c_main",
    )(proj, proj, misct, gq, kvc, ksel, vselt, kwin, vwin, cmptab, near, farrow, wintab, ovt, expand)


def _head_cols(tab):
    return jnp.moveaxis(tab, -1, -2).reshape(*tab.shape[:-2], tab.shape[-1] * tab.shape[-2])


def mixer_c(proj, misct, b, s, gq, k_norm, cmp_pos, phi_w1, phi_w2, bias_c):
    kvc, ksel, vselt, kwin, vwin = c_pre(proj, b, s, k_norm, cmp_pos, phi_w1, phi_w2)
    cmptab = _head_cols(_bias_lookup(bias_c, _CMP_BUCKET_T) + jnp.asarray(_CMP_MASK_T)[..., None])
    near = _head_cols(_bias_lookup(bias_c, _NEAR_BUCKET_T))
    wintab = _head_cols(_bias_lookup(bias_c, _WIN_BUCKET_T) + jnp.asarray(_WIN_MASK_T)[..., None])
    farrow = jnp.repeat(bias_c[_FAR_BUCKET], QBLOCK).reshape(1, C_HEADS * QBLOCK)
    return c_main(proj, misct, gq.reshape(1, C_HEAD_DIM), kvc, ksel, vselt, kwin, vwin, cmptab, near, farrow,
                  wintab, b, s)


B_HPS = 2
B_GROUP = 4


def _b_kernel(alog_ref, dtb_ref, q_ref, k_ref, v_ref, z_ref, misc_ref, arow_ref, cwq_ref, cwk_ref, cwv_ref,
              gon_ref, o_ref, qs, ks, vs, ws, at, gtok, btok, gcrow):
    s = q_ref.shape[0]
    dh = B_HEAD_DIM
    c_len = GDN_CHUNK
    j = pl.program_id(1)
    f32, bf16 = jnp.float32, jnp.bfloat16
    hi = lax.Precision.HIGHEST
    nt = (((1,), (1,)), ((), ()))
    tn = (((0,), (0,)), ((), ()))

    head_rows = lax.broadcasted_iota(jnp.int32, (SUBLANES, B_HPS * dh), 0)

    def conv_silu(x_ref, w_ref):
        x8 = x_ref[0:SUBLANES, :]
        w_last = w_ref[CONV_WIDTH - 1:CONV_WIDTH, :]
        head = x8 * w_last
        body = x_ref[SUBLANES:s, :] * w_last
        for k in range(1, CONV_WIDTH):
            wk = w_ref[CONV_WIDTH - 1 - k:CONV_WIDTH - k, :]
            head = head + jnp.where(head_rows >= k, pltpu.roll(x8, k, axis=0), 0.0) * wk
            body = body + x_ref[SUBLANES - k:s - k, :] * wk
        return _silu(jnp.concatenate([head, body], axis=0))

    def l2n(x):
        return x * lax.rsqrt(jnp.sum(x * x, axis=-1, keepdims=True) + RMS_EPS)

    qc = conv_silu(q_ref, cwq_ref)
    kc = conv_silu(k_ref, cwk_ref)
    vs[...] = conv_silu(v_ref, cwv_ref)
    misc = misc_ref[...]
    lane = lax.broadcasted_iota(jnp.int32, misc.shape, 1)
    ri = lax.broadcasted_iota(jnp.int32, (c_len, c_len), 0)
    ci = lax.broadcasted_iota(jnp.int32, (c_len, c_len), 1)
    lower = ci <= ri
    strict = ci < ri
    tri_u = jnp.where(ri <= ci, 1.0, 0.0)
    row_in_chunk = lax.broadcasted_iota(jnp.int32, (s, dh), 0) & (c_len - 1)
    heads = [B_HPS * j + hh for hh in range(B_HPS)]
    neg_a = [-jnp.exp(alog_ref[h]) for h in heads]
    lane_row = lax.broadcasted_iota(jnp.int32, (1, misc.shape[1]), 1)
    dtb_row = jnp.zeros((1, misc.shape[1]), f32)
    nega_row = jnp.zeros((1, misc.shape[1]), f32)
    for hh, h in enumerate(heads):
        dtb_row = jnp.where(lane_row == B_A_LANE + h, dtb_ref[h], dtb_row)
        nega_row = jnp.where(lane_row == B_A_LANE + h, neg_a[hh], nega_row)
    g_all = nega_row * jax.nn.softplus(misc + dtb_row)
    for sh in (1, 2, 4, 8, 16, 32):
        g_all = g_all + jnp.where(row_in_chunk >= sh, pltpu.roll(g_all, sh, axis=0), 0.0)
    beta_all = _sigmoid(misc)
    for hh, h in enumerate(heads):
        sl = slice(hh * dh, (hh + 1) * dh)
        qs[:, sl] = l2n(qc[:, sl]) * (dh ** -0.5)
        ks[:, sl] = l2n(kc[:, sl])
        g_tok = jnp.sum(jnp.where(lane == B_A_LANE + h, g_all, 0.0), axis=-1, keepdims=True)
        b_tok = jnp.sum(jnp.where(lane == B_B_LANE + h, beta_all, 0.0), axis=-1, keepdims=True)
        gtok[hh] = jnp.broadcast_to(g_tok, (s, dh))
        btok[hh] = jnp.broadcast_to(b_tok, (s, dh))
        g_row = neg_a[hh] * jax.nn.softplus(arow_ref[0, hh] + dtb_ref[h])
        gcrow[hh] = jnp.dot(g_row, tri_u, precision=hi, preferred_element_type=f32)

    lane3 = lax.broadcasted_iota(jnp.int32, (c_len, 3 * c_len), 1)
    lo_lanes = (lane3 >= c_len) & (lane3 < 2 * c_len)

    def split_lhs(p):
        p4 = jnp.concatenate([p, p, p], axis=1)
        hi4 = p4.astype(bf16).astype(f32)
        return jnp.where(lo_lanes, p4 - hi4, hi4).astype(bf16)

    def split_rhs(x):
        xh = x.astype(bf16)
        xl = (x - xh.astype(f32)).astype(bf16)
        return jnp.concatenate([xh, xh, xl], axis=0)

    def mm(p_split, x):
        return jnp.dot(p_split, split_rhs(x), preferred_element_type=f32)

    hs = range(B_HPS)
    sls = [slice(hh * dh, (hh + 1) * dh) for hh in hs]
    n_groups = s // (B_GROUP * c_len)

    def chunk_rows(c):
        return pl.ds(pl.multiple_of(c * c_len, c_len), c_len)

    def intra_group(i, between):
        where, q_c, k_c, v_c, gb, beta, g_row = [], [], [], [], [], [], []
        for c in [B_GROUP * i + cc for cc in range(B_GROUP)]:
            rs = chunk_rows(c)
            for hh in hs:
                where.append((rs, hh))
                q_c.append(qs[rs, sls[hh]])
                k_c.append(ks[rs, sls[hh]])
                v_c.append(vs[rs, sls[hh]])
                gb.append(gtok[hh, rs, :])
                beta.append(btok[hh, rs, :])
                g_row.append(gcrow[hh, pl.ds(c, 1), :])
        nch = range(len(where))
        decay = [jnp.where(lower, jnp.exp(jnp.where(lower, gb[n][:, 0:c_len] - g_row[n], 0.0)), 0.0) for n in nch]
        kb = [k_c[n] * beta[n] for n in nch]
        kk = [lax.dot_general(kb[n].astype(bf16), k_c[n].astype(bf16), nt, preferred_element_type=f32) for n in nch]
        between()
        p = [-jnp.where(strict, kk[n] * decay[n], 0.0) for n in nch]
        eg = [jnp.exp(gb[n]) for n in nch]
        x = [jnp.concatenate([v_c[n] * beta[n], kb[n] * eg[n]], axis=1) for n in nch]
        ps = [split_lhs(p[n]) for n in nch]
        x = [x[n] + mm(ps[n], x[n]) for n in nch]
        between()
        for _ in range(5):
            p = [mm(ps[n], p[n]) for n in nch]
            between()
            ps = [split_lhs(p[n]) for n in nch]
            x = [x[n] + mm(ps[n], x[n]) for n in nch]
            between()
        attn = [jnp.where(lower, lax.dot_general(q_c[n].astype(bf16), k_c[n].astype(bf16), nt,
                                                 preferred_element_type=f32) * decay[n], 0.0) for n in nch]
        between()

        def commit():
            for n, (rs, hh) in enumerate(where):
                g_last = g_row[n][:, c_len - 1:c_len]
                vs[rs, sls[hh]] = x[n][:, 0:dh]
                ws[rs, sls[hh]] = x[n][:, dh:2 * dh]
                qs[rs, sls[hh]] = q_c[n] * eg[n]
                ks[rs, sls[hh]] = k_c[n] * jnp.exp(g_last - gb[n])
                at[rs, hh * c_len:(hh + 1) * c_len] = attn[n]

        return commit

    def recur_steps(group, states, result):
        for c in [B_GROUP * group + cc for cc in range(B_GROUP)]:
            rs = chunk_rows(c)
            st_b = [states[hh].astype(bf16) for hh in hs]
            v_new = [vs[rs, sls[hh]] - jnp.dot(ws[rs, sls[hh]].astype(bf16), st_b[hh], preferred_element_type=f32)
                     for hh in hs]
            o_st = [jnp.dot(qs[rs, sls[hh]].astype(bf16), st_b[hh], preferred_element_type=f32) for hh in hs]
            yield
            v_nb = [v_new[hh].astype(bf16) for hh in hs]
            states = tuple(states[hh] * jnp.exp(gcrow[hh, pl.ds(c, 1), c_len - 1:c_len])
                           + lax.dot_general(ks[rs, sls[hh]].astype(bf16), v_nb[hh], tn, preferred_element_type=f32)
                           for hh in hs)
            for hh in hs:
                o_ref[rs, sls[hh]] = o_st[hh] + jnp.dot(at[rs, hh * c_len:(hh + 1) * c_len].astype(bf16), v_nb[hh],
                                                        preferred_element_type=f32)
            yield
        result.append(states)

    def overlapped(g, states):
        result = []
        steps = recur_steps(g - 1, states, result)
        commit = intra_group(g, lambda: next(steps, None))
        for _ in steps:
            pass
        commit()
        return result[0]

    intra_group(0, lambda: None)()
    states = lax.fori_loop(1, n_groups, overlapped, tuple(jnp.zeros((dh, dh), f32) for _ in hs))
    for _ in recur_steps(n_groups - 1, states, []):
        pass

    for hh in range(B_HPS):
        sl = slice(hh * dh, (hh + 1) * dh)
        o = o_ref[:, sl]
        o = o * lax.rsqrt(jnp.mean(o * o, axis=-1, keepdims=True) + RMS_EPS) * gon_ref[...]
        o_ref[:, sl] = o * _silu(z_ref[:, sl])


def mixer_b(proj, misct, b, s, conv_w, a_log, dt_bias, out_norm):
    n = b * s
    w2 = B_HPS * B_HEAD_DIM
    nc = s // GDN_CHUNK
    arow = misct[:, B_A_LANE:B_A_LANE + B_HEADS, :].reshape(b, B_HEADS, nc, GDN_CHUNK)
    qkv0 = OFF["b_qkv"] // w2
    kstep = B_WIDTH // w2
    smem = pl.BlockSpec(memory_space=pltpu.SMEM)
    return pl.pallas_call(
        _b_kernel,
        grid=(b, B_HEADS // B_HPS),
        in_specs=[
            smem, smem,
            pl.BlockSpec((s, w2), lambda bb, j: (bb, qkv0 + j)),
            pl.BlockSpec((s, w2), lambda bb, j: (bb, qkv0 + kstep + j)),
            pl.BlockSpec((s, w2), lambda bb, j: (bb, qkv0 + 2 * kstep + j)),
            pl.BlockSpec((s, w2), lambda bb, j: (bb, OFF["b_z"] // w2 + j)),
            pl.BlockSpec((s, 128), lambda bb, j: (bb, MISC_OFF // 128)),
            pl.BlockSpec((1, B_HPS, nc, GDN_CHUNK), lambda bb, j: (bb, j, 0, 0)),
            pl.BlockSpec((CONV_WIDTH, w2), lambda bb, j: (0, j)),
            pl.BlockSpec((CONV_WIDTH, w2), lambda bb, j: (0, kstep + j)),
            pl.BlockSpec((CONV_WIDTH, w2), lambda bb, j: (0, 2 * kstep + j)),
            pl.BlockSpec((1, B_HEAD_DIM), lambda bb, j: (0, 0)),
        ],
        out_specs=pl.BlockSpec((s, w2), lambda bb, j: (bb, j)),
        out_shape=jax.ShapeDtypeStruct((n, B_WIDTH), jnp.float32),
        scratch_shapes=[
            pltpu.VMEM((s, w2), jnp.float32), pltpu.VMEM((s, w2), jnp.float32), pltpu.VMEM((s, w2), jnp.float32),
            pltpu.VMEM((s, w2), jnp.float32), pltpu.VMEM((s, B_HPS * GDN_CHUNK), jnp.float32),
            pltpu.VMEM((B_HPS, s, B_HEAD_DIM), jnp.float32), pltpu.VMEM((B_HPS, s, B_HEAD_DIM), jnp.float32),
            pltpu.VMEM((B_HPS, nc, GDN_CHUNK), jnp.float32),
        ],
        compiler_params=pltpu.CompilerParams(
            dimension_semantics=("arbitrary", "arbitrary"), vmem_limit_bytes=VMEM_LIMIT),
        name="gdn",
    )(a_log, dt_bias, proj, proj, proj, proj, proj, arow, conv_w, conv_w, conv_w, out_norm.reshape(1, B_HEAD_DIM))


def kernel(x, norm_g, w_in, a_kv_norm, a_w_ukv, a_q_norm, a_k_norm, b_conv, b_a_log, b_dt_bias, b_out_norm, c_q_norm, c_k_norm, c_cmp_pos, c_phi_w1, c_phi_w2, w_branch, w_out, rel_bias):
    b, s, _ = x.shape
    n = b * s
    bias_a = rel_bias[:, :A_HEADS]
    bias_c = rel_bias[:, A_HEADS:]
    xf = x.reshape(n, D_MODEL)
    fields = [w_in[:, :, _ORIG[name][0]:_ORIG[name][0] + _ORIG[name][1]] for name in _NEW_ORDER]
    pad = jnp.zeros((DEPTH, D_MODEL, N_PAD - N_IN), w_in.dtype)
    w_in_p = jnp.concatenate(fields + [pad], axis=-1).astype(jnp.bfloat16)
    for l in range(DEPTH):
        proj = in_proj(xf, norm_g[l], w_in_p[l])
        p3 = proj.reshape(b, s, N_PAD)

        misct = jnp.transpose(p3[..., MISC_OFF:], (0, 2, 1))
        y_a = mixer_a(proj, misct, b, s, a_kv_norm[l], a_w_ukv[l], a_q_norm[l], a_k_norm[l], bias_a)

        y_b = mixer_b(proj, misct, b, s, b_conv[l], b_a_log[l], b_dt_bias[l], b_out_norm[l])
        y_c = mixer_c(proj, misct, b, s, c_q_norm[l], c_k_norm[l], c_cmp_pos[l], c_phi_w1[l], c_phi_w2[l], bias_c)

        wbr = w_branch[l].astype(jnp.bfloat16)
        xf = merge(xf, proj, y_a, y_b, y_c, wbr[:A_WIDTH], wbr[A_WIDTH:A_WIDTH + B_WIDTH],
                   wbr[A_WIDTH + B_WIDTH:], w_out[l].astype(jnp.bfloat16))
    return xf.reshape(b, s, D_MODEL)
```

```python
import functools
import math

import jax
import jax.numpy as jnp
import numpy as np
from jax import lax
from jax.experimental import pallas as pl
from jax.experimental.pallas import tpu as pltpu

D_MODEL = 1024
SEQ = 2048
DEPTH = 4
QBLOCK = 128
NEG_INF = -1e30
FORCE_SCORE = 1e9
RMS_EPS = 1e-6

A_HEADS = 4
A_HEAD_DIM = 64
A_WIDTH = A_HEADS * A_HEAD_DIM
A_KV_LATENT = 128
IDX_HEADS = 8
IDX_DIM = 64
TOPK_MAX = 256

B_HEADS = 4
B_HEAD_DIM = 128
B_WIDTH = B_HEADS * B_HEAD_DIM
CONV_WIDTH = 4
GDN_CHUNK = 64

C_HEADS = 4
C_HEAD_DIM = 64
C_WIDTH = C_HEADS * C_HEAD_DIM
CMP_BLOCK = 32
CMP_STRIDE = 16
SEL_BLOCK = 64
N_SEL = 16
WINDOW = 512
PHI_HIDDEN = 256

N_BUCKETS = 32
BUCKET_MAX_EXACT = 16
BUCKET_MAX_DIST = 128

VMEM_LIMIT = 48 * 1024 * 1024
SUBLANES = 8
KEY_BITS = 32

_ORIG = {}
_o = 0
for _name, _w in (("a_q", 256), ("a_ckv", 128), ("a_qi", 512), ("a_ki", 64), ("a_wi", 8), ("a_z", 256),
                  ("b_qkv", 1536), ("b_a", 4), ("b_b", 4), ("b_z", 512),
                  ("c_q", 256), ("c_kv", 384), ("c_g", 12), ("c_z", 256), ("g", 3072)):
    _ORIG[_name] = (_o, _w)
    _o += _w
N_IN = _o

_NEW_ORDER = ("g", "b_qkv", "b_z", "a_qi", "a_q", "a_z", "c_q", "c_z", "c_kv", "a_ckv",
              "a_ki", "a_wi", "b_a", "b_b", "c_g")
OFF = {}
_o = 0
for _name in _NEW_ORDER:
    OFF[_name] = _o
    _o += _ORIG[_name][1]
MISC_OFF = OFF["a_ki"]
N_PAD = 7296
assert MISC_OFF == 7168 and _o <= N_PAD
A_WI_LANE = OFF["a_wi"] - MISC_OFF
B_A_LANE = OFF["b_a"] - MISC_OFF
B_B_LANE = OFF["b_b"] - MISC_OFF
C_G_LANE = OFF["c_g"] - MISC_OFF


IN_TM = 1024
IN_TN = N_PAD // 3


def _in_proj_kernel(x_ref, g_ref, w_ref, o_ref):
    x = x_ref[...]
    ms = jnp.mean(x * x, axis=-1, keepdims=True)
    h = (x * lax.rsqrt(ms + RMS_EPS)) * g_ref[...]
    o_ref[...] = jnp.dot(h.astype(jnp.bfloat16), w_ref[...], preferred_element_type=jnp.float32)


def in_proj(xf, g, w_bf16):
    n = xf.shape[0]
    return pl.pallas_call(
        _in_proj_kernel,
        grid=(N_PAD // IN_TN, n // IN_TM),
        in_specs=[
            pl.BlockSpec((IN_TM, D_MODEL), lambda j, i: (i, 0)),
            pl.BlockSpec((1, D_MODEL), lambda j, i: (0, 0)),
            pl.BlockSpec((D_MODEL, IN_TN), lambda j, i: (0, j)),
        ],
        out_specs=pl.BlockSpec((IN_TM, IN_TN), lambda j, i: (i, j)),
        out_shape=jax.ShapeDtypeStruct((n, N_PAD), jnp.float32),
        compiler_params=pltpu.CompilerParams(
            dimension_semantics=("arbitrary", "arbitrary"), vmem_limit_bytes=VMEM_LIMIT),
        name="in_proj",
    )(xf, g.reshape(1, D_MODEL), w_bf16)


MG_TM = 512


def _merge_kernel(x_ref, g_ref, ya_ref, yb_ref, yc_ref, wa_ref, wb_ref, wc_ref, wo_ref, o_ref):
    def branch(y_ref, w_ref, k):
        p = jnp.dot(y_ref[...].astype(jnp.bfloat16), w_ref[...], preferred_element_type=jnp.float32)
        return _sigmoid(g_ref[:, k * D_MODEL:(k + 1) * D_MODEL]) * p

    merged = branch(ya_ref, wa_ref, 0) + branch(yb_ref, wb_ref, 1) + branch(yc_ref, wc_ref, 2)
    o_ref[...] = x_ref[...] + jnp.dot(merged.astype(jnp.bfloat16), wo_ref[...],
                                      preferred_element_type=jnp.float32)


def merge(xf, proj, ya, yb, yc, wa, wb, wc, wo):
    n = xf.shape[0]
    row = lambda w: pl.BlockSpec((MG_TM, w), lambda i: (i, 0))
    full = lambda a: pl.BlockSpec(a.shape, lambda i: (0, 0))
    return pl.pallas_call(
        _merge_kernel,
        grid=(n // MG_TM,),
        in_specs=[row(D_MODEL), row(3 * D_MODEL), row(A_WIDTH), row(B_WIDTH), row(C_WIDTH),
                  full(wa), full(wb), full(wc), full(wo)],
        out_specs=row(D_MODEL),
        out_shape=jax.ShapeDtypeStruct((n, D_MODEL), jnp.float32),
        compiler_params=pltpu.CompilerParams(
            dimension_semantics=("arbitrary",), vmem_limit_bytes=VMEM_LIMIT),
        name="merge",
    )(xf, proj, ya, yb, yc, wa, wb, wc, wo)


def _bucket_np(dist):
    d = np.maximum(np.asarray(dist, np.int64), 0)
    ratio = np.log(np.maximum(d, 1).astype(np.float64) / BUCKET_MAX_EXACT) / math.log(BUCKET_MAX_DIST / BUCKET_MAX_EXACT)
    scaled = ratio * (N_BUCKETS - BUCKET_MAX_EXACT)
    frac = scaled - np.floor(scaled)
    edge = (d > BUCKET_MAX_EXACT) & (d < BUCKET_MAX_DIST) & ((frac < 1e-4) | (frac > 1 - 1e-4))
    assert not edge.any(), "bucket boundary too close to an integer distance"
    large = np.minimum(BUCKET_MAX_EXACT + np.floor(scaled + 1e-9).astype(np.int64), N_BUCKETS - 1)
    return np.where(d < BUCKET_MAX_EXACT, d, large).astype(np.int32)


_NEAR_BUCKET_T = _bucket_np(np.arange(QBLOCK)[None, :] + QBLOCK - np.arange(2 * QBLOCK)[:, None])
_FAR_BUCKET = int(_bucket_np(np.array([BUCKET_MAX_DIST]))[0])
assert (_bucket_np(np.arange(BUCKET_MAX_DIST, 4096)) == _FAR_BUCKET).all()

INT_MIN = -2 ** 31


COL_SLAB = 64


def _col_reduce(x, op):
    rows = x.shape[0]
    if rows > COL_SLAB and rows % COL_SLAB == 0:
        x = op(x.reshape(rows // COL_SLAB, COL_SLAB, x.shape[1]), axis=0)
    return op(x, axis=0, keepdims=True)


def _sigmoid(x):
    return 0.5 * jnp.tanh(0.5 * x) + 0.5


def _silu(x):
    return x * _sigmoid(x)


def _bias_lookup(table, buckets):
    idx = jnp.asarray(buckets)[..., None]
    out = jnp.zeros(idx.shape[:-1] + (table.shape[1],), table.dtype)
    for bkt in range(N_BUCKETS):
        out = jnp.where(idx == bkt, table[bkt], out)
    return out


A_PRE_TM = 512


def _a_pre_kernel(ckv_ref, q_ref, gkv_ref, wukv_ref, gq_ref, gk_ref, hm_ref, kn_ref, vt_ref, qn_ref):
    c = ckv_ref[...]
    c = c * lax.rsqrt(jnp.mean(c * c, axis=-1, keepdims=True) + RMS_EPS) * gkv_ref[...]
    kv = jnp.dot(c.astype(jnp.bfloat16), wukv_ref[...], preferred_element_type=jnp.float32)
    hm = hm_ref[...]

    def head_rms(x, g):
        ms = jnp.dot(x * x, hm, precision=lax.Precision.HIGHEST, preferred_element_type=jnp.float32)
        return x * lax.rsqrt(ms + RMS_EPS) * g

    kn_ref[...] = head_rms(kv[:, :A_WIDTH], gk_ref[...])
    qn_ref[...] = head_rms(q_ref[...], gq_ref[...]) * (A_HEAD_DIM ** -0.5)
    vt_ref[0] = kv[:, A_WIDTH:].T


def a_pre(proj, b, s, gkv, wukv_bf16, gq, gk):
    n = b * s
    nt = s // A_PRE_TM
    hm = jnp.asarray(np.kron(np.eye(A_HEADS), np.ones((A_HEAD_DIM, A_HEAD_DIM))) / A_HEAD_DIM, jnp.float32)
    row = lambda bb, j: (bb * nt + j, 0)
    full = lambda a: pl.BlockSpec(a.shape, lambda bb, j: (0,) * a.ndim)
    gq = jnp.tile(gq, A_HEADS).reshape(1, A_WIDTH)
    gk = jnp.tile(gk, A_HEADS).reshape(1, A_WIDTH)
    gkv = gkv.reshape(1, A_KV_LATENT)
    return pl.pallas_call(
        _a_pre_kernel,
        grid=(b, nt),
        in_specs=[
            pl.BlockSpec((A_PRE_TM, A_KV_LATENT), lambda bb, j: (bb * nt + j, OFF["a_ckv"] // A_KV_LATENT)),
            pl.BlockSpec((A_PRE_TM, A_WIDTH), lambda bb, j: (bb * nt + j, OFF["a_q"] // A_WIDTH)),
            full(gkv), full(wukv_bf16), full(gq), full(gk), full(hm),
        ],
        out_specs=[
            pl.BlockSpec((A_PRE_TM, A_WIDTH), row),
            pl.BlockSpec((1, A_WIDTH, A_PRE_TM), lambda bb, j: (bb, 0, j)),
            pl.BlockSpec((A_PRE_TM, A_WIDTH), row),
        ],
        out_shape=[
            jax.ShapeDtypeStruct((n, A_WIDTH), jnp.float32),
            jax.ShapeDtypeStruct((b, A_WIDTH, s), jnp.float32),
            jax.ShapeDtypeStruct((n, A_WIDTH), jnp.float32),
        ],
        compiler_params=pltpu.CompilerParams(
            dimension_semantics=("arbitrary", "arbitrary"), vmem_limit_bytes=VMEM_LIMIT),
        name="a_pre",
    )(proj, proj, gkv, wukv_bf16, gq, gk, hm)


A_QPS = 8


def _a_main_kernel(far_ref, qn_ref, qi_ref, z_ref, misct_ref, kn_ref, vt_ref, ki_ref, near_ref, o_ref,
                   key_scr, lg_scr, j_scr):
    s = kn_ref.shape[0]
    f32, bf16 = jnp.float32, jnp.bfloat16
    nt = (((1,), (1,)), ((), ()))
    kf = float(TOPK_MAX)
    groups = range(A_QPS)

    def run(step):
        t0 = [(A_QPS * step + g) * QBLOCK for g in groups]
        nr = [t + QBLOCK for t in t0]
        qrows = [slice(g * QBLOCK, (g + 1) * QBLOCK) for g in groups]

        for g in groups:
            qi = qi_ref[qrows[g], :] * (IDX_DIM ** -0.5)
            qstack = jnp.concatenate([qi[:, h * IDX_DIM:(h + 1) * IDX_DIM] for h in range(IDX_HEADS)], axis=0)
            ki = ki_ref[0:nr[g], 0:IDX_DIM]
            sc = lax.dot_general(ki.astype(bf16), qstack.astype(bf16), nt, preferred_element_type=f32)
            wt = misct_ref[0, A_WI_LANE:A_WI_LANE + IDX_HEADS, qrows[g]] * (IDX_HEADS ** -0.5)
            score = jnp.zeros((nr[g], QBLOCK), f32)
            for h in range(IDX_HEADS):
                score = score + wt[h:h + 1, :] * jnp.maximum(sc[:, h * QBLOCK:(h + 1) * QBLOCK], 0.0)
            score = score + 0.0
            bits = pltpu.bitcast(score, jnp.int32)
            key = jnp.where(bits < 0, bits ^ jnp.int32(0x7FFFFFFF), bits)
            upper = (lax.broadcasted_iota(jnp.int32, (QBLOCK, QBLOCK), 0)
                     <= lax.broadcasted_iota(jnp.int32, (QBLOCK, QBLOCK), 1))
            if t0[g]:
                key_scr[g, 0:t0[g], :] = key[0:t0[g]]
            key_scr[g, t0[g]:nr[g], :] = jnp.where(upper, key[t0[g]:nr[g]], jnp.int32(INT_MIN))

        def count_ge(g, cand):
            return _col_reduce(jnp.where(key_scr[g, 0:nr[g], :] >= cand, 1.0, 0.0), jnp.sum)

        def bisect(it, thrs):
            bit = lax.shift_left(jnp.int32(1), KEY_BITS - 1 - it)
            cnt = [count_ge(g, thrs[g] + bit) for g in groups]
            return tuple(jnp.where(cnt[g] >= kf, thrs[g] + bit, thrs[g]) for g in groups)

        thrs = lax.fori_loop(0, KEY_BITS, bisect, tuple(jnp.full((1, QBLOCK), INT_MIN, jnp.int32) for _ in groups))

        for g in groups:
            nrows, thr = nr[g], thrs[g]
            n_ge = count_ge(g, thr)
            need = kf - count_ge(g, thr + 1)
            kpos = lax.broadcasted_iota(jnp.int32, (nrows, QBLOCK), 0)

            j_scr[g] = jnp.full((1, QBLOCK), nrows - 1, jnp.int32)
            surplus = jnp.where((n_ge > kf) & (thr > INT_MIN), 1.0, 0.0)

            @pl.when(jnp.max(surplus) > 0.0)
            def _():
                def bisect_idx(it, lohi):
                    lo, hi = lohi
                    mid = lax.shift_right_arithmetic(lo + hi, 1)
                    k = key_scr[g, 0:nrows, :]
                    c = _col_reduce(jnp.where((k == thr) & (kpos <= mid), 1.0, 0.0), jnp.sum)
                    ok = c >= need
                    return jnp.where(ok, lo, mid), jnp.where(ok, mid, hi)

                lo0 = jnp.full((1, QBLOCK), -1, jnp.int32)
                hi0 = jnp.full((1, QBLOCK), nrows - 1, jnp.int32)
                _, hi = lax.fori_loop(0, (nrows - 1).bit_length(), bisect_idx, (lo0, hi0))
                j_scr[g] = hi

            key = key_scr[g, 0:nrows, :]
            thr_c = jnp.maximum(thr, INT_MIN + 1)
            tie_ok = jnp.where(kpos <= j_scr[g], 0.0, NEG_INF)
            sel_bias = jnp.where(key > thr_c, 0.0, jnp.where(key == thr_c, tie_ok, NEG_INF))

            qn = qn_ref[qrows[g], :]
            lane = lax.broadcasted_iota(jnp.int32, (QBLOCK, A_WIDTH), 1)
            qblk = jnp.concatenate(
                [jnp.where((lane >= h * A_HEAD_DIM) & (lane < (h + 1) * A_HEAD_DIM), qn, 0.0)
                 for h in range(A_HEADS)], axis=0)
            lg_scr[0:QBLOCK, :] = jnp.zeros((QBLOCK, A_HEADS * QBLOCK), f32)
            lg_scr[QBLOCK:QBLOCK + nrows, :] = lax.dot_general(kn_ref[0:nrows, :].astype(bf16), qblk.astype(bf16),
                                                               nt, preferred_element_type=f32)
            outs = []
            for h in range(A_HEADS):
                cols = slice(h * QBLOCK, (h + 1) * QBLOCK)
                lg_scr[t0[g]:t0[g] + 2 * QBLOCK, cols] += near_ref[h] - far_ref[h]
                l = lg_scr[QBLOCK:QBLOCK + nrows, cols] + sel_bias
                m = _col_reduce(l, jnp.max)
                p = jnp.exp(l - m)
                den = _col_reduce(p, jnp.sum)
                vt = vt_ref[0, h * A_HEAD_DIM:(h + 1) * A_HEAD_DIM, 0:nrows]
                o_t = jnp.dot(vt.astype(bf16), p.astype(bf16), preferred_element_type=f32)
                outs.append(o_t / den)
            o = jnp.concatenate(outs, axis=0).T
            o_ref[qrows[g], :] = o * _silu(z_ref[qrows[g], :])

    step = pl.program_id(1)
    for v in range(s // (A_QPS * QBLOCK)):
        pl.when(step == v)(functools.partial(run, v))


def a_main(proj, misct, qn, kn, vt, near_t, far, b, s):
    n = b * s
    tq = A_QPS * QBLOCK
    nq = s // tq
    row = lambda bb, i: bb * nq + i
    return pl.pallas_call(
        _a_main_kernel,
        grid=(b, nq),
        in_specs=[
            pl.BlockSpec(memory_space=pltpu.SMEM),
            pl.BlockSpec((tq, A_WIDTH), lambda bb, i: (row(bb, i), 0)),
            pl.BlockSpec((tq, IDX_HEADS * IDX_DIM), lambda bb, i: (row(bb, i), OFF["a_qi"] // (IDX_HEADS * IDX_DIM))),
            pl.BlockSpec((tq, A_WIDTH), lambda bb, i: (row(bb, i), OFF["a_z"] // A_WIDTH)),
            pl.BlockSpec((1, 128, tq), lambda bb, i: (bb, 0, i)),
            pl.BlockSpec((s, A_WIDTH), lambda bb, i: (bb, 0)),
            pl.BlockSpec((1, A_WIDTH, s), lambda bb, i: (bb, 0, 0)),
            pl.BlockSpec((s, 128), lambda bb, i: (bb, MISC_OFF // 128)),
            pl.BlockSpec((A_HEADS, 2 * QBLOCK, QBLOCK), lambda bb, i: (0, 0, 0)),
        ],
        out_specs=pl.BlockSpec((tq, A_WIDTH), lambda bb, i: (row(bb, i), 0)),
        out_shape=jax.ShapeDtypeStruct((n, A_WIDTH), jnp.float32),
        scratch_shapes=[
            pltpu.VMEM((A_QPS, s, QBLOCK), jnp.int32),
            pltpu.VMEM((s + QBLOCK, A_HEADS * QBLOCK), jnp.float32),
            pltpu.VMEM((A_QPS, 1, QBLOCK), jnp.int32),
        ],
        compiler_params=pltpu.CompilerParams(
            dimension_semantics=("arbitrary", "arbitrary"), vmem_limit_bytes=VMEM_LIMIT),
        name="a_main",
    )(far, qn, proj, proj, misct, kn, vt, proj, near_t)


def mixer_a(proj, misct, b, s, gkv, wukv, gq, gk, bias_a):
    kn, vt, qn = a_pre(proj, b, s, gkv, wukv.astype(jnp.bfloat16), gq, gk)
    near_t = jnp.transpose(_bias_lookup(bias_a, _NEAR_BUCKET_T), (2, 0, 1))
    far = bias_a[_FAR_BUCKET]
    return a_main(proj, misct, qn, kn, vt, near_t, far, b, s)


N_CMP_PAD = 128
N_SB = SEQ // SEL_BLOCK
WIN_KEYS = WINDOW + QBLOCK
CMP_GROUPS = CMP_BLOCK // CMP_STRIDE

_CMP_BUCKET_T = _bucket_np((np.arange(SEQ // QBLOCK)[:, None, None] * QBLOCK + np.arange(QBLOCK)[None, None, :])
                           - (np.arange(N_CMP_PAD)[None, :, None] * CMP_STRIDE + CMP_BLOCK - 1))
_WIN_DIST_T = np.arange(QBLOCK)[None, :] + WINDOW - np.arange(WIN_KEYS)[:, None]
_WIN_BUCKET_T = _bucket_np(_WIN_DIST_T)
_CMP_MASK_T = np.where((np.arange(N_CMP_PAD)[None, :, None] * CMP_STRIDE + CMP_BLOCK - 1)
                       <= (np.arange(SEQ // QBLOCK)[:, None, None] * QBLOCK + np.arange(QBLOCK)[None, None, :]),
                       0.0, NEG_INF).astype(np.float32)
_WIN_MASK_T = np.where((_WIN_DIST_T >= 0) & (_WIN_DIST_T < WINDOW), 0.0, NEG_INF).astype(np.float32)
_OVERLAP_T = np.array([[1.0 if (n * CMP_STRIDE < j * SEL_BLOCK + SEL_BLOCK and n * CMP_STRIDE + CMP_BLOCK > j * SEL_BLOCK
                              and n < N_CMP_PAD - 1) else 0.0 for n in range(N_CMP_PAD)] for j in range(N_SB)], np.float32)
_SB_EXPAND = (np.arange(SEQ)[:, None] // SEL_BLOCK == np.arange(N_SB)[None, :]).astype(np.float32)


def _c_pre_kernel(cmp_ref, sel_ref, win_ref, kn_ref, pos_ref, w1_ref, w2_ref,
                  kvc_ref, ksel_ref, vselt_ref, kwin_ref, vwin_ref):
    f32, bf16 = jnp.float32, jnp.bfloat16
    d = C_HEAD_DIM

    def rms(x, g):
        return x * lax.rsqrt(jnp.mean(x * x, axis=-1, keepdims=True) + RMS_EPS) * g

    acc = [jnp.zeros((N_CMP_PAD, 2 * PHI_HIDDEN), f32) for _ in range(CMP_GROUPS)]
    for j in range(CMP_STRIDE):
        xs = cmp_ref[pl.ds(j, N_CMP_PAD, stride=CMP_STRIDE), :]
        for half in range(CMP_GROUPS):
            jj = half * CMP_STRIDE + j
            acc[half] = acc[half] + jnp.dot((xs + pos_ref[jj:jj + 1, :]).astype(bf16), w1_ref[jj],
                                            preferred_element_type=f32)
    hid = acc[0] + pltpu.roll(acc[1], N_CMP_PAD - 1, axis=0)
    kv = jnp.dot(_silu(hid).astype(bf16), w2_ref[...], preferred_element_type=f32)
    kvc_ref[0, :, 0:d] = rms(kv[:, 0:d], kn_ref[0:1, :])
    kvc_ref[0, :, d:2 * d] = kv[:, d:2 * d]

    sel = sel_ref[...]
    ksel_ref[...] = rms(sel[:, 0:d], kn_ref[1:2, :])
    vselt_ref[0] = sel.T[d:2 * d, :]
    win = win_ref[...]
    kwin_ref[0, 0:WINDOW, :] = jnp.zeros((WINDOW, d), f32)
    vwin_ref[0, 0:WINDOW, :] = jnp.zeros((WINDOW, d), f32)
    kwin_ref[0, WINDOW:, :] = rms(win[:, 0:d], kn_ref[2:3, :])
    vwin_ref[0, WINDOW:, :] = win[:, d:2 * d]


def c_pre(proj, b, s, k_norm, cmp_pos, phi_w1, phi_w2):
    d = C_HEAD_DIM
    kv0 = OFF["c_kv"] // 128
    pos = jnp.concatenate([cmp_pos[0], cmp_pos[1]], axis=-1)
    w1 = phi_w1.reshape(2, CMP_BLOCK, d, PHI_HIDDEN)
    zero = jnp.zeros((CMP_BLOCK, d, PHI_HIDDEN), jnp.float32)
    w1c = jnp.concatenate([jnp.concatenate([w1[0], zero], axis=-1),
                           jnp.concatenate([zero, w1[1]], axis=-1)], axis=1).astype(jnp.bfloat16)
    z2 = jnp.zeros((PHI_HIDDEN, d), jnp.float32)
    w2c = jnp.concatenate([jnp.concatenate([phi_w2[0], z2], axis=-1),
                           jnp.concatenate([z2, phi_w2[1]], axis=-1)], axis=0).astype(jnp.bfloat16)
    full = lambda a: pl.BlockSpec(a.shape, lambda bb: (0,) * a.ndim)
    return pl.pallas_call(
        _c_pre_kernel,
        grid=(b,),
        in_specs=[
            pl.BlockSpec((s, 128), lambda bb: (bb, kv0)),
            pl.BlockSpec((s, 128), lambda bb: (bb, kv0 + 1)),
            pl.BlockSpec((s, 128), lambda bb: (bb, kv0 + 2)),
            full(k_norm), full(pos), full(w1c), full(w2c),
        ],
        out_specs=[
            pl.BlockSpec((1, N_CMP_PAD, 128), lambda bb: (bb, 0, 0)),
            pl.BlockSpec((s, d), lambda bb: (bb, 0)),
            pl.BlockSpec((1, d, s), lambda bb: (bb, 0, 0)),
            pl.BlockSpec((1, s + WINDOW, d), lambda bb: (bb, 0, 0)),
            pl.BlockSpec((1, s + WINDOW, d), lambda bb: (bb, 0, 0)),
        ],
        out_shape=[
            jax.ShapeDtypeStruct((b, N_CMP_PAD, 128), jnp.float32),
            jax.ShapeDtypeStruct((b * s, d), jnp.float32),
            jax.ShapeDtypeStruct((b, d, s), jnp.float32),
            jax.ShapeDtypeStruct((b, s + WINDOW, d), jnp.float32),
            jax.ShapeDtypeStruct((b, s + WINDOW, d), jnp.float32),
        ],
        compiler_params=pltpu.CompilerParams(dimension_semantics=("arbitrary",), vmem_limit_bytes=VMEM_LIMIT),
        name="c_pre",
    )(proj, proj, proj, k_norm, pos, w1c, w2c)


C_QPS = 4


def _c_main_kernel(cq_ref, z_ref, misct_ref, gq_ref, kvc_ref, ksel_ref, vselt_ref, kwin_ref, vwin_ref,
                   cmptab_ref, near_ref, farrow_ref, wintab_ref, ovt_ref, exp_ref, o_ref, ls_scr):
    s = ksel_ref.shape[0]
    d = C_HEAD_DIM
    hq = C_HEADS * QBLOCK
    f32, bf16 = jnp.float32, jnp.bfloat16
    nt = (((1,), (1,)), ((), ()))
    tn = (((0,), (0,)), ((), ()))

    def softmax_rows(l, valid):
        l = jnp.where(valid, l, NEG_INF)
        m = _col_reduce(l, jnp.max)
        p = jnp.where(valid, jnp.exp(l - m), 0.0)
        den = _col_reduce(p, jnp.sum)
        return p, den

    def block(g, t0):
        nrows = t0 + QBLOCK
        qrows = slice(g * QBLOCK, (g + 1) * QBLOCK)

        cq = cq_ref[qrows, :]
        qs = jnp.concatenate([cq[:, h * d:(h + 1) * d] for h in range(C_HEADS)], axis=0)
        qs = qs * lax.rsqrt(jnp.mean(qs * qs, axis=-1, keepdims=True) + RMS_EPS) * gq_ref[...] * (d ** -0.5)
        qs = qs.astype(bf16)

        kvc = kvc_ref[0]
        lc = lax.dot_general(kvc[:, 0:d].astype(bf16), qs, nt, preferred_element_type=f32) + cmptab_ref[g]
        if t0 < CMP_BLOCK:
            n_idx = lax.broadcasted_iota(jnp.int32, (N_CMP_PAD, hq), 0)
            t_c = t0 + (lax.broadcasted_iota(jnp.int32, (N_CMP_PAD, hq), 1) & (QBLOCK - 1))
            pc, den_c = softmax_rows(lc, n_idx * CMP_STRIDE + (CMP_BLOCK - 1) <= t_c)
            pc = pc * jnp.where(den_c > 0.0, 1.0 / den_c, 0.0)
        else:
            pc = jnp.exp(lc - _col_reduce(lc, jnp.max))
            pc = pc * (1.0 / _col_reduce(pc, jnp.sum))
        o_cmp = lax.dot_general(kvc[:, d:2 * d].astype(bf16), pc.astype(bf16), tn, preferred_element_type=f32)

        psum = pc[:, 0:QBLOCK]
        for h in range(1, C_HEADS):
            psum = psum + pc[:, h * QBLOCK:(h + 1) * QBLOCK]
        p_hi = psum.astype(bf16)
        p_lo = (psum - p_hi.astype(f32)).astype(bf16)
        ovt = ovt_ref[...]
        imp = jnp.dot(ovt, p_hi, preferred_element_type=f32) + jnp.dot(ovt, p_lo, preferred_element_type=f32)
        j_idx = lax.broadcasted_iota(jnp.int32, (N_SB, QBLOCK), 0)
        t_b = t0 + lax.broadcasted_iota(jnp.int32, (N_SB, QBLOCK), 1)
        cur = lax.shift_right_arithmetic(t_b, SEL_BLOCK.bit_length() - 1)
        forced = (j_idx == 0) | (j_idx == cur) | (j_idx == jnp.maximum(cur - 1, 0))
        imp = jnp.where(j_idx * SEL_BLOCK <= t_b, jnp.where(forced, FORCE_SCORE, imp), NEG_INF)
        rank = jnp.zeros((N_SB, QBLOCK), f32)
        for r in range(N_SB):
            row = imp[r:r + 1, :]
            rank = rank + jnp.where((row > imp) | ((row == imp) & (j_idx > r)), 1.0, 0.0)
        picked = jnp.where(rank < float(N_SEL), 1.0, 0.0).astype(bf16)

        kw = kwin_ref[0, t0:t0 + WIN_KEYS, :]
        vw = vwin_ref[0, t0:t0 + WIN_KEYS, :]
        lw = lax.dot_general(kw.astype(bf16), qs, nt, preferred_element_type=f32) + wintab_ref[...]
        if t0 < WINDOW:
            r_idx = lax.broadcasted_iota(jnp.int32, (WIN_KEYS, hq), 0)
            lw = jnp.where(r_idx >= WINDOW - t0, lw, NEG_INF)
        pw = jnp.exp(lw - _col_reduce(lw, jnp.max))
        den_w = _col_reduce(pw, jnp.sum)
        o_win = lax.dot_general(vw.astype(bf16), pw.astype(bf16), tn, preferred_element_type=f32) / den_w

        gate = _sigmoid(misct_ref[0, C_G_LANE:C_G_LANE + 3 * C_HEADS, qrows])

        in_picked = jnp.dot(exp_ref[0:nrows, :], picked, preferred_element_type=f32) > 0.5
        sel_bias = jnp.where(in_picked, 0.0, NEG_INF)
        tri = jnp.where(lax.broadcasted_iota(jnp.int32, (QBLOCK, QBLOCK), 0)
                        <= lax.broadcasted_iota(jnp.int32, (QBLOCK, QBLOCK), 1), 0.0, NEG_INF)
        tail = sel_bias[t0:nrows] + tri
        sel_bias = jnp.concatenate([sel_bias[0:t0], tail], axis=0) if t0 else tail
        ls_scr[0:QBLOCK, :] = jnp.zeros((QBLOCK, hq), f32)
        ls_scr[QBLOCK:QBLOCK + nrows, :] = lax.dot_general(ksel_ref[0:nrows, :].astype(bf16), qs, nt,
                                                           preferred_element_type=f32)
        ls_scr[t0:t0 + 2 * QBLOCK, :] += near_ref[...] - farrow_ref[...]
        outs = []
        for h in range(C_HEADS):
            cols = slice(h * QBLOCK, (h + 1) * QBLOCK)
            l = ls_scr[QBLOCK:QBLOCK + nrows, cols] + sel_bias
            m = _col_reduce(l, jnp.max)
            p = jnp.exp(l - m)
            den = _col_reduce(p, jnp.sum)
            o_sel = jnp.dot(vselt_ref[0, :, 0:nrows].astype(bf16), p.astype(bf16), preferred_element_type=f32) / den
            outs.append(gate[3 * h:3 * h + 1, :] * o_cmp[:, cols] + gate[3 * h + 1:3 * h + 2, :] * o_sel
                        + gate[3 * h + 2:3 * h + 3, :] * o_win[:, cols])
        o_ref[qrows, :] = jnp.concatenate(outs, axis=0).T * _silu(z_ref[qrows, :])

    def run(step):
        for g in range(C_QPS):
            block(g, (C_QPS * step + g) * QBLOCK)

    step = pl.program_id(1)
    for v in range(s // (C_QPS * QBLOCK)):
        pl.when(step == v)(functools.partial(run, v))


def c_main(proj, misct, gq, kvc, ksel, vselt, kwin, vwin, cmptab, near, farrow, wintab, b, s):
    n = b * s
    tq = C_QPS * QBLOCK
    nq = s // tq
    d = C_HEAD_DIM
    hq = C_HEADS * QBLOCK
    row = lambda bb, i: bb * nq + i
    ovt = jnp.asarray(_OVERLAP_T, jnp.bfloat16)
    expand = jnp.asarray(_SB_EXPAND, jnp.bfloat16)
    const = lambda a: pl.BlockSpec(a.shape, lambda bb, i: (0,) * a.ndim)
    return pl.pallas_call(
        _c_main_kernel,
        grid=(b, nq),
        in_specs=[
            pl.BlockSpec((tq, C_WIDTH), lambda bb, i: (row(bb, i), OFF["c_q"] // C_WIDTH)),
            pl.BlockSpec((tq, C_WIDTH), lambda bb, i: (row(bb, i), OFF["c_z"] // C_WIDTH)),
            pl.BlockSpec((1, 128, tq), lambda bb, i: (bb, 0, i)),
            const(gq),
            pl.BlockSpec((1, N_CMP_PAD, 128), lambda bb, i: (bb, 0, 0)),
            pl.BlockSpec((s, d), lambda bb, i: (bb, 0)),
            pl.BlockSpec((1, d, s), lambda bb, i: (bb, 0, 0)),
            pl.BlockSpec((1, s + WINDOW, d), lambda bb, i: (bb, 0, 0)),
            pl.BlockSpec((1, s + WINDOW, d), lambda bb, i: (bb, 0, 0)),
            pl.BlockSpec((C_QPS, N_CMP_PAD, hq), lambda bb, i: (i, 0, 0)),
            const(near), const(farrow), const(wintab), const(ovt), const(expand),
        ],
        out_specs=pl.BlockSpec((tq, C_WIDTH), lambda bb, i: (row(bb, i), 0)),
        out_shape=jax.ShapeDtypeStruct((n, C_WIDTH), jnp.float32),
        scratch_shapes=[pltpu.VMEM((s + QBLOCK, hq), jnp.float32)],
        compiler_params=pltpu.CompilerParams(
            dimension_semantics=("arbitrary", "arbitrary"), vmem_limit_bytes=VMEM_LIMIT),
        name="c_main",
    )(proj, proj, misct, gq, kvc, ksel, vselt, kwin, vwin, cmptab, near, farrow, wintab, ovt, expand)


def _head_cols(tab):
    return jnp.moveaxis(tab, -1, -2).reshape(*tab.shape[:-2], tab.shape[-1] * tab.shape[-2])


def mixer_c(proj, misct, b, s, gq, k_norm, cmp_pos, phi_w1, phi_w2, bias_c):
    kvc, ksel, vselt, kwin, vwin = c_pre(proj, b, s, k_norm, cmp_pos, phi_w1, phi_w2)
    cmptab = _head_cols(_bias_lookup(bias_c, _CMP_BUCKET_T) + jnp.asarray(_CMP_MASK_T)[..., None])
    near = _head_cols(_bias_lookup(bias_c, _NEAR_BUCKET_T))
    wintab = _head_cols(_bias_lookup(bias_c, _WIN_BUCKET_T) + jnp.asarray(_WIN_MASK_T)[..., None])
    farrow = jnp.repeat(bias_c[_FAR_BUCKET], QBLOCK).reshape(1, C_HEADS * QBLOCK)
    return c_main(proj, misct, gq.reshape(1, C_HEAD_DIM), kvc, ksel, vselt, kwin, vwin, cmptab, near, farrow,
                  wintab, b, s)


B_HPS = 2
B_GROUP = 4


def _b_kernel(alog_ref, dtb_ref, q_ref, k_ref, v_ref, z_ref, misc_ref, arow_ref, cwq_ref, cwk_ref, cwv_ref,
              gon_ref, o_ref, qs, ks, vs, ws, at, gtok, btok, gcrow):
    s = q_ref.shape[0]
    dh = B_HEAD_DIM
    c_len = GDN_CHUNK
    j = pl.program_id(1)
    f32, bf16 = jnp.float32, jnp.bfloat16
    hi = lax.Precision.HIGHEST
    nt = (((1,), (1,)), ((), ()))
    tn = (((0,), (0,)), ((), ()))

    head_rows = lax.broadcasted_iota(jnp.int32, (SUBLANES, B_HPS * dh), 0)

    def conv_silu(x_ref, w_ref):
        x8 = x_ref[0:SUBLANES, :]
        w_last = w_ref[CONV_WIDTH - 1:CONV_WIDTH, :]
        head = x8 * w_last
        body = x_ref[SUBLANES:s, :] * w_last
        for k in range(1, CONV_WIDTH):
            wk = w_ref[CONV_WIDTH - 1 - k:CONV_WIDTH - k, :]
            head = head + jnp.where(head_rows >= k, pltpu.roll(x8, k, axis=0), 0.0) * wk
            body = body + x_ref[SUBLANES - k:s - k, :] * wk
        return _silu(jnp.concatenate([head, body], axis=0))

    def l2n(x):
        return x * lax.rsqrt(jnp.sum(x * x, axis=-1, keepdims=True) + RMS_EPS)

    qc = conv_silu(q_ref, cwq_ref)
    kc = conv_silu(k_ref, cwk_ref)
    vs[...] = conv_silu(v_ref, cwv_ref)
    misc = misc_ref[...]
    lane = lax.broadcasted_iota(jnp.int32, misc.shape, 1)
    ri = lax.broadcasted_iota(jnp.int32, (c_len, c_len), 0)
    ci = lax.broadcasted_iota(jnp.int32, (c_len, c_len), 1)
    lower = ci <= ri
    strict = ci < ri
    tri_u = jnp.where(ri <= ci, 1.0, 0.0)
    row_in_chunk = lax.broadcasted_iota(jnp.int32, (s, dh), 0) & (c_len - 1)
    heads = [B_HPS * j + hh for hh in range(B_HPS)]
    neg_a = [-jnp.exp(alog_ref[h]) for h in heads]
    lane_row = lax.broadcasted_iota(jnp.int32, (1, misc.shape[1]), 1)
    dtb_row = jnp.zeros((1, misc.shape[1]), f32)
    nega_row = jnp.zeros((1, misc.shape[1]), f32)
    for hh, h in enumerate(heads):
        dtb_row = jnp.where(lane_row == B_A_LANE + h, dtb_ref[h], dtb_row)
        nega_row = jnp.where(lane_row == B_A_LANE + h, neg_a[hh], nega_row)
    g_all = nega_row * jax.nn.softplus(misc + dtb_row)
    for sh in (1, 2, 4, 8, 16, 32):
        g_all = g_all + jnp.where(row_in_chunk >= sh, pltpu.roll(g_all, sh, axis=0), 0.0)
    beta_all = _sigmoid(misc)
    for hh, h in enumerate(heads):
        sl = slice(hh * dh, (hh + 1) * dh)
        qs[:, sl] = l2n(qc[:, sl]) * (dh ** -0.5)
        ks[:, sl] = l2n(kc[:, sl])
        g_tok = jnp.sum(jnp.where(lane == B_A_LANE + h, g_all, 0.0), axis=-1, keepdims=True)
        b_tok = jnp.sum(jnp.where(lane == B_B_LANE + h, beta_all, 0.0), axis=-1, keepdims=True)
        gtok[hh] = jnp.broadcast_to(g_tok, (s, dh))
        btok[hh] = jnp.broadcast_to(b_tok, (s, dh))
        g_row = neg_a[hh] * jax.nn.softplus(arow_ref[0, hh] + dtb_ref[h])
        gcrow[hh] = jnp.dot(g_row, tri_u, precision=hi, preferred_element_type=f32)

    lane3 = lax.broadcasted_iota(jnp.int32, (c_len, 3 * c_len), 1)
    lo_lanes = (lane3 >= c_len) & (lane3 < 2 * c_len)

    def split_lhs(p):
        p4 = jnp.concatenate([p, p, p], axis=1)
        hi4 = p4.astype(bf16).astype(f32)
        return jnp.where(lo_lanes, p4 - hi4, hi4).astype(bf16)

    def split_rhs(x):
        xh = x.astype(bf16)
        xl = (x - xh.astype(f32)).astype(bf16)
        return jnp.concatenate([xh, xh, xl], axis=0)

    def mm(p_split, x):
        return jnp.dot(p_split, split_rhs(x), preferred_element_type=f32)

    hs = range(B_HPS)
    sls = [slice(hh * dh, (hh + 1) * dh) for hh in hs]
    n_groups = s // (B_GROUP * c_len)

    def chunk_rows(c):
        return pl.ds(pl.multiple_of(c * c_len, c_len), c_len)

    def intra_group(i, between):
        where, q_c, k_c, v_c, gb, beta, g_row = [], [], [], [], [], [], []
        for c in [B_GROUP * i + cc for cc in range(B_GROUP)]:
            rs = chunk_rows(c)
            for hh in hs:
                where.append((rs, hh))
                q_c.append(qs[rs, sls[hh]])
                k_c.append(ks[rs, sls[hh]])
                v_c.append(vs[rs, sls[hh]])
                gb.append(gtok[hh, rs, :])
                beta.append(btok[hh, rs, :])
                g_row.append(gcrow[hh, pl.ds(c, 1), :])
        nch = range(len(where))
        decay = [jnp.where(lower, jnp.exp(jnp.where(lower, gb[n][:, 0:c_len] - g_row[n], 0.0)), 0.0) for n in nch]
        kb = [k_c[n] * beta[n] for n in nch]
        kk = [lax.dot_general(kb[n].astype(bf16), k_c[n].astype(bf16), nt, preferred_element_type=f32) for n in nch]
        between()
        p = [-jnp.where(strict, kk[n] * decay[n], 0.0) for n in nch]
        eg = [jnp.exp(gb[n]) for n in nch]
        x = [jnp.concatenate([v_c[n] * beta[n], kb[n] * eg[n]], axis=1) for n in nch]
        ps = [split_lhs(p[n]) for n in nch]
        x = [x[n] + mm(ps[n], x[n]) for n in nch]
        between()
        for _ in range(5):
            p = [mm(ps[n], p[n]) for n in nch]
            between()
            ps = [split_lhs(p[n]) for n in nch]
            x = [x[n] + mm(ps[n], x[n]) for n in nch]
            between()
        attn = [jnp.where(lower, lax.dot_general(q_c[n].astype(bf16), k_c[n].astype(bf16), nt,
                                                 preferred_element_type=f32) * decay[n], 0.0) for n in nch]
        between()

        def commit():
            for n, (rs, hh) in enumerate(where):
                g_last = g_row[n][:, c_len - 1:c_len]
                vs[rs, sls[hh]] = x[n][:, 0:dh]
                ws[rs, sls[hh]] = x[n][:, dh:2 * dh]
                qs[rs, sls[hh]] = q_c[n] * eg[n]
                ks[rs, sls[hh]] = k_c[n] * jnp.exp(g_last - gb[n])
                at[rs, hh * c_len:(hh + 1) * c_len] = attn[n]

        return commit

    def recur_steps(group, states, result):
        for c in [B_GROUP * group + cc for cc in range(B_GROUP)]:
            rs = chunk_rows(c)
            st_b = [states[hh].astype(bf16) for hh in hs]
            v_new = [vs[rs, sls[hh]] - jnp.dot(ws[rs, sls[hh]].astype(bf16), st_b[hh], preferred_element_type=f32)
                     for hh in hs]
            o_st = [jnp.dot(qs[rs, sls[hh]].astype(bf16), st_b[hh], preferred_element_type=f32) for hh in hs]
            yield
            v_nb = [v_new[hh].astype(bf16) for hh in hs]
            states = tuple(states[hh] * jnp.exp(gcrow[hh, pl.ds(c, 1), c_len - 1:c_len])
                           + lax.dot_general(ks[rs, sls[hh]].astype(bf16), v_nb[hh], tn, preferred_element_type=f32)
                           for hh in hs)
            for hh in hs:
                o_ref[rs, sls[hh]] = o_st[hh] + jnp.dot(at[rs, hh * c_len:(hh + 1) * c_len].astype(bf16), v_nb[hh],
                                                        preferred_element_type=f32)
            yield
        result.append(states)

    def overlapped(g, states):
        result = []
        steps = recur_steps(g - 1, states, result)
        commit = intra_group(g, lambda: next(steps, None))
        for _ in steps:
            pass
        commit()
        return result[0]

    intra_group(0, lambda: None)()
    states = lax.fori_loop(1, n_groups, overlapped, tuple(jnp.zeros((dh, dh), f32) for _ in hs))
    for _ in recur_steps(n_groups - 1, states, []):
        pass

    for hh in range(B_HPS):
        sl = slice(hh * dh, (hh + 1) * dh)
        o = o_ref[:, sl]
        o = o * lax.rsqrt(jnp.mean(o * o, axis=-1, keepdims=True) + RMS_EPS) * gon_ref[...]
        o_ref[:, sl] = o * _silu(z_ref[:, sl])


def mixer_b(proj, misct, b, s, conv_w, a_log, dt_bias, out_norm):
    n = b * s
    w2 = B_HPS * B_HEAD_DIM
    nc = s // GDN_CHUNK
    arow = misct[:, B_A_LANE:B_A_LANE + B_HEADS, :].reshape(b, B_HEADS, nc, GDN_CHUNK)
    qkv0 = OFF["b_qkv"] // w2
    kstep = B_WIDTH // w2
    smem = pl.BlockSpec(memory_space=pltpu.SMEM)
    return pl.pallas_call(
        _b_kernel,
        grid=(b, B_HEADS // B_HPS),
        in_specs=[
            smem, smem,
            pl.BlockSpec((s, w2), lambda bb, j: (bb, qkv0 + j)),
            pl.BlockSpec((s, w2), lambda bb, j: (bb, qkv0 + kstep + j)),
            pl.BlockSpec((s, w2), lambda bb, j: (bb, qkv0 + 2 * kstep + j)),
            pl.BlockSpec((s, w2), lambda bb, j: (bb, OFF["b_z"] // w2 + j)),
            pl.BlockSpec((s, 128), lambda bb, j: (bb, MISC_OFF // 128)),
            pl.BlockSpec((1, B_HPS, nc, GDN_CHUNK), lambda bb, j: (bb, j, 0, 0)),
            pl.BlockSpec((CONV_WIDTH, w2), lambda bb, j: (0, j)),
            pl.BlockSpec((CONV_WIDTH, w2), lambda bb, j: (0, kstep + j)),
            pl.BlockSpec((CONV_WIDTH, w2), lambda bb, j: (0, 2 * kstep + j)),
            pl.BlockSpec((1, B_HEAD_DIM), lambda bb, j: (0, 0)),
        ],
        out_specs=pl.BlockSpec((s, w2), lambda bb, j: (bb, j)),
        out_shape=jax.ShapeDtypeStruct((n, B_WIDTH), jnp.float32),
        scratch_shapes=[
            pltpu.VMEM((s, w2), jnp.float32), pltpu.VMEM((s, w2), jnp.float32), pltpu.VMEM((s, w2), jnp.float32),
            pltpu.VMEM((s, w2), jnp.float32), pltpu.VMEM((s, B_HPS * GDN_CHUNK), jnp.float32),
            pltpu.VMEM((B_HPS, s, B_HEAD_DIM), jnp.float32), pltpu.VMEM((B_HPS, s, B_HEAD_DIM), jnp.float32),
            pltpu.VMEM((B_HPS, nc, GDN_CHUNK), jnp.float32),
        ],
        compiler_params=pltpu.CompilerParams(
            dimension_semantics=("arbitrary", "arbitrary"), vmem_limit_bytes=VMEM_LIMIT),
        name="gdn",
    )(a_log, dt_bias, proj, proj, proj, proj, proj, arow, conv_w, conv_w, conv_w, out_norm.reshape(1, B_HEAD_DIM))


def kernel(x, norm_g, w_in, a_kv_norm, a_w_ukv, a_q_norm, a_k_norm, b_conv, b_a_log, b_dt_bias, b_out_norm, c_q_norm, c_k_norm, c_cmp_pos, c_phi_w1, c_phi_w2, w_branch, w_out, rel_bias):
    b, s, d_model = x.shape
    assert (s, d_model) == (SEQ, D_MODEL) and w_in.shape == (DEPTH, D_MODEL, N_IN), (x.shape, w_in.shape)
    n = b * s
    bias_a = rel_bias[:, :A_HEADS]
    bias_c = rel_bias[:, A_HEADS:]
    xf = x.reshape(n, D_MODEL)
    fields = [w_in[:, :, _ORIG[name][0]:_ORIG[name][0] + _ORIG[name][1]] for name in _NEW_ORDER]
    pad = jnp.zeros((DEPTH, D_MODEL, N_PAD - N_IN), w_in.dtype)
    w_in_p = jnp.concatenate(fields + [pad], axis=-1).astype(jnp.bfloat16)
    for l in range(DEPTH):
        proj = in_proj(xf, norm_g[l], w_in_p[l])
        p3 = proj.reshape(b, s, N_PAD)

        misct = jnp.transpose(p3[..., MISC_OFF:], (0, 2, 1))
        y_a = mixer_a(proj, misct, b, s, a_kv_norm[l], a_w_ukv[l], a_q_norm[l], a_k_norm[l], bias_a)

        y_b = mixer_b(proj, misct, b, s, b_conv[l], b_a_log[l], b_dt_bias[l], b_out_norm[l])
        y_c = mixer_c(proj, misct, b, s, c_q_norm[l], c_k_norm[l], c_cmp_pos[l], c_phi_w1[l], c_phi_w2[l], bias_c)

        wbr = w_branch[l].astype(jnp.bfloat16)
        xf = merge(xf, proj, y_a, y_b, y_c, wbr[:A_WIDTH], wbr[A_WIDTH:A_WIDTH + B_WIDTH],
                   wbr[A_WIDTH + B_WIDTH:], w_out[l].astype(jnp.bfloat16))
    return xf.reshape(b, s, D_MODEL)
```
